```python
import math
import jax, jax.numpy as jnp
from jax import lax
import numpy as np

D_MODEL = 2048
BATCH = 8
SEQ = 2048
DEPTH = 2

HEAD_DIM = 64
A_HEADS = 16
A_KV_HEADS = 4
A_WINDOW = 128
A_BLOCK = 128
B_HEADS = 8
B_BLOCK = 256
B_TOPK = 3
B_QCHUNK = 32
C_HEADS = 8
C_LATENT = 128
IDX_HEADS = 8
IDX_DIM = 32
C_TOPK_MAX = 256
C_QCHUNK = 128
REL_BUCKETS = 32
REL_MAX_DIST = 2048
N_ATTN_HEADS = A_HEADS + B_HEADS + C_HEADS
PEER_HEADS = 8
PEER_KEYS = 128
PEER_N = PEER_KEYS * PEER_KEYS
PEER_QDIM = 256
PEER_TOPK = 16
PEER_TCHUNK = 128
EPS = 1e-6
NEG = -1e30

IN_SPLITS = (A_HEADS * HEAD_DIM, A_KV_HEADS * HEAD_DIM, A_KV_HEADS * HEAD_DIM,
             B_HEADS * HEAD_DIM, B_HEADS * HEAD_DIM, B_HEADS * HEAD_DIM,
             C_HEADS * HEAD_DIM, C_LATENT, IDX_HEADS * IDX_DIM, IDX_DIM, IDX_HEADS,
             D_MODEL, D_MODEL, D_MODEL)
IN_COLS = sum(IN_SPLITS)

kernel_name = "hybrid_gated_swa_moba_dsa_peer"


def rmsnorm(x, g):
    x32 = x.astype(jnp.float32)
    y = x32 * lax.rsqrt(jnp.mean(x32 * x32, axis=-1, keepdims=True) + EPS)
    return (y * g.astype(jnp.float32)).astype(x.dtype)


def rel_bucket(dist):
    d = jnp.maximum(dist, 0)
    max_exact = REL_BUCKETS // 2
    df = jnp.maximum(d, max_exact).astype(jnp.float32)
    large = max_exact + (jnp.log(df / max_exact) / math.log(REL_MAX_DIST / max_exact)
                         * (REL_BUCKETS - max_exact)).astype(jnp.int32)
    large = jnp.minimum(large, REL_BUCKETS - 1)
    return jnp.where(d < max_exact, d, large)


def swa_sink_attention(q, k, v, sinks, bt):
    B, T = q.shape[0], q.shape[1]
    nq = T // A_BLOCK
    G = A_HEADS // A_KV_HEADS
    qb = q.reshape(B, nq, A_BLOCK, A_KV_HEADS, G, HEAD_DIM)
    pad = jnp.zeros((B, A_BLOCK, A_KV_HEADS, HEAD_DIM), k.dtype)
    kp = jnp.concatenate([pad, k], axis=1).reshape(B, nq + 1, A_BLOCK, A_KV_HEADS, HEAD_DIM)
    vp = jnp.concatenate([pad.astype(v.dtype), v], axis=1).reshape(B, nq + 1, A_BLOCK, A_KV_HEADS, HEAD_DIM)
    kb = jnp.concatenate([kp[:, :-1], kp[:, 1:]], axis=2)
    vb = jnp.concatenate([vp[:, :-1], vp[:, 1:]], axis=2)
    s = jnp.einsum("bnqhgd,bnkhd->bnhgqk", qb, kb).astype(jnp.float32) * (HEAD_DIM ** -0.5)
    qpos = jnp.arange(A_BLOCK)[:, None] + A_BLOCK
    kpos = jnp.arange(2 * A_BLOCK)[None, :]
    dist = qpos - kpos
    band = (dist >= 0) & (dist < A_WINDOW)
    real = (jnp.arange(nq)[:, None, None] > 0) | (kpos[None] >= A_BLOCK)
    mask = band[None] & real
    bias = bt[rel_bucket(dist)].astype(jnp.float32)
    bias = jnp.transpose(bias, (2, 0, 1)).reshape(A_KV_HEADS, G, A_BLOCK, 2 * A_BLOCK)
    s = jnp.where(mask[None, :, None, None], s + bias[None, None], NEG)
    sink = jnp.broadcast_to(sinks.astype(jnp.float32).reshape(A_KV_HEADS, G, 1, 1)[None, None],
                            s.shape[:-1] + (1,))
    p = jax.nn.softmax(jnp.concatenate([s, sink], axis=-1), axis=-1)[..., :-1]
    o = jnp.einsum("bnhgqk,bnkhd->bnqhgd", p.astype(v.dtype), vb)
    return o.reshape(B, T, A_HEADS * HEAD_DIM)


def moba_attention(q, k, v, bt):
    B, T = q.shape[0], q.shape[1]
    H = B_HEADS
    nb = -(-T // B_BLOCK)
    pad = nb * B_BLOCK - T
    kb = jnp.pad(k, ((0, 0), (0, pad), (0, 0), (0, 0))).reshape(B, nb, B_BLOCK, H, HEAD_DIM).transpose(0, 3, 1, 2, 4)
    vb = jnp.pad(v, ((0, 0), (0, pad), (0, 0), (0, 0))).reshape(B, nb, B_BLOCK, H, HEAD_DIM).transpose(0, 3, 1, 2, 4)
    kmean = jnp.mean(kb.astype(jnp.float32), axis=3)
    qblk = jnp.arange(T) // B_BLOCK
    gate = jnp.einsum("bthd,bhnd->bhtn", q.astype(jnp.float32), kmean)
    past = jnp.arange(nb)[None, :] < qblk[:, None]
    gate = jnp.where(past, gate, NEG)
    topb = min(B_TOPK, nb)
    _, sel = lax.top_k(gate, topb)
    valid = sel < qblk[None, None, :, None]
    nc = T // B_QCHUNK
    qc = q.reshape(B, nc, B_QCHUNK, H, HEAD_DIM).transpose(1, 0, 3, 2, 4)
    selc = sel.reshape(B, H, nc, B_QCHUNK, topb).transpose(2, 0, 1, 3, 4)
    validc = valid.reshape(B, H, nc, B_QCHUNK, topb).transpose(2, 0, 1, 3, 4)
    btT = bt.T.astype(jnp.float32)
    bi = jnp.arange(B)[:, None, None, None]
    hi = jnp.arange(H)[None, :, None, None]
    scale = HEAD_DIM ** -0.5

    def chunk(args):
        c, qx, sx, vx = args
        tq = c * B_QCHUNK + jnp.arange(B_QCHUNK)
        ks = kb[bi, hi, sx]
        vs = vb[bi, hi, sx]
        s_sel = jnp.einsum("bhqd,bhqnkd->bhqnk", qx, ks).astype(jnp.float32) * scale
        pos_sel = sx[..., None] * B_BLOCK + jnp.arange(B_BLOCK)
        b_sel = btT[hi[..., None], rel_bucket(tq[:, None, None] - pos_sel)]
        s_sel = jnp.where(vx[..., None], s_sel + b_sel, NEG).reshape(B, H, B_QCHUNK, topb * B_BLOCK)
        j = (c * B_QCHUNK) // B_BLOCK
        k_own = lax.dynamic_index_in_dim(kb, j, axis=2, keepdims=False)
        v_own = lax.dynamic_index_in_dim(vb, j, axis=2, keepdims=False)
        pos_own = j * B_BLOCK + jnp.arange(B_BLOCK)
        d_own = tq[:, None] - pos_own[None, :]
        s_own = jnp.einsum("bhqd,bhkd->bhqk", qx, k_own).astype(jnp.float32) * scale + btT[:, rel_bucket(d_own)][None]
        s_own = jnp.where((d_own >= 0)[None, None], s_own, NEG)
        p = jax.nn.softmax(jnp.concatenate([s_sel, s_own], axis=-1), axis=-1).astype(v.dtype)
        p_sel = p[..., :topb * B_BLOCK].reshape(B, H, B_QCHUNK, topb, B_BLOCK)
        p_own = p[..., topb * B_BLOCK:]
        return (jnp.einsum("bhqnk,bhqnkd->bhqd", p_sel, vs)
                + jnp.einsum("bhqk,bhkd->bhqd", p_own, v_own))

    o = lax.map(chunk, (jnp.arange(nc), qc, selc, validc))
    return o.transpose(1, 0, 3, 2, 4).reshape(B, T, H * HEAD_DIM)


def dsa_attention(q, c_lat, w_uk, w_uv, iq, ik, iw, bt):
    B, T = q.shape[0], q.shape[1]
    H = C_HEADS
    kc = min(C_TOPK_MAX, T // 4)
    nc = T // C_QCHUNK
    q_abs = jnp.einsum("bthd,chd->bthc", q, w_uk)
    qac = q_abs.reshape(B, nc, C_QCHUNK, H, C_LATENT).transpose(1, 0, 2, 3, 4)
    iqc = iq.reshape(B, nc, C_QCHUNK, IDX_HEADS, IDX_DIM).transpose(1, 0, 2, 3, 4)
    iwc = iw.reshape(B, nc, C_QCHUNK, IDX_HEADS).transpose(1, 0, 2, 3)
    btT = bt.T.astype(jnp.float32)
    kpos = jnp.arange(T)
    bi = jnp.arange(B)[:, None, None]
    hi = jnp.arange(H)[None, :, None, None]
    idx_scale = (IDX_DIM ** -0.5) * (IDX_HEADS ** -0.5)
    scale = HEAD_DIM ** -0.5

    def chunk(args):
        c, qx, iqx, iwx = args
        tq = c * C_QCHUNK + jnp.arange(C_QCHUNK)
        logits = jnp.einsum("bqhd,bsd->bqhs", iqx, ik).astype(jnp.float32)
        score = jnp.einsum("bqh,bqhs->bqs", iwx.astype(jnp.float32), jax.nn.relu(logits)) * idx_scale
        score = jnp.where(kpos[None, None, :] <= tq[None, :, None], score, NEG)
        _, sel = lax.top_k(score, kc)
        c_sel = c_lat[bi, sel]
        dist = tq[None, :, None] - sel
        s = (jnp.einsum("bqhc,bqkc->bhqk", qx, c_sel).astype(jnp.float32) * scale
             + btT[hi, rel_bucket(dist)[:, None]])
        s = jnp.where((dist >= 0)[:, None], s, NEG)
        p = jax.nn.softmax(s, axis=-1).astype(c_lat.dtype)
        oc = jnp.einsum("bhqk,bqkc->bqhc", p, c_sel)
        return jnp.einsum("bqhc,chd->bqhd", oc, w_uv).reshape(B, C_QCHUNK, H * HEAD_DIM)

    o = lax.map(chunk, (jnp.arange(nc), qac, iqc, iwc))
    return o.transpose(1, 0, 2, 3).reshape(B, T, H * HEAD_DIM)


def peer_ffn(h, w_q, sub_keys, u_tab, v_tab):
    B, T, D = h.shape
    n = B * T
    xt = h.reshape(n, D)
    q = (xt @ w_q).reshape(n, PEER_HEADS, 2, PEER_QDIM // 2)
    s = jnp.einsum("nhpd,hpkd->nhpk", q, sub_keys).astype(jnp.float32)
    s1, i1 = lax.top_k(s[:, :, 0], PEER_TOPK)
    s2, i2 = lax.top_k(s[:, :, 1], PEER_TOPK)
    cand_s = (s1[..., :, None] + s2[..., None, :]).reshape(n, PEER_HEADS, PEER_TOPK * PEER_TOPK)
    cand_i = (i1[..., :, None] * PEER_KEYS + i2[..., None, :]).reshape(n, PEER_HEADS, PEER_TOPK * PEER_TOPK)
    top_s, pos = lax.top_k(cand_s, PEER_TOPK)
    experts = jnp.take_along_axis(cand_i, pos, axis=-1)
    g = jax.nn.softmax(top_s, axis=-1).astype(h.dtype)
    nch = n // PEER_TCHUNK

    def chunk(args):
        xc, ec, gc = args
        a = jax.nn.gelu(jnp.einsum("td,thkd->thk", xc, u_tab[ec]))
        return jnp.einsum("thk,thkd->td", gc * a, v_tab[ec])

    out = lax.map(chunk, (xt.reshape(nch, PEER_TCHUNK, D),
                          experts.reshape(nch, PEER_TCHUNK, PEER_HEADS, PEER_TOPK),
                          g.reshape(nch, PEER_TCHUNK, PEER_HEADS, PEER_TOPK)))
    return out.reshape(B, T, D)


def setup_inputs(seed: int = 0) -> dict:
    key = jax.random.key(seed)
    ks = jax.random.split(key, 18)

    def nrm(k, shape, scale):
        return jax.random.normal(k, shape, jnp.float32) * scale

    L = DEPTH
    return {
        "x": nrm(ks[0], (BATCH, SEQ, D_MODEL), 1.0),
        "rel_bias": nrm(ks[1], (REL_BUCKETS, N_ATTN_HEADS), 0.5),
        "g_mix": 1.0 + nrm(ks[2], (L, D_MODEL), 0.02),
        "w_in": nrm(ks[3], (L, D_MODEL, IN_COLS), D_MODEL ** -0.5),
        "a_sinks": nrm(ks[4], (L, A_HEADS), 0.5),
        "c_lat_gain": 1.0 + nrm(ks[5], (L, C_LATENT), 0.02),
        "c_w_uk": nrm(ks[6], (L, C_LATENT, C_HEADS, HEAD_DIM), C_LATENT ** -0.5),
        "c_w_uv": nrm(ks[7], (L, C_LATENT, C_HEADS, HEAD_DIM), C_LATENT ** -0.5),
        "w_br_a": nrm(ks[8], (L, A_HEADS * HEAD_DIM, D_MODEL), (A_HEADS * HEAD_DIM) ** -0.5),
        "w_br_b": nrm(ks[9], (L, B_HEADS * HEAD_DIM, D_MODEL), (B_HEADS * HEAD_DIM) ** -0.5),
        "w_br_c": nrm(ks[10], (L, C_HEADS * HEAD_DIM, D_MODEL), (C_HEADS * HEAD_DIM) ** -0.5),
        "w_out": nrm(ks[11], (L, D_MODEL, D_MODEL), D_MODEL ** -0.5),
        "g_ffn": 1.0 + nrm(ks[12], (L, D_MODEL), 0.02),
        "peer_w_q": nrm(ks[13], (L, D_MODEL, PEER_HEADS * PEER_QDIM), D_MODEL ** -0.5),
        "peer_sub_keys": nrm(ks[14], (L, PEER_HEADS, 2, PEER_KEYS, PEER_QDIM // 2), (PEER_QDIM // 2) ** -0.5),
        "peer_u": nrm(ks[15], (L, PEER_N, D_MODEL), D_MODEL ** -0.5),
        "peer_v": nrm(ks[16], (L, PEER_N, D_MODEL), 0.25),
        "g_final": 1.0 + nrm(ks[17], (D_MODEL,), 0.02),
    }


def reference(x, rel_bias, g_mix, w_in, a_sinks, c_lat_gain, c_w_uk, c_w_uv, w_br_a, w_br_b, w_br_c,
              w_out, g_ffn, peer_w_q, peer_sub_keys, peer_u, peer_v, g_final):
    B, T, D = x.shape
    splits = [int(o) for o in np.cumsum(IN_SPLITS)[:-1]]
    bt_a = rel_bias[:, :A_HEADS]
    bt_b = rel_bias[:, A_HEADS:A_HEADS + B_HEADS]
    bt_c = rel_bias[:, A_HEADS + B_HEADS:]
    for l in range(DEPTH):
        h = rmsnorm(x, g_mix[l])
        (qa, ka, va, qb, kb, vb, qc, clat, iq, ik, iw, ga, gb, gc) = jnp.split(h @ w_in[l], splits, axis=-1)
        ya = swa_sink_attention(qa.reshape(B, T, A_HEADS, HEAD_DIM),
                                ka.reshape(B, T, A_KV_HEADS, HEAD_DIM),
                                va.reshape(B, T, A_KV_HEADS, HEAD_DIM), a_sinks[l], bt_a)
        yb = moba_attention(qb.reshape(B, T, B_HEADS, HEAD_DIM),
                            kb.reshape(B, T, B_HEADS, HEAD_DIM),
                            vb.reshape(B, T, B_HEADS, HEAD_DIM), bt_b)
        yc = dsa_attention(qc.reshape(B, T, C_HEADS, HEAD_DIM), rmsnorm(clat, c_lat_gain[l]),
                           c_w_uk[l], c_w_uv[l], iq.reshape(B, T, IDX_HEADS, IDX_DIM), ik, iw, bt_c)
        merged = (jax.nn.sigmoid(ga) * (ya @ w_br_a[l])
                  + jax.nn.sigmoid(gb) * (yb @ w_br_b[l])
                  + jax.nn.sigmoid(gc) * (yc @ w_br_c[l]))
        x = x + merged @ w_out[l]
        x = x + peer_ffn(rmsnorm(x, g_ffn[l]), peer_w_q[l], peer_sub_keys[l], peer_u[l], peer_v[l])
    return rmsnorm(x, g_final)
```

```python
import functools
import math

import jax
import jax.numpy as jnp
import numpy as np
from jax import lax
from jax.experimental import pallas as pl
from jax.experimental.pallas import tpu as pltpu

F32 = jnp.float32
BF16 = jnp.bfloat16
I32 = jnp.int32

D_MODEL = 2048
HEAD_DIM = 64
A_HEADS, A_KV_HEADS, A_BLOCK, A_WINDOW = 16, 4, 128, 128
B_HEADS, B_BLOCK, B_TOPK = 8, 256, 3
C_HEADS, C_LATENT, IDX_HEADS, IDX_DIM, C_TOPK_MAX = 8, 128, 8, 32, 256
REL_BUCKETS, REL_MAX_DIST = 32, 2048
PEER_HEADS, PEER_KEYS, PEER_QDIM, PEER_TOPK = 8, 128, 256, 16
PEER_N = PEER_KEYS * PEER_KEYS
EPS = 1e-6
NEG = -1e30
INT_MIN = -(2 ** 31)
IN_SPLITS = (A_HEADS * HEAD_DIM, A_KV_HEADS * HEAD_DIM, A_KV_HEADS * HEAD_DIM,
             B_HEADS * HEAD_DIM, B_HEADS * HEAD_DIM, B_HEADS * HEAD_DIM,
             C_HEADS * HEAD_DIM, C_LATENT, IDX_HEADS * IDX_DIM, IDX_DIM, IDX_HEADS,
             D_MODEL, D_MODEL, D_MODEL)

TILE = 256
VMEM_LIMIT = 56 * 1024 * 1024
NT = (((1,), (1,)), ((), ()))


def _params(sem, vmem=VMEM_LIMIT):
    return pltpu.CompilerParams(dimension_semantics=sem, vmem_limit_bytes=vmem)


def _rms(x, g):
    return x * lax.rsqrt(jnp.mean(x * x, axis=-1, keepdims=True) + EPS) * g


def _sortable(v):
    bits = pltpu.bitcast(v, I32)
    return jnp.where(bits < 0, bits ^ jnp.int32(0x7FFFFFFF), bits)


def _unsortable(k):
    return pltpu.bitcast(jnp.where(k < 0, k ^ jnp.int32(0x7FFFFFFF), k), F32)


def _norm_matmul_kernel(x_ref, g_ref, w_ref, o_ref, *rest, emit_norm):
    if emit_norm:
        xn_ref, h_ref = rest
    else:
        (h_ref,) = rest

    @pl.when(pl.program_id(1) == 0)
    def _():
        h = _rms(x_ref[...], g_ref[...]).astype(BF16)
        h_ref[...] = h
        if emit_norm:
            xn_ref[...] = h

    o_ref[...] = jnp.dot(h_ref[...], w_ref[...], preferred_element_type=F32).astype(o_ref.dtype)


def norm_matmul(x, g, w, out_dtype, emit_norm=False, tm=512, tn=512):
    n, d = x.shape
    ncol = w.shape[1]
    out_shape = [jax.ShapeDtypeStruct((n, ncol), out_dtype)]
    out_specs = [pl.BlockSpec((tm, tn), lambda i, j: (i, j))]
    if emit_norm:
        out_shape.append(jax.ShapeDtypeStruct((n, d), BF16))
        out_specs.append(pl.BlockSpec((tm, d), lambda i, j: (i, 0)))
    res = pl.pallas_call(
        functools.partial(_norm_matmul_kernel, emit_norm=emit_norm),
        grid=(n // tm, ncol // tn),
        in_specs=[pl.BlockSpec((tm, d), lambda i, j: (i, 0)),
                  pl.BlockSpec((1, d), lambda i, j: (0, 0)),
                  pl.BlockSpec((d, tn), lambda i, j: (0, j))],
        out_specs=out_specs,
        out_shape=out_shape,
        scratch_shapes=[pltpu.VMEM((tm, d), BF16)],
        compiler_params=_params(("parallel", "arbitrary")),
        name="norm_matmul",
    )(x, g.reshape(1, d), w)
    return res if emit_norm else res[0]


def _matmul_res_kernel(a_ref, w_ref, r_ref, o_ref):
    o_ref[...] = r_ref[...] + jnp.dot(a_ref[...], w_ref[...], preferred_element_type=F32)


def matmul_res(a, w, res, tm=512, tn=512):
    n, k = a.shape
    ncol = w.shape[1]
    return pl.pallas_call(
        _matmul_res_kernel,
        grid=(n // tm, ncol // tn),
        in_specs=[pl.BlockSpec((tm, k), lambda i, j: (i, 0)),
                  pl.BlockSpec((k, tn), lambda i, j: (0, j)),
                  pl.BlockSpec((tm, tn), lambda i, j: (i, j))],
        out_specs=pl.BlockSpec((tm, tn), lambda i, j: (i, j)),
        out_shape=jax.ShapeDtypeStruct((n, ncol), F32),
        compiler_params=_params(("parallel", "arbitrary")),
        name="matmul_res",
    )(a, w, res)


def _merge_kernel(x_ref, g_ref, ya_ref, yb_ref, yc_ref, wga_ref, wgb_ref, wgc_ref,
                  wa_ref, wb_ref, wc_ref, o_ref, h_ref):
    @pl.when(pl.program_id(1) == 0)
    def _():
        h_ref[...] = _rms(x_ref[...], g_ref[...]).astype(BF16)

    h = h_ref[...]

    def branch(y_ref, wg_ref, w_ref):
        gate = jnp.dot(h, wg_ref[...], preferred_element_type=F32)
        proj = jnp.dot(y_ref[...], w_ref[...], preferred_element_type=F32)
        return jax.nn.sigmoid(gate) * proj

    merged = branch(ya_ref, wga_ref, wa_ref) + branch(yb_ref, wgb_ref, wb_ref) + branch(yc_ref, wgc_ref, wc_ref)
    o_ref[...] = merged.astype(o_ref.dtype)


def merge(x, g, ya, yb, yc, w_gate, wa, wb, wc, tm=512, tn=512):
    n, d = x.shape
    nj = d // tn
    return pl.pallas_call(
        _merge_kernel,
        grid=(n // tm, nj),
        in_specs=[pl.BlockSpec((tm, d), lambda i, j: (i, 0)),
                  pl.BlockSpec((1, d), lambda i, j: (0, 0)),
                  pl.BlockSpec((tm, ya.shape[1]), lambda i, j: (i, 0)),
                  pl.BlockSpec((tm, yb.shape[1]), lambda i, j: (i, 0)),
                  pl.BlockSpec((tm, yc.shape[1]), lambda i, j: (i, 0)),
                  pl.BlockSpec((d, tn), lambda i, j: (0, j)),
                  pl.BlockSpec((d, tn), lambda i, j: (0, j + nj)),
                  pl.BlockSpec((d, tn), lambda i, j: (0, j + 2 * nj)),
                  pl.BlockSpec((wa.shape[0], tn), lambda i, j: (0, j)),
                  pl.BlockSpec((wb.shape[0], tn), lambda i, j: (0, j)),
                  pl.BlockSpec((wc.shape[0], tn), lambda i, j: (0, j))],
        out_specs=pl.BlockSpec((tm, tn), lambda i, j: (i, j)),
        out_shape=jax.ShapeDtypeStruct((n, d), BF16),
        scratch_shapes=[pltpu.VMEM((tm, d), BF16)],
        compiler_params=_params(("parallel", "arbitrary")),
        name="merge",
    )(x, g.reshape(1, d), ya, yb, yc, w_gate, w_gate, w_gate, wa, wb, wc)


def _final_norm_kernel(x_ref, g_ref, o_ref):
    o_ref[...] = _rms(x_ref[...], g_ref[...])


def final_norm(x, g, tm=512):
    n, d = x.shape
    return pl.pallas_call(
        _final_norm_kernel,
        grid=(n // tm,),
        in_specs=[pl.BlockSpec((tm, d), lambda i: (i, 0)), pl.BlockSpec((1, d), lambda i: (0, 0))],
        out_specs=pl.BlockSpec((tm, d), lambda i: (i, 0)),
        out_shape=jax.ShapeDtypeStruct((n, d), F32),
        compiler_params=_params(("parallel",)),
        name="final_norm",
    )(x, g.reshape(1, d))


def _rel_bucket(dist):
    d = jnp.maximum(dist, 0)
    max_exact = REL_BUCKETS // 2
    df = jnp.maximum(d, max_exact).astype(F32)
    large = max_exact + (jnp.log(df / max_exact) / math.log(REL_MAX_DIST / max_exact)
                         * (REL_BUCKETS - max_exact)).astype(I32)
    large = jnp.minimum(large, REL_BUCKETS - 1)
    return jnp.where(d < max_exact, d, large)


def _bias_windows(bt, seq):
    dist = (jnp.arange(seq // TILE)[:, None] * TILE - (TILE - 1) + jnp.arange(2 * TILE)[None, :])
    w = bt[_rel_bucket(dist)].astype(F32)
    w = jnp.where((dist >= 0)[..., None], w, 0.0)
    return jnp.transpose(w, (2, 0, 1))


def _gen_bias_tiles(win_ref, tiles_ref):
    n_heads, n_rel = win_ref.shape[0], win_ref.shape[1]

    def body(t, carry):
        h = t // n_rel
        r = t % n_rel
        row = win_ref[h, pl.ds(r, 1), :]
        x = jnp.broadcast_to(row, (TILE, 2 * TILE))
        tiles_ref[h, r] = pltpu.roll(x, TILE + 1, 1, stride=1, stride_axis=0)[:, :TILE]
        return carry

    lax.fori_loop(0, n_heads * n_rel, body, 0)


def _swa_kernel(q_ref, kp_ref, kc_ref, vp_ref, vc_ref, bias_ref, sink_ref, o_ref):
    i = pl.program_id(1)
    which = jnp.minimum(i, 1)
    group = A_HEADS // A_KV_HEADS
    outs = []
    for kvh in range(A_KV_HEADS):
        lo, hi = kvh * HEAD_DIM, (kvh + 1) * HEAD_DIM
        kk = jnp.concatenate([kp_ref[0, :, lo:hi], kc_ref[0, :, lo:hi]], axis=0)
        vv = jnp.concatenate([vp_ref[0, :, lo:hi], vc_ref[0, :, lo:hi]], axis=0)
        for g in range(group):
            h = kvh * group + g
            q = q_ref[0, :, h * HEAD_DIM:(h + 1) * HEAD_DIM]
            s = lax.dot_general(q, kk, NT, preferred_element_type=F32) * (HEAD_DIM ** -0.5)
            s = s + bias_ref[which, h]
            sink = sink_ref[h]
            m = jnp.maximum(jnp.max(s, axis=-1, keepdims=True), sink)
            p = jnp.exp(s - m)
            denom = jnp.sum(p, axis=-1, keepdims=True) + jnp.exp(sink - m)
            o = jnp.dot(p.astype(BF16), vv, preferred_element_type=F32) / denom
            outs.append(o.astype(BF16))
    o_ref[0] = jnp.concatenate(outs, axis=-1)


def _swa_bias(bt_a):
    qpos = jnp.arange(A_BLOCK)[:, None] + A_BLOCK
    kpos = jnp.arange(2 * A_BLOCK)[None, :]
    dist = qpos - kpos
    band = (dist >= 0) & (dist < A_WINDOW)
    bias = jnp.transpose(bt_a[_rel_bucket(dist)].astype(F32), (2, 0, 1))
    general = jnp.where(band[None], bias, NEG)
    first = jnp.where((band & (kpos >= A_BLOCK))[None], bias, NEG)
    return jnp.stack([first, general])


def swa(p_a, bias, sinks):
    bsz, seq, _ = p_a.shape
    qw, kw = A_HEADS * HEAD_DIM, A_KV_HEADS * HEAD_DIM
    kblk, vblk = qw // kw, qw // kw + 1
    return pl.pallas_call(
        _swa_kernel,
        grid=(bsz, seq // A_BLOCK),
        in_specs=[pl.BlockSpec((1, A_BLOCK, qw), lambda b, i: (b, i, 0)),
                  pl.BlockSpec((1, A_BLOCK, kw), lambda b, i: (b, jnp.maximum(i - 1, 0), kblk)),
                  pl.BlockSpec((1, A_BLOCK, kw), lambda b, i: (b, i, kblk)),
                  pl.BlockSpec((1, A_BLOCK, kw), lambda b, i: (b, jnp.maximum(i - 1, 0), vblk)),
                  pl.BlockSpec((1, A_BLOCK, kw), lambda b, i: (b, i, vblk)),
                  pl.BlockSpec(bias.shape, lambda b, i: (0, 0, 0, 0)),
                  pl.BlockSpec(memory_space=pltpu.SMEM)],
        out_specs=pl.BlockSpec((1, A_BLOCK, qw), lambda b, i: (b, i, 0)),
        out_shape=jax.ShapeDtypeStruct((bsz, seq, qw), BF16),
        compiler_params=_params(("parallel", "arbitrary")),
        name="swa",
    )(p_a, p_a, p_a, p_a, p_a, bias, sinks)


def _moba_kernel(q_ref, k_ref, vt_ref, win_ref, o_ref, tiles_ref, kmean_ref, sel_ref):
    b, h, i = pl.program_id(0), pl.program_id(1), pl.program_id(2)

    @pl.when((b == 0) & (h == 0) & (i == 0))
    def _():
        _gen_bias_tiles(win_ref, tiles_ref)

    @pl.when(i == 0)
    def _():
        kmean_ref[...] = jnp.mean(k_ref[0, 0].astype(F32), axis=1)

    q = q_ref[0, 0]
    n_blk = kmean_ref.shape[0]
    kmean = kmean_ref[...]
    kmean_hi = kmean.astype(BF16)
    kmean_lo = (kmean - kmean_hi.astype(F32)).astype(BF16)
    gate = (lax.dot_general(kmean_hi, q, NT, preferred_element_type=F32)
            + lax.dot_general(kmean_lo, q, NT, preferred_element_type=F32))
    row = lax.broadcasted_iota(I32, gate.shape, 0)
    gate = jnp.where(row < i, gate, NEG)
    sel = jnp.zeros(gate.shape, F32)
    for _ in range(B_TOPK):
        best = jnp.max(gate, axis=0, keepdims=True)
        first = jnp.min(jnp.where(gate == best, row, n_blk), axis=0, keepdims=True)
        pick = row == first
        sel = jnp.where(pick, 1.0, sel)
        gate = jnp.where(pick, -jnp.inf, gate)
    sel_ref[...] = jnp.where(row < i, sel, 0.0)

    key_pos = lax.broadcasted_iota(I32, (TILE, TILE), 0)
    qry_pos = lax.broadcasted_iota(I32, (TILE, TILE), 1)
    causal = jnp.where(qry_pos >= key_pos, 1.0, 0.0)

    def body(j, carry):
        m, l, acc = carry
        s = lax.dot_general(k_ref[0, 0, j], q, NT, preferred_element_type=F32) * (HEAD_DIM ** -0.5)
        s = s + tiles_ref[h, i - j]
        chosen = jnp.broadcast_to(sel_ref[pl.ds(j, 1), :], (TILE, TILE))
        ok = jnp.where(j == i, causal, chosen) > 0.0
        s = jnp.where(ok, s, NEG)
        m_new = jnp.maximum(m, jnp.max(s, axis=0, keepdims=True))
        alpha = jnp.exp(m - m_new)
        p = jnp.where(ok, jnp.exp(s - m_new), 0.0)
        l = alpha * l + jnp.sum(p, axis=0, keepdims=True)
        acc = alpha * acc + jnp.dot(vt_ref[0, 0, j], p.astype(BF16), preferred_element_type=F32)
        return m_new, l, acc

    init = (jnp.full((1, TILE), NEG, F32), jnp.zeros((1, TILE), F32), jnp.zeros((HEAD_DIM, TILE), F32))
    _, l, acc = lax.fori_loop(0, i + 1, body, init)
    o_ref[0, 0] = (acc / l).astype(o_ref.dtype)


def moba(q, k, vt, windows):
    bsz, n_heads, seq, d = q.shape
    n_blk = seq // TILE
    return pl.pallas_call(
        _moba_kernel,
        grid=(bsz, n_heads, n_blk),
        in_specs=[pl.BlockSpec((1, 1, TILE, d), lambda b, h, i: (b, h, i, 0)),
                  pl.BlockSpec((1, 1, n_blk, TILE, d), lambda b, h, i: (b, h, 0, 0, 0)),
                  pl.BlockSpec((1, 1, n_blk, d, TILE), lambda b, h, i: (b, h, 0, 0, 0)),
                  pl.BlockSpec(windows.shape, lambda b, h, i: (0, 0, 0))],
        out_specs=pl.BlockSpec((1, 1, d, TILE), lambda b, h, i: (b, h, 0, i)),
        out_shape=jax.ShapeDtypeStruct((bsz, n_heads, d, seq), BF16),
        scratch_shapes=[pltpu.VMEM((n_heads, n_blk, TILE, TILE), F32),
                        pltpu.VMEM((n_blk, d), F32),
                        pltpu.VMEM((n_blk, TILE), F32)],
        compiler_params=_params(("arbitrary", "arbitrary", "arbitrary")),
        name="moba",
    )(q, k, vt, windows)


def _dsa_kernel(qc_ref, iq_ref, mq_ref, clat_ref, mk_ref, gain_ref, wuk_ref, wuvt_ref, win_ref, o_ref,
                tiles_ref, cn_ref, cnt_ref, ik_ref, keys_ref, mask_ref, *, top_k):
    b, c = pl.program_id(0), pl.program_id(1)
    seq = clat_ref.shape[1]
    n_blk = seq // TILE

    @pl.when((b == 0) & (c == 0))
    def _():
        _gen_bias_tiles(win_ref, tiles_ref)

    @pl.when(c == 0)
    def _():
        lat = _rms(clat_ref[0], gain_ref[...])
        cn_ref[...] = lat.astype(BF16)
        cnt_ref[...] = jnp.transpose(lat).astype(BF16)
        ik_ref[...] = mk_ref[0][:, :IDX_DIM].astype(BF16)

    iq = iq_ref[0].astype(BF16)
    iw_t = jnp.transpose(mq_ref[0])[IDX_DIM:IDX_DIM + IDX_HEADS, :]
    score = jnp.zeros((seq, TILE), F32)
    for hh in range(IDX_HEADS):
        logits = lax.dot_general(ik_ref[...], iq[:, hh * IDX_DIM:(hh + 1) * IDX_DIM], NT,
                                 preferred_element_type=F32)
        score = score + jnp.maximum(logits, 0.0) * iw_t[hh:hh + 1, :]
    score = score * ((IDX_DIM ** -0.5) * (IDX_HEADS ** -0.5))
    score = jnp.where(score == 0.0, 0.0, score)
    key_pos = lax.broadcasted_iota(I32, (seq, TILE), 0)
    qry_pos = c * TILE + lax.broadcasted_iota(I32, (seq, TILE), 1)
    causal = key_pos <= qry_pos
    keys_ref[...] = jnp.where(causal, _sortable(score), INT_MIN)

    def value_bit(t, tau):
        cand = tau + lax.shift_left(jnp.int32(1), 31 - t)
        cnt = jnp.sum(jnp.where(keys_ref[...] >= cand, 1.0, 0.0), axis=0, keepdims=True)
        return jnp.where(cnt >= top_k, cand, tau)

    tau = lax.fori_loop(0, 32, value_bit, jnp.full((1, TILE), INT_MIN, I32))
    keys = keys_ref[...]
    above = keys > tau
    tied = keys == tau
    need = top_k - jnp.sum(jnp.where(above, 1.0, 0.0), axis=0, keepdims=True)

    def index_bit(t, bound):
        cand = bound + lax.shift_left(jnp.int32(1), 11 - t)
        cnt = jnp.sum(jnp.where(keys_ref[...] == tau, jnp.where(key_pos < cand, 1.0, 0.0), 0.0),
                      axis=0, keepdims=True)
        return jnp.where(cnt <= need, cand, bound)

    bound = lax.fori_loop(0, 12, index_bit, jnp.zeros((1, TILE), I32))
    chosen = jnp.where(above, 1.0, jnp.where(tied, jnp.where(key_pos < bound, 1.0, 0.0), 0.0))
    mask_ref[...] = jnp.where(causal, chosen, 0.0)

    qc = qc_ref[0].astype(BF16)
    for h in range(C_HEADS):
        q_abs_t = lax.dot_general(wuk_ref[h], qc[:, h * HEAD_DIM:(h + 1) * HEAD_DIM], NT,
                                  preferred_element_type=F32)
        s = jnp.dot(cn_ref[...], q_abs_t.astype(BF16), preferred_element_type=F32) * (HEAD_DIM ** -0.5)
        bias = jnp.concatenate([tiles_ref[h, jnp.maximum(c - kb, 0)] for kb in range(n_blk)], axis=0)
        s = jnp.where(mask_ref[...] > 0.0, s + bias, NEG)
        m = jnp.max(s, axis=0, keepdims=True)
        p = jnp.exp(s - m)
        l = jnp.sum(p, axis=0, keepdims=True)
        oc_t = jnp.dot(cnt_ref[...], p.astype(BF16), preferred_element_type=F32) / l
        out_t = jnp.dot(wuvt_ref[h], oc_t.astype(BF16), preferred_element_type=F32)
        o_ref[0, h * HEAD_DIM:(h + 1) * HEAD_DIM, :] = out_t.astype(o_ref.dtype)


def dsa(p_c, gain, w_uk, w_uv_t, windows):
    bsz, seq, _ = p_c.shape
    top_k = min(C_TOPK_MAX, seq // 4)
    qw = C_HEADS * HEAD_DIM
    return pl.pallas_call(
        functools.partial(_dsa_kernel, top_k=top_k),
        grid=(bsz, seq // TILE),
        in_specs=[pl.BlockSpec((1, TILE, qw), lambda b, c: (b, c, 0)),
                  pl.BlockSpec((1, TILE, 256), lambda b, c: (b, c, 2)),
                  pl.BlockSpec((1, TILE, 128), lambda b, c: (b, c, 7)),
                  pl.BlockSpec((1, seq, 128), lambda b, c: (b, 0, 6)),
                  pl.BlockSpec((1, seq, 128), lambda b, c: (b, 0, 7)),
                  pl.BlockSpec((1, C_LATENT), lambda b, c: (0, 0)),
                  pl.BlockSpec(w_uk.shape, lambda b, c: (0, 0, 0)),
                  pl.BlockSpec(w_uv_t.shape, lambda b, c: (0, 0, 0)),
                  pl.BlockSpec(windows.shape, lambda b, c: (0, 0, 0))],
        out_specs=pl.BlockSpec((1, qw, TILE), lambda b, c: (b, 0, c)),
        out_shape=jax.ShapeDtypeStruct((bsz, qw, seq), BF16),
        scratch_shapes=[pltpu.VMEM((C_HEADS, seq // TILE, TILE, TILE), F32),
                        pltpu.VMEM((seq, C_LATENT), BF16),
                        pltpu.VMEM((C_LATENT, seq), BF16),
                        pltpu.VMEM((seq, IDX_DIM), BF16),
                        pltpu.VMEM((seq, TILE), I32),
                        pltpu.VMEM((seq, TILE), F32)],
        compiler_params=_params(("arbitrary", "arbitrary")),
        name="dsa",
    )(p_c, p_c, p_c, p_c, p_c, gain.reshape(1, C_LATENT), w_uk, w_uv_t, windows)


ROUTE_TM = 256


def _top_values(s, count):
    vals, mult = [], []
    for _ in range(count):
        best = jnp.max(s, axis=0, keepdims=True)
        hit = s == best
        vals.append(best)
        mult.append(jnp.sum(jnp.where(hit, 1.0, 0.0), axis=0, keepdims=True))
        s = jnp.where(hit, -jnp.inf, s)
    return jnp.concatenate(vals, axis=0), jnp.concatenate(mult, axis=0)


def _route_kernel(q_ref, k_ref, s1_ref, s2_ref, sc_ref, keys_ref, wts_ref):
    q = q_ref[...]
    half = PEER_QDIM // 2
    s1 = lax.dot_general(k_ref[0, 0], q[:, :half], NT, preferred_element_type=F32)
    s2 = lax.dot_general(k_ref[0, 1], q[:, half:], NT, preferred_element_type=F32)
    s1_ref[0] = s1
    s2_ref[0] = s2
    t1, w1 = _top_values(s1, PEER_TOPK)
    t2, w2 = _top_values(s2, PEER_TOPK)
    cand = jnp.concatenate([t1[r:r + 1] + t2 for r in range(PEER_TOPK)], axis=0)
    wts = jnp.concatenate([w1[r:r + 1] * w2 for r in range(PEER_TOPK)], axis=0)
    keys_ref[...] = _sortable(cand)
    wts_ref[...] = wts

    def value_bit(t, tau):
        c = tau + lax.shift_left(jnp.int32(1), 31 - t)
        cnt = jnp.sum(jnp.where(keys_ref[...] >= c, wts_ref[...], 0.0), axis=0, keepdims=True)
        return jnp.where(cnt >= PEER_TOPK, c, tau)

    tau_key = lax.fori_loop(0, 32, value_bit, jnp.full((1, q.shape[0]), INT_MIN, I32))
    top = t1[0:1] + t2[0:1]
    z = jnp.sum(jnp.where(keys_ref[...] >= tau_key, wts * jnp.exp(cand - top), 0.0), axis=0, keepdims=True)
    rows = [_unsortable(tau_key), t2[0:1], t1[0:1] + jnp.log(z)]
    sc_ref[0] = jnp.concatenate(rows + [jnp.zeros_like(z)] * (8 - len(rows)), axis=0)


def peer_route(qp, sub_keys):
    n = qp.shape[0]
    tm = ROUTE_TM
    half = PEER_QDIM // 2
    return pl.pallas_call(
        _route_kernel,
        grid=(n // tm, PEER_HEADS),
        in_specs=[pl.BlockSpec((tm, PEER_QDIM), lambda i, h: (i, h)),
                  pl.BlockSpec((1, 2, PEER_KEYS, half), lambda i, h: (h, 0, 0, 0))],
        out_specs=[pl.BlockSpec((1, PEER_KEYS, tm), lambda i, h: (h, 0, i)),
                   pl.BlockSpec((1, PEER_KEYS, tm), lambda i, h: (h, 0, i)),
                   pl.BlockSpec((1, 8, tm), lambda i, h: (h, 0, i))],
        out_shape=[jax.ShapeDtypeStruct((PEER_HEADS, PEER_KEYS, n), F32),
                   jax.ShapeDtypeStruct((PEER_HEADS, PEER_KEYS, n), F32),
                   jax.ShapeDtypeStruct((PEER_HEADS, 8, n), F32)],
        scratch_shapes=[pltpu.VMEM((PEER_TOPK * PEER_TOPK, tm), I32),
                        pltpu.VMEM((PEER_TOPK * PEER_TOPK, tm), F32)],
        compiler_params=_params(("parallel", "arbitrary")),
        name="peer_route",
    )(qp, sub_keys)


def _peer_kernel(x_ref, xn_ref, s1_ref, s2_ref, sc_ref, u_ref, vt_ref, o_ref,
                 acc_ref, e2_ref, c1_ref, h_ref):
    j = pl.program_id(1)
    te = u_ref.shape[0]
    n_sub = te // PEER_KEYS

    @pl.when(j == 0)
    def _():
        acc_ref[...] = jnp.zeros_like(acc_ref)
        for h in range(PEER_HEADS):
            e2_ref[h] = jnp.exp(s2_ref[h] - sc_ref[h, 1:2, :])
            c1_ref[h] = jnp.exp(s1_ref[h] - sc_ref[h, 2:3, :])

    act_t = lax.dot_general(u_ref[...], xn_ref[...], NT, preferred_element_type=F32)
    for ab in range(n_sub):
        a = j * n_sub + ab
        gate = jnp.zeros((PEER_KEYS, act_t.shape[1]), F32)
        for h in range(PEER_HEADS):
            total = s2_ref[h] + s1_ref[h, pl.ds(a, 1), :]
            gate = gate + jnp.where(total >= sc_ref[h, 0:1, :], e2_ref[h] * c1_ref[h, pl.ds(a, 1), :], 0.0)
        hidden = gate * jax.nn.gelu(act_t[ab * PEER_KEYS:(ab + 1) * PEER_KEYS, :])
        h_ref[ab * PEER_KEYS:(ab + 1) * PEER_KEYS, :] = hidden.astype(BF16)
    acc_ref[...] += jnp.dot(vt_ref[...], h_ref[...], preferred_element_type=F32)

    @pl.when(j == pl.num_programs(1) - 1)
    def _():
        o_ref[...] = x_ref[...] + jnp.transpose(acc_ref[...])


def peer(x, xn, s1, s2, sc, u, vt, tm=512, te=512):
    n, d = x.shape
    return pl.pallas_call(
        _peer_kernel,
        grid=(n // tm, PEER_N // te),
        in_specs=[pl.BlockSpec((tm, d), lambda i, j: (i, 0)),
                  pl.BlockSpec((tm, d), lambda i, j: (i, 0)),
                  pl.BlockSpec((PEER_HEADS, PEER_KEYS, tm), lambda i, j: (0, 0, i)),
                  pl.BlockSpec((PEER_HEADS, PEER_KEYS, tm), lambda i, j: (0, 0, i)),
                  pl.BlockSpec((PEER_HEADS, 8, tm), lambda i, j: (0, 0, i)),
                  pl.BlockSpec((te, d), lambda i, j: (j, 0)),
                  pl.BlockSpec((d, te), lambda i, j: (0, j))],
        out_specs=pl.BlockSpec((tm, d), lambda i, j: (i, 0)),
        out_shape=jax.ShapeDtypeStruct((n, d), F32),
        scratch_shapes=[pltpu.VMEM((d, tm), F32),
                        pltpu.VMEM((PEER_HEADS, PEER_KEYS, tm), F32),
                        pltpu.VMEM((PEER_HEADS, PEER_KEYS, tm), F32),
                        pltpu.VMEM((te, tm), BF16)],
        compiler_params=_params(("parallel", "arbitrary")),
        name="peer",
    )(x, xn, s1, s2, sc, u, vt)


def kernel(x, rel_bias, g_mix, w_in, a_sinks, c_lat_gain, c_w_uk, c_w_uv, w_br_a, w_br_b, w_br_c, w_out,
           g_ffn, peer_w_q, peer_sub_keys, peer_u, peer_v, g_final):
    bsz, seq, d = x.shape
    n = bsz * seq
    depth = w_in.shape[0]
    n_blk = seq // TILE
    off = [0] + [int(o) for o in np.cumsum(IN_SPLITS)]
    bt_a = rel_bias[:, :A_HEADS]
    bt_b = rel_bias[:, A_HEADS:A_HEADS + B_HEADS]
    bt_c = rel_bias[:, A_HEADS + B_HEADS:]
    bias_a = _swa_bias(bt_a)
    win_b = _bias_windows(bt_b, seq)
    win_c = _bias_windows(bt_c, seq)

    xf = x.reshape(n, d)
    for l in range(depth):
        w = w_in[l]
        w_a = w[:, off[0]:off[3]].astype(BF16)
        w_b = w[:, off[3]:off[6]].astype(BF16)
        pad = jnp.zeros((d, 1024 - (off[11] - off[6])), F32)
        w_c = jnp.concatenate([w[:, off[6]:off[7]], w[:, off[8]:off[9]], w[:, off[7]:off[8]],
                               w[:, off[9]:off[11]], pad], axis=1).astype(BF16)
        w_g = w[:, off[11]:off[14]].astype(BF16)

        p_a = norm_matmul(xf, g_mix[l], w_a, BF16).reshape(bsz, seq, -1)
        p_b = norm_matmul(xf, g_mix[l], w_b, BF16).reshape(bsz, seq, 3, B_HEADS, HEAD_DIM)
        p_c = norm_matmul(xf, g_mix[l], w_c, F32).reshape(bsz, seq, -1)

        ya = swa(p_a, bias_a, a_sinks[l]).reshape(n, -1)

        qb = jnp.transpose(p_b[:, :, 0], (0, 2, 1, 3))
        kb = jnp.transpose(p_b[:, :, 1], (0, 2, 1, 3)).reshape(bsz, B_HEADS, n_blk, TILE, HEAD_DIM)
        vbt = jnp.transpose(p_b[:, :, 2].reshape(bsz, n_blk, TILE, B_HEADS, HEAD_DIM), (0, 3, 1, 4, 2))
        yb_t = moba(qb, kb, vbt, win_b)
        yb = jnp.transpose(yb_t, (0, 3, 1, 2)).reshape(n, -1)

        w_uk = jnp.transpose(c_w_uk[l], (1, 0, 2)).astype(BF16)
        w_uv_t = jnp.transpose(c_w_uv[l], (1, 2, 0)).astype(BF16)
        yc_t = dsa(p_c, c_lat_gain[l], w_uk, w_uv_t, win_c)
        yc = jnp.transpose(yc_t, (0, 2, 1)).reshape(n, -1)

        merged = merge(xf, g_mix[l], ya, yb, yc, w_g, w_br_a[l].astype(BF16), w_br_b[l].astype(BF16),
                       w_br_c[l].astype(BF16))
        xf = matmul_res(merged, w_out[l].astype(BF16), xf)

        qp, xn = norm_matmul(xf, g_ffn[l], peer_w_q[l].astype(BF16), BF16, emit_norm=True)
        s1, s2, sc = peer_route(qp, peer_sub_keys[l].astype(BF16))
        xf = peer(xf, xn, s1, s2, sc, peer_u[l].astype(BF16), jnp.transpose(peer_v[l]).astype(BF16))

    return final_norm(xf, g_final).reshape(bsz, seq, d)
```

```python
import functools
import math

import jax
import jax.numpy as jnp
import numpy as np
from jax import lax
from jax.experimental import pallas as pl
from jax.experimental.pallas import tpu as pltpu

F32 = jnp.float32
BF16 = jnp.bfloat16
I32 = jnp.int32

D_MODEL = 2048
HEAD_DIM = 64
A_HEADS, A_KV_HEADS, A_BLOCK, A_WINDOW = 16, 4, 128, 128
B_HEADS, B_BLOCK, B_TOPK = 8, 256, 3
C_HEADS, C_LATENT, IDX_HEADS, IDX_DIM, C_TOPK_MAX = 8, 128, 8, 32, 256
REL_BUCKETS, REL_MAX_DIST = 32, 2048
PEER_HEADS, PEER_KEYS, PEER_QDIM, PEER_TOPK = 8, 128, 256, 16
PEER_N = PEER_KEYS * PEER_KEYS
EPS = 1e-6
NEG = -1e30
INT_MIN = -(2 ** 31)
IN_SPLITS = (A_HEADS * HEAD_DIM, A_KV_HEADS * HEAD_DIM, A_KV_HEADS * HEAD_DIM,
             B_HEADS * HEAD_DIM, B_HEADS * HEAD_DIM, B_HEADS * HEAD_DIM,
             C_HEADS * HEAD_DIM, C_LATENT, IDX_HEADS * IDX_DIM, IDX_DIM, IDX_HEADS,
             D_MODEL, D_MODEL, D_MODEL)
A_COLS = sum(IN_SPLITS[0:3])
B_COLS = sum(IN_SPLITS[3:6])
C_COLS = 1024

TILE = 256
LANES = 128
VMEM_LIMIT = 56 * 1024 * 1024
NT = (((1,), (1,)), ((), ()))


def _params(sem, vmem=VMEM_LIMIT, flags=None):
    return pltpu.CompilerParams(dimension_semantics=sem, vmem_limit_bytes=vmem, flags=flags)


def _rms(x, g):
    return x * lax.rsqrt(jnp.mean(x * x, axis=-1, keepdims=True) + EPS) * g


def _sortable(v):
    bits = pltpu.bitcast(v, I32)
    return jnp.where(bits < 0, bits ^ jnp.int32(0x7FFFFFFF), bits)


def _unsortable(k):
    return pltpu.bitcast(jnp.where(k < 0, k ^ jnp.int32(0x7FFFFFFF), k), F32)


def _in_proj_kernel(x_ref, g_ref, w_ref, oa_ref, ob_ref, oc_ref, h_ref, *, na, nb):
    j = pl.program_id(1)

    @pl.when(j == 0)
    def _():
        h_ref[...] = _rms(x_ref[...], g_ref[...]).astype(BF16)

    y = jnp.dot(h_ref[...], w_ref[...], preferred_element_type=F32)

    @pl.when(j < na)
    def _():
        oa_ref[...] = y.astype(oa_ref.dtype)

    @pl.when((j >= na) & (j < na + nb))
    def _():
        ob_ref[...] = y.astype(ob_ref.dtype)

    @pl.when(j >= na + nb)
    def _():
        oc_ref[...] = y.astype(oc_ref.dtype)


def in_proj(x, g, w, tm=512, tn=512):
    n, d = x.shape
    na, nb, nc = A_COLS // tn, B_COLS // tn, C_COLS // tn
    return pl.pallas_call(
        functools.partial(_in_proj_kernel, na=na, nb=nb),
        grid=(n // tm, na + nb + nc),
        in_specs=[pl.BlockSpec((tm, d), lambda i, j: (i, 0)),
                  pl.BlockSpec((1, d), lambda i, j: (0, 0)),
                  pl.BlockSpec((d, tn), lambda i, j: (0, j))],
        out_specs=[pl.BlockSpec((tm, tn), lambda i, j: (i, jnp.minimum(j, na - 1))),
                   pl.BlockSpec((tm, tn), lambda i, j: (i, jnp.clip(j - na, 0, nb - 1))),
                   pl.BlockSpec((tm, tn), lambda i, j: (i, jnp.maximum(j - na - nb, 0)))],
        out_shape=[jax.ShapeDtypeStruct((n, A_COLS), BF16),
                   jax.ShapeDtypeStruct((n, B_COLS), BF16),
                   jax.ShapeDtypeStruct((n, C_COLS), F32)],
        scratch_shapes=[pltpu.VMEM((tm, d), BF16)],
        compiler_params=_params(("parallel", "arbitrary")),
        name="in_proj",
    )(x, g.reshape(1, d), w)


def _norm_matmul_kernel(x_ref, g_ref, w_ref, o_ref, xn_ref, h_ref):
    @pl.when(pl.program_id(1) == 0)
    def _():
        h = _rms(x_ref[...], g_ref[...]).astype(BF16)
        h_ref[...] = h
        xn_ref[...] = h

    o_ref[...] = jnp.dot(h_ref[...], w_ref[...], preferred_element_type=F32).astype(o_ref.dtype)


def norm_matmul(x, g, w, out_dtype, tm=512, tn=512):
    n, d = x.shape
    ncol = w.shape[1]
    return pl.pallas_call(
        _norm_matmul_kernel,
        grid=(n // tm, ncol // tn),
        in_specs=[pl.BlockSpec((tm, d), lambda i, j: (i, 0)),
                  pl.BlockSpec((1, d), lambda i, j: (0, 0)),
                  pl.BlockSpec((d, tn), lambda i, j: (0, j))],
        out_specs=[pl.BlockSpec((tm, tn), lambda i, j: (i, j)),
                   pl.BlockSpec((tm, d), lambda i, j: (i, 0))],
        out_shape=[jax.ShapeDtypeStruct((n, ncol), out_dtype),
                   jax.ShapeDtypeStruct((n, d), BF16)],
        scratch_shapes=[pltpu.VMEM((tm, d), BF16)],
        compiler_params=_params(("parallel", "arbitrary")),
        name="norm_matmul",
    )(x, g.reshape(1, d), w)


def _matmul_res_kernel(a_ref, w_ref, r_ref, o_ref):
    o_ref[...] = r_ref[...] + jnp.dot(a_ref[...], w_ref[...], preferred_element_type=F32)


def matmul_res(a, w, res, tm=512, tn=512):
    n, k = a.shape
    ncol = w.shape[1]
    return pl.pallas_call(
        _matmul_res_kernel,
        grid=(n // tm, ncol // tn),
        in_specs=[pl.BlockSpec((tm, k), lambda i, j: (i, 0)),
                  pl.BlockSpec((k, tn), lambda i, j: (0, j)),
                  pl.BlockSpec((tm, tn), lambda i, j: (i, j))],
        out_specs=pl.BlockSpec((tm, tn), lambda i, j: (i, j)),
        out_shape=jax.ShapeDtypeStruct((n, ncol), F32),
        compiler_params=_params(("parallel", "arbitrary")),
        name="matmul_res",
    )(a, w, res)


def _merge_kernel(x_ref, g_ref, ya_ref, yb_ref, yc_ref, wga_ref, wgb_ref, wgc_ref,
                  wa_ref, wb_ref, wc_ref, o_ref, h_ref):
    @pl.when(pl.program_id(1) == 0)
    def _():
        h_ref[...] = _rms(x_ref[...], g_ref[...]).astype(BF16)

    h = h_ref[...]

    def branch(y_ref, wg_ref, w_ref):
        gate = jnp.dot(h, wg_ref[...], preferred_element_type=F32)
        proj = jnp.dot(y_ref[...], w_ref[...], preferred_element_type=F32)
        return jax.nn.sigmoid(gate) * proj

    merged = branch(ya_ref, wga_ref, wa_ref) + branch(yb_ref, wgb_ref, wb_ref) + branch(yc_ref, wgc_ref, wc_ref)
    o_ref[...] = merged.astype(o_ref.dtype)


def merge(x, g, ya, yb, yc, w_gate, wa, wb, wc, tm=512, tn=512):
    n, d = x.shape
    nj = d // tn
    return pl.pallas_call(
        _merge_kernel,
        grid=(n // tm, nj),
        in_specs=[pl.BlockSpec((tm, d), lambda i, j: (i, 0)),
                  pl.BlockSpec((1, d), lambda i, j: (0, 0)),
                  pl.BlockSpec((tm, ya.shape[1]), lambda i, j: (i, 0)),
                  pl.BlockSpec((tm, yb.shape[1]), lambda i, j: (i, 0)),
                  pl.BlockSpec((tm, yc.shape[1]), lambda i, j: (i, 0)),
                  pl.BlockSpec((d, tn), lambda i, j: (0, j)),
                  pl.BlockSpec((d, tn), lambda i, j: (0, j + nj)),
                  pl.BlockSpec((d, tn), lambda i, j: (0, j + 2 * nj)),
                  pl.BlockSpec((wa.shape[0], tn), lambda i, j: (0, j)),
                  pl.BlockSpec((wb.shape[0], tn), lambda i, j: (0, j)),
                  pl.BlockSpec((wc.shape[0], tn), lambda i, j: (0, j))],
        out_specs=pl.BlockSpec((tm, tn), lambda i, j: (i, j)),
        out_shape=jax.ShapeDtypeStruct((n, d), BF16),
        scratch_shapes=[pltpu.VMEM((tm, d), BF16)],
        compiler_params=_params(("parallel", "arbitrary")),
        name="merge",
    )(x, g.reshape(1, d), ya, yb, yc, w_gate, w_gate, w_gate, wa, wb, wc)


def _final_norm_kernel(x_ref, g_ref, o_ref):
    o_ref[...] = _rms(x_ref[...], g_ref[...])


def final_norm(x, g, tm=512):
    n, d = x.shape
    return pl.pallas_call(
        _final_norm_kernel,
        grid=(n // tm,),
        in_specs=[pl.BlockSpec((tm, d), lambda i: (i, 0)), pl.BlockSpec((1, d), lambda i: (0, 0))],
        out_specs=pl.BlockSpec((tm, d), lambda i: (i, 0)),
        out_shape=jax.ShapeDtypeStruct((n, d), F32),
        compiler_params=_params(("parallel",)),
        name="final_norm",
    )(x, g.reshape(1, d))


def _rel_bucket(dist):
    d = jnp.maximum(dist, 0)
    max_exact = REL_BUCKETS // 2
    df = jnp.maximum(d, max_exact).astype(F32)
    large = max_exact + (jnp.log(df / max_exact) / math.log(REL_MAX_DIST / max_exact)
                         * (REL_BUCKETS - max_exact)).astype(I32)
    large = jnp.minimum(large, REL_BUCKETS - 1)
    return jnp.where(d < max_exact, d, large)


def _bias_windows(bt, seq):
    dist = (jnp.arange(seq // TILE)[:, None] * TILE - (TILE - 1) + jnp.arange(2 * TILE)[None, :])
    w = bt[_rel_bucket(dist)].astype(F32)
    w = jnp.where((dist >= 0)[..., None], w, 0.0)
    return jnp.transpose(w, (2, 0, 1))


def _gen_bias_tiles(win_ref, tiles_ref):
    n_heads, n_rel = win_ref.shape[0], win_ref.shape[1]

    def body(t, carry):
        h = t // n_rel
        r = t % n_rel
        row = win_ref[h, pl.ds(r, 1), :]
        x = jnp.broadcast_to(row, (TILE, 2 * TILE))
        tiles_ref[h, r] = pltpu.roll(x, TILE + 1, 1, stride=1, stride_axis=0)[:, :TILE]
        return carry

    lax.fori_loop(0, n_heads * n_rel, body, 0)


def _swa_kernel(q_ref, kp_ref, kc_ref, vp_ref, vc_ref, bias_ref, sink_ref, o_ref):
    i = pl.program_id(1)
    which = jnp.minimum(i, 1)
    group = A_HEADS // A_KV_HEADS
    outs = []
    for kvh in range(A_KV_HEADS):
        lo, hi = kvh * HEAD_DIM, (kvh + 1) * HEAD_DIM
        kk = jnp.concatenate([kp_ref[0, :, lo:hi], kc_ref[0, :, lo:hi]], axis=0)
        vv = jnp.concatenate([vp_ref[0, :, lo:hi], vc_ref[0, :, lo:hi]], axis=0)
        for g in range(group):
            h = kvh * group + g
            q = q_ref[0, :, h * HEAD_DIM:(h + 1) * HEAD_DIM]
            s = lax.dot_general(q, kk, NT, preferred_element_type=F32) * (HEAD_DIM ** -0.5)
            s = s + bias_ref[which, h]
            sink = sink_ref[h]
            m = jnp.maximum(jnp.max(s, axis=-1, keepdims=True), sink)
            p = jnp.exp(s - m)
            denom = jnp.sum(p, axis=-1, keepdims=True) + jnp.exp(sink - m)
            o = jnp.dot(p.astype(BF16), vv, preferred_element_type=F32) / denom
            outs.append(o.astype(BF16))
    o_ref[0] = jnp.concatenate(outs, axis=-1)


def _swa_bias(bt_a):
    qpos = jnp.arange(A_BLOCK)[:, None] + A_BLOCK
    kpos = jnp.arange(2 * A_BLOCK)[None, :]
    dist = qpos - kpos
    band = (dist >= 0) & (dist < A_WINDOW)
    bias = jnp.transpose(bt_a[_rel_bucket(dist)].astype(F32), (2, 0, 1))
    general = jnp.where(band[None], bias, NEG)
    first = jnp.where((band & (kpos >= A_BLOCK))[None], bias, NEG)
    return jnp.stack([first, general])


def swa(p_a, bias, sinks):
    bsz, seq, _ = p_a.shape
    qw, kw = A_HEADS * HEAD_DIM, A_KV_HEADS * HEAD_DIM
    kblk, vblk = qw // kw, qw // kw + 1
    return pl.pallas_call(
        _swa_kernel,
        grid=(bsz, seq // A_BLOCK),
        in_specs=[pl.BlockSpec((1, A_BLOCK, qw), lambda b, i: (b, i, 0)),
                  pl.BlockSpec((1, A_BLOCK, kw), lambda b, i: (b, jnp.maximum(i - 1, 0), kblk)),
                  pl.BlockSpec((1, A_BLOCK, kw), lambda b, i: (b, i, kblk)),
                  pl.BlockSpec((1, A_BLOCK, kw), lambda b, i: (b, jnp.maximum(i - 1, 0), vblk)),
                  pl.BlockSpec((1, A_BLOCK, kw), lambda b, i: (b, i, vblk)),
                  pl.BlockSpec(bias.shape, lambda b, i: (0, 0, 0, 0)),
                  pl.BlockSpec(memory_space=pltpu.SMEM)],
        out_specs=pl.BlockSpec((1, A_BLOCK, qw), lambda b, i: (b, i, 0)),
        out_shape=jax.ShapeDtypeStruct((bsz, seq, qw), BF16),
        compiler_params=_params(("parallel", "arbitrary")),
        name="swa",
    )(p_a, p_a, p_a, p_a, p_a, bias, sinks)


def _moba_kernel(q_ref, k_ref, vt_ref, win_ref, o_ref, tiles_ref, kmean_ref, sel_ref, m_ref, l_ref, acc_ref):
    b, i = pl.program_id(0), pl.program_id(1)
    n_heads, n_blk = kmean_ref.shape[0], kmean_ref.shape[1]

    @pl.when((b == 0) & (i == 0))
    def _():
        _gen_bias_tiles(win_ref, tiles_ref)

    @pl.when(i == 0)
    def _():
        for h in range(n_heads):
            kmean_ref[h] = jnp.mean(k_ref[0, h].astype(F32), axis=1)

    row = lax.broadcasted_iota(I32, (n_blk, TILE), 0)
    for h in range(n_heads):
        q = q_ref[0, h]
        kmean = kmean_ref[h]
        kmean_hi = kmean.astype(BF16)
        kmean_lo = (kmean - kmean_hi.astype(F32)).astype(BF16)
        gate = (lax.dot_general(kmean_hi, q, NT, preferred_element_type=F32)
                + lax.dot_general(kmean_lo, q, NT, preferred_element_type=F32))
        gate = jnp.where(row < i, gate, NEG)
        sel = jnp.zeros(gate.shape, F32)
        for _ in range(B_TOPK):
            best = jnp.max(gate, axis=0, keepdims=True)
            first = jnp.min(jnp.where(gate == best, row, n_blk), axis=0, keepdims=True)
            pick = row == first
            sel = jnp.where(pick, 1.0, sel)
            gate = jnp.where(pick, -jnp.inf, gate)
        sel_ref[h] = jnp.where(row < i, sel, 0.0)

    m_ref[...] = jnp.full(m_ref.shape, NEG, F32)
    l_ref[...] = jnp.zeros_like(l_ref)
    acc_ref[...] = jnp.zeros_like(acc_ref)
    key_pos = lax.broadcasted_iota(I32, (TILE, TILE), 0)
    qry_pos = lax.broadcasted_iota(I32, (TILE, TILE), 1)
    causal = jnp.where(qry_pos >= key_pos, 1.0, 0.0)

    def body(j, carry):
        for h in range(n_heads):
            s = lax.dot_general(k_ref[0, h, j], q_ref[0, h], NT, preferred_element_type=F32) * (HEAD_DIM ** -0.5)
            s = s + tiles_ref[h, i - j]
            chosen = jnp.broadcast_to(sel_ref[h, pl.ds(j, 1), :], (TILE, TILE))
            ok = jnp.where(j == i, causal, chosen) > 0.0
            s = jnp.where(ok, s, NEG)
            m_old = m_ref[h]
            m_new = jnp.maximum(m_old, jnp.max(s, axis=0, keepdims=True))
            alpha = jnp.exp(m_old - m_new)
            p = jnp.where(ok, jnp.exp(s - m_new), 0.0)
            m_ref[h] = m_new
            l_ref[h] = alpha * l_ref[h] + jnp.sum(p, axis=0, keepdims=True)
            acc_ref[h] = alpha * acc_ref[h] + jnp.dot(vt_ref[0, h, j], p.astype(BF16), preferred_element_type=F32)
        return carry

    lax.fori_loop(0, i + 1, body, 0)
    for h in range(n_heads):
        o_ref[0, h] = (acc_ref[h] / l_ref[h]).astype(o_ref.dtype)


def moba(q, k, vt, windows):
    bsz, n_heads, seq, d = q.shape
    n_blk = seq // TILE
    return pl.pallas_call(
        _moba_kernel,
        grid=(bsz, n_blk),
        in_specs=[pl.BlockSpec((1, n_heads, TILE, d), lambda b, i: (b, 0, i, 0)),
                  pl.BlockSpec((1, n_heads, n_blk, TILE, d), lambda b, i: (b, 0, 0, 0, 0)),
                  pl.BlockSpec((1, n_heads, n_blk, d, TILE), lambda b, i: (b, 0, 0, 0, 0)),
                  pl.BlockSpec(windows.shape, lambda b, i: (0, 0, 0))],
        out_specs=pl.BlockSpec((1, n_heads, d, TILE), lambda b, i: (b, 0, 0, i)),
        out_shape=jax.ShapeDtypeStruct((bsz, n_heads, d, seq), BF16),
        scratch_shapes=[pltpu.VMEM((n_heads, n_blk, TILE, TILE), F32),
                        pltpu.VMEM((n_heads, n_blk, d), F32),
                        pltpu.VMEM((n_heads, n_blk, TILE), F32),
                        pltpu.VMEM((n_heads, 1, TILE), F32),
                        pltpu.VMEM((n_heads, 1, TILE), F32),
                        pltpu.VMEM((n_heads, d, TILE), F32)],
        compiler_params=_params(("arbitrary", "arbitrary")),
        name="moba",
    )(q, k, vt, windows)


def _dsa_kernel(qc_ref, iq_ref, mq_ref, clat_ref, mk_ref, gain_ref, wuk_ref, wuvt_ref, win_ref, o_ref,
                tiles_ref, cn_ref, cnt_ref, ik_ref, keys_ref, mask_ref, qabs_ref, m_ref, l_ref, acc_ref, *, top_k):
    b, c = pl.program_id(0), pl.program_id(1)
    n_blk = cn_ref.shape[0]
    n_kb = c + 1

    @pl.when((b == 0) & (c == 0))
    def _():
        _gen_bias_tiles(win_ref, tiles_ref)

    @pl.when(c == 0)
    def _():
        lat = _rms(clat_ref[0], gain_ref[...])
        for kb in range(n_blk):
            blk = lat[kb * TILE:(kb + 1) * TILE]
            cn_ref[kb] = blk.astype(BF16)
            cnt_ref[kb] = jnp.transpose(blk).astype(BF16)
            ik_ref[kb] = mk_ref[0, kb * TILE:(kb + 1) * TILE, :IDX_DIM].astype(BF16)

    key_in = lax.broadcasted_iota(I32, (TILE, TILE), 0)
    qry_in = lax.broadcasted_iota(I32, (TILE, TILE), 1)
    diag_causal = key_in <= qry_in

    iq = iq_ref[0].astype(BF16)
    iw_t = jnp.transpose(mq_ref[0])[IDX_DIM:IDX_DIM + IDX_HEADS, :]

    def score_block(kb, carry):
        score = jnp.zeros((TILE, TILE), F32)
        for hh in range(IDX_HEADS):
            logits = lax.dot_general(ik_ref[kb], iq[:, hh * IDX_DIM:(hh + 1) * IDX_DIM], NT,
                                     preferred_element_type=F32)
            score = score + jnp.maximum(logits, 0.0) * iw_t[hh:hh + 1, :]
        score = score * ((IDX_DIM ** -0.5) * (IDX_HEADS ** -0.5))
        score = jnp.where(score == 0.0, 0.0, score)
        key = _sortable(score)
        keys_ref[kb] = jnp.where(kb < c, key, jnp.where(diag_causal, key, INT_MIN))
        return carry

    lax.fori_loop(0, n_kb, score_block, 0)

    def count(pred):
        def blk(kb, cnt):
            return cnt + jnp.sum(pred(keys_ref[kb], kb), axis=0, keepdims=True)
        return lax.fori_loop(0, n_kb, blk, jnp.zeros((1, TILE), F32))

    def value_bit(t, tau):
        cand = tau + lax.shift_left(jnp.int32(1), 31 - t)
        cnt = count(lambda key, kb: jnp.where(key >= cand, 1.0, 0.0))
        return jnp.where(cnt >= top_k, cand, tau)

    tau = lax.fori_loop(0, 32, value_bit, jnp.full((1, TILE), INT_MIN, I32))
    need = top_k - count(lambda key, kb: jnp.where(key > tau, 1.0, 0.0))

    def index_bit(t, bound):
        cand = bound + lax.shift_left(jnp.int32(1), 11 - t)
        cnt = count(lambda key, kb: jnp.where(key == tau, jnp.where(kb * TILE + key_in < cand, 1.0, 0.0), 0.0))
        return jnp.where(cnt <= need, cand, bound)

    bound = lax.fori_loop(0, 12, index_bit, jnp.zeros((1, TILE), I32))

    def mask_block(kb, carry):
        key = keys_ref[kb]
        tied = jnp.where(key == tau, jnp.where(kb * TILE + key_in < bound, 1.0, 0.0), 0.0)
        chosen = jnp.where(key > tau, 1.0, tied)
        mask_ref[kb] = jnp.where(kb < c, chosen, jnp.where(diag_causal, chosen, 0.0))
        return carry

    lax.fori_loop(0, n_kb, mask_block, 0)

    qc = qc_ref[0].astype(BF16)
    for h in range(C_HEADS):
        qabs_ref[h] = lax.dot_general(wuk_ref[h], qc[:, h * HEAD_DIM:(h + 1) * HEAD_DIM], NT,
                                      preferred_element_type=F32).astype(BF16)
    m_ref[...] = jnp.full(m_ref.shape, NEG, F32)
    l_ref[...] = jnp.zeros_like(l_ref)
    acc_ref[...] = jnp.zeros_like(acc_ref)

    def attend_block(kb, carry):
        ok = mask_ref[kb] > 0.0
        for h in range(C_HEADS):
            s = jnp.dot(cn_ref[kb], qabs_ref[h], preferred_element_type=F32) * (HEAD_DIM ** -0.5)
            s = jnp.where(ok, s + tiles_ref[h, c - kb], NEG)
            m_old = m_ref[h]
            m_new = jnp.maximum(m_old, jnp.max(s, axis=0, keepdims=True))
            alpha = jnp.exp(m_old - m_new)
            p = jnp.where(ok, jnp.exp(s - m_new), 0.0)
            m_ref[h] = m_new
            l_ref[h] = alpha * l_ref[h] + jnp.sum(p, axis=0, keepdims=True)
            acc_ref[h] = alpha * acc_ref[h] + jnp.dot(cnt_ref[kb], p.astype(BF16), preferred_element_type=F32)
        return carry

    lax.fori_loop(0, n_kb, attend_block, 0)
    for h in range(C_HEADS):
        oc_t = (acc_ref[h] / l_ref[h]).astype(BF16)
        out_t = jnp.dot(wuvt_ref[h], oc_t, preferred_element_type=F32)
        o_ref[0, h * HEAD_DIM:(h + 1) * HEAD_DIM, :] = out_t.astype(o_ref.dtype)


def dsa(p_c, gain, w_uk, w_uv_t, windows):
    bsz, seq, _ = p_c.shape
    top_k = min(C_TOPK_MAX, seq // 4)
    qw = C_HEADS * HEAD_DIM
    n_blk = seq // TILE
    return pl.pallas_call(
        functools.partial(_dsa_kernel, top_k=top_k),
        grid=(bsz, n_blk),
        in_specs=[pl.BlockSpec((1, TILE, qw), lambda b, c: (b, c, 0)),
                  pl.BlockSpec((1, TILE, 256), lambda b, c: (b, c, 2)),
                  pl.BlockSpec((1, TILE, 128), lambda b, c: (b, c, 7)),
                  pl.BlockSpec((1, seq, 128), lambda b, c: (b, 0, 6)),
                  pl.BlockSpec((1, seq, 128), lambda b, c: (b, 0, 7)),
                  pl.BlockSpec((1, C_LATENT), lambda b, c: (0, 0)),
                  pl.BlockSpec(w_uk.shape, lambda b, c: (0, 0, 0)),
                  pl.BlockSpec(w_uv_t.shape, lambda b, c: (0, 0, 0)),
                  pl.BlockSpec(windows.shape, lambda b, c: (0, 0, 0))],
        out_specs=pl.BlockSpec((1, qw, TILE), lambda b, c: (b, 0, c)),
        out_shape=jax.ShapeDtypeStruct((bsz, qw, seq), BF16),
        scratch_shapes=[pltpu.VMEM((C_HEADS, n_blk, TILE, TILE), F32),
                        pltpu.VMEM((n_blk, TILE, C_LATENT), BF16),
                        pltpu.VMEM((n_blk, C_LATENT, TILE), BF16),
                        pltpu.VMEM((n_blk, TILE, IDX_DIM), BF16),
                        pltpu.VMEM((n_blk, TILE, TILE), I32),
                        pltpu.VMEM((n_blk, TILE, TILE), F32),
                        pltpu.VMEM((C_HEADS, C_LATENT, TILE), BF16),
                        pltpu.VMEM((C_HEADS, 1, TILE), F32),
                        pltpu.VMEM((C_HEADS, 1, TILE), F32),
                        pltpu.VMEM((C_HEADS, C_LATENT, TILE), F32)],
        compiler_params=_params(("arbitrary", "arbitrary")),
        name="dsa",
    )(p_c, p_c, p_c, p_c, p_c, gain.reshape(1, C_LATENT), w_uk, w_uv_t, windows)


ROUTE_TM = 256
ROUTE_ROWS = 72


def _top_values(s, count):
    vals, mult = [], []
    for _ in range(count):
        best = jnp.max(s, axis=0, keepdims=True)
        hit = s == best
        vals.append(best)
        mult.append(jnp.sum(jnp.where(hit, 1.0, 0.0), axis=0, keepdims=True))
        s = jnp.where(hit, -jnp.inf, s)
    return jnp.concatenate(vals, axis=0), jnp.concatenate(mult, axis=0)


def _route_kernel(q_ref, k_ref, s1_ref, s2_ref, sc_ref, keys_ref, wts_ref):
    q = q_ref[...]
    tm = q.shape[0]
    half = PEER_QDIM // 2
    s1 = lax.dot_general(k_ref[0, 0], q[:, :half], NT, preferred_element_type=F32)
    s2 = lax.dot_general(k_ref[0, 1], q[:, half:], NT, preferred_element_type=F32)
    s1_ref[0] = s1
    s2_ref[0] = s2
    t1, w1 = _top_values(s1, PEER_TOPK)
    t2, w2 = _top_values(s2, PEER_TOPK)

    r16 = lax.broadcasted_iota(I32, (16, tm), 0)
    r8 = lax.broadcasted_iota(I32, (8, tm), 0)
    pieces = [
        (t1[0:1] + t2, w1[0:1] * w2),
        (t1[1:2] + t2[:8], w1[1:2] * w2[:8]),
        (t1[2:3] + t2[:8], jnp.where(r8 < 5, w1[2:3] * w2[:8], 0.0)),
        (t1[3:4] + t2[:8], jnp.where(r8 < 4, w1[3:4] * w2[:8], 0.0)),
        (t2[0:1] + t1, jnp.where(r16 >= 4, w2[0:1] * w1, 0.0)),
        (t2[1:2] + t1[:8], jnp.where(r8 >= 4, w2[1:2] * w1[:8], 0.0)),
        (t2[2:3] + t1[:8], jnp.where(r8 == 4, w2[2:3] * w1[:8], 0.0)),
    ]
    cand = jnp.concatenate([p[0] for p in pieces], axis=0)
    wts = jnp.concatenate([p[1] for p in pieces], axis=0)
    keys_ref[...] = _sortable(cand)
    wts_ref[...] = wts

    def value_bit(t, tau):
        c = tau + lax.shift_left(jnp.int32(1), 31 - t)
        cnt = jnp.sum(jnp.where(keys_ref[...] >= c, wts_ref[...], 0.0), axis=0, keepdims=True)
        return jnp.where(cnt >= PEER_TOPK, c, tau)

    tau_key = lax.fori_loop(0, 32, value_bit, jnp.full((1, tm), INT_MIN, I32))
    top = t1[0:1] + t2[0:1]
    z = jnp.sum(jnp.where(keys_ref[...] >= tau_key, wts * jnp.exp(cand - top), 0.0), axis=0, keepdims=True)
    rows = [_unsortable(tau_key), t2[0:1], t1[0:1] + jnp.log(z)]
    sc_ref[0] = jnp.concatenate(rows + [jnp.zeros_like(z)] * (8 - len(rows)), axis=0)


def peer_route(qp, sub_keys):
    n = qp.shape[0]
    tm = ROUTE_TM
    half = PEER_QDIM // 2
    return pl.pallas_call(
        _route_kernel,
        grid=(n // tm, PEER_HEADS),
        in_specs=[pl.BlockSpec((tm, PEER_QDIM), lambda i, h: (i, h)),
                  pl.BlockSpec((1, 2, PEER_KEYS, half), lambda i, h: (h, 0, 0, 0))],
        out_specs=[pl.BlockSpec((1, PEER_KEYS, tm), lambda i, h: (h, 0, i)),
                   pl.BlockSpec((1, PEER_KEYS, tm), lambda i, h: (h, 0, i)),
                   pl.BlockSpec((1, 8, tm), lambda i, h: (h, 0, i))],
        out_shape=[jax.ShapeDtypeStruct((PEER_HEADS, PEER_KEYS, n), F32),
                   jax.ShapeDtypeStruct((PEER_HEADS, PEER_KEYS, n), F32),
                   jax.ShapeDtypeStruct((PEER_HEADS, 8, n), F32)],
        scratch_shapes=[pltpu.VMEM((ROUTE_ROWS, tm), I32),
                        pltpu.VMEM((ROUTE_ROWS, tm), F32)],
        compiler_params=_params(("parallel", "arbitrary")),
        name="peer_route",
    )(qp, sub_keys)


def _peer_kernel(x_ref, xn_ref, s1_ref, s2_ref, sc_ref, u_ref, vt_ref, o_ref, acc_ref, e2_ref, xnt_ref, hid_ref):
    j = pl.program_id(1)
    n_sub = u_ref.shape[0] // PEER_KEYS

    @pl.when(j == 0)
    def _():
        acc_ref[...] = jnp.zeros_like(acc_ref)
        xnt_ref[...] = jnp.transpose(xn_ref[...].astype(F32)).astype(BF16)
        for h in range(PEER_HEADS):
            e2_ref[h] = jnp.exp(s2_ref[h] - sc_ref[h, 1:2, :])

    act = jnp.dot(u_ref[...], xnt_ref[...], preferred_element_type=F32)
    for ab in range(n_sub):
        rows = slice(ab * PEER_KEYS, (ab + 1) * PEER_KEYS)
        s1_rows = [s1_ref[h, 0, ab:ab + 1, :] for h in range(PEER_HEADS)]
        c1_rows = [jnp.exp(s1_rows[h] - sc_ref[h, 2:3, :]) for h in range(PEER_HEADS)]
        for tc in range(act.shape[1] // LANES):
            cols = slice(tc * LANES, (tc + 1) * LANES)
            gate = jnp.zeros((PEER_KEYS, LANES), F32)
            for h in range(PEER_HEADS):
                total = s2_ref[h, :, cols] + s1_rows[h][:, cols]
                weight = e2_ref[h, :, cols] * c1_rows[h][:, cols]
                gate = gate + jnp.where(total >= sc_ref[h, 0:1, cols], weight, 0.0)
            hid_ref[rows, cols] = (gate * jax.nn.gelu(act[rows, cols])).astype(BF16)
    acc_ref[...] += jnp.dot(vt_ref[...], hid_ref[...], preferred_element_type=F32)

    @pl.when(j == pl.num_programs(1) - 1)
    def _():
        o_ref[...] = x_ref[...] + jnp.transpose(acc_ref[...])


def peer(x, xn, s1, s2, sc, u, vt, tm=512, te=512):
    n, d = x.shape
    n_chunks = PEER_N // te
    n_sub = te // PEER_KEYS
    s1_chunks = s1.reshape(PEER_HEADS, n_chunks, n_sub, n)
    return pl.pallas_call(
        _peer_kernel,
        grid=(n // tm, n_chunks),
        in_specs=[pl.BlockSpec((tm, d), lambda i, j: (i, 0)),
                  pl.BlockSpec((tm, d), lambda i, j: (i, 0)),
                  pl.BlockSpec((PEER_HEADS, 1, n_sub, tm), lambda i, j: (0, j, 0, i)),
                  pl.BlockSpec((PEER_HEADS, PEER_KEYS, tm), lambda i, j: (0, 0, i)),
                  pl.BlockSpec((PEER_HEADS, 8, tm), lambda i, j: (0, 0, i)),
                  pl.BlockSpec((te, d), lambda i, j: (j, 0)),
                  pl.BlockSpec((d, te), lambda i, j: (0, j))],
        out_specs=pl.BlockSpec((tm, d), lambda i, j: (i, 0)),
        out_shape=jax.ShapeDtypeStruct((n, d), F32),
        scratch_shapes=[pltpu.VMEM((d, tm), F32),
                        pltpu.VMEM((PEER_HEADS, PEER_KEYS, tm), F32),
                        pltpu.VMEM((d, tm), BF16),
                        pltpu.VMEM((te, tm), BF16)],
        compiler_params=_params(("parallel", "arbitrary")),
        name="peer",
    )(x, xn, s1_chunks, s2, sc, u, vt)


def kernel(x, rel_bias, g_mix, w_in, a_sinks, c_lat_gain, c_w_uk, c_w_uv, w_br_a, w_br_b, w_br_c, w_out,
           g_ffn, peer_w_q, peer_sub_keys, peer_u, peer_v, g_final):
    bsz, seq, d = x.shape
    n = bsz * seq
    depth = w_in.shape[0]
    n_blk = seq // TILE
    off = [0] + [int(o) for o in np.cumsum(IN_SPLITS)]
    bt_a = rel_bias[:, :A_HEADS]
    bt_b = rel_bias[:, A_HEADS:A_HEADS + B_HEADS]
    bt_c = rel_bias[:, A_HEADS + B_HEADS:]
    bias_a = _swa_bias(bt_a)
    win_b = _bias_windows(bt_b, seq)
    win_c = _bias_windows(bt_c, seq)

    xf = x.reshape(n, d)
    for l in range(depth):
        w = w_in[l]
        pad = jnp.zeros((d, C_COLS - (off[11] - off[6])), F32)
        w_abc = jnp.concatenate([w[:, off[0]:off[6]],
                                 w[:, off[6]:off[7]], w[:, off[8]:off[9]], w[:, off[7]:off[8]],
                                 w[:, off[9]:off[11]], pad], axis=1).astype(BF16)
        w_g = w[:, off[11]:off[14]].astype(BF16)

        p_a, p_b, p_c = in_proj(xf, g_mix[l], w_abc)
        p_a = p_a.reshape(bsz, seq, -1)
        p_b = p_b.reshape(bsz, seq, 3, B_HEADS, HEAD_DIM)
        p_c = p_c.reshape(bsz, seq, -1)

        ya = swa(p_a, bias_a, a_sinks[l]).reshape(n, -1)

        qb = jnp.transpose(p_b[:, :, 0], (0, 2, 1, 3))
        kb = jnp.transpose(p_b[:, :, 1], (0, 2, 1, 3)).reshape(bsz, B_HEADS, n_blk, TILE, HEAD_DIM)
        vbt = jnp.transpose(p_b[:, :, 2].reshape(bsz, n_blk, TILE, B_HEADS, HEAD_DIM), (0, 3, 1, 4, 2))
        yb_t = moba(qb, kb, vbt, win_b)
        yb = jnp.transpose(yb_t, (0, 3, 1, 2)).reshape(n, -1)

        w_uk = jnp.transpose(c_w_uk[l], (1, 0, 2)).astype(BF16)
        w_uv_t = jnp.transpose(c_w_uv[l], (1, 2, 0)).astype(BF16)
        yc_t = dsa(p_c, c_lat_gain[l], w_uk, w_uv_t, win_c)
        yc = jnp.transpose(yc_t, (0, 2, 1)).reshape(n, -1)

        merged = merge(xf, g_mix[l], ya, yb, yc, w_g, w_br_a[l].astype(BF16), w_br_b[l].astype(BF16),
                       w_br_c[l].astype(BF16))
        xf = matmul_res(merged, w_out[l].astype(BF16), xf)

        qp, xn = norm_matmul(xf, g_ffn[l], peer_w_q[l].astype(BF16), BF16)
        s1, s2, sc = peer_route(qp, peer_sub_keys[l].astype(BF16))
        xf = peer(xf, xn, s1, s2, sc, peer_u[l].astype(BF16), jnp.transpose(peer_v[l]).astype(BF16))

    return final_norm(xf, g_final).reshape(bsz, seq, d)
```

```python
import functools
import math

import jax
import jax.numpy as jnp
import numpy as np
from jax import lax
from jax.experimental import pallas as pl
from jax.experimental.pallas import tpu as pltpu

F32 = jnp.float32
BF16 = jnp.bfloat16
I32 = jnp.int32

D_MODEL = 2048
HEAD_DIM = 64
A_HEADS, A_KV_HEADS, A_BLOCK, A_WINDOW = 16, 4, 128, 128
B_HEADS, B_BLOCK, B_TOPK = 8, 256, 3
C_HEADS, C_LATENT, IDX_HEADS, IDX_DIM, C_TOPK_MAX = 8, 128, 8, 32, 256
REL_BUCKETS, REL_MAX_DIST = 32, 2048
PEER_HEADS, PEER_KEYS, PEER_QDIM, PEER_TOPK = 8, 128, 256, 16
PEER_N = PEER_KEYS * PEER_KEYS
EPS = 1e-6
NEG = -1e30
INT_MIN = -(2 ** 31)
IN_SPLITS = (A_HEADS * HEAD_DIM, A_KV_HEADS * HEAD_DIM, A_KV_HEADS * HEAD_DIM,
             B_HEADS * HEAD_DIM, B_HEADS * HEAD_DIM, B_HEADS * HEAD_DIM,
             C_HEADS * HEAD_DIM, C_LATENT, IDX_HEADS * IDX_DIM, IDX_DIM, IDX_HEADS,
             D_MODEL, D_MODEL, D_MODEL)
A_COLS = sum(IN_SPLITS[0:3])
B_COLS = sum(IN_SPLITS[3:6])
C_COLS = 1024

TILE = 256
LANES = 128
VMEM_LIMIT = 56 * 1024 * 1024
NT = (((1,), (1,)), ((), ()))


def _params(sem, vmem=VMEM_LIMIT, flags=None):
    return pltpu.CompilerParams(dimension_semantics=sem, vmem_limit_bytes=vmem, flags=flags)


def _rms(x, g):
    return x * lax.rsqrt(jnp.mean(x * x, axis=-1, keepdims=True) + EPS) * g


def _sortable(v):
    bits = pltpu.bitcast(v, I32)
    return jnp.where(bits < 0, bits ^ jnp.int32(0x7FFFFFFF), bits)


def _unsortable(k):
    return pltpu.bitcast(jnp.where(k < 0, k ^ jnp.int32(0x7FFFFFFF), k), F32)


def _in_proj_kernel(x_ref, g_ref, w_ref, oa_ref, ob_ref, oc_ref, h_ref, *, na, nb):
    j = pl.program_id(1)

    @pl.when(j == 0)
    def _():
        h_ref[...] = _rms(x_ref[...], g_ref[...]).astype(BF16)

    y = jnp.dot(h_ref[...], w_ref[...], preferred_element_type=F32)

    @pl.when(j < na)
    def _():
        oa_ref[...] = y.astype(oa_ref.dtype)

    @pl.when((j >= na) & (j < na + nb))
    def _():
        ob_ref[...] = y.astype(ob_ref.dtype)

    @pl.when(j >= na + nb)
    def _():
        oc_ref[...] = y.astype(oc_ref.dtype)


def in_proj(x, g, w, tm=512, tn=512):
    n, d = x.shape
    na, nb, nc = A_COLS // tn, B_COLS // tn, C_COLS // tn
    return pl.pallas_call(
        functools.partial(_in_proj_kernel, na=na, nb=nb),
        grid=(n // tm, na + nb + nc),
        in_specs=[pl.BlockSpec((tm, d), lambda i, j: (i, 0)),
                  pl.BlockSpec((1, d), lambda i, j: (0, 0)),
                  pl.BlockSpec((d, tn), lambda i, j: (0, j))],
        out_specs=[pl.BlockSpec((tm, tn), lambda i, j: (i, jnp.minimum(j, na - 1))),
                   pl.BlockSpec((tm, tn), lambda i, j: (i, jnp.clip(j - na, 0, nb - 1))),
                   pl.BlockSpec((tm, tn), lambda i, j: (i, jnp.maximum(j - na - nb, 0)))],
        out_shape=[jax.ShapeDtypeStruct((n, A_COLS), BF16),
                   jax.ShapeDtypeStruct((n, B_COLS), BF16),
                   jax.ShapeDtypeStruct((n, C_COLS), F32)],
        scratch_shapes=[pltpu.VMEM((tm, d), BF16)],
        compiler_params=_params(("parallel", "arbitrary")),
        name="in_proj",
    )(x, g.reshape(1, d), w)


def _norm_matmul_kernel(x_ref, g_ref, w_ref, o_ref, xn_ref, h_ref):
    @pl.when(pl.program_id(1) == 0)
    def _():
        h = _rms(x_ref[...], g_ref[...]).astype(BF16)
        h_ref[...] = h
        xn_ref[...] = h

    o_ref[...] = jnp.dot(h_ref[...], w_ref[...], preferred_element_type=F32).astype(o_ref.dtype)


def norm_matmul(x, g, w, out_dtype, tm=512, tn=512):
    n, d = x.shape
    ncol = w.shape[1]
    return pl.pallas_call(
        _norm_matmul_kernel,
        grid=(n // tm, ncol // tn),
        in_specs=[pl.BlockSpec((tm, d), lambda i, j: (i, 0)),
                  pl.BlockSpec((1, d), lambda i, j: (0, 0)),
                  pl.BlockSpec((d, tn), lambda i, j: (0, j))],
        out_specs=[pl.BlockSpec((tm, tn), lambda i, j: (i, j)),
                   pl.BlockSpec((tm, d), lambda i, j: (i, 0))],
        out_shape=[jax.ShapeDtypeStruct((n, ncol), out_dtype),
                   jax.ShapeDtypeStruct((n, d), BF16)],
        scratch_shapes=[pltpu.VMEM((tm, d), BF16)],
        compiler_params=_params(("parallel", "arbitrary")),
        name="norm_matmul",
    )(x, g.reshape(1, d), w)


def _matmul_res_kernel(a_ref, w_ref, r_ref, o_ref):
    o_ref[...] = r_ref[...] + jnp.dot(a_ref[...], w_ref[...], preferred_element_type=F32)


def matmul_res(a, w, res, tm=512, tn=512):
    n, k = a.shape
    ncol = w.shape[1]
    return pl.pallas_call(
        _matmul_res_kernel,
        grid=(n // tm, ncol // tn),
        in_specs=[pl.BlockSpec((tm, k), lambda i, j: (i, 0)),
                  pl.BlockSpec((k, tn), lambda i, j: (0, j)),
                  pl.BlockSpec((tm, tn), lambda i, j: (i, j))],
        out_specs=pl.BlockSpec((tm, tn), lambda i, j: (i, j)),
        out_shape=jax.ShapeDtypeStruct((n, ncol), F32),
        compiler_params=_params(("parallel", "arbitrary")),
        name="matmul_res",
    )(a, w, res)


def _merge_kernel(x_ref, g_ref, ya_ref, yb_ref, yc_ref, wga_ref, wgb_ref, wgc_ref,
                  wa_ref, wb_ref, wc_ref, o_ref, h_ref):
    @pl.when(pl.program_id(1) == 0)
    def _():
        h_ref[...] = _rms(x_ref[...], g_ref[...]).astype(BF16)

    h = h_ref[...]

    def branch(y_ref, wg_ref, w_ref):
        gate = jnp.dot(h, wg_ref[...], preferred_element_type=F32)
        proj = jnp.dot(y_ref[...], w_ref[...], preferred_element_type=F32)
        return jax.nn.sigmoid(gate) * proj

    merged = branch(ya_ref, wga_ref, wa_ref) + branch(yb_ref, wgb_ref, wb_ref) + branch(yc_ref, wgc_ref, wc_ref)
    o_ref[...] = merged.astype(o_ref.dtype)


def merge(x, g, ya, yb, yc, w_gate, wa, wb, wc, tm=512, tn=512):
    n, d = x.shape
    nj = d // tn
    return pl.pallas_call(
        _merge_kernel,
        grid=(n // tm, nj),
        in_specs=[pl.BlockSpec((tm, d), lambda i, j: (i, 0)),
                  pl.BlockSpec((1, d), lambda i, j: (0, 0)),
                  pl.BlockSpec((tm, ya.shape[1]), lambda i, j: (i, 0)),
                  pl.BlockSpec((tm, yb.shape[1]), lambda i, j: (i, 0)),
                  pl.BlockSpec((tm, yc.shape[1]), lambda i, j: (i, 0)),
                  pl.BlockSpec((d, tn), lambda i, j: (0, j)),
                  pl.BlockSpec((d, tn), lambda i, j: (0, j + nj)),
                  pl.BlockSpec((d, tn), lambda i, j: (0, j + 2 * nj)),
                  pl.BlockSpec((wa.shape[0], tn), lambda i, j: (0, j)),
                  pl.BlockSpec((wb.shape[0], tn), lambda i, j: (0, j)),
                  pl.BlockSpec((wc.shape[0], tn), lambda i, j: (0, j))],
        out_specs=pl.BlockSpec((tm, tn), lambda i, j: (i, j)),
        out_shape=jax.ShapeDtypeStruct((n, d), BF16),
        scratch_shapes=[pltpu.VMEM((tm, d), BF16)],
        compiler_params=_params(("parallel", "arbitrary")),
        name="merge",
    )(x, g.reshape(1, d), ya, yb, yc, w_gate, w_gate, w_gate, wa, wb, wc)


def _final_norm_kernel(x_ref, g_ref, o_ref):
    o_ref[...] = _rms(x_ref[...], g_ref[...])


def final_norm(x, g, tm=512):
    n, d = x.shape
    return pl.pallas_call(
        _final_norm_kernel,
        grid=(n // tm,),
        in_specs=[pl.BlockSpec((tm, d), lambda i: (i, 0)), pl.BlockSpec((1, d), lambda i: (0, 0))],
        out_specs=pl.BlockSpec((tm, d), lambda i: (i, 0)),
        out_shape=jax.ShapeDtypeStruct((n, d), F32),
        compiler_params=_params(("parallel",)),
        name="final_norm",
    )(x, g.reshape(1, d))


def _rel_bucket(dist):
    d = jnp.maximum(dist, 0)
    max_exact = REL_BUCKETS // 2
    df = jnp.maximum(d, max_exact).astype(F32)
    large = max_exact + (jnp.log(df / max_exact) / math.log(REL_MAX_DIST / max_exact)
                         * (REL_BUCKETS - max_exact)).astype(I32)
    large = jnp.minimum(large, REL_BUCKETS - 1)
    return jnp.where(d < max_exact, d, large)


def _bias_windows(bt, seq):
    dist = (jnp.arange(seq // TILE)[:, None] * TILE - (TILE - 1) + jnp.arange(2 * TILE)[None, :])
    w = bt[_rel_bucket(dist)].astype(F32)
    w = jnp.where((dist >= 0)[..., None], w, 0.0)
    return jnp.transpose(w, (2, 0, 1))


def _gen_bias_tiles(win_ref, tiles_ref):
    n_heads, n_rel = win_ref.shape[0], win_ref.shape[1]

    def body(t, carry):
        h = t // n_rel
        r = t % n_rel
        row = win_ref[h, pl.ds(r, 1), :]
        x = jnp.broadcast_to(row, (TILE, 2 * TILE))
        tiles_ref[h, r] = pltpu.roll(x, TILE + 1, 1, stride=1, stride_axis=0)[:, :TILE]
        return carry

    lax.fori_loop(0, n_heads * n_rel, body, 0)


def _swa_kernel(q_ref, kp_ref, kc_ref, vp_ref, vc_ref, bias_ref, sink_ref, o_ref):
    i = pl.program_id(1)
    which = jnp.minimum(i, 1)
    group = A_HEADS // A_KV_HEADS
    outs = []
    for kvh in range(A_KV_HEADS):
        lo, hi = kvh * HEAD_DIM, (kvh + 1) * HEAD_DIM
        kk = jnp.concatenate([kp_ref[0, :, lo:hi], kc_ref[0, :, lo:hi]], axis=0)
        vv = jnp.concatenate([vp_ref[0, :, lo:hi], vc_ref[0, :, lo:hi]], axis=0)
        for g in range(group):
            h = kvh * group + g
            q = q_ref[0, :, h * HEAD_DIM:(h + 1) * HEAD_DIM]
            s = lax.dot_general(q, kk, NT, preferred_element_type=F32) * (HEAD_DIM ** -0.5)
            s = s + bias_ref[which, h]
            sink = sink_ref[h]
            m = jnp.maximum(jnp.max(s, axis=-1, keepdims=True), sink)
            p = jnp.exp(s - m)
            denom = jnp.sum(p, axis=-1, keepdims=True) + jnp.exp(sink - m)
            o = jnp.dot(p.astype(BF16), vv, preferred_element_type=F32) / denom
            outs.append(o.astype(BF16))
    o_ref[0] = jnp.concatenate(outs, axis=-1)


def _swa_bias(bt_a):
    qpos = jnp.arange(A_BLOCK)[:, None] + A_BLOCK
    kpos = jnp.arange(2 * A_BLOCK)[None, :]
    dist = qpos - kpos
    band = (dist >= 0) & (dist < A_WINDOW)
    bias = jnp.transpose(bt_a[_rel_bucket(dist)].astype(F32), (2, 0, 1))
    general = jnp.where(band[None], bias, NEG)
    first = jnp.where((band & (kpos >= A_BLOCK))[None], bias, NEG)
    return jnp.stack([first, general])


def swa(p_a, bias, sinks):
    bsz, seq, _ = p_a.shape
    qw, kw = A_HEADS * HEAD_DIM, A_KV_HEADS * HEAD_DIM
    kblk, vblk = qw // kw, qw // kw + 1
    return pl.pallas_call(
        _swa_kernel,
        grid=(bsz, seq // A_BLOCK),
        in_specs=[pl.BlockSpec((1, A_BLOCK, qw), lambda b, i: (b, i, 0)),
                  pl.BlockSpec((1, A_BLOCK, kw), lambda b, i: (b, jnp.maximum(i - 1, 0), kblk)),
                  pl.BlockSpec((1, A_BLOCK, kw), lambda b, i: (b, i, kblk)),
                  pl.BlockSpec((1, A_BLOCK, kw), lambda b, i: (b, jnp.maximum(i - 1, 0), vblk)),
                  pl.BlockSpec((1, A_BLOCK, kw), lambda b, i: (b, i, vblk)),
                  pl.BlockSpec(bias.shape, lambda b, i: (0, 0, 0, 0)),
                  pl.BlockSpec(memory_space=pltpu.SMEM)],
        out_specs=pl.BlockSpec((1, A_BLOCK, qw), lambda b, i: (b, i, 0)),
        out_shape=jax.ShapeDtypeStruct((bsz, seq, qw), BF16),
        compiler_params=_params(("parallel", "arbitrary")),
        name="swa",
    )(p_a, p_a, p_a, p_a, p_a, bias, sinks)


def _moba_kernel(q_ref, k_ref, vt_ref, win_ref, o_ref, tiles_ref, kmean_ref, sel_ref, m_ref, l_ref, acc_ref):
    b, i = pl.program_id(0), pl.program_id(1)
    n_heads, n_blk = kmean_ref.shape[0], kmean_ref.shape[1]

    @pl.when((b == 0) & (i == 0))
    def _():
        _gen_bias_tiles(win_ref, tiles_ref)

    @pl.when(i == 0)
    def _():
        for h in range(n_heads):
            kmean_ref[h] = jnp.mean(k_ref[0, h].astype(F32), axis=1)

    row = lax.broadcasted_iota(I32, (n_blk, TILE), 0)
    for h in range(n_heads):
        q = q_ref[0, h]
        kmean = kmean_ref[h]
        kmean_hi = kmean.astype(BF16)
        kmean_lo = (kmean - kmean_hi.astype(F32)).astype(BF16)
        gate = (lax.dot_general(kmean_hi, q, NT, preferred_element_type=F32)
                + lax.dot_general(kmean_lo, q, NT, preferred_element_type=F32))
        gate = jnp.where(row < i, gate, NEG)
        sel = jnp.zeros(gate.shape, F32)
        for _ in range(B_TOPK):
            best = jnp.max(gate, axis=0, keepdims=True)
            first = jnp.min(jnp.where(gate == best, row, n_blk), axis=0, keepdims=True)
            pick = row == first
            sel = jnp.where(pick, 1.0, sel)
            gate = jnp.where(pick, -jnp.inf, gate)
        sel_ref[h] = jnp.where(row < i, sel, 0.0)

    m_ref[...] = jnp.full(m_ref.shape, NEG, F32)
    l_ref[...] = jnp.zeros_like(l_ref)
    acc_ref[...] = jnp.zeros_like(acc_ref)
    key_pos = lax.broadcasted_iota(I32, (TILE, TILE), 0)
    qry_pos = lax.broadcasted_iota(I32, (TILE, TILE), 1)
    causal = jnp.where(qry_pos >= key_pos, 1.0, 0.0)

    def body(j, carry):
        for h in range(n_heads):
            s = lax.dot_general(k_ref[0, h, j], q_ref[0, h], NT, preferred_element_type=F32) * (HEAD_DIM ** -0.5)
            s = s + tiles_ref[h, i - j]
            chosen = jnp.broadcast_to(sel_ref[h, pl.ds(j, 1), :], (TILE, TILE))
            ok = jnp.where(j == i, causal, chosen) > 0.0
            s = jnp.where(ok, s, NEG)
            m_old = m_ref[h]
            m_new = jnp.maximum(m_old, jnp.max(s, axis=0, keepdims=True))
            alpha = jnp.exp(m_old - m_new)
            p = jnp.where(ok, jnp.exp(s - m_new), 0.0)
            m_ref[h] = m_new
            l_ref[h] = alpha * l_ref[h] + jnp.sum(p, axis=0, keepdims=True)
            acc_ref[h] = alpha * acc_ref[h] + jnp.dot(vt_ref[0, h, j], p.astype(BF16), preferred_element_type=F32)
        return carry

    lax.fori_loop(0, i + 1, body, 0)
    for h in range(n_heads):
        o_ref[0, h] = (acc_ref[h] / l_ref[h]).astype(o_ref.dtype)


def moba(q, k, vt, windows):
    bsz, n_heads, seq, d = q.shape
    n_blk = seq // TILE
    return pl.pallas_call(
        _moba_kernel,
        grid=(bsz, n_blk),
        in_specs=[pl.BlockSpec((1, n_heads, TILE, d), lambda b, i: (b, 0, i, 0)),
                  pl.BlockSpec((1, n_heads, n_blk, TILE, d), lambda b, i: (b, 0, 0, 0, 0)),
                  pl.BlockSpec((1, n_heads, n_blk, d, TILE), lambda b, i: (b, 0, 0, 0, 0)),
                  pl.BlockSpec(windows.shape, lambda b, i: (0, 0, 0))],
        out_specs=pl.BlockSpec((1, n_heads, d, TILE), lambda b, i: (b, 0, 0, i)),
        out_shape=jax.ShapeDtypeStruct((bsz, n_heads, d, seq), BF16),
        scratch_shapes=[pltpu.VMEM((n_heads, n_blk, TILE, TILE), F32),
                        pltpu.VMEM((n_heads, n_blk, d), F32),
                        pltpu.VMEM((n_heads, n_blk, TILE), F32),
                        pltpu.VMEM((n_heads, 1, TILE), F32),
                        pltpu.VMEM((n_heads, 1, TILE), F32),
                        pltpu.VMEM((n_heads, d, TILE), F32)],
        compiler_params=_params(("arbitrary", "arbitrary")),
        name="moba",
    )(q, k, vt, windows)


def _dsa_kernel(qc_ref, iq_ref, mq_ref, clat_ref, mk_ref, gain_ref, wuk_ref, wuvt_ref, win_ref, o_ref,
                tiles_ref, cn_ref, cnt_ref, ik_ref, keys_ref, mask_ref, qabs_ref, m_ref, l_ref, acc_ref, *, top_k):
    b, c = pl.program_id(0), pl.program_id(1)
    n_blk = cn_ref.shape[0]
    n_kb = c + 1

    @pl.when((b == 0) & (c == 0))
    def _():
        _gen_bias_tiles(win_ref, tiles_ref)

    @pl.when(c == 0)
    def _():
        lat = _rms(clat_ref[0], gain_ref[...])
        for kb in range(n_blk):
            blk = lat[kb * TILE:(kb + 1) * TILE]
            cn_ref[kb] = blk.astype(BF16)
            cnt_ref[kb] = jnp.transpose(blk).astype(BF16)
            ik_ref[kb] = mk_ref[0, kb * TILE:(kb + 1) * TILE, :IDX_DIM].astype(BF16)

    key_in = lax.broadcasted_iota(I32, (TILE, TILE), 0)
    qry_in = lax.broadcasted_iota(I32, (TILE, TILE), 1)
    diag_causal = key_in <= qry_in

    iq = iq_ref[0].astype(BF16)
    iw_t = jnp.transpose(mq_ref[0])[IDX_DIM:IDX_DIM + IDX_HEADS, :]

    def score_block(kb, carry):
        score = jnp.zeros((TILE, TILE), F32)
        for hh in range(IDX_HEADS):
            logits = lax.dot_general(ik_ref[kb], iq[:, hh * IDX_DIM:(hh + 1) * IDX_DIM], NT,
                                     preferred_element_type=F32)
            score = score + jnp.maximum(logits, 0.0) * iw_t[hh:hh + 1, :]
        score = score * ((IDX_DIM ** -0.5) * (IDX_HEADS ** -0.5))
        score = jnp.where(score == 0.0, 0.0, score)
        key = _sortable(score)
        keys_ref[kb] = jnp.where(kb < c, key, jnp.where(diag_causal, key, INT_MIN))
        return carry

    lax.fori_loop(0, n_kb, score_block, 0)

    def count(pred):
        def blk(kb, cnt):
            return cnt + jnp.sum(pred(keys_ref[kb], kb), axis=0, keepdims=True)
        return lax.fori_loop(0, n_kb, blk, jnp.zeros((1, TILE), F32))

    def value_bit(t, tau):
        cand = tau + lax.shift_left(jnp.int32(1), 31 - t)
        cnt = count(lambda key, kb: jnp.where(key >= cand, 1.0, 0.0))
        return jnp.where(cnt >= top_k, cand, tau)

    tau = lax.fori_loop(0, 32, value_bit, jnp.full((1, TILE), INT_MIN, I32))
    need = top_k - count(lambda key, kb: jnp.where(key > tau, 1.0, 0.0))

    def index_bit(t, bound):
        cand = bound + lax.shift_left(jnp.int32(1), 11 - t)
        cnt = count(lambda key, kb: jnp.where(key == tau, jnp.where(kb * TILE + key_in < cand, 1.0, 0.0), 0.0))
        return jnp.where(cnt <= need, cand, bound)

    bound = lax.fori_loop(0, 12, index_bit, jnp.zeros((1, TILE), I32))

    def mask_block(kb, carry):
        key = keys_ref[kb]
        tied = jnp.where(key == tau, jnp.where(kb * TILE + key_in < bound, 1.0, 0.0), 0.0)
        chosen = jnp.where(key > tau, 1.0, tied)
        mask_ref[kb] = jnp.where(kb < c, chosen, jnp.where(diag_causal, chosen, 0.0))
        return carry

    lax.fori_loop(0, n_kb, mask_block, 0)

    qc = qc_ref[0].astype(BF16)
    for h in range(C_HEADS):
        qabs_ref[h] = lax.dot_general(wuk_ref[h], qc[:, h * HEAD_DIM:(h + 1) * HEAD_DIM], NT,
                                      preferred_element_type=F32).astype(BF16)
    m_ref[...] = jnp.full(m_ref.shape, NEG, F32)
    l_ref[...] = jnp.zeros_like(l_ref)
    acc_ref[...] = jnp.zeros_like(acc_ref)

    def attend_block(kb, carry):
        ok = mask_ref[kb] > 0.0
        for h in range(C_HEADS):
            s = jnp.dot(cn_ref[kb], qabs_ref[h], preferred_element_type=F32) * (HEAD_DIM ** -0.5)
            s = jnp.where(ok, s + tiles_ref[h, c - kb], NEG)
            m_old = m_ref[h]
            m_new = jnp.maximum(m_old, jnp.max(s, axis=0, keepdims=True))
            alpha = jnp.exp(m_old - m_new)
            p = jnp.where(ok, jnp.exp(s - m_new), 0.0)
            m_ref[h] = m_new
            l_ref[h] = alpha * l_ref[h] + jnp.sum(p, axis=0, keepdims=True)
            acc_ref[h] = alpha * acc_ref[h] + jnp.dot(cnt_ref[kb], p.astype(BF16), preferred_element_type=F32)
        return carry

    lax.fori_loop(0, n_kb, attend_block, 0)
    for h in range(C_HEADS):
        oc_t = (acc_ref[h] / l_ref[h]).astype(BF16)
        out_t = jnp.dot(wuvt_ref[h], oc_t, preferred_element_type=F32)
        o_ref[0, h * HEAD_DIM:(h + 1) * HEAD_DIM, :] = out_t.astype(o_ref.dtype)


def dsa(p_c, gain, w_uk, w_uv_t, windows):
    bsz, seq, _ = p_c.shape
    top_k = min(C_TOPK_MAX, seq // 4)
    qw = C_HEADS * HEAD_DIM
    n_blk = seq // TILE
    return pl.pallas_call(
        functools.partial(_dsa_kernel, top_k=top_k),
        grid=(bsz, n_blk),
        in_specs=[pl.BlockSpec((1, TILE, qw), lambda b, c: (b, c, 0)),
                  pl.BlockSpec((1, TILE, 256), lambda b, c: (b, c, 2)),
                  pl.BlockSpec((1, TILE, 128), lambda b, c: (b, c, 7)),
                  pl.BlockSpec((1, seq, 128), lambda b, c: (b, 0, 6)),
                  pl.BlockSpec((1, seq, 128), lambda b, c: (b, 0, 7)),
                  pl.BlockSpec((1, C_LATENT), lambda b, c: (0, 0)),
                  pl.BlockSpec(w_uk.shape, lambda b, c: (0, 0, 0)),
                  pl.BlockSpec(w_uv_t.shape, lambda b, c: (0, 0, 0)),
                  pl.BlockSpec(windows.shape, lambda b, c: (0, 0, 0))],
        out_specs=pl.BlockSpec((1, qw, TILE), lambda b, c: (b, 0, c)),
        out_shape=jax.ShapeDtypeStruct((bsz, qw, seq), BF16),
        scratch_shapes=[pltpu.VMEM((C_HEADS, n_blk, TILE, TILE), F32),
                        pltpu.VMEM((n_blk, TILE, C_LATENT), BF16),
                        pltpu.VMEM((n_blk, C_LATENT, TILE), BF16),
                        pltpu.VMEM((n_blk, TILE, IDX_DIM), BF16),
                        pltpu.VMEM((n_blk, TILE, TILE), I32),
                        pltpu.VMEM((n_blk, TILE, TILE), F32),
                        pltpu.VMEM((C_HEADS, C_LATENT, TILE), BF16),
                        pltpu.VMEM((C_HEADS, 1, TILE), F32),
                        pltpu.VMEM((C_HEADS, 1, TILE), F32),
                        pltpu.VMEM((C_HEADS, C_LATENT, TILE), F32)],
        compiler_params=_params(("arbitrary", "arbitrary")),
        name="dsa",
    )(p_c, p_c, p_c, p_c, p_c, gain.reshape(1, C_LATENT), w_uk, w_uv_t, windows)


ROUTE_TM = 256
ROUTE_ROWS = 72


def _top_values(s, count):
    vals, mult = [], []
    for _ in range(count):
        best = jnp.max(s, axis=0, keepdims=True)
        hit = s == best
        vals.append(best)
        mult.append(jnp.sum(jnp.where(hit, 1.0, 0.0), axis=0, keepdims=True))
        s = jnp.where(hit, -jnp.inf, s)
    return jnp.concatenate(vals, axis=0), jnp.concatenate(mult, axis=0)


def _route_kernel(q_ref, k_ref, s1_ref, s2_ref, sc_ref, keys_ref, wts_ref):
    q = q_ref[...]
    tm = q.shape[0]
    half = PEER_QDIM // 2
    s1 = lax.dot_general(k_ref[0, 0], q[:, :half], NT, preferred_element_type=F32)
    s2 = lax.dot_general(k_ref[0, 1], q[:, half:], NT, preferred_element_type=F32)
    s1_ref[0] = s1
    s2_ref[0] = s2
    t1, w1 = _top_values(s1, PEER_TOPK)
    t2, w2 = _top_values(s2, PEER_TOPK)

    r16 = lax.broadcasted_iota(I32, (16, tm), 0)
    r8 = lax.broadcasted_iota(I32, (8, tm), 0)
    pieces = [
        (t1[0:1] + t2, w1[0:1] * w2),
        (t1[1:2] + t2[:8], w1[1:2] * w2[:8]),
        (t1[2:3] + t2[:8], jnp.where(r8 < 5, w1[2:3] * w2[:8], 0.0)),
        (t1[3:4] + t2[:8], jnp.where(r8 < 4, w1[3:4] * w2[:8], 0.0)),
        (t2[0:1] + t1, jnp.where(r16 >= 4, w2[0:1] * w1, 0.0)),
        (t2[1:2] + t1[:8], jnp.where(r8 >= 4, w2[1:2] * w1[:8], 0.0)),
        (t2[2:3] + t1[:8], jnp.where(r8 == 4, w2[2:3] * w1[:8], 0.0)),
    ]
    cand = jnp.concatenate([p[0] for p in pieces], axis=0)
    wts = jnp.concatenate([p[1] for p in pieces], axis=0)
    keys_ref[...] = _sortable(cand)
    wts_ref[...] = wts

    def value_bit(t, tau):
        c = tau + lax.shift_left(jnp.int32(1), 31 - t)
        cnt = jnp.sum(jnp.where(keys_ref[...] >= c, wts_ref[...], 0.0), axis=0, keepdims=True)
        return jnp.where(cnt >= PEER_TOPK, c, tau)

    tau_key = lax.fori_loop(0, 32, value_bit, jnp.full((1, tm), INT_MIN, I32))
    top = t1[0:1] + t2[0:1]
    z = jnp.sum(jnp.where(keys_ref[...] >= tau_key, wts * jnp.exp(cand - top), 0.0), axis=0, keepdims=True)
    rows = [_unsortable(tau_key), t2[0:1], t1[0:1] + jnp.log(z)]
    sc_ref[0] = jnp.concatenate(rows + [jnp.zeros_like(z)] * (8 - len(rows)), axis=0)


def peer_route(qp, sub_keys):
    n = qp.shape[0]
    tm = ROUTE_TM
    half = PEER_QDIM // 2
    return pl.pallas_call(
        _route_kernel,
        grid=(n // tm, PEER_HEADS),
        in_specs=[pl.BlockSpec((tm, PEER_QDIM), lambda i, h: (i, h)),
                  pl.BlockSpec((1, 2, PEER_KEYS, half), lambda i, h: (h, 0, 0, 0))],
        out_specs=[pl.BlockSpec((1, PEER_KEYS, tm), lambda i, h: (h, 0, i)),
                   pl.BlockSpec((1, PEER_KEYS, tm), lambda i, h: (h, 0, i)),
                   pl.BlockSpec((1, 8, tm), lambda i, h: (h, 0, i))],
        out_shape=[jax.ShapeDtypeStruct((PEER_HEADS, PEER_KEYS, n), F32),
                   jax.ShapeDtypeStruct((PEER_HEADS, PEER_KEYS, n), F32),
                   jax.ShapeDtypeStruct((PEER_HEADS, 8, n), F32)],
        scratch_shapes=[pltpu.VMEM((ROUTE_ROWS, tm), I32),
                        pltpu.VMEM((ROUTE_ROWS, tm), F32)],
        compiler_params=_params(("parallel", "arbitrary")),
        name="peer_route",
    )(qp, sub_keys)


def _peer_kernel(x_ref, xn_ref, s1_ref, s2_ref, sc_ref, u_hbm, vt_hbm, o_ref,
                 acc_ref, e2_ref, s2c_ref, scc_ref, xnt_ref, act0_ref, act1_ref, hid0_ref, hid1_ref,
                 ubuf_ref, vbuf_ref, sem_ref):
    i, j = pl.program_id(0), pl.program_id(1)
    n_i, n_j = pl.num_programs(0), pl.num_programs(1)
    n_chunks = n_j - 2
    te = ubuf_ref.shape[1]
    n_sub = te // PEER_KEYS

    def table_copies(step, slot):
        cu = jnp.minimum(step, n_chunks - 1)
        cv = jnp.clip(step - 2, 0, n_chunks - 1)
        return (pltpu.make_async_copy(u_hbm.at[pl.ds(cu * te, te), :], ubuf_ref.at[slot], sem_ref.at[0, slot]),
                pltpu.make_async_copy(vt_hbm.at[:, pl.ds(cv * te, te)], vbuf_ref.at[slot], sem_ref.at[1, slot]))

    @pl.when((i == 0) & (j == 0))
    def _():
        for cp in table_copies(0, 0):
            cp.start()

    @pl.when(j == 0)
    def _():
        acc_ref[...] = jnp.zeros_like(acc_ref)
        act1_ref[...] = jnp.zeros_like(act1_ref)
        hid0_ref[...] = jnp.zeros_like(hid0_ref)
        xnt_ref[...] = jnp.transpose(xn_ref[...].astype(F32)).astype(BF16)
        s2c_ref[...] = s2_ref[...]
        scc_ref[...] = sc_ref[...]
        for h in range(PEER_HEADS):
            e2_ref[h] = jnp.exp(s2_ref[h] - sc_ref[h, 1:2, :])

    def step(slot, act_w, act_r, hid_w, hid_r):
        for cp in table_copies(j, slot):
            cp.wait()

        @pl.when((i < n_i - 1) | (j < n_j - 1))
        def _():
            nxt = jnp.where(j == n_j - 1, 0, j + 1)
            for cp in table_copies(nxt, 1 - slot):
                cp.start()

        s1_rows = [[s1_ref[h, 0, ab:ab + 1, :] for h in range(PEER_HEADS)] for ab in range(n_sub)]
        for ab in range(n_sub):
            rows = slice(ab * PEER_KEYS, (ab + 1) * PEER_KEYS)
            c1_rows = [jnp.exp(s1_rows[ab][h] - scc_ref[h, 2:3, :]) for h in range(PEER_HEADS)]
            for tc in range(act_r.shape[1] // LANES):
                cols = slice(tc * LANES, (tc + 1) * LANES)
                gate = jnp.zeros((PEER_KEYS, LANES), F32)
                for h in range(PEER_HEADS):
                    total = s2c_ref[h, :, cols] + s1_rows[ab][h][:, cols]
                    weight = e2_ref[h, :, cols] * c1_rows[h][:, cols]
                    gate = gate + jnp.where(total >= scc_ref[h, 0:1, cols], weight, 0.0)
                hid_w[rows, cols] = (gate * jax.nn.gelu(act_r[rows, cols])).astype(BF16)
        act_w[...] = jnp.dot(ubuf_ref[slot], xnt_ref[...], preferred_element_type=F32)
        acc_ref[...] += jnp.dot(vbuf_ref[slot], hid_r[...], preferred_element_type=F32)

    @pl.when(j % 2 == 0)
    def _():
        step(0, act0_ref, act1_ref, hid1_ref, hid0_ref)

    @pl.when(j % 2 == 1)
    def _():
        step(1, act1_ref, act0_ref, hid0_ref, hid1_ref)

    @pl.when(j == n_j - 1)
    def _():
        o_ref[...] = x_ref[...] + jnp.transpose(acc_ref[...])


def peer(x, xn, s1, s2, sc, u, vt, tm=512, te=512):
    n, d = x.shape
    n_chunks = PEER_N // te
    n_sub = te // PEER_KEYS
    assert n_chunks % 2 == 0
    s1_chunks = s1.reshape(PEER_HEADS, n_chunks, n_sub, n)
    return pl.pallas_call(
        _peer_kernel,
        grid=(n // tm, n_chunks + 2),
        in_specs=[pl.BlockSpec((tm, d), lambda i, j: (i, 0)),
                  pl.BlockSpec((tm, d), lambda i, j: (i, 0)),
                  pl.BlockSpec((PEER_HEADS, 1, n_sub, tm), lambda i, j: (0, jnp.clip(j - 1, 0, n_chunks - 1), 0, i)),
                  pl.BlockSpec((PEER_HEADS, PEER_KEYS, tm), lambda i, j: (0, 0, i)),
                  pl.BlockSpec((PEER_HEADS, 8, tm), lambda i, j: (0, 0, i)),
                  pl.BlockSpec(memory_space=pl.ANY),
                  pl.BlockSpec(memory_space=pl.ANY)],
        out_specs=pl.BlockSpec((tm, d), lambda i, j: (i, 0)),
        out_shape=jax.ShapeDtypeStruct((n, d), F32),
        scratch_shapes=[pltpu.VMEM((d, tm), F32),
                        pltpu.VMEM((PEER_HEADS, PEER_KEYS, tm), F32),
                        pltpu.VMEM((PEER_HEADS, PEER_KEYS, tm), F32),
                        pltpu.VMEM((PEER_HEADS, 8, tm), F32),
                        pltpu.VMEM((d, tm), BF16),
                        pltpu.VMEM((te, tm), F32),
                        pltpu.VMEM((te, tm), F32),
                        pltpu.VMEM((te, tm), BF16),
                        pltpu.VMEM((te, tm), BF16),
                        pltpu.VMEM((2, te, d), BF16),
                        pltpu.VMEM((2, d, te), BF16),
                        pltpu.SemaphoreType.DMA((2, 2))],
        compiler_params=_params(("arbitrary", "arbitrary")),
        name="peer",
    )(x, xn, s1_chunks, s2, sc, u, vt)


def kernel(x, rel_bias, g_mix, w_in, a_sinks, c_lat_gain, c_w_uk, c_w_uv, w_br_a, w_br_b, w_br_c, w_out,
           g_ffn, peer_w_q, peer_sub_keys, peer_u, peer_v, g_final):
    bsz, seq, d = x.shape
    n = bsz * seq
    depth = w_in.shape[0]
    n_blk = seq // TILE
    off = [0] + [int(o) for o in np.cumsum(IN_SPLITS)]
    bt_a = rel_bias[:, :A_HEADS]
    bt_b = rel_bias[:, A_HEADS:A_HEADS + B_HEADS]
    bt_c = rel_bias[:, A_HEADS + B_HEADS:]
    bias_a = _swa_bias(bt_a)
    win_b = _bias_windows(bt_b, seq)
    win_c = _bias_windows(bt_c, seq)

    xf = x.reshape(n, d)
    for l in range(depth):
        w = w_in[l]
        pad = jnp.zeros((d, C_COLS - (off[11] - off[6])), F32)
        w_abc = jnp.concatenate([w[:, off[0]:off[6]],
                                 w[:, off[6]:off[7]], w[:, off[8]:off[9]], w[:, off[7]:off[8]],
                                 w[:, off[9]:off[11]], pad], axis=1).astype(BF16)
        w_g = w[:, off[11]:off[14]].astype(BF16)

        p_a, p_b, p_c = in_proj(xf, g_mix[l], w_abc)
        p_a = p_a.reshape(bsz, seq, -1)
        p_b = p_b.reshape(bsz, seq, 3, B_HEADS, HEAD_DIM)
        p_c = p_c.reshape(bsz, seq, -1)

        ya = swa(p_a, bias_a, a_sinks[l]).reshape(n, -1)

        qb = jnp.transpose(p_b[:, :, 0], (0, 2, 1, 3))
        kb = jnp.transpose(p_b[:, :, 1], (0, 2, 1, 3)).reshape(bsz, B_HEADS, n_blk, TILE, HEAD_DIM)
        vbt = jnp.transpose(p_b[:, :, 2].reshape(bsz, n_blk, TILE, B_HEADS, HEAD_DIM), (0, 3, 1, 4, 2))
        yb_t = moba(qb, kb, vbt, win_b)
        yb = jnp.transpose(yb_t, (0, 3, 1, 2)).reshape(n, -1)

        w_uk = jnp.transpose(c_w_uk[l], (1, 0, 2)).astype(BF16)
        w_uv_t = jnp.transpose(c_w_uv[l], (1, 2, 0)).astype(BF16)
        yc_t = dsa(p_c, c_lat_gain[l], w_uk, w_uv_t, win_c)
        yc = jnp.transpose(yc_t, (0, 2, 1)).reshape(n, -1)

        merged = merge(xf, g_mix[l], ya, yb, yc, w_g, w_br_a[l].astype(BF16), w_br_b[l].astype(BF16),
                       w_br_c[l].astype(BF16))
        xf = matmul_res(merged, w_out[l].astype(BF16), xf)

        qp, xn = norm_matmul(xf, g_ffn[l], peer_w_q[l].astype(BF16), BF16)
        s1, s2, sc = peer_route(qp, peer_sub_keys[l].astype(BF16))
        xf = peer(xf, xn, s1, s2, sc, peer_u[l].astype(BF16), jnp.transpose(peer_v[l]).astype(BF16))

    return final_norm(xf, g_final).reshape(bsz, seq, d)
```

```python
import functools
import math

import jax
import jax.numpy as jnp
import numpy as np
from jax import lax
from jax.experimental import pallas as pl
from jax.experimental.pallas import tpu as pltpu

F32 = jnp.float32
BF16 = jnp.bfloat16
I32 = jnp.int32

D_MODEL = 2048
HEAD_DIM = 64
A_HEADS, A_KV_HEADS, A_BLOCK, A_WINDOW = 16, 4, 128, 128
B_HEADS, B_BLOCK, B_TOPK = 8, 256, 3
C_HEADS, C_LATENT, IDX_HEADS, IDX_DIM, C_TOPK_MAX = 8, 128, 8, 32, 256
REL_BUCKETS, REL_MAX_DIST = 32, 2048
PEER_HEADS, PEER_KEYS, PEER_QDIM, PEER_TOPK = 8, 128, 256, 16
PEER_N = PEER_KEYS * PEER_KEYS
EPS = 1e-6
NEG = -1e30
INT_MIN = -(2 ** 31)
IN_SPLITS = (A_HEADS * HEAD_DIM, A_KV_HEADS * HEAD_DIM, A_KV_HEADS * HEAD_DIM,
             B_HEADS * HEAD_DIM, B_HEADS * HEAD_DIM, B_HEADS * HEAD_DIM,
             C_HEADS * HEAD_DIM, C_LATENT, IDX_HEADS * IDX_DIM, IDX_DIM, IDX_HEADS,
             D_MODEL, D_MODEL, D_MODEL)
A_COLS = sum(IN_SPLITS[0:3])
B_COLS = sum(IN_SPLITS[3:6])
C_COLS = 1024

TILE = 256
LANES = 128
VMEM_LIMIT = 56 * 1024 * 1024
NT = (((1,), (1,)), ((), ()))


def _params(sem, vmem=VMEM_LIMIT, flags=None):
    return pltpu.CompilerParams(dimension_semantics=sem, vmem_limit_bytes=vmem, flags=flags)


def _rms(x, g):
    return x * lax.rsqrt(jnp.mean(x * x, axis=-1, keepdims=True) + EPS) * g


def _sortable(v):
    bits = pltpu.bitcast(v, I32)
    return jnp.where(bits < 0, bits ^ jnp.int32(0x7FFFFFFF), bits)


def _unsortable(k):
    return pltpu.bitcast(jnp.where(k < 0, k ^ jnp.int32(0x7FFFFFFF), k), F32)


def _in_proj_kernel(x_ref, g_ref, w_ref, oa_ref, ob_ref, oc_ref, h_ref, *, na, nb):
    j = pl.program_id(1)

    @pl.when(j == 0)
    def _():
        h_ref[...] = _rms(x_ref[...], g_ref[...]).astype(BF16)

    y = jnp.dot(h_ref[...], w_ref[...], preferred_element_type=F32)

    @pl.when(j < na)
    def _():
        oa_ref[...] = y.astype(oa_ref.dtype)

    @pl.when((j >= na) & (j < na + nb))
    def _():
        ob_ref[...] = y.astype(ob_ref.dtype)

    @pl.when(j >= na + nb)
    def _():
        oc_ref[...] = y.astype(oc_ref.dtype)


def in_proj(x, g, w, tm=512, tn=512):
    n, d = x.shape
    na, nb, nc = A_COLS // tn, B_COLS // tn, C_COLS // tn
    return pl.pallas_call(
        functools.partial(_in_proj_kernel, na=na, nb=nb),
        grid=(n // tm, na + nb + nc),
        in_specs=[pl.BlockSpec((tm, d), lambda i, j: (i, 0)),
                  pl.BlockSpec((1, d), lambda i, j: (0, 0)),
                  pl.BlockSpec((d, tn), lambda i, j: (0, j))],
        out_specs=[pl.BlockSpec((tm, tn), lambda i, j: (i, jnp.minimum(j, na - 1))),
                   pl.BlockSpec((tm, tn), lambda i, j: (i, jnp.clip(j - na, 0, nb - 1))),
                   pl.BlockSpec((tm, tn), lambda i, j: (i, jnp.maximum(j - na - nb, 0)))],
        out_shape=[jax.ShapeDtypeStruct((n, A_COLS), BF16),
                   jax.ShapeDtypeStruct((n, B_COLS), BF16),
                   jax.ShapeDtypeStruct((n, C_COLS), F32)],
        scratch_shapes=[pltpu.VMEM((tm, d), BF16)],
        compiler_params=_params(("parallel", "arbitrary")),
        name="in_proj",
    )(x, g.reshape(1, d), w)


def _norm_matmul_kernel(x_ref, g_ref, w_ref, o_ref, xn_ref, h_ref):
    @pl.when(pl.program_id(1) == 0)
    def _():
        h = _rms(x_ref[...], g_ref[...]).astype(BF16)
        h_ref[...] = h
        xn_ref[...] = h

    o_ref[...] = jnp.dot(h_ref[...], w_ref[...], preferred_element_type=F32).astype(o_ref.dtype)


def norm_matmul(x, g, w, out_dtype, tm=512, tn=512):
    n, d = x.shape
    ncol = w.shape[1]
    return pl.pallas_call(
        _norm_matmul_kernel,
        grid=(n // tm, ncol // tn),
        in_specs=[pl.BlockSpec((tm, d), lambda i, j: (i, 0)),
                  pl.BlockSpec((1, d), lambda i, j: (0, 0)),
                  pl.BlockSpec((d, tn), lambda i, j: (0, j))],
        out_specs=[pl.BlockSpec((tm, tn), lambda i, j: (i, j)),
                   pl.BlockSpec((tm, d), lambda i, j: (i, 0))],
        out_shape=[jax.ShapeDtypeStruct((n, ncol), out_dtype),
                   jax.ShapeDtypeStruct((n, d), BF16)],
        scratch_shapes=[pltpu.VMEM((tm, d), BF16)],
        compiler_params=_params(("parallel", "arbitrary")),
        name="norm_matmul",
    )(x, g.reshape(1, d), w)


def _matmul_res_kernel(a_ref, w_ref, r_ref, o_ref):
    o_ref[...] = r_ref[...] + jnp.dot(a_ref[...], w_ref[...], preferred_element_type=F32)


def matmul_res(a, w, res, tm=512, tn=512):
    n, k = a.shape
    ncol = w.shape[1]
    return pl.pallas_call(
        _matmul_res_kernel,
        grid=(n // tm, ncol // tn),
        in_specs=[pl.BlockSpec((tm, k), lambda i, j: (i, 0)),
                  pl.BlockSpec((k, tn), lambda i, j: (0, j)),
                  pl.BlockSpec((tm, tn), lambda i, j: (i, j))],
        out_specs=pl.BlockSpec((tm, tn), lambda i, j: (i, j)),
        out_shape=jax.ShapeDtypeStruct((n, ncol), F32),
        compiler_params=_params(("parallel", "arbitrary")),
        name="matmul_res",
    )(a, w, res)


def _merge_kernel(x_ref, g_ref, ya_ref, yb_ref, yc_ref, wga_ref, wgb_ref, wgc_ref,
                  wa_ref, wb_ref, wc_ref, o_ref, h_ref):
    @pl.when(pl.program_id(1) == 0)
    def _():
        h_ref[...] = _rms(x_ref[...], g_ref[...]).astype(BF16)

    h = h_ref[...]

    def branch(y_ref, wg_ref, w_ref):
        gate = jnp.dot(h, wg_ref[...], preferred_element_type=F32)
        proj = jnp.dot(y_ref[...], w_ref[...], preferred_element_type=F32)
        return jax.nn.sigmoid(gate) * proj

    merged = branch(ya_ref, wga_ref, wa_ref) + branch(yb_ref, wgb_ref, wb_ref) + branch(yc_ref, wgc_ref, wc_ref)
    o_ref[...] = merged.astype(o_ref.dtype)


def merge(x, g, ya, yb, yc, w_gate, wa, wb, wc, tm=512, tn=512):
    n, d = x.shape
    nj = d // tn
    return pl.pallas_call(
        _merge_kernel,
        grid=(n // tm, nj),
        in_specs=[pl.BlockSpec((tm, d), lambda i, j: (i, 0)),
                  pl.BlockSpec((1, d), lambda i, j: (0, 0)),
                  pl.BlockSpec((tm, ya.shape[1]), lambda i, j: (i, 0)),
                  pl.BlockSpec((tm, yb.shape[1]), lambda i, j: (i, 0)),
                  pl.BlockSpec((tm, yc.shape[1]), lambda i, j: (i, 0)),
                  pl.BlockSpec((d, tn), lambda i, j: (0, j)),
                  pl.BlockSpec((d, tn), lambda i, j: (0, j + nj)),
                  pl.BlockSpec((d, tn), lambda i, j: (0, j + 2 * nj)),
                  pl.BlockSpec((wa.shape[0], tn), lambda i, j: (0, j)),
                  pl.BlockSpec((wb.shape[0], tn), lambda i, j: (0, j)),
                  pl.BlockSpec((wc.shape[0], tn), lambda i, j: (0, j))],
        out_specs=pl.BlockSpec((tm, tn), lambda i, j: (i, j)),
        out_shape=jax.ShapeDtypeStruct((n, d), BF16),
        scratch_shapes=[pltpu.VMEM((tm, d), BF16)],
        compiler_params=_params(("parallel", "arbitrary")),
        name="merge",
    )(x, g.reshape(1, d), ya, yb, yc, w_gate, w_gate, w_gate, wa, wb, wc)


def _final_norm_kernel(x_ref, g_ref, o_ref):
    o_ref[...] = _rms(x_ref[...], g_ref[...])


def final_norm(x, g, tm=512):
    n, d = x.shape
    return pl.pallas_call(
        _final_norm_kernel,
        grid=(n // tm,),
        in_specs=[pl.BlockSpec((tm, d), lambda i: (i, 0)), pl.BlockSpec((1, d), lambda i: (0, 0))],
        out_specs=pl.BlockSpec((tm, d), lambda i: (i, 0)),
        out_shape=jax.ShapeDtypeStruct((n, d), F32),
        compiler_params=_params(("parallel",)),
        name="final_norm",
    )(x, g.reshape(1, d))


def _rel_bucket(dist):
    d = jnp.maximum(dist, 0)
    max_exact = REL_BUCKETS // 2
    df = jnp.maximum(d, max_exact).astype(F32)
    large = max_exact + (jnp.log(df / max_exact) / math.log(REL_MAX_DIST / max_exact)
                         * (REL_BUCKETS - max_exact)).astype(I32)
    large = jnp.minimum(large, REL_BUCKETS - 1)
    return jnp.where(d < max_exact, d, large)


def _bias_windows(bt, seq):
    dist = (jnp.arange(seq // TILE)[:, None] * TILE - (TILE - 1) + jnp.arange(2 * TILE)[None, :])
    w = bt[_rel_bucket(dist)].astype(F32)
    w = jnp.where((dist >= 0)[..., None], w, 0.0)
    return jnp.transpose(w, (2, 0, 1))


def _gen_bias_tiles(win_ref, tiles_ref):
    n_heads, n_rel = win_ref.shape[0], win_ref.shape[1]

    def body(t, carry):
        h = t // n_rel
        r = t % n_rel
        row = win_ref[h, pl.ds(r, 1), :]
        x = jnp.broadcast_to(row, (TILE, 2 * TILE))
        tiles_ref[h, r] = pltpu.roll(x, TILE + 1, 1, stride=1, stride_axis=0)[:, :TILE]
        return carry

    lax.fori_loop(0, n_heads * n_rel, body, 0)


def _swa_kernel(q_ref, kp_ref, kc_ref, vp_ref, vc_ref, bias_ref, sink_ref, o_ref):
    i = pl.program_id(1)
    which = jnp.minimum(i, 1)
    group = A_HEADS // A_KV_HEADS
    outs = []
    for kvh in range(A_KV_HEADS):
        lo, hi = kvh * HEAD_DIM, (kvh + 1) * HEAD_DIM
        kk = jnp.concatenate([kp_ref[0, :, lo:hi], kc_ref[0, :, lo:hi]], axis=0)
        vv = jnp.concatenate([vp_ref[0, :, lo:hi], vc_ref[0, :, lo:hi]], axis=0)
        for g in range(group):
            h = kvh * group + g
            q = q_ref[0, :, h * HEAD_DIM:(h + 1) * HEAD_DIM]
            s = lax.dot_general(q, kk, NT, preferred_element_type=F32) * (HEAD_DIM ** -0.5)
            s = s + bias_ref[which, h]
            sink = sink_ref[h]
            m = jnp.maximum(jnp.max(s, axis=-1, keepdims=True), sink)
            p = jnp.exp(s - m)
            denom = jnp.sum(p, axis=-1, keepdims=True) + jnp.exp(sink - m)
            o = jnp.dot(p.astype(BF16), vv, preferred_element_type=F32) / denom
            outs.append(o.astype(BF16))
    o_ref[0] = jnp.concatenate(outs, axis=-1)


def _swa_bias(bt_a):
    qpos = jnp.arange(A_BLOCK)[:, None] + A_BLOCK
    kpos = jnp.arange(2 * A_BLOCK)[None, :]
    dist = qpos - kpos
    band = (dist >= 0) & (dist < A_WINDOW)
    bias = jnp.transpose(bt_a[_rel_bucket(dist)].astype(F32), (2, 0, 1))
    general = jnp.where(band[None], bias, NEG)
    first = jnp.where((band & (kpos >= A_BLOCK))[None], bias, NEG)
    return jnp.stack([first, general])


def swa(p_a, bias, sinks):
    bsz, seq, _ = p_a.shape
    qw, kw = A_HEADS * HEAD_DIM, A_KV_HEADS * HEAD_DIM
    kblk, vblk = qw // kw, qw // kw + 1
    return pl.pallas_call(
        _swa_kernel,
        grid=(bsz, seq // A_BLOCK),
        in_specs=[pl.BlockSpec((1, A_BLOCK, qw), lambda b, i: (b, i, 0)),
                  pl.BlockSpec((1, A_BLOCK, kw), lambda b, i: (b, jnp.maximum(i - 1, 0), kblk)),
                  pl.BlockSpec((1, A_BLOCK, kw), lambda b, i: (b, i, kblk)),
                  pl.BlockSpec((1, A_BLOCK, kw), lambda b, i: (b, jnp.maximum(i - 1, 0), vblk)),
                  pl.BlockSpec((1, A_BLOCK, kw), lambda b, i: (b, i, vblk)),
                  pl.BlockSpec(bias.shape, lambda b, i: (0, 0, 0, 0)),
                  pl.BlockSpec(memory_space=pltpu.SMEM)],
        out_specs=pl.BlockSpec((1, A_BLOCK, qw), lambda b, i: (b, i, 0)),
        out_shape=jax.ShapeDtypeStruct((bsz, seq, qw), BF16),
        compiler_params=_params(("parallel", "arbitrary")),
        name="swa",
    )(p_a, p_a, p_a, p_a, p_a, bias, sinks)


def _moba_kernel(q_ref, k_ref, vt_ref, win_ref, o_ref, tiles_ref, kmean_ref, sel_ref):
    b, i = pl.program_id(0), pl.program_id(1)
    n_heads, n_blk = kmean_ref.shape[0], kmean_ref.shape[1]

    @pl.when((b == 0) & (i == 0))
    def _():
        _gen_bias_tiles(win_ref, tiles_ref)

    @pl.when(i == 0)
    def _():
        for h in range(n_heads):
            kmean_ref[h] = jnp.mean(k_ref[0, h].astype(F32), axis=1)

    row = lax.broadcasted_iota(I32, (n_blk, TILE), 0)
    for h in range(n_heads):
        q = q_ref[0, h]
        kmean = kmean_ref[h]
        kmean_hi = kmean.astype(BF16)
        kmean_lo = (kmean - kmean_hi.astype(F32)).astype(BF16)
        gate = (lax.dot_general(kmean_hi, q, NT, preferred_element_type=F32)
                + lax.dot_general(kmean_lo, q, NT, preferred_element_type=F32))
        gate = jnp.where(row < i, gate, NEG)
        sel = jnp.zeros(gate.shape, F32)
        for _ in range(B_TOPK):
            best = jnp.max(gate, axis=0, keepdims=True)
            first = jnp.min(jnp.where(gate == best, row, n_blk), axis=0, keepdims=True)
            pick = row == first
            sel = jnp.where(pick, 1.0, sel)
            gate = jnp.where(pick, -jnp.inf, gate)
        sel_ref[h] = jnp.where(row < i, sel, 0.0)

    key_pos = lax.broadcasted_iota(I32, (TILE, TILE), 0)
    qry_pos = lax.broadcasted_iota(I32, (TILE, TILE), 1)
    causal = jnp.where(qry_pos >= key_pos, 1.0, 0.0)

    def body(j, carry):
        out = []
        for h in range(n_heads):
            m_old, l_old, acc_old = carry[h]
            s = lax.dot_general(k_ref[0, h, j], q_ref[0, h], NT, preferred_element_type=F32) * (HEAD_DIM ** -0.5)
            s = s + tiles_ref[h, i - j]
            chosen = jnp.broadcast_to(sel_ref[h, pl.ds(j, 1), :], (TILE, TILE))
            ok = jnp.where(j == i, causal, chosen) > 0.0
            s = jnp.where(ok, s, NEG)
            m_new = jnp.maximum(m_old, jnp.max(s, axis=0, keepdims=True))
            alpha = jnp.exp(m_old - m_new)
            p = jnp.where(ok, jnp.exp(s - m_new), 0.0)
            l_new = alpha * l_old + jnp.sum(p, axis=0, keepdims=True)
            acc_new = alpha * acc_old + jnp.dot(vt_ref[0, h, j], p.astype(BF16), preferred_element_type=F32)
            out.append((m_new, l_new, acc_new))
        return tuple(out)

    init = tuple((jnp.full((1, TILE), NEG, F32), jnp.zeros((1, TILE), F32), jnp.zeros((HEAD_DIM, TILE), F32))
                 for _ in range(n_heads))
    final = lax.fori_loop(0, i + 1, body, init)
    for h in range(n_heads):
        _, l, acc = final[h]
        o_ref[0, h] = (acc / l).astype(o_ref.dtype)


def moba(q, k, vt, windows):
    bsz, n_heads, seq, d = q.shape
    n_blk = seq // TILE
    return pl.pallas_call(
        _moba_kernel,
        grid=(bsz, n_blk),
        in_specs=[pl.BlockSpec((1, n_heads, TILE, d), lambda b, i: (b, 0, i, 0)),
                  pl.BlockSpec((1, n_heads, n_blk, TILE, d), lambda b, i: (b, 0, 0, 0, 0)),
                  pl.BlockSpec((1, n_heads, n_blk, d, TILE), lambda b, i: (b, 0, 0, 0, 0)),
                  pl.BlockSpec(windows.shape, lambda b, i: (0, 0, 0))],
        out_specs=pl.BlockSpec((1, n_heads, d, TILE), lambda b, i: (b, 0, 0, i)),
        out_shape=jax.ShapeDtypeStruct((bsz, n_heads, d, seq), BF16),
        scratch_shapes=[pltpu.VMEM((n_heads, n_blk, TILE, TILE), F32),
                        pltpu.VMEM((n_heads, n_blk, d), F32),
                        pltpu.VMEM((n_heads, n_blk, TILE), F32)],
        compiler_params=_params(("arbitrary", "arbitrary")),
        name="moba",
    )(q, k, vt, windows)


def _dsa_kernel(qc_ref, iq_ref, mq_ref, clat_ref, mk_ref, gain_ref, wuk_ref, wuvt_ref, win_ref, o_ref,
                tiles_ref, cn_ref, cnt_ref, ik_ref, keys_ref, mask_ref, qabs_ref, *, top_k):
    b, c = pl.program_id(0), pl.program_id(1)
    n_blk = cn_ref.shape[0]
    n_kb = c + 1

    @pl.when((b == 0) & (c == 0))
    def _():
        _gen_bias_tiles(win_ref, tiles_ref)

    @pl.when(c == 0)
    def _():
        lat = _rms(clat_ref[0], gain_ref[...])
        for kb in range(n_blk):
            blk = lat[kb * TILE:(kb + 1) * TILE]
            cn_ref[kb] = blk.astype(BF16)
            cnt_ref[kb] = jnp.transpose(blk).astype(BF16)
            ik_ref[kb] = mk_ref[0, kb * TILE:(kb + 1) * TILE, :IDX_DIM].astype(BF16)

    key_in = lax.broadcasted_iota(I32, (TILE, TILE), 0)
    qry_in = lax.broadcasted_iota(I32, (TILE, TILE), 1)
    diag_causal = key_in <= qry_in

    iq = iq_ref[0].astype(BF16)
    iw_t = jnp.transpose(mq_ref[0])[IDX_DIM:IDX_DIM + IDX_HEADS, :]

    def score_block(kb, carry):
        score = jnp.zeros((TILE, TILE), F32)
        for hh in range(IDX_HEADS):
            logits = lax.dot_general(ik_ref[kb], iq[:, hh * IDX_DIM:(hh + 1) * IDX_DIM], NT,
                                     preferred_element_type=F32)
            score = score + jnp.maximum(logits, 0.0) * iw_t[hh:hh + 1, :]
        score = score * ((IDX_DIM ** -0.5) * (IDX_HEADS ** -0.5))
        score = jnp.where(score == 0.0, 0.0, score)
        key = _sortable(score)
        keys_ref[kb] = jnp.where(kb < c, key, jnp.where(diag_causal, key, INT_MIN))
        return carry

    lax.fori_loop(0, n_kb, score_block, 0)

    def count(pred):
        def blk(kb, cnt):
            return cnt + jnp.sum(pred(keys_ref[kb], kb), axis=0, keepdims=True)
        return lax.fori_loop(0, n_kb, blk, jnp.zeros((1, TILE), F32))

    def value_bit(t, tau):
        cand = tau + lax.shift_left(jnp.int32(1), 31 - t)
        cnt = count(lambda key, kb: jnp.where(key >= cand, 1.0, 0.0))
        return jnp.where(cnt >= top_k, cand, tau)

    tau = lax.fori_loop(0, 32, value_bit, jnp.full((1, TILE), INT_MIN, I32))
    need = top_k - count(lambda key, kb: jnp.where(key > tau, 1.0, 0.0))

    def index_bit(t, bound):
        cand = bound + lax.shift_left(jnp.int32(1), 11 - t)
        cnt = count(lambda key, kb: jnp.where(key == tau, jnp.where(kb * TILE + key_in < cand, 1.0, 0.0), 0.0))
        return jnp.where(cnt <= need, cand, bound)

    at_or_above = count(lambda key, kb: jnp.where(key >= tau, 1.0, 0.0))
    surplus = jnp.max(jnp.where(tau > INT_MIN, at_or_above - top_k, 0.0))
    bound = lax.cond(surplus > 0.0,
                     lambda: lax.fori_loop(0, 12, index_bit, jnp.zeros((1, TILE), I32)),
                     lambda: jnp.full((1, TILE), 2 ** 30, I32))

    def mask_block(kb, carry):
        key = keys_ref[kb]
        tied = jnp.where(key == tau, jnp.where(kb * TILE + key_in < bound, 1.0, 0.0), 0.0)
        chosen = jnp.where(key > tau, 1.0, tied)
        mask_ref[kb] = jnp.where(kb < c, chosen, jnp.where(diag_causal, chosen, 0.0))
        return carry

    lax.fori_loop(0, n_kb, mask_block, 0)

    qc = qc_ref[0].astype(BF16)
    for h in range(C_HEADS):
        qabs_ref[h] = lax.dot_general(wuk_ref[h], qc[:, h * HEAD_DIM:(h + 1) * HEAD_DIM], NT,
                                      preferred_element_type=F32).astype(BF16)

    def attend_block(kb, carry):
        ok = mask_ref[kb] > 0.0
        out = []
        for h in range(C_HEADS):
            m_old, l_old, acc_old = carry[h]
            s = jnp.dot(cn_ref[kb], qabs_ref[h], preferred_element_type=F32) * (HEAD_DIM ** -0.5)
            s = jnp.where(ok, s + tiles_ref[h, c - kb], NEG)
            m_new = jnp.maximum(m_old, jnp.max(s, axis=0, keepdims=True))
            alpha = jnp.exp(m_old - m_new)
            p = jnp.where(ok, jnp.exp(s - m_new), 0.0)
            l_new = alpha * l_old + jnp.sum(p, axis=0, keepdims=True)
            acc_new = alpha * acc_old + jnp.dot(cnt_ref[kb], p.astype(BF16), preferred_element_type=F32)
            out.append((m_new, l_new, acc_new))
        return tuple(out)

    init = tuple((jnp.full((1, TILE), NEG, F32), jnp.zeros((1, TILE), F32), jnp.zeros((C_LATENT, TILE), F32))
                 for _ in range(C_HEADS))
    final = lax.fori_loop(0, n_kb, attend_block, init)
    for h in range(C_HEADS):
        _, l, acc = final[h]
        oc_t = (acc / l).astype(BF16)
        out_t = jnp.dot(wuvt_ref[h], oc_t, preferred_element_type=F32)
        o_ref[0, h * HEAD_DIM:(h + 1) * HEAD_DIM, :] = out_t.astype(o_ref.dtype)


def dsa(p_c, gain, w_uk, w_uv_t, windows):
    bsz, seq, _ = p_c.shape
    top_k = min(C_TOPK_MAX, seq // 4)
    qw = C_HEADS * HEAD_DIM
    n_blk = seq // TILE
    return pl.pallas_call(
        functools.partial(_dsa_kernel, top_k=top_k),
        grid=(bsz, n_blk),
        in_specs=[pl.BlockSpec((1, TILE, qw), lambda b, c: (b, c, 0)),
                  pl.BlockSpec((1, TILE, 256), lambda b, c: (b, c, 2)),
                  pl.BlockSpec((1, TILE, 128), lambda b, c: (b, c, 7)),
                  pl.BlockSpec((1, seq, 128), lambda b, c: (b, 0, 6)),
                  pl.BlockSpec((1, seq, 128), lambda b, c: (b, 0, 7)),
                  pl.BlockSpec((1, C_LATENT), lambda b, c: (0, 0)),
                  pl.BlockSpec(w_uk.shape, lambda b, c: (0, 0, 0)),
                  pl.BlockSpec(w_uv_t.shape, lambda b, c: (0, 0, 0)),
                  pl.BlockSpec(windows.shape, lambda b, c: (0, 0, 0))],
        out_specs=pl.BlockSpec((1, qw, TILE), lambda b, c: (b, 0, c)),
        out_shape=jax.ShapeDtypeStruct((bsz, qw, seq), BF16),
        scratch_shapes=[pltpu.VMEM((C_HEADS, n_blk, TILE, TILE), F32),
                        pltpu.VMEM((n_blk, TILE, C_LATENT), BF16),
                        pltpu.VMEM((n_blk, C_LATENT, TILE), BF16),
                        pltpu.VMEM((n_blk, TILE, IDX_DIM), BF16),
                        pltpu.VMEM((n_blk, TILE, TILE), I32),
                        pltpu.VMEM((n_blk, TILE, TILE), F32),
                        pltpu.VMEM((C_HEADS, C_LATENT, TILE), BF16)],
        compiler_params=_params(("arbitrary", "arbitrary")),
        name="dsa",
    )(p_c, p_c, p_c, p_c, p_c, gain.reshape(1, C_LATENT), w_uk, w_uv_t, windows)


ROUTE_TM = 256
ROUTE_ROWS = 72


def _top_values(s, count):
    vals, mult = [], []
    for _ in range(count):
        best = jnp.max(s, axis=0, keepdims=True)
        hit = s == best
        vals.append(best)
        mult.append(jnp.sum(jnp.where(hit, 1.0, 0.0), axis=0, keepdims=True))
        s = jnp.where(hit, -jnp.inf, s)
    return jnp.concatenate(vals, axis=0), jnp.concatenate(mult, axis=0)


def _route_kernel(q_ref, k_ref, s1_ref, s2_ref, sc_ref, keys_ref, wts_ref):
    q = q_ref[...]
    tm = q.shape[0]
    half = PEER_QDIM // 2
    s1 = lax.dot_general(k_ref[0, 0], q[:, :half], NT, preferred_element_type=F32)
    s2 = lax.dot_general(k_ref[0, 1], q[:, half:], NT, preferred_element_type=F32)
    s1_ref[0] = s1
    s2_ref[0] = s2
    t1, w1 = _top_values(s1, PEER_TOPK)
    t2, w2 = _top_values(s2, PEER_TOPK)

    r16 = lax.broadcasted_iota(I32, (16, tm), 0)
    r8 = lax.broadcasted_iota(I32, (8, tm), 0)
    pieces = [
        (t1[0:1] + t2, w1[0:1] * w2),
        (t1[1:2] + t2[:8], w1[1:2] * w2[:8]),
        (t1[2:3] + t2[:8], jnp.where(r8 < 5, w1[2:3] * w2[:8], 0.0)),
        (t1[3:4] + t2[:8], jnp.where(r8 < 4, w1[3:4] * w2[:8], 0.0)),
        (t2[0:1] + t1, jnp.where(r16 >= 4, w2[0:1] * w1, 0.0)),
        (t2[1:2] + t1[:8], jnp.where(r8 >= 4, w2[1:2] * w1[:8], 0.0)),
        (t2[2:3] + t1[:8], jnp.where(r8 == 4, w2[2:3] * w1[:8], 0.0)),
    ]
    cand = jnp.concatenate([p[0] for p in pieces], axis=0)
    wts = jnp.concatenate([p[1] for p in pieces], axis=0)
    keys_ref[...] = _sortable(cand)
    wts_ref[...] = wts

    def value_bit(t, tau):
        c = tau + lax.shift_left(jnp.int32(1), 31 - t)
        cnt = jnp.sum(jnp.where(keys_ref[...] >= c, wts_ref[...], 0.0), axis=0, keepdims=True)
        return jnp.where(cnt >= PEER_TOPK, c, tau)

    tau_key = lax.fori_loop(0, 32, value_bit, jnp.full((1, tm), INT_MIN, I32))
    top = t1[0:1] + t2[0:1]
    z = jnp.sum(jnp.where(keys_ref[...] >= tau_key, wts * jnp.exp(cand - top), 0.0), axis=0, keepdims=True)
    rows = [_unsortable(tau_key), t2[0:1], t1[0:1] + jnp.log(z)]
    sc_ref[0] = jnp.concatenate(rows + [jnp.zeros_like(z)] * (8 - len(rows)), axis=0)


def peer_route(qp, sub_keys):
    n = qp.shape[0]
    tm = ROUTE_TM
    half = PEER_QDIM // 2
    return pl.pallas_call(
        _route_kernel,
        grid=(n // tm, PEER_HEADS),
        in_specs=[pl.BlockSpec((tm, PEER_QDIM), lambda i, h: (i, h)),
                  pl.BlockSpec((1, 2, PEER_KEYS, half), lambda i, h: (h, 0, 0, 0))],
        out_specs=[pl.BlockSpec((1, PEER_KEYS, tm), lambda i, h: (h, 0, i)),
                   pl.BlockSpec((1, PEER_KEYS, tm), lambda i, h: (h, 0, i)),
                   pl.BlockSpec((1, 8, tm), lambda i, h: (h, 0, i))],
        out_shape=[jax.ShapeDtypeStruct((PEER_HEADS, PEER_KEYS, n), F32),
                   jax.ShapeDtypeStruct((PEER_HEADS, PEER_KEYS, n), F32),
                   jax.ShapeDtypeStruct((PEER_HEADS, 8, n), F32)],
        scratch_shapes=[pltpu.VMEM((ROUTE_ROWS, tm), I32),
                        pltpu.VMEM((ROUTE_ROWS, tm), F32)],
        compiler_params=_params(("parallel", "arbitrary")),
        name="peer_route",
    )(qp, sub_keys)


def _peer_kernel(x_ref, xn_ref, s1_ref, s2_ref, sc_ref, u_hbm, vt_hbm, o_ref,
                 acc_ref, e2_ref, s2c_ref, scc_ref, xnt_ref, act0_ref, act1_ref, hid0_ref, hid1_ref,
                 ubuf_ref, vbuf_ref, sem_ref):
    i, j = pl.program_id(0), pl.program_id(1)
    n_i, n_j = pl.num_programs(0), pl.num_programs(1)
    n_chunks = n_j - 2
    te = ubuf_ref.shape[1]
    n_sub = te // PEER_KEYS

    def table_copies(step, slot):
        cu = jnp.minimum(step, n_chunks - 1)
        cv = jnp.clip(step - 2, 0, n_chunks - 1)
        return (pltpu.make_async_copy(u_hbm.at[pl.ds(cu * te, te), :], ubuf_ref.at[slot], sem_ref.at[0, slot]),
                pltpu.make_async_copy(vt_hbm.at[cv], vbuf_ref.at[slot], sem_ref.at[1, slot]))

    @pl.when((i == 0) & (j == 0))
    def _():
        for cp in table_copies(0, 0):
            cp.start()

    @pl.when(j == 0)
    def _():
        acc_ref[...] = jnp.zeros_like(acc_ref)
        act1_ref[...] = jnp.zeros_like(act1_ref)
        hid0_ref[...] = jnp.zeros_like(hid0_ref)
        xnt_ref[...] = jnp.transpose(xn_ref[...].astype(F32)).astype(BF16)
        s2c_ref[...] = s2_ref[...]
        scc_ref[...] = sc_ref[...]
        for h in range(PEER_HEADS):
            e2_ref[h] = jnp.exp(s2_ref[h] - sc_ref[h, 1:2, :])

    def step(slot, act_w, act_r, hid_w, hid_r):
        for cp in table_copies(j, slot):
            cp.wait()

        @pl.when((i < n_i - 1) | (j < n_j - 1))
        def _():
            nxt = jnp.where(j == n_j - 1, 0, j + 1)
            for cp in table_copies(nxt, 1 - slot):
                cp.start()

        s1_rows = [[s1_ref[h, 0, ab:ab + 1, :] for h in range(PEER_HEADS)] for ab in range(n_sub)]
        for ab in range(n_sub):
            rows = slice(ab * PEER_KEYS, (ab + 1) * PEER_KEYS)
            c1_rows = [jnp.exp(s1_rows[ab][h] - scc_ref[h, 2:3, :]) for h in range(PEER_HEADS)]
            for tc in range(act_r.shape[1] // LANES):
                cols = slice(tc * LANES, (tc + 1) * LANES)
                gate = jnp.zeros((PEER_KEYS, LANES), F32)
                for h in range(PEER_HEADS):
                    total = s2c_ref[h, :, cols] + s1_rows[ab][h][:, cols]
                    weight = e2_ref[h, :, cols] * c1_rows[h][:, cols]
                    gate = gate + jnp.where(total >= scc_ref[h, 0:1, cols], weight, 0.0)
                hid_w[rows, cols] = (gate * jax.nn.gelu(act_r[rows, cols])).astype(BF16)
        act_w[...] = jnp.dot(ubuf_ref[slot], xnt_ref[...], preferred_element_type=F32)
        acc_ref[...] += jnp.dot(vbuf_ref[slot], hid_r[...], preferred_element_type=F32)

    @pl.when(j % 2 == 0)
    def _():
        step(0, act0_ref, act1_ref, hid1_ref, hid0_ref)

    @pl.when(j % 2 == 1)
    def _():
        step(1, act1_ref, act0_ref, hid0_ref, hid1_ref)

    @pl.when(j == n_j - 1)
    def _():
        o_ref[...] = x_ref[...] + jnp.transpose(acc_ref[...])


PEER_TE = 512


def peer(x, xn, s1, s2, sc, u, vt, tm=512):
    n, d = x.shape
    te = PEER_TE
    n_chunks = PEER_N // te
    n_sub = te // PEER_KEYS
    assert n_chunks % 2 == 0
    s1_chunks = s1.reshape(PEER_HEADS, n_chunks, n_sub, n)
    return pl.pallas_call(
        _peer_kernel,
        grid=(n // tm, n_chunks + 2),
        in_specs=[pl.BlockSpec((tm, d), lambda i, j: (i, 0)),
                  pl.BlockSpec((tm, d), lambda i, j: (i, 0)),
                  pl.BlockSpec((PEER_HEADS, 1, n_sub, tm), lambda i, j: (0, jnp.clip(j - 1, 0, n_chunks - 1), 0, i)),
                  pl.BlockSpec((PEER_HEADS, PEER_KEYS, tm), lambda i, j: (0, 0, i)),
                  pl.BlockSpec((PEER_HEADS, 8, tm), lambda i, j: (0, 0, i)),
                  pl.BlockSpec(memory_space=pl.ANY),
                  pl.BlockSpec(memory_space=pl.ANY)],
        out_specs=pl.BlockSpec((tm, d), lambda i, j: (i, 0)),
        out_shape=jax.ShapeDtypeStruct((n, d), F32),
        scratch_shapes=[pltpu.VMEM((d, tm), F32),
                        pltpu.VMEM((PEER_HEADS, PEER_KEYS, tm), F32),
                        pltpu.VMEM((PEER_HEADS, PEER_KEYS, tm), F32),
                        pltpu.VMEM((PEER_HEADS, 8, tm), F32),
                        pltpu.VMEM((d, tm), BF16),
                        pltpu.VMEM((te, tm), F32),
                        pltpu.VMEM((te, tm), F32),
                        pltpu.VMEM((te, tm), BF16),
                        pltpu.VMEM((te, tm), BF16),
                        pltpu.VMEM((2, te, d), BF16),
                        pltpu.VMEM((2, d, te), BF16),
                        pltpu.SemaphoreType.DMA((2, 2))],
        compiler_params=_params(("arbitrary", "arbitrary")),
        name="peer",
    )(x, xn, s1_chunks, s2, sc, u, vt)


def kernel(x, rel_bias, g_mix, w_in, a_sinks, c_lat_gain, c_w_uk, c_w_uv, w_br_a, w_br_b, w_br_c, w_out,
           g_ffn, peer_w_q, peer_sub_keys, peer_u, peer_v, g_final):
    bsz, seq, d = x.shape
    n = bsz * seq
    depth = w_in.shape[0]
    n_blk = seq // TILE
    off = [0] + [int(o) for o in np.cumsum(IN_SPLITS)]
    bt_a = rel_bias[:, :A_HEADS]
    bt_b = rel_bias[:, A_HEADS:A_HEADS + B_HEADS]
    bt_c = rel_bias[:, A_HEADS + B_HEADS:]
    bias_a = _swa_bias(bt_a)
    win_b = _bias_windows(bt_b, seq)
    win_c = _bias_windows(bt_c, seq)

    xf = x.reshape(n, d)
    for l in range(depth):
        w = w_in[l]
        pad = jnp.zeros((d, C_COLS - (off[11] - off[6])), F32)
        w_abc = jnp.concatenate([w[:, off[0]:off[6]],
                                 w[:, off[6]:off[7]], w[:, off[8]:off[9]], w[:, off[7]:off[8]],
                                 w[:, off[9]:off[11]], pad], axis=1).astype(BF16)
        w_g = w[:, off[11]:off[14]].astype(BF16)

        p_a, p_b, p_c = in_proj(xf, g_mix[l], w_abc)
        p_a = p_a.reshape(bsz, seq, -1)
        p_b = p_b.reshape(bsz, seq, 3, B_HEADS, HEAD_DIM)
        p_c = p_c.reshape(bsz, seq, -1)

        ya = swa(p_a, bias_a, a_sinks[l]).reshape(n, -1)

        qb = jnp.transpose(p_b[:, :, 0], (0, 2, 1, 3))
        kb = jnp.transpose(p_b[:, :, 1], (0, 2, 1, 3)).reshape(bsz, B_HEADS, n_blk, TILE, HEAD_DIM)
        vbt = jnp.transpose(p_b[:, :, 2].reshape(bsz, n_blk, TILE, B_HEADS, HEAD_DIM), (0, 3, 1, 4, 2))
        yb_t = moba(qb, kb, vbt, win_b)
        yb = jnp.transpose(yb_t, (0, 3, 1, 2)).reshape(n, -1)

        w_uk = jnp.transpose(c_w_uk[l], (1, 0, 2)).astype(BF16)
        w_uv_t = jnp.transpose(c_w_uv[l], (1, 2, 0)).astype(BF16)
        yc_t = dsa(p_c, c_lat_gain[l], w_uk, w_uv_t, win_c)
        yc = jnp.transpose(yc_t, (0, 2, 1)).reshape(n, -1)

        merged = merge(xf, g_mix[l], ya, yb, yc, w_g, w_br_a[l].astype(BF16), w_br_b[l].astype(BF16),
                       w_br_c[l].astype(BF16))
        xf = matmul_res(merged, w_out[l].astype(BF16), xf)

        qp, xn = norm_matmul(xf, g_ffn[l], peer_w_q[l].astype(BF16), BF16)
        s1, s2, sc = peer_route(qp, peer_sub_keys[l].astype(BF16))
        vt = jnp.transpose(peer_v[l].reshape(PEER_N // PEER_TE, PEER_TE, d), (0, 2, 1)).astype(BF16)
        xf = peer(xf, xn, s1, s2, sc, peer_u[l].astype(BF16), vt)

    return final_norm(xf, g_final).reshape(bsz, seq, d)
```

```python
import functools
import math

import jax
import jax.numpy as jnp
import numpy as np
from jax import lax
from jax.experimental import pallas as pl
from jax.experimental.pallas import tpu as pltpu

F32 = jnp.float32
BF16 = jnp.bfloat16
I32 = jnp.int32

D_MODEL = 2048
HEAD_DIM = 64
A_HEADS, A_KV_HEADS, A_BLOCK, A_WINDOW = 16, 4, 128, 128
B_HEADS, B_BLOCK, B_TOPK = 8, 256, 3
C_HEADS, C_LATENT, IDX_HEADS, IDX_DIM, C_TOPK_MAX = 8, 128, 8, 32, 256
REL_BUCKETS, REL_MAX_DIST = 32, 2048
PEER_HEADS, PEER_KEYS, PEER_QDIM, PEER_TOPK = 8, 128, 256, 16
PEER_N = PEER_KEYS * PEER_KEYS
EPS = 1e-6
NEG = -1e30
INT_MIN = -(2 ** 31)
IN_SPLITS = (A_HEADS * HEAD_DIM, A_KV_HEADS * HEAD_DIM, A_KV_HEADS * HEAD_DIM,
             B_HEADS * HEAD_DIM, B_HEADS * HEAD_DIM, B_HEADS * HEAD_DIM,
             C_HEADS * HEAD_DIM, C_LATENT, IDX_HEADS * IDX_DIM, IDX_DIM, IDX_HEADS,
             D_MODEL, D_MODEL, D_MODEL)
A_COLS = sum(IN_SPLITS[0:3])
B_COLS = sum(IN_SPLITS[3:6])
C_COLS = 1024

TILE = 256
LANES = 128
VMEM_LIMIT = 56 * 1024 * 1024
NT = (((1,), (1,)), ((), ()))


def _params(sem, vmem=VMEM_LIMIT, flags=None):
    return pltpu.CompilerParams(dimension_semantics=sem, vmem_limit_bytes=vmem, flags=flags)


def _rms(x, g):
    return x * lax.rsqrt(jnp.mean(x * x, axis=-1, keepdims=True) + EPS) * g


def _sortable(v):
    bits = pltpu.bitcast(v, I32)
    return jnp.where(bits < 0, bits ^ jnp.int32(0x7FFFFFFF), bits)


def _unsortable(k):
    return pltpu.bitcast(jnp.where(k < 0, k ^ jnp.int32(0x7FFFFFFF), k), F32)


def _in_proj_kernel(x_ref, g_ref, w_ref, oa_ref, ob_ref, oc_ref, h_ref, *, na, nb):
    j = pl.program_id(1)

    @pl.when(j == 0)
    def _():
        h_ref[...] = _rms(x_ref[...], g_ref[...]).astype(BF16)

    y = jnp.dot(h_ref[...], w_ref[...], preferred_element_type=F32)

    @pl.when(j < na)
    def _():
        oa_ref[...] = y.astype(oa_ref.dtype)

    @pl.when((j >= na) & (j < na + nb))
    def _():
        ob_ref[...] = y.astype(ob_ref.dtype)

    @pl.when(j >= na + nb)
    def _():
        oc_ref[...] = y.astype(oc_ref.dtype)


def in_proj(x, g, w, tm=512, tn=512):
    n, d = x.shape
    na, nb, nc = A_COLS // tn, B_COLS // tn, C_COLS // tn
    return pl.pallas_call(
        functools.partial(_in_proj_kernel, na=na, nb=nb),
        grid=(n // tm, na + nb + nc),
        in_specs=[pl.BlockSpec((tm, d), lambda i, j: (i, 0)),
                  pl.BlockSpec((1, d), lambda i, j: (0, 0)),
                  pl.BlockSpec((d, tn), lambda i, j: (0, j))],
        out_specs=[pl.BlockSpec((tm, tn), lambda i, j: (i, jnp.minimum(j, na - 1))),
                   pl.BlockSpec((tm, tn), lambda i, j: (i, jnp.clip(j - na, 0, nb - 1))),
                   pl.BlockSpec((tm, tn), lambda i, j: (i, jnp.maximum(j - na - nb, 0)))],
        out_shape=[jax.ShapeDtypeStruct((n, A_COLS), BF16),
                   jax.ShapeDtypeStruct((n, B_COLS), BF16),
                   jax.ShapeDtypeStruct((n, C_COLS), F32)],
        scratch_shapes=[pltpu.VMEM((tm, d), BF16)],
        compiler_params=_params(("parallel", "arbitrary")),
        name="in_proj",
    )(x, g.reshape(1, d), w)


def _norm_matmul_kernel(x_ref, g_ref, w_ref, o_ref, xn_ref, h_ref):
    @pl.when(pl.program_id(1) == 0)
    def _():
        h = _rms(x_ref[...], g_ref[...]).astype(BF16)
        h_ref[...] = h
        xn_ref[...] = h

    o_ref[...] = jnp.dot(h_ref[...], w_ref[...], preferred_element_type=F32).astype(o_ref.dtype)


def norm_matmul(x, g, w, out_dtype, tm=512, tn=512):
    n, d = x.shape
    ncol = w.shape[1]
    return pl.pallas_call(
        _norm_matmul_kernel,
        grid=(n // tm, ncol // tn),
        in_specs=[pl.BlockSpec((tm, d), lambda i, j: (i, 0)),
                  pl.BlockSpec((1, d), lambda i, j: (0, 0)),
                  pl.BlockSpec((d, tn), lambda i, j: (0, j))],
        out_specs=[pl.BlockSpec((tm, tn), lambda i, j: (i, j)),
                   pl.BlockSpec((tm, d), lambda i, j: (i, 0))],
        out_shape=[jax.ShapeDtypeStruct((n, ncol), out_dtype),
                   jax.ShapeDtypeStruct((n, d), BF16)],
        scratch_shapes=[pltpu.VMEM((tm, d), BF16)],
        compiler_params=_params(("parallel", "arbitrary")),
        name="norm_matmul",
    )(x, g.reshape(1, d), w)


def _matmul_res_kernel(a_ref, w_ref, r_ref, o_ref):
    o_ref[...] = r_ref[...] + jnp.dot(a_ref[...], w_ref[...], preferred_element_type=F32)


def matmul_res(a, w, res, tm=512, tn=512):
    n, k = a.shape
    ncol = w.shape[1]
    return pl.pallas_call(
        _matmul_res_kernel,
        grid=(n // tm, ncol // tn),
        in_specs=[pl.BlockSpec((tm, k), lambda i, j: (i, 0)),
                  pl.BlockSpec((k, tn), lambda i, j: (0, j)),
                  pl.BlockSpec((tm, tn), lambda i, j: (i, j))],
        out_specs=pl.BlockSpec((tm, tn), lambda i, j: (i, j)),
        out_shape=jax.ShapeDtypeStruct((n, ncol), F32),
        compiler_params=_params(("parallel", "arbitrary")),
        name="matmul_res",
    )(a, w, res)


def _merge_kernel(x_ref, g_ref, ya_ref, yb_ref, yc_ref, wga_ref, wgb_ref, wgc_ref,
                  wa_ref, wb_ref, wc_ref, o_ref, h_ref):
    @pl.when(pl.program_id(1) == 0)
    def _():
        h_ref[...] = _rms(x_ref[...], g_ref[...]).astype(BF16)

    h = h_ref[...]

    def branch(y_ref, wg_ref, w_ref):
        gate = jnp.dot(h, wg_ref[...], preferred_element_type=F32)
        proj = jnp.dot(y_ref[...], w_ref[...], preferred_element_type=F32)
        return jax.nn.sigmoid(gate) * proj

    merged = branch(ya_ref, wga_ref, wa_ref) + branch(yb_ref, wgb_ref, wb_ref) + branch(yc_ref, wgc_ref, wc_ref)
    o_ref[...] = merged.astype(o_ref.dtype)


def merge(x, g, ya, yb, yc, w_gate, wa, wb, wc, tm=512, tn=512):
    n, d = x.shape
    nj = d // tn
    return pl.pallas_call(
        _merge_kernel,
        grid=(n // tm, nj),
        in_specs=[pl.BlockSpec((tm, d), lambda i, j: (i, 0)),
                  pl.BlockSpec((1, d), lambda i, j: (0, 0)),
                  pl.BlockSpec((tm, ya.shape[1]), lambda i, j: (i, 0)),
                  pl.BlockSpec((tm, yb.shape[1]), lambda i, j: (i, 0)),
                  pl.BlockSpec((tm, yc.shape[1]), lambda i, j: (i, 0)),
                  pl.BlockSpec((d, tn), lambda i, j: (0, j)),
                  pl.BlockSpec((d, tn), lambda i, j: (0, j + nj)),
                  pl.BlockSpec((d, tn), lambda i, j: (0, j + 2 * nj)),
                  pl.BlockSpec((wa.shape[0], tn), lambda i, j: (0, j)),
                  pl.BlockSpec((wb.shape[0], tn), lambda i, j: (0, j)),
                  pl.BlockSpec((wc.shape[0], tn), lambda i, j: (0, j))],
        out_specs=pl.BlockSpec((tm, tn), lambda i, j: (i, j)),
        out_shape=jax.ShapeDtypeStruct((n, d), BF16),
        scratch_shapes=[pltpu.VMEM((tm, d), BF16)],
        compiler_params=_params(("parallel", "arbitrary")),
        name="merge",
    )(x, g.reshape(1, d), ya, yb, yc, w_gate, w_gate, w_gate, wa, wb, wc)


def _final_norm_kernel(x_ref, g_ref, o_ref):
    o_ref[...] = _rms(x_ref[...], g_ref[...])


def final_norm(x, g, tm=512):
    n, d = x.shape
    return pl.pallas_call(
        _final_norm_kernel,
        grid=(n // tm,),
        in_specs=[pl.BlockSpec((tm, d), lambda i: (i, 0)), pl.BlockSpec((1, d), lambda i: (0, 0))],
        out_specs=pl.BlockSpec((tm, d), lambda i: (i, 0)),
        out_shape=jax.ShapeDtypeStruct((n, d), F32),
        compiler_params=_params(("parallel",)),
        name="final_norm",
    )(x, g.reshape(1, d))


def _rel_bucket(dist):
    d = jnp.maximum(dist, 0)
    max_exact = REL_BUCKETS // 2
    df = jnp.maximum(d, max_exact).astype(F32)
    large = max_exact + (jnp.log(df / max_exact) / math.log(REL_MAX_DIST / max_exact)
                         * (REL_BUCKETS - max_exact)).astype(I32)
    large = jnp.minimum(large, REL_BUCKETS - 1)
    return jnp.where(d < max_exact, d, large)


def _bucket_bias(bt, dist):
    onehot = jax.nn.one_hot(_rel_bucket(dist), REL_BUCKETS, dtype=F32)
    return jnp.einsum("...k,kh->...h", onehot, bt.astype(F32), precision=lax.Precision.HIGHEST)


def _bias_windows(bt, seq):
    dist = (jnp.arange(seq // TILE)[:, None] * TILE - (TILE - 1) + jnp.arange(2 * TILE)[None, :])
    w = _bucket_bias(bt, dist)
    w = jnp.where((dist >= 0)[..., None], w, 0.0)
    return jnp.transpose(w, (2, 0, 1))


def _gen_bias_tiles(win_ref, tiles_ref):
    n_heads, n_rel = win_ref.shape[0], win_ref.shape[1]

    def body(t, carry):
        h = t // n_rel
        r = t % n_rel
        row = win_ref[h, pl.ds(r, 1), :]
        x = jnp.broadcast_to(row, (TILE, 2 * TILE))
        tiles_ref[h, r] = pltpu.roll(x, TILE + 1, 1, stride=1, stride_axis=0)[:, :TILE]
        return carry

    lax.fori_loop(0, n_heads * n_rel, body, 0)


def _swa_kernel(q_ref, kp_ref, kc_ref, vp_ref, vc_ref, bias_ref, sink_ref, o_ref):
    i = pl.program_id(1)
    which = jnp.minimum(i, 1)
    group = A_HEADS // A_KV_HEADS
    outs = []
    for kvh in range(A_KV_HEADS):
        lo, hi = kvh * HEAD_DIM, (kvh + 1) * HEAD_DIM
        kk = jnp.concatenate([kp_ref[0, :, lo:hi], kc_ref[0, :, lo:hi]], axis=0)
        vv = jnp.concatenate([vp_ref[0, :, lo:hi], vc_ref[0, :, lo:hi]], axis=0)
        for g in range(group):
            h = kvh * group + g
            q = q_ref[0, :, h * HEAD_DIM:(h + 1) * HEAD_DIM]
            s = lax.dot_general(q, kk, NT, preferred_element_type=F32) * (HEAD_DIM ** -0.5)
            s = s + bias_ref[which, h]
            sink = sink_ref[h]
            m = jnp.maximum(jnp.max(s, axis=-1, keepdims=True), sink)
            p = jnp.exp(s - m)
            denom = jnp.sum(p, axis=-1, keepdims=True) + jnp.exp(sink - m)
            o = jnp.dot(p.astype(BF16), vv, preferred_element_type=F32) / denom
            outs.append(o.astype(BF16))
    o_ref[0] = jnp.concatenate(outs, axis=-1)


def _swa_bias(bt_a):
    qpos = jnp.arange(A_BLOCK)[:, None] + A_BLOCK
    kpos = jnp.arange(2 * A_BLOCK)[None, :]
    dist = qpos - kpos
    band = (dist >= 0) & (dist < A_WINDOW)
    bias = jnp.transpose(_bucket_bias(bt_a, dist), (2, 0, 1))
    general = jnp.where(band[None], bias, NEG)
    first = jnp.where((band & (kpos >= A_BLOCK))[None], bias, NEG)
    return jnp.stack([first, general])


def swa(p_a, bias, sinks):
    bsz, seq, _ = p_a.shape
    qw, kw = A_HEADS * HEAD_DIM, A_KV_HEADS * HEAD_DIM
    kblk, vblk = qw // kw, qw // kw + 1
    return pl.pallas_call(
        _swa_kernel,
        grid=(bsz, seq // A_BLOCK),
        in_specs=[pl.BlockSpec((1, A_BLOCK, qw), lambda b, i: (b, i, 0)),
                  pl.BlockSpec((1, A_BLOCK, kw), lambda b, i: (b, jnp.maximum(i - 1, 0), kblk)),
                  pl.BlockSpec((1, A_BLOCK, kw), lambda b, i: (b, i, kblk)),
                  pl.BlockSpec((1, A_BLOCK, kw), lambda b, i: (b, jnp.maximum(i - 1, 0), vblk)),
                  pl.BlockSpec((1, A_BLOCK, kw), lambda b, i: (b, i, vblk)),
                  pl.BlockSpec(bias.shape, lambda b, i: (0, 0, 0, 0)),
                  pl.BlockSpec(memory_space=pltpu.SMEM)],
        out_specs=pl.BlockSpec((1, A_BLOCK, qw), lambda b, i: (b, i, 0)),
        out_shape=jax.ShapeDtypeStruct((bsz, seq, qw), BF16),
        compiler_params=_params(("parallel", "arbitrary")),
        name="swa",
    )(p_a, p_a, p_a, p_a, p_a, bias, sinks)


def _moba_kernel(q_ref, k_ref, vt_ref, win_ref, o_ref, tiles_ref, kmean_ref, sel_ref):
    b, i = pl.program_id(0), pl.program_id(1)
    n_heads, n_blk = kmean_ref.shape[0], kmean_ref.shape[1]

    @pl.when((b == 0) & (i == 0))
    def _():
        _gen_bias_tiles(win_ref, tiles_ref)

    @pl.when(i == 0)
    def _():
        for h in range(n_heads):
            kmean_ref[h] = jnp.mean(k_ref[0, h].astype(F32), axis=1)

    row = lax.broadcasted_iota(I32, (n_blk, TILE), 0)
    for h in range(n_heads):
        q = q_ref[0, h]
        kmean = kmean_ref[h]
        kmean_hi = kmean.astype(BF16)
        kmean_lo = (kmean - kmean_hi.astype(F32)).astype(BF16)
        gate = (lax.dot_general(kmean_hi, q, NT, preferred_element_type=F32)
                + lax.dot_general(kmean_lo, q, NT, preferred_element_type=F32))
        gate = jnp.where(row < i, gate, NEG)
        sel = jnp.zeros(gate.shape, F32)
        for _ in range(B_TOPK):
            best = jnp.max(gate, axis=0, keepdims=True)
            first = jnp.min(jnp.where(gate == best, row, n_blk), axis=0, keepdims=True)
            pick = row == first
            sel = jnp.where(pick, 1.0, sel)
            gate = jnp.where(pick, -jnp.inf, gate)
        sel_ref[h] = jnp.where(row < i, sel, 0.0)

    key_pos = lax.broadcasted_iota(I32, (TILE, TILE), 0)
    qry_pos = lax.broadcasted_iota(I32, (TILE, TILE), 1)
    causal = jnp.where(qry_pos >= key_pos, 1.0, 0.0)

    def body(j, carry):
        out = []
        for h in range(n_heads):
            m_old, l_old, acc_old = carry[h]
            s = lax.dot_general(k_ref[0, h, j], q_ref[0, h], NT, preferred_element_type=F32) * (HEAD_DIM ** -0.5)
            s = s + tiles_ref[h, i - j]
            chosen = jnp.broadcast_to(sel_ref[h, pl.ds(j, 1), :], (TILE, TILE))
            ok = jnp.where(j == i, causal, chosen) > 0.0
            s = jnp.where(ok, s, NEG)
            m_new = jnp.maximum(m_old, jnp.max(s, axis=0, keepdims=True))
            alpha = jnp.exp(m_old - m_new)
            p = jnp.where(ok, jnp.exp(s - m_new), 0.0)
            l_new = alpha * l_old + jnp.sum(p, axis=0, keepdims=True)
            acc_new = alpha * acc_old + jnp.dot(vt_ref[0, h, j], p.astype(BF16), preferred_element_type=F32)
            out.append((m_new, l_new, acc_new))
        return tuple(out)

    init = tuple((jnp.full((1, TILE), NEG, F32), jnp.zeros((1, TILE), F32), jnp.zeros((HEAD_DIM, TILE), F32))
                 for _ in range(n_heads))
    final = lax.fori_loop(0, i + 1, body, init)
    for h in range(n_heads):
        _, l, acc = final[h]
        o_ref[0, h] = (acc / l).astype(o_ref.dtype)


def moba(q, k, vt, windows):
    bsz, n_heads, seq, d = q.shape
    n_blk = seq // TILE
    return pl.pallas_call(
        _moba_kernel,
        grid=(bsz, n_blk),
        in_specs=[pl.BlockSpec((1, n_heads, TILE, d), lambda b, i: (b, 0, i, 0)),
                  pl.BlockSpec((1, n_heads, n_blk, TILE, d), lambda b, i: (b, 0, 0, 0, 0)),
                  pl.BlockSpec((1, n_heads, n_blk, d, TILE), lambda b, i: (b, 0, 0, 0, 0)),
                  pl.BlockSpec(windows.shape, lambda b, i: (0, 0, 0))],
        out_specs=pl.BlockSpec((1, n_heads, d, TILE), lambda b, i: (b, 0, 0, i)),
        out_shape=jax.ShapeDtypeStruct((bsz, n_heads, d, seq), BF16),
        scratch_shapes=[pltpu.VMEM((n_heads, n_blk, TILE, TILE), F32),
                        pltpu.VMEM((n_heads, n_blk, d), F32),
                        pltpu.VMEM((n_heads, n_blk, TILE), F32)],
        compiler_params=_params(("arbitrary", "arbitrary")),
        name="moba",
    )(q, k, vt, windows)


def _dsa_kernel(qc_ref, iq_ref, mq_ref, clat_ref, mk_ref, gain_ref, wuk_ref, wuvt_ref, win_ref, o_ref,
                tiles_ref, cn_ref, cnt_ref, ik_ref, keys_ref, mask_ref, qabs_ref, *, top_k):
    b, c = pl.program_id(0), pl.program_id(1)
    n_blk = cn_ref.shape[0]
    n_kb = c + 1

    @pl.when((b == 0) & (c == 0))
    def _():
        _gen_bias_tiles(win_ref, tiles_ref)

    @pl.when(c == 0)
    def _():
        lat = _rms(clat_ref[0], gain_ref[...])
        for kb in range(n_blk):
            blk = lat[kb * TILE:(kb + 1) * TILE]
            cn_ref[kb] = blk.astype(BF16)
            cnt_ref[kb] = jnp.transpose(blk).astype(BF16)
            ik_ref[kb] = mk_ref[0, kb * TILE:(kb + 1) * TILE, :IDX_DIM].astype(BF16)

    key_in = lax.broadcasted_iota(I32, (TILE, TILE), 0)
    qry_in = lax.broadcasted_iota(I32, (TILE, TILE), 1)
    diag_causal = key_in <= qry_in

    iq = iq_ref[0].astype(BF16)
    iw_t = jnp.transpose(mq_ref[0])[IDX_DIM:IDX_DIM + IDX_HEADS, :]

    def score_block(kb, carry):
        score = jnp.zeros((TILE, TILE), F32)
        for hh in range(IDX_HEADS):
            logits = lax.dot_general(ik_ref[kb], iq[:, hh * IDX_DIM:(hh + 1) * IDX_DIM], NT,
                                     preferred_element_type=F32)
            score = score + jnp.maximum(logits, 0.0) * iw_t[hh:hh + 1, :]
        score = score * ((IDX_DIM ** -0.5) * (IDX_HEADS ** -0.5))
        score = jnp.where(score == 0.0, 0.0, score)
        key = _sortable(score)
        keys_ref[kb] = jnp.where(kb < c, key, jnp.where(diag_causal, key, INT_MIN))
        return carry

    lax.fori_loop(0, n_kb, score_block, 0)

    def count(pred):
        def blk(kb, cnt):
            return cnt + jnp.sum(pred(keys_ref[kb], kb), axis=0, keepdims=True)
        return lax.fori_loop(0, n_kb, blk, jnp.zeros((1, TILE), F32))

    def value_bit(t, tau):
        cand = tau + lax.shift_left(jnp.int32(1), 31 - t)
        cnt = count(lambda key, kb: jnp.where(key >= cand, 1.0, 0.0))
        return jnp.where(cnt >= top_k, cand, tau)

    tau = lax.fori_loop(0, 32, value_bit, jnp.full((1, TILE), INT_MIN, I32))
    need = top_k - count(lambda key, kb: jnp.where(key > tau, 1.0, 0.0))

    def index_bit(t, bound):
        cand = bound + lax.shift_left(jnp.int32(1), 11 - t)
        cnt = count(lambda key, kb: jnp.where(key == tau, jnp.where(kb * TILE + key_in < cand, 1.0, 0.0), 0.0))
        return jnp.where(cnt <= need, cand, bound)

    at_or_above = count(lambda key, kb: jnp.where(key >= tau, 1.0, 0.0))
    surplus = jnp.max(jnp.where(tau > INT_MIN, at_or_above - top_k, 0.0))
    bound = lax.cond(surplus > 0.0,
                     lambda: lax.fori_loop(0, 12, index_bit, jnp.zeros((1, TILE), I32)),
                     lambda: jnp.full((1, TILE), 2 ** 30, I32))

    def mask_block(kb, carry):
        key = keys_ref[kb]
        tied = jnp.where(key == tau, jnp.where(kb * TILE + key_in < bound, 1.0, 0.0), 0.0)
        chosen = jnp.where(key > tau, 1.0, tied)
        mask_ref[kb] = jnp.where(kb < c, chosen, jnp.where(diag_causal, chosen, 0.0))
        return carry

    lax.fori_loop(0, n_kb, mask_block, 0)

    qc = qc_ref[0].astype(BF16)
    for h in range(C_HEADS):
        qabs_ref[h] = lax.dot_general(wuk_ref[h], qc[:, h * HEAD_DIM:(h + 1) * HEAD_DIM], NT,
                                      preferred_element_type=F32).astype(BF16)

    def attend_block(kb, carry):
        ok = mask_ref[kb] > 0.0
        out = []
        for h in range(C_HEADS):
            m_old, l_old, acc_old = carry[h]
            s = jnp.dot(cn_ref[kb], qabs_ref[h], preferred_element_type=F32) * (HEAD_DIM ** -0.5)
            s = jnp.where(ok, s + tiles_ref[h, c - kb], NEG)
            m_new = jnp.maximum(m_old, jnp.max(s, axis=0, keepdims=True))
            alpha = jnp.exp(m_old - m_new)
            p = jnp.where(ok, jnp.exp(s - m_new), 0.0)
            l_new = alpha * l_old + jnp.sum(p, axis=0, keepdims=True)
            acc_new = alpha * acc_old + jnp.dot(cnt_ref[kb], p.astype(BF16), preferred_element_type=F32)
            out.append((m_new, l_new, acc_new))
        return tuple(out)

    init = tuple((jnp.full((1, TILE), NEG, F32), jnp.zeros((1, TILE), F32), jnp.zeros((C_LATENT, TILE), F32))
                 for _ in range(C_HEADS))
    final = lax.fori_loop(0, n_kb, attend_block, init)
    for h in range(C_HEADS):
        _, l, acc = final[h]
        oc_t = (acc / l).astype(BF16)
        out_t = jnp.dot(wuvt_ref[h], oc_t, preferred_element_type=F32)
        o_ref[0, h * HEAD_DIM:(h + 1) * HEAD_DIM, :] = out_t.astype(o_ref.dtype)


def dsa(p_c, gain, w_uk, w_uv_t, windows):
    bsz, seq, _ = p_c.shape
    top_k = min(C_TOPK_MAX, seq // 4)
    qw = C_HEADS * HEAD_DIM
    n_blk = seq // TILE
    return pl.pallas_call(
        functools.partial(_dsa_kernel, top_k=top_k),
        grid=(bsz, n_blk),
        in_specs=[pl.BlockSpec((1, TILE, qw), lambda b, c: (b, c, 0)),
                  pl.BlockSpec((1, TILE, 256), lambda b, c: (b, c, 2)),
                  pl.BlockSpec((1, TILE, 128), lambda b, c: (b, c, 7)),
                  pl.BlockSpec((1, seq, 128), lambda b, c: (b, 0, 6)),
                  pl.BlockSpec((1, seq, 128), lambda b, c: (b, 0, 7)),
                  pl.BlockSpec((1, C_LATENT), lambda b, c: (0, 0)),
                  pl.BlockSpec(w_uk.shape, lambda b, c: (0, 0, 0)),
                  pl.BlockSpec(w_uv_t.shape, lambda b, c: (0, 0, 0)),
                  pl.BlockSpec(windows.shape, lambda b, c: (0, 0, 0))],
        out_specs=pl.BlockSpec((1, qw, TILE), lambda b, c: (b, 0, c)),
        out_shape=jax.ShapeDtypeStruct((bsz, qw, seq), BF16),
        scratch_shapes=[pltpu.VMEM((C_HEADS, n_blk, TILE, TILE), F32),
                        pltpu.VMEM((n_blk, TILE, C_LATENT), BF16),
                        pltpu.VMEM((n_blk, C_LATENT, TILE), BF16),
                        pltpu.VMEM((n_blk, TILE, IDX_DIM), BF16),
                        pltpu.VMEM((n_blk, TILE, TILE), I32),
                        pltpu.VMEM((n_blk, TILE, TILE), F32),
                        pltpu.VMEM((C_HEADS, C_LATENT, TILE), BF16)],
        compiler_params=_params(("arbitrary", "arbitrary")),
        name="dsa",
    )(p_c, p_c, p_c, p_c, p_c, gain.reshape(1, C_LATENT), w_uk, w_uv_t, windows)


ROUTE_TM = 256
ROUTE_ROWS = 72


def _top_values(s, count):
    vals, mult = [], []
    for _ in range(count):
        best = jnp.max(s, axis=0, keepdims=True)
        hit = s == best
        vals.append(best)
        mult.append(jnp.sum(jnp.where(hit, 1.0, 0.0), axis=0, keepdims=True))
        s = jnp.where(hit, -jnp.inf, s)
    return jnp.concatenate(vals, axis=0), jnp.concatenate(mult, axis=0)


def _route_kernel(q_ref, k_ref, c1_ref, r1_ref, e2_ref, reach_ref, keys_ref, wts_ref):
    q = q_ref[...]
    tm = q.shape[0]
    half = PEER_QDIM // 2
    s1 = lax.dot_general(k_ref[0, 0], q[:, :half], NT, preferred_element_type=F32)
    s2 = lax.dot_general(k_ref[0, 1], q[:, half:], NT, preferred_element_type=F32)
    t1, w1 = _top_values(s1, PEER_TOPK)
    t2, w2 = _top_values(s2, PEER_TOPK)

    r16 = lax.broadcasted_iota(I32, (16, tm), 0)
    r8 = lax.broadcasted_iota(I32, (8, tm), 0)
    pieces = [
        (t1[0:1] + t2, w1[0:1] * w2),
        (t1[1:2] + t2[:8], w1[1:2] * w2[:8]),
        (t1[2:3] + t2[:8], jnp.where(r8 < 5, w1[2:3] * w2[:8], 0.0)),
        (t1[3:4] + t2[:8], jnp.where(r8 < 4, w1[3:4] * w2[:8], 0.0)),
        (t2[0:1] + t1, jnp.where(r16 >= 4, w2[0:1] * w1, 0.0)),
        (t2[1:2] + t1[:8], jnp.where(r8 >= 4, w2[1:2] * w1[:8], 0.0)),
        (t2[2:3] + t1[:8], jnp.where(r8 == 4, w2[2:3] * w1[:8], 0.0)),
    ]
    cand = jnp.concatenate([p[0] for p in pieces], axis=0)
    wts = jnp.concatenate([p[1] for p in pieces], axis=0)
    keys_ref[...] = _sortable(cand)
    wts_ref[...] = wts

    def value_bit(t, tau):
        c = tau + lax.shift_left(jnp.int32(1), 31 - t)
        cnt = jnp.sum(jnp.where(keys_ref[...] >= c, wts_ref[...], 0.0), axis=0, keepdims=True)
        return jnp.where(cnt >= PEER_TOPK, c, tau)

    tau_key = lax.fori_loop(0, 32, value_bit, jnp.full((1, tm), INT_MIN, I32))
    tau = _unsortable(tau_key)
    top = t1[0:1] + t2[0:1]
    z = jnp.sum(jnp.where(keys_ref[...] >= tau_key, wts * jnp.exp(cand - top), 0.0), axis=0, keepdims=True)

    rank = jnp.zeros(s1.shape, F32)
    reach = jnp.zeros(s2.shape, F32)
    for r in range(PEER_TOPK):
        rank = rank + jnp.where(s1 < t1[r:r + 1], 1.0, 0.0)
        reach = reach + jnp.where(t1[r:r + 1] + s2 >= tau, 1.0, 0.0)
    c1_ref[0] = jnp.exp(s1 - (t1[0:1] + jnp.log(z)))
    r1_ref[0] = rank
    e2_ref[0] = jnp.exp(s2 - t2[0:1]).astype(BF16)
    reach_ref[0] = reach.astype(BF16)


def peer_route(qp, sub_keys):
    n = qp.shape[0]
    tm = ROUTE_TM
    half = PEER_QDIM // 2
    spec = pl.BlockSpec((1, PEER_KEYS, tm), lambda i, h: (h, 0, i))
    return pl.pallas_call(
        _route_kernel,
        grid=(n // tm, PEER_HEADS),
        in_specs=[pl.BlockSpec((tm, PEER_QDIM), lambda i, h: (i, h)),
                  pl.BlockSpec((1, 2, PEER_KEYS, half), lambda i, h: (h, 0, 0, 0))],
        out_specs=[spec, spec, spec, spec],
        out_shape=[jax.ShapeDtypeStruct((PEER_HEADS, PEER_KEYS, n), F32),
                   jax.ShapeDtypeStruct((PEER_HEADS, PEER_KEYS, n), F32),
                   jax.ShapeDtypeStruct((PEER_HEADS, PEER_KEYS, n), BF16),
                   jax.ShapeDtypeStruct((PEER_HEADS, PEER_KEYS, n), BF16)],
        scratch_shapes=[pltpu.VMEM((ROUTE_ROWS, tm), I32),
                        pltpu.VMEM((ROUTE_ROWS, tm), F32)],
        compiler_params=_params(("parallel", "arbitrary")),
        name="peer_route",
    )(qp, sub_keys)


PEER_TE = 512
PACK = 16


def _peer_kernel(x_ref, xn_ref, c1_ref, r1_ref, e2_ref, reach_ref, u_ref, vt_ref, o_ref, acc_ref, xnt_ref, hid_ref):
    j = pl.program_id(1)
    n_sub = u_ref.shape[0] // PEER_KEYS
    tm = xn_ref.shape[0]

    @pl.when(j == 0)
    def _():
        acc_ref[...] = jnp.zeros_like(acc_ref)
        xnt_ref[...] = jnp.transpose(xn_ref[...].astype(F32)).astype(BF16)

    act = jnp.dot(u_ref[...], xnt_ref[...], preferred_element_type=F32)
    for ab in range(n_sub):
        rows = slice(ab * PEER_KEYS, (ab + 1) * PEER_KEYS)
        rank = [jnp.broadcast_to(r1_ref[h, 0, ab:ab + 1, :], (PACK, tm)).astype(BF16) for h in range(PEER_HEADS)]
        fac = [jnp.broadcast_to(c1_ref[h, 0, ab:ab + 1, :], (PACK, tm)).astype(BF16) for h in range(PEER_HEADS)]
        for tc in range(tm // LANES):
            cols = slice(tc * LANES, (tc + 1) * LANES)
            gate = jnp.zeros((PEER_KEYS, LANES), BF16)
            for h in range(PEER_HEADS):
                chosen = jnp.tile(rank[h][:, cols], (PEER_KEYS // PACK, 1)) < reach_ref[h, :, cols]
                weight = e2_ref[h, :, cols] * jnp.tile(fac[h][:, cols], (PEER_KEYS // PACK, 1))
                gate = gate + jnp.where(chosen, weight, jnp.zeros_like(weight))
            hid_ref[rows, cols] = gate * jax.nn.gelu(act[rows, cols]).astype(BF16)
    acc_ref[...] += jnp.dot(vt_ref[...], hid_ref[...], preferred_element_type=F32)

    @pl.when(j == pl.num_programs(1) - 1)
    def _():
        o_ref[...] = x_ref[...] + jnp.transpose(acc_ref[...])


def peer(x, xn, c1, r1, e2, reach, u, vt, tm=512):
    n, d = x.shape
    te = PEER_TE
    n_chunks = PEER_N // te
    n_sub = te // PEER_KEYS
    c1_chunks = c1.reshape(PEER_HEADS, n_chunks, n_sub, n)
    r1_chunks = r1.reshape(PEER_HEADS, n_chunks, n_sub, n)
    chunk_spec = pl.BlockSpec((PEER_HEADS, 1, n_sub, tm), lambda i, j: (0, j, 0, i))
    tile_spec = pl.BlockSpec((PEER_HEADS, PEER_KEYS, tm), lambda i, j: (0, 0, i))
    return pl.pallas_call(
        _peer_kernel,
        grid=(n // tm, n_chunks),
        in_specs=[pl.BlockSpec((tm, d), lambda i, j: (i, 0)),
                  pl.BlockSpec((tm, d), lambda i, j: (i, 0)),
                  chunk_spec, chunk_spec, tile_spec, tile_spec,
                  pl.BlockSpec((te, d), lambda i, j: (j, 0)),
                  pl.BlockSpec((d, te), lambda i, j: (0, j))],
        out_specs=pl.BlockSpec((tm, d), lambda i, j: (i, 0)),
        out_shape=jax.ShapeDtypeStruct((n, d), F32),
        scratch_shapes=[pltpu.VMEM((d, tm), F32),
                        pltpu.VMEM((d, tm), BF16),
                        pltpu.VMEM((te, tm), BF16)],
        compiler_params=_params(("parallel", "arbitrary")),
        name="peer",
    )(x, xn, c1_chunks, r1_chunks, e2, reach, u, vt)


def kernel(x, rel_bias, g_mix, w_in, a_sinks, c_lat_gain, c_w_uk, c_w_uv, w_br_a, w_br_b, w_br_c, w_out,
           g_ffn, peer_w_q, peer_sub_keys, peer_u, peer_v, g_final):
    bsz, seq, d = x.shape
    n = bsz * seq
    depth = w_in.shape[0]
    n_blk = seq // TILE
    off = [0] + [int(o) for o in np.cumsum(IN_SPLITS)]
    bt_a = rel_bias[:, :A_HEADS]
    bt_b = rel_bias[:, A_HEADS:A_HEADS + B_HEADS]
    bt_c = rel_bias[:, A_HEADS + B_HEADS:]
    bias_a = _swa_bias(bt_a)
    win_b = _bias_windows(bt_b, seq)
    win_c = _bias_windows(bt_c, seq)

    xf = x.reshape(n, d)
    for l in range(depth):
        w = w_in[l]
        pad = jnp.zeros((d, C_COLS - (off[11] - off[6])), F32)
        w_abc = jnp.concatenate([w[:, off[0]:off[6]],
                                 w[:, off[6]:off[7]], w[:, off[8]:off[9]], w[:, off[7]:off[8]],
                                 w[:, off[9]:off[11]], pad], axis=1).astype(BF16)
        w_g = w[:, off[11]:off[14]].astype(BF16)

        p_a, p_b, p_c = in_proj(xf, g_mix[l], w_abc)
        p_a = p_a.reshape(bsz, seq, -1)
        p_b = p_b.reshape(bsz, seq, 3, B_HEADS, HEAD_DIM)
        p_c = p_c.reshape(bsz, seq, -1)

        ya = swa(p_a, bias_a, a_sinks[l]).reshape(n, -1)

        qb = jnp.transpose(p_b[:, :, 0], (0, 2, 1, 3))
        kb = jnp.transpose(p_b[:, :, 1], (0, 2, 1, 3)).reshape(bsz, B_HEADS, n_blk, TILE, HEAD_DIM)
        vbt = jnp.transpose(p_b[:, :, 2].reshape(bsz, n_blk, TILE, B_HEADS, HEAD_DIM), (0, 3, 1, 4, 2))
        yb_t = moba(qb, kb, vbt, win_b)
        yb = jnp.transpose(yb_t, (0, 3, 1, 2)).reshape(n, -1)

        w_uk = jnp.transpose(c_w_uk[l], (1, 0, 2)).astype(BF16)
        w_uv_t = jnp.transpose(c_w_uv[l], (1, 2, 0)).astype(BF16)
        yc_t = dsa(p_c, c_lat_gain[l], w_uk, w_uv_t, win_c)
        yc = jnp.transpose(yc_t, (0, 2, 1)).reshape(n, -1)

        merged = merge(xf, g_mix[l], ya, yb, yc, w_g, w_br_a[l].astype(BF16), w_br_b[l].astype(BF16),
                       w_br_c[l].astype(BF16))
        xf = matmul_res(merged, w_out[l].astype(BF16), xf)

        qp, xn = norm_matmul(xf, g_ffn[l], peer_w_q[l].astype(BF16), BF16)
        c1, r1, e2, reach = peer_route(qp, peer_sub_keys[l].astype(BF16))
        xf = peer(xf, xn, c1, r1, e2, reach, peer_u[l].astype(BF16), jnp.transpose(peer_v[l]).astype(BF16))

    return final_norm(xf, g_final).reshape(bsz, seq, d)
```

```python
import functools
import math

import jax
import jax.numpy as jnp
import numpy as np
from jax import lax
from jax.experimental import pallas as pl
from jax.experimental.pallas import tpu as pltpu

F32 = jnp.float32
BF16 = jnp.bfloat16
I32 = jnp.int32

D_MODEL = 2048
HEAD_DIM = 64
A_HEADS, A_KV_HEADS, A_BLOCK, A_WINDOW = 16, 4, 128, 128
B_HEADS, B_BLOCK, B_TOPK = 8, 256, 3
C_HEADS, C_LATENT, IDX_HEADS, IDX_DIM, C_TOPK_MAX = 8, 128, 8, 32, 256
REL_BUCKETS, REL_MAX_DIST = 32, 2048
PEER_HEADS, PEER_KEYS, PEER_QDIM, PEER_TOPK = 8, 128, 256, 16
PEER_N = PEER_KEYS * PEER_KEYS
EPS = 1e-6
NEG = -1e30
INT_MIN = -(2 ** 31)
IN_SPLITS = (A_HEADS * HEAD_DIM, A_KV_HEADS * HEAD_DIM, A_KV_HEADS * HEAD_DIM,
             B_HEADS * HEAD_DIM, B_HEADS * HEAD_DIM, B_HEADS * HEAD_DIM,
             C_HEADS * HEAD_DIM, C_LATENT, IDX_HEADS * IDX_DIM, IDX_DIM, IDX_HEADS,
             D_MODEL, D_MODEL, D_MODEL)
A_COLS = sum(IN_SPLITS[0:3])
B_COLS = sum(IN_SPLITS[3:6])
C_COLS = 1024

TILE = 256
LANES = 128
LOCKSTEP = 4
VMEM_LIMIT = 56 * 1024 * 1024
NT = (((1,), (1,)), ((), ()))


def _params(sem, vmem=VMEM_LIMIT, flags=None):
    return pltpu.CompilerParams(dimension_semantics=sem, vmem_limit_bytes=vmem, flags=flags)


def _rms(x, g):
    return x * lax.rsqrt(jnp.mean(x * x, axis=-1, keepdims=True) + EPS) * g


def _sortable(v):
    bits = pltpu.bitcast(v, I32)
    return jnp.where(bits < 0, bits ^ jnp.int32(0x7FFFFFFF), bits)


def _unsortable(k):
    return pltpu.bitcast(jnp.where(k < 0, k ^ jnp.int32(0x7FFFFFFF), k), F32)


def _in_proj_kernel(x_ref, g_ref, w_ref, oab_ref, oc_ref, h_ref, *, nab):
    j = pl.program_id(1)

    @pl.when(j == 0)
    def _():
        h_ref[...] = _rms(x_ref[...], g_ref[...]).astype(BF16)

    y = jnp.dot(h_ref[...], w_ref[...], preferred_element_type=F32)

    @pl.when(j < nab)
    def _():
        oab_ref[...] = y.astype(oab_ref.dtype)

    @pl.when(j >= nab)
    def _():
        oc_ref[...] = y.astype(oc_ref.dtype)


def in_proj(x, g, w, tm=512, tn=1024):
    n, d = x.shape
    nab, nc = (A_COLS + B_COLS) // tn, C_COLS // tn
    return pl.pallas_call(
        functools.partial(_in_proj_kernel, nab=nab),
        grid=(n // tm, nab + nc),
        in_specs=[pl.BlockSpec((tm, d), lambda i, j: (i, 0)),
                  pl.BlockSpec((1, d), lambda i, j: (0, 0)),
                  pl.BlockSpec((d, tn), lambda i, j: (0, j))],
        out_specs=[pl.BlockSpec((tm, tn), lambda i, j: (i, jnp.minimum(j, nab - 1))),
                   pl.BlockSpec((tm, tn), lambda i, j: (i, jnp.maximum(j - nab, 0)))],
        out_shape=[jax.ShapeDtypeStruct((n, A_COLS + B_COLS), BF16),
                   jax.ShapeDtypeStruct((n, C_COLS), F32)],
        scratch_shapes=[pltpu.VMEM((tm, d), BF16)],
        compiler_params=_params(("parallel", "arbitrary")),
        name="in_proj",
    )(x, g.reshape(1, d), w)


def _norm_matmul_kernel(x_ref, g_ref, w_ref, o_ref, xn_ref, h_ref):
    @pl.when(pl.program_id(1) == 0)
    def _():
        h = _rms(x_ref[...], g_ref[...]).astype(BF16)
        h_ref[...] = h
        xn_ref[...] = h

    o_ref[...] = jnp.dot(h_ref[...], w_ref[...], preferred_element_type=F32).astype(o_ref.dtype)


def norm_matmul(x, g, w, out_dtype, tm=512, tn=1024):
    n, d = x.shape
    ncol = w.shape[1]
    return pl.pallas_call(
        _norm_matmul_kernel,
        grid=(n // tm, ncol // tn),
        in_specs=[pl.BlockSpec((tm, d), lambda i, j: (i, 0)),
                  pl.BlockSpec((1, d), lambda i, j: (0, 0)),
                  pl.BlockSpec((d, tn), lambda i, j: (0, j))],
        out_specs=[pl.BlockSpec((tm, tn), lambda i, j: (i, j)),
                   pl.BlockSpec((tm, d), lambda i, j: (i, 0))],
        out_shape=[jax.ShapeDtypeStruct((n, ncol), out_dtype),
                   jax.ShapeDtypeStruct((n, d), BF16)],
        scratch_shapes=[pltpu.VMEM((tm, d), BF16)],
        compiler_params=_params(("parallel", "arbitrary")),
        name="norm_matmul",
    )(x, g.reshape(1, d), w)


def _matmul_res_kernel(a_ref, w_ref, r_ref, o_ref):
    o_ref[...] = r_ref[...] + jnp.dot(a_ref[...], w_ref[...], preferred_element_type=F32)


def matmul_res(a, w, res, tm=512, tn=1024):
    n, k = a.shape
    ncol = w.shape[1]
    return pl.pallas_call(
        _matmul_res_kernel,
        grid=(n // tm, ncol // tn),
        in_specs=[pl.BlockSpec((tm, k), lambda i, j: (i, 0)),
                  pl.BlockSpec((k, tn), lambda i, j: (0, j)),
                  pl.BlockSpec((tm, tn), lambda i, j: (i, j))],
        out_specs=pl.BlockSpec((tm, tn), lambda i, j: (i, j)),
        out_shape=jax.ShapeDtypeStruct((n, ncol), F32),
        compiler_params=_params(("parallel", "arbitrary")),
        name="matmul_res",
    )(a, w, res)


def _merge_kernel(x_ref, g_ref, ya_ref, yb_ref, yc_ref, wga_ref, wgb_ref, wgc_ref,
                  wa_ref, wb_ref, wc_ref, o_ref, h_ref):
    @pl.when(pl.program_id(1) == 0)
    def _():
        h_ref[...] = _rms(x_ref[...], g_ref[...]).astype(BF16)

    h = h_ref[...]

    def branch(y_ref, wg_ref, w_ref):
        gate = jnp.dot(h, wg_ref[...], preferred_element_type=F32)
        proj = jnp.dot(y_ref[...], w_ref[...], preferred_element_type=F32)
        return jax.nn.sigmoid(gate) * proj

    merged = branch(ya_ref, wga_ref, wa_ref) + branch(yb_ref, wgb_ref, wb_ref) + branch(yc_ref, wgc_ref, wc_ref)
    o_ref[...] = merged.astype(o_ref.dtype)


def merge(x, g, ya, yb, yc, w_gate, wa, wb, wc, tm=512, tn=512):
    n, d = x.shape
    nj = d // tn
    return pl.pallas_call(
        _merge_kernel,
        grid=(n // tm, nj),
        in_specs=[pl.BlockSpec((tm, d), lambda i, j: (i, 0)),
                  pl.BlockSpec((1, d), lambda i, j: (0, 0)),
                  pl.BlockSpec((tm, ya.shape[1]), lambda i, j: (i, 0)),
                  pl.BlockSpec((tm, yb.shape[1]), lambda i, j: (i, 0)),
                  pl.BlockSpec((tm, yc.shape[1]), lambda i, j: (i, 0)),
                  pl.BlockSpec((d, tn), lambda i, j: (0, j)),
                  pl.BlockSpec((d, tn), lambda i, j: (0, j + nj)),
                  pl.BlockSpec((d, tn), lambda i, j: (0, j + 2 * nj)),
                  pl.BlockSpec((wa.shape[0], tn), lambda i, j: (0, j)),
                  pl.BlockSpec((wb.shape[0], tn), lambda i, j: (0, j)),
                  pl.BlockSpec((wc.shape[0], tn), lambda i, j: (0, j))],
        out_specs=pl.BlockSpec((tm, tn), lambda i, j: (i, j)),
        out_shape=jax.ShapeDtypeStruct((n, d), BF16),
        scratch_shapes=[pltpu.VMEM((tm, d), BF16)],
        compiler_params=_params(("parallel", "arbitrary")),
        name="merge",
    )(x, g.reshape(1, d), ya, yb, yc, w_gate, w_gate, w_gate, wa, wb, wc)


def _final_norm_kernel(x_ref, g_ref, o_ref):
    o_ref[...] = _rms(x_ref[...], g_ref[...])


def final_norm(x, g, tm=512):
    n, d = x.shape
    return pl.pallas_call(
        _final_norm_kernel,
        grid=(n // tm,),
        in_specs=[pl.BlockSpec((tm, d), lambda i: (i, 0)), pl.BlockSpec((1, d), lambda i: (0, 0))],
        out_specs=pl.BlockSpec((tm, d), lambda i: (i, 0)),
        out_shape=jax.ShapeDtypeStruct((n, d), F32),
        compiler_params=_params(("parallel",)),
        name="final_norm",
    )(x, g.reshape(1, d))


def _rel_bucket(dist):
    d = jnp.maximum(dist, 0)
    max_exact = REL_BUCKETS // 2
    df = jnp.maximum(d, max_exact).astype(F32)
    large = max_exact + (jnp.log(df / max_exact) / math.log(REL_MAX_DIST / max_exact)
                         * (REL_BUCKETS - max_exact)).astype(I32)
    large = jnp.minimum(large, REL_BUCKETS - 1)
    return jnp.where(d < max_exact, d, large)


def _bucket_bias(bt, dist):
    onehot = jax.nn.one_hot(_rel_bucket(dist), REL_BUCKETS, dtype=F32)
    return jnp.einsum("...k,kh->...h", onehot, bt.astype(F32), precision=lax.Precision.HIGHEST)


def _bias_windows(bt, seq):
    dist = (jnp.arange(seq // TILE)[:, None] * TILE - (TILE - 1) + jnp.arange(2 * TILE)[None, :])
    w = _bucket_bias(bt, dist)
    w = jnp.where((dist >= 0)[..., None], w, 0.0)
    return jnp.transpose(w, (2, 0, 1))


def _gen_bias_tiles(win_ref, tiles_ref):
    n_heads, n_rel = win_ref.shape[0], win_ref.shape[1]

    def body(t, carry):
        h = t // n_rel
        r = t % n_rel
        row = win_ref[h, pl.ds(r, 1), :]
        x = jnp.broadcast_to(row, (TILE, 2 * TILE))
        tiles_ref[h, r] = pltpu.roll(x, TILE + 1, 1, stride=1, stride_axis=0)[:, :TILE]
        return carry

    lax.fori_loop(0, n_heads * n_rel, body, 0)


def _swa_kernel(q_ref, kp_ref, kc_ref, vp_ref, vc_ref, bias_ref, sink_ref, o_ref):
    i = pl.program_id(1)
    which = jnp.minimum(i, 1)
    group = A_HEADS // A_KV_HEADS
    outs = []
    for kvh in range(A_KV_HEADS):
        lo, hi = kvh * HEAD_DIM, (kvh + 1) * HEAD_DIM
        kk = jnp.concatenate([kp_ref[0, :, lo:hi], kc_ref[0, :, lo:hi]], axis=0)
        vv = jnp.concatenate([vp_ref[0, :, lo:hi], vc_ref[0, :, lo:hi]], axis=0)
        for g in range(group):
            h = kvh * group + g
            q = q_ref[0, :, h * HEAD_DIM:(h + 1) * HEAD_DIM]
            s = lax.dot_general(q, kk, NT, preferred_element_type=F32) * (HEAD_DIM ** -0.5)
            s = s + bias_ref[which, h]
            sink = sink_ref[h]
            m = jnp.maximum(jnp.max(s, axis=-1, keepdims=True), sink)
            p = jnp.exp(s - m)
            denom = jnp.sum(p, axis=-1, keepdims=True) + jnp.exp(sink - m)
            o = jnp.dot(p.astype(BF16), vv, preferred_element_type=F32) / denom
            outs.append(o.astype(BF16))
    o_ref[0] = jnp.concatenate(outs, axis=-1)


def _swa_bias(bt_a):
    qpos = jnp.arange(A_BLOCK)[:, None] + A_BLOCK
    kpos = jnp.arange(2 * A_BLOCK)[None, :]
    dist = qpos - kpos
    band = (dist >= 0) & (dist < A_WINDOW)
    bias = jnp.transpose(_bucket_bias(bt_a, dist), (2, 0, 1))
    general = jnp.where(band[None], bias, NEG)
    first = jnp.where((band & (kpos >= A_BLOCK))[None], bias, NEG)
    return jnp.stack([first, general])


def swa(p_a, bias, sinks):
    bsz, seq, _ = p_a.shape
    qw, kw = A_HEADS * HEAD_DIM, A_KV_HEADS * HEAD_DIM
    kblk, vblk = qw // kw, qw // kw + 1
    return pl.pallas_call(
        _swa_kernel,
        grid=(bsz, seq // A_BLOCK),
        in_specs=[pl.BlockSpec((1, A_BLOCK, qw), lambda b, i: (b, i, 0)),
                  pl.BlockSpec((1, A_BLOCK, kw), lambda b, i: (b, jnp.maximum(i - 1, 0), kblk)),
                  pl.BlockSpec((1, A_BLOCK, kw), lambda b, i: (b, i, kblk)),
                  pl.BlockSpec((1, A_BLOCK, kw), lambda b, i: (b, jnp.maximum(i - 1, 0), vblk)),
                  pl.BlockSpec((1, A_BLOCK, kw), lambda b, i: (b, i, vblk)),
                  pl.BlockSpec(bias.shape, lambda b, i: (0, 0, 0, 0)),
                  pl.BlockSpec(memory_space=pltpu.SMEM)],
        out_specs=pl.BlockSpec((1, A_BLOCK, qw), lambda b, i: (b, i, 0)),
        out_shape=jax.ShapeDtypeStruct((bsz, seq, qw), BF16),
        compiler_params=_params(("parallel", "arbitrary")),
        name="swa",
    )(p_a, p_a, p_a, p_a, p_a, bias, sinks)


def _moba_kernel(q_ref, k_ref, vt_ref, win_ref, o_ref, tiles_ref, kmean_ref, sel_ref):
    b, i = pl.program_id(0), pl.program_id(1)
    n_heads, n_blk = kmean_ref.shape[0], kmean_ref.shape[1]

    @pl.when((b == 0) & (i == 0))
    def _():
        _gen_bias_tiles(win_ref, tiles_ref)

    @pl.when(i == 0)
    def _():
        for h in range(n_heads):
            kmean_ref[h] = jnp.mean(k_ref[0, h].astype(F32), axis=1)

    row = lax.broadcasted_iota(I32, (n_blk, TILE), 0)
    for h in range(n_heads):
        q = q_ref[0, h]
        kmean = kmean_ref[h]
        kmean_hi = kmean.astype(BF16)
        kmean_lo = (kmean - kmean_hi.astype(F32)).astype(BF16)
        gate = (lax.dot_general(kmean_hi, q, NT, preferred_element_type=F32)
                + lax.dot_general(kmean_lo, q, NT, preferred_element_type=F32))
        gate = jnp.where(row < i, gate, NEG)
        sel = jnp.zeros(gate.shape, F32)
        for _ in range(B_TOPK):
            best = jnp.max(gate, axis=0, keepdims=True)
            first = jnp.min(jnp.where(gate == best, row, n_blk), axis=0, keepdims=True)
            pick = row == first
            sel = jnp.where(pick, 1.0, sel)
            gate = jnp.where(pick, -jnp.inf, gate)
        sel_ref[h] = jnp.where(row < i, sel, 0.0)

    key_pos = lax.broadcasted_iota(I32, (TILE, TILE), 0)
    qry_pos = lax.broadcasted_iota(I32, (TILE, TILE), 1)
    causal = jnp.where(qry_pos >= key_pos, 1.0, 0.0)

    def body(j, carry):
        out = []
        for g in range(0, n_heads, LOCKSTEP):
            pair = tuple(range(g, g + LOCKSTEP))
            s = [lax.dot_general(k_ref[0, h, j], q_ref[0, h], NT, preferred_element_type=F32) * (HEAD_DIM ** -0.5)
                 for h in pair]
            s = [s[k] + tiles_ref[pair[k], i - j] for k in range(LOCKSTEP)]
            ok = [jnp.where(j == i, causal, jnp.broadcast_to(sel_ref[h, pl.ds(j, 1), :], (TILE, TILE))) > 0.0
                  for h in pair]
            s = [jnp.where(ok[k], s[k], NEG) for k in range(LOCKSTEP)]
            m_new = [jnp.maximum(carry[pair[k]][0], jnp.max(s[k], axis=0, keepdims=True)) for k in range(LOCKSTEP)]
            alpha = [jnp.exp(carry[pair[k]][0] - m_new[k]) for k in range(LOCKSTEP)]
            p = [jnp.where(ok[k], jnp.exp(s[k] - m_new[k]), 0.0) for k in range(LOCKSTEP)]
            l_new = [alpha[k] * carry[pair[k]][1] + jnp.sum(p[k], axis=0, keepdims=True) for k in range(LOCKSTEP)]
            pv = [jnp.dot(vt_ref[0, pair[k], j], p[k].astype(BF16), preferred_element_type=F32) for k in range(LOCKSTEP)]
            for k in range(LOCKSTEP):
                out.append((m_new[k], l_new[k], alpha[k] * carry[pair[k]][2] + pv[k]))
        return tuple(out)

    init = tuple((jnp.full((1, TILE), NEG, F32), jnp.zeros((1, TILE), F32), jnp.zeros((HEAD_DIM, TILE), F32))
                 for _ in range(n_heads))
    final = lax.fori_loop(0, i + 1, body, init)
    for h in range(n_heads):
        _, l, acc = final[h]
        o_ref[0, h] = (acc / l).astype(o_ref.dtype)


def moba(q, k, vt, windows):
    bsz, n_heads, seq, d = q.shape
    n_blk = seq // TILE
    return pl.pallas_call(
        _moba_kernel,
        grid=(bsz, n_blk),
        in_specs=[pl.BlockSpec((1, n_heads, TILE, d), lambda b, i: (b, 0, i, 0)),
                  pl.BlockSpec((1, n_heads, n_blk, TILE, d), lambda b, i: (b, 0, 0, 0, 0)),
                  pl.BlockSpec((1, n_heads, n_blk, d, TILE), lambda b, i: (b, 0, 0, 0, 0)),
                  pl.BlockSpec(windows.shape, lambda b, i: (0, 0, 0))],
        out_specs=pl.BlockSpec((1, n_heads, d, TILE), lambda b, i: (b, 0, 0, i)),
        out_shape=jax.ShapeDtypeStruct((bsz, n_heads, d, seq), BF16),
        scratch_shapes=[pltpu.VMEM((n_heads, n_blk, TILE, TILE), F32),
                        pltpu.VMEM((n_heads, n_blk, d), F32),
                        pltpu.VMEM((n_heads, n_blk, TILE), F32)],
        compiler_params=_params(("arbitrary", "arbitrary")),
        name="moba",
    )(q, k, vt, windows)


def _dsa_kernel(qc_ref, iq_ref, mq_ref, clat_ref, mk_ref, gain_ref, wuk_ref, wuvt_ref, win_ref, o_ref,
                tiles_ref, cn_ref, cnt_ref, ik_ref, keys_ref, mask_ref, qabs_ref, *, top_k):
    b, c = pl.program_id(0), pl.program_id(1)
    n_blk = cn_ref.shape[0]
    n_kb = c + 1

    @pl.when((b == 0) & (c == 0))
    def _():
        _gen_bias_tiles(win_ref, tiles_ref)

    @pl.when(c == 0)
    def _():
        lat = _rms(clat_ref[0], gain_ref[...])
        for kb in range(n_blk):
            blk = lat[kb * TILE:(kb + 1) * TILE]
            cn_ref[kb] = blk.astype(BF16)
            cnt_ref[kb] = jnp.transpose(blk).astype(BF16)
            ik_ref[kb] = mk_ref[0, kb * TILE:(kb + 1) * TILE, :IDX_DIM].astype(BF16)

    key_in = lax.broadcasted_iota(I32, (TILE, TILE), 0)
    qry_in = lax.broadcasted_iota(I32, (TILE, TILE), 1)
    diag_causal = key_in <= qry_in

    iq = iq_ref[0].astype(BF16)
    iw_t = jnp.transpose(mq_ref[0])[IDX_DIM:IDX_DIM + IDX_HEADS, :]

    def score_block(kb, carry):
        score = jnp.zeros((TILE, TILE), F32)
        ik = ik_ref[kb]
        logits = [lax.dot_general(ik, iq[:, hh * IDX_DIM:(hh + 1) * IDX_DIM], NT, preferred_element_type=F32)
                  for hh in range(IDX_HEADS)]
        for hh in range(IDX_HEADS):
            score = score + jnp.maximum(logits[hh], 0.0) * iw_t[hh:hh + 1, :]
        score = score * ((IDX_DIM ** -0.5) * (IDX_HEADS ** -0.5))
        score = jnp.where(score == 0.0, 0.0, score)
        key = _sortable(score)
        keys_ref[kb] = jnp.where(kb < c, key, jnp.where(diag_causal, key, INT_MIN))
        return carry

    lax.fori_loop(0, n_kb, score_block, 0)

    def count(pred):
        def blk(kb, cnt):
            return cnt + jnp.sum(pred(keys_ref[kb], kb), axis=0, keepdims=True)
        return lax.fori_loop(0, n_kb, blk, jnp.zeros((1, TILE), F32))

    def value_bit(t, tau):
        cand = tau + lax.shift_left(jnp.int32(1), 31 - t)
        cnt = count(lambda key, kb: jnp.where(key >= cand, 1.0, 0.0))
        return jnp.where(cnt >= top_k, cand, tau)

    tau = lax.fori_loop(0, 32, value_bit, jnp.full((1, TILE), INT_MIN, I32))
    need = top_k - count(lambda key, kb: jnp.where(key > tau, 1.0, 0.0))

    def index_bit(t, bound):
        cand = bound + lax.shift_left(jnp.int32(1), 11 - t)
        cnt = count(lambda key, kb: jnp.where(key == tau, jnp.where(kb * TILE + key_in < cand, 1.0, 0.0), 0.0))
        return jnp.where(cnt <= need, cand, bound)

    at_or_above = count(lambda key, kb: jnp.where(key >= tau, 1.0, 0.0))
    surplus = jnp.max(jnp.where(tau > INT_MIN, at_or_above - top_k, 0.0))
    bound = lax.cond(surplus > 0.0,
                     lambda: lax.fori_loop(0, 12, index_bit, jnp.zeros((1, TILE), I32)),
                     lambda: jnp.full((1, TILE), 2 ** 30, I32))

    def mask_block(kb, carry):
        key = keys_ref[kb]
        tied = jnp.where(key == tau, jnp.where(kb * TILE + key_in < bound, 1.0, 0.0), 0.0)
        chosen = jnp.where(key > tau, 1.0, tied)
        mask_ref[kb] = jnp.where(kb < c, chosen, jnp.where(diag_causal, chosen, 0.0))
        return carry

    lax.fori_loop(0, n_kb, mask_block, 0)

    qc = qc_ref[0].astype(BF16)
    for h in range(C_HEADS):
        qabs_ref[h] = lax.dot_general(wuk_ref[h], qc[:, h * HEAD_DIM:(h + 1) * HEAD_DIM], NT,
                                      preferred_element_type=F32).astype(BF16)

    def attend_block(kb, carry):
        ok = mask_ref[kb] > 0.0
        keys_lat, lat_keys = cn_ref[kb], cnt_ref[kb]
        out = []
        for g in range(0, C_HEADS, LOCKSTEP):
            pair = tuple(range(g, g + LOCKSTEP))
            s = [jnp.dot(keys_lat, qabs_ref[h], preferred_element_type=F32) * (HEAD_DIM ** -0.5) for h in pair]
            s = [jnp.where(ok, s[k] + tiles_ref[pair[k], c - kb], NEG) for k in range(LOCKSTEP)]
            m_new = [jnp.maximum(carry[pair[k]][0], jnp.max(s[k], axis=0, keepdims=True)) for k in range(LOCKSTEP)]
            alpha = [jnp.exp(carry[pair[k]][0] - m_new[k]) for k in range(LOCKSTEP)]
            p = [jnp.where(ok, jnp.exp(s[k] - m_new[k]), 0.0) for k in range(LOCKSTEP)]
            l_new = [alpha[k] * carry[pair[k]][1] + jnp.sum(p[k], axis=0, keepdims=True) for k in range(LOCKSTEP)]
            pv = [jnp.dot(lat_keys, p[k].astype(BF16), preferred_element_type=F32) for k in range(LOCKSTEP)]
            for k in range(LOCKSTEP):
                out.append((m_new[k], l_new[k], alpha[k] * carry[pair[k]][2] + pv[k]))
        return tuple(out)

    init = tuple((jnp.full((1, TILE), NEG, F32), jnp.zeros((1, TILE), F32), jnp.zeros((C_LATENT, TILE), F32))
                 for _ in range(C_HEADS))
    final = lax.fori_loop(0, n_kb, attend_block, init)
    for h in range(C_HEADS):
        _, l, acc = final[h]
        oc_t = (acc / l).astype(BF16)
        out_t = jnp.dot(wuvt_ref[h], oc_t, preferred_element_type=F32)
        o_ref[0, h * HEAD_DIM:(h + 1) * HEAD_DIM, :] = out_t.astype(o_ref.dtype)


def dsa(p_c, gain, w_uk, w_uv_t, windows):
    bsz, seq, _ = p_c.shape
    top_k = min(C_TOPK_MAX, seq // 4)
    qw = C_HEADS * HEAD_DIM
    n_blk = seq // TILE
    return pl.pallas_call(
        functools.partial(_dsa_kernel, top_k=top_k),
        grid=(bsz, n_blk),
        in_specs=[pl.BlockSpec((1, TILE, qw), lambda b, c: (b, c, 0)),
                  pl.BlockSpec((1, TILE, 256), lambda b, c: (b, c, 2)),
                  pl.BlockSpec((1, TILE, 128), lambda b, c: (b, c, 7)),
                  pl.BlockSpec((1, seq, 128), lambda b, c: (b, 0, 6)),
                  pl.BlockSpec((1, seq, 128), lambda b, c: (b, 0, 7)),
                  pl.BlockSpec((1, C_LATENT), lambda b, c: (0, 0)),
                  pl.BlockSpec(w_uk.shape, lambda b, c: (0, 0, 0)),
                  pl.BlockSpec(w_uv_t.shape, lambda b, c: (0, 0, 0)),
                  pl.BlockSpec(windows.shape, lambda b, c: (0, 0, 0))],
        out_specs=pl.BlockSpec((1, qw, TILE), lambda b, c: (b, 0, c)),
        out_shape=jax.ShapeDtypeStruct((bsz, qw, seq), BF16),
        scratch_shapes=[pltpu.VMEM((C_HEADS, n_blk, TILE, TILE), F32),
                        pltpu.VMEM((n_blk, TILE, C_LATENT), BF16),
                        pltpu.VMEM((n_blk, C_LATENT, TILE), BF16),
                        pltpu.VMEM((n_blk, TILE, IDX_DIM), BF16),
                        pltpu.VMEM((n_blk, TILE, TILE), I32),
                        pltpu.VMEM((n_blk, TILE, TILE), F32),
                        pltpu.VMEM((C_HEADS, C_LATENT, TILE), BF16)],
        compiler_params=_params(("arbitrary", "arbitrary")),
        name="dsa",
    )(p_c, p_c, p_c, p_c, p_c, gain.reshape(1, C_LATENT), w_uk, w_uv_t, windows)


ROUTE_TM = 256
ROUTE_HEADS = 2
ROUTE_ROWS = 72


def _top_values(s, count):
    vals, mult = [], []
    for _ in range(count):
        best = jnp.max(s, axis=0, keepdims=True)
        hit = s == best
        vals.append(best)
        mult.append(jnp.sum(jnp.where(hit, 1.0, 0.0), axis=0, keepdims=True))
        s = jnp.where(hit, -jnp.inf, s)
    return jnp.concatenate(vals, axis=0), jnp.concatenate(mult, axis=0)


def _route_kernel(q_ref, k_ref, c1_ref, r1_ref, e2_ref, reach_ref, keys_ref, wts_ref):
    for hh in range(k_ref.shape[0]):
        _route_head(q_ref[:, hh * PEER_QDIM:(hh + 1) * PEER_QDIM], k_ref.at[hh], c1_ref.at[hh], r1_ref.at[hh],
                    e2_ref.at[hh], reach_ref.at[hh], keys_ref, wts_ref)


def _route_head(q, k_ref, c1_ref, r1_ref, e2_ref, reach_ref, keys_ref, wts_ref):
    tm = q.shape[0]
    half = PEER_QDIM // 2
    s1 = lax.dot_general(k_ref[0], q[:, :half], NT, preferred_element_type=F32)
    s2 = lax.dot_general(k_ref[1], q[:, half:], NT, preferred_element_type=F32)
    t1, w1 = _top_values(s1, PEER_TOPK)
    t2, w2 = _top_values(s2, PEER_TOPK)

    r16 = lax.broadcasted_iota(I32, (16, tm), 0)
    r8 = lax.broadcasted_iota(I32, (8, tm), 0)
    pieces = [
        (t1[0:1] + t2, w1[0:1] * w2),
        (t1[1:2] + t2[:8], w1[1:2] * w2[:8]),
        (t1[2:3] + t2[:8], jnp.where(r8 < 5, w1[2:3] * w2[:8], 0.0)),
        (t1[3:4] + t2[:8], jnp.where(r8 < 4, w1[3:4] * w2[:8], 0.0)),
        (t2[0:1] + t1, jnp.where(r16 >= 4, w2[0:1] * w1, 0.0)),
        (t2[1:2] + t1[:8], jnp.where(r8 >= 4, w2[1:2] * w1[:8], 0.0)),
        (t2[2:3] + t1[:8], jnp.where(r8 == 4, w2[2:3] * w1[:8], 0.0)),
    ]
    cand = jnp.concatenate([p[0] for p in pieces], axis=0)
    wts = jnp.concatenate([p[1] for p in pieces], axis=0)
    keys_ref[...] = _sortable(cand)
    wts_ref[...] = wts

    def value_bit(t, tau):
        c = tau + lax.shift_left(jnp.int32(1), 31 - t)
        cnt = jnp.sum(jnp.where(keys_ref[...] >= c, wts_ref[...], 0.0), axis=0, keepdims=True)
        return jnp.where(cnt >= PEER_TOPK, c, tau)

    tau_key = lax.fori_loop(0, 32, value_bit, jnp.full((1, tm), INT_MIN, I32))
    tau = _unsortable(tau_key)
    top = t1[0:1] + t2[0:1]
    z = jnp.sum(jnp.where(keys_ref[...] >= tau_key, wts * jnp.exp(cand - top), 0.0), axis=0, keepdims=True)

    rank = jnp.zeros(s1.shape, F32)
    reach = jnp.zeros(s2.shape, F32)
    for r in range(PEER_TOPK):
        rank = rank + jnp.where(s1 < t1[r:r + 1], 1.0, 0.0)
        reach = reach + jnp.where(t1[r:r + 1] + s2 >= tau, 1.0, 0.0)
    c1_ref[...] = jnp.exp(s1 - (t1[0:1] + jnp.log(z)))
    r1_ref[...] = rank
    e2_ref[...] = jnp.exp(s2 - t2[0:1]).astype(BF16)
    reach_ref[...] = reach.astype(BF16)


def peer_route(qp, sub_keys):
    n = qp.shape[0]
    tm = ROUTE_TM
    half = PEER_QDIM // 2
    hs = ROUTE_HEADS
    spec = pl.BlockSpec((hs, PEER_KEYS, tm), lambda i, h: (h, 0, i))
    return pl.pallas_call(
        _route_kernel,
        grid=(n // tm, PEER_HEADS // hs),
        in_specs=[pl.BlockSpec((tm, hs * PEER_QDIM), lambda i, h: (i, h)),
                  pl.BlockSpec((hs, 2, PEER_KEYS, half), lambda i, h: (h, 0, 0, 0))],
        out_specs=[spec, spec, spec, spec],
        out_shape=[jax.ShapeDtypeStruct((PEER_HEADS, PEER_KEYS, n), F32),
                   jax.ShapeDtypeStruct((PEER_HEADS, PEER_KEYS, n), F32),
                   jax.ShapeDtypeStruct((PEER_HEADS, PEER_KEYS, n), BF16),
                   jax.ShapeDtypeStruct((PEER_HEADS, PEER_KEYS, n), BF16)],
        scratch_shapes=[pltpu.VMEM((ROUTE_ROWS, tm), I32),
                        pltpu.VMEM((ROUTE_ROWS, tm), F32)],
        compiler_params=_params(("parallel", "arbitrary")),
        name="peer_route",
    )(qp, sub_keys)


PEER_TE = 1024
PACK = 16


def _peer_kernel(x_ref, xn_ref, c1_ref, r1_ref, e2_ref, reach_ref, u_ref, vt_ref, o_ref, acc_ref, xnt_ref, hid_ref):
    j = pl.program_id(1)
    n_sub = u_ref.shape[0] // PEER_KEYS
    tm = xn_ref.shape[0]

    @pl.when(j == 0)
    def _():
        acc_ref[...] = jnp.zeros_like(acc_ref)
        xnt_ref[...] = jnp.transpose(xn_ref[...].astype(F32)).astype(BF16)

    act = jnp.dot(u_ref[...], xnt_ref[...], preferred_element_type=F32)
    for ab in range(n_sub):
        rows = slice(ab * PEER_KEYS, (ab + 1) * PEER_KEYS)
        rank = [jnp.broadcast_to(r1_ref[h, 0, ab:ab + 1, :], (PACK, tm)).astype(BF16) for h in range(PEER_HEADS)]
        fac = [jnp.broadcast_to(c1_ref[h, 0, ab:ab + 1, :], (PACK, tm)).astype(BF16) for h in range(PEER_HEADS)]
        for tc in range(tm // LANES):
            cols = slice(tc * LANES, (tc + 1) * LANES)
            gate = jnp.zeros((PEER_KEYS, LANES), BF16)
            for h in range(PEER_HEADS):
                chosen = jnp.tile(rank[h][:, cols], (PEER_KEYS // PACK, 1)) < reach_ref[h, :, cols]
                weight = e2_ref[h, :, cols] * jnp.tile(fac[h][:, cols], (PEER_KEYS // PACK, 1))
                gate = gate + jnp.where(chosen, weight, jnp.zeros_like(weight))
            hid_ref[rows, cols] = gate * jax.nn.gelu(act[rows, cols]).astype(BF16)
    acc_ref[...] += jnp.dot(vt_ref[...], hid_ref[...], preferred_element_type=F32)

    @pl.when(j == pl.num_programs(1) - 1)
    def _():
        o_ref[...] = x_ref[...] + jnp.transpose(acc_ref[...])


def peer(x, xn, c1, r1, e2, reach, u, vt, tm=512):
    n, d = x.shape
    te = PEER_TE
    n_chunks = PEER_N // te
    n_sub = te // PEER_KEYS
    c1_chunks = c1.reshape(PEER_HEADS, n_chunks, n_sub, n)
    r1_chunks = r1.reshape(PEER_HEADS, n_chunks, n_sub, n)
    chunk_spec = pl.BlockSpec((PEER_HEADS, 1, n_sub, tm), lambda i, j: (0, j, 0, i))
    tile_spec = pl.BlockSpec((PEER_HEADS, PEER_KEYS, tm), lambda i, j: (0, 0, i))
    return pl.pallas_call(
        _peer_kernel,
        grid=(n // tm, n_chunks),
        in_specs=[pl.BlockSpec((tm, d), lambda i, j: (i, 0)),
                  pl.BlockSpec((tm, d), lambda i, j: (i, 0)),
                  chunk_spec, chunk_spec, tile_spec, tile_spec,
                  pl.BlockSpec((te, d), lambda i, j: (j, 0)),
                  pl.BlockSpec((d, te), lambda i, j: (0, j))],
        out_specs=pl.BlockSpec((tm, d), lambda i, j: (i, 0)),
        out_shape=jax.ShapeDtypeStruct((n, d), F32),
        scratch_shapes=[pltpu.VMEM((d, tm), F32),
                        pltpu.VMEM((d, tm), BF16),
                        pltpu.VMEM((te, tm), BF16)],
        compiler_params=_params(("parallel", "arbitrary")),
        name="peer",
    )(x, xn, c1_chunks, r1_chunks, e2, reach, u, vt)


def kernel(x, rel_bias, g_mix, w_in, a_sinks, c_lat_gain, c_w_uk, c_w_uv, w_br_a, w_br_b, w_br_c, w_out,
           g_ffn, peer_w_q, peer_sub_keys, peer_u, peer_v, g_final):
    bsz, seq, d = x.shape
    n = bsz * seq
    depth = w_in.shape[0]
    n_blk = seq // TILE
    off = [0] + [int(o) for o in np.cumsum(IN_SPLITS)]
    bt_a = rel_bias[:, :A_HEADS]
    bt_b = rel_bias[:, A_HEADS:A_HEADS + B_HEADS]
    bt_c = rel_bias[:, A_HEADS + B_HEADS:]
    bias_a = _swa_bias(bt_a)
    win_b = _bias_windows(bt_b, seq)
    win_c = _bias_windows(bt_c, seq)

    xf = x.reshape(n, d)
    for l in range(depth):
        w = w_in[l]
        pad = jnp.zeros((d, C_COLS - (off[11] - off[6])), F32)
        w_abc = jnp.concatenate([w[:, off[0]:off[6]],
                                 w[:, off[6]:off[7]], w[:, off[8]:off[9]], w[:, off[7]:off[8]],
                                 w[:, off[9]:off[11]], pad], axis=1).astype(BF16)
        w_g = w[:, off[11]:off[14]].astype(BF16)

        p_ab, p_c = in_proj(xf, g_mix[l], w_abc)
        p_a = p_ab.reshape(bsz, seq, -1)
        p_b = p_ab[:, A_COLS:].reshape(bsz, seq, 3, B_HEADS, HEAD_DIM)
        p_c = p_c.reshape(bsz, seq, -1)

        ya = swa(p_a, bias_a, a_sinks[l]).reshape(n, -1)

        qb = jnp.transpose(p_b[:, :, 0], (0, 2, 1, 3))
        kb = jnp.transpose(p_b[:, :, 1], (0, 2, 1, 3)).reshape(bsz, B_HEADS, n_blk, TILE, HEAD_DIM)
        vbt = jnp.transpose(p_b[:, :, 2].reshape(bsz, n_blk, TILE, B_HEADS, HEAD_DIM), (0, 3, 1, 4, 2))
        yb_t = moba(qb, kb, vbt, win_b)
        yb = jnp.transpose(yb_t, (0, 3, 1, 2)).reshape(n, -1)

        w_uk = jnp.transpose(c_w_uk[l], (1, 0, 2)).astype(BF16)
        w_uv_t = jnp.transpose(c_w_uv[l], (1, 2, 0)).astype(BF16)
        yc_t = dsa(p_c, c_lat_gain[l], w_uk, w_uv_t, win_c)
        yc = jnp.transpose(yc_t, (0, 2, 1)).reshape(n, -1)

        merged = merge(xf, g_mix[l], ya, yb, yc, w_g, w_br_a[l].astype(BF16), w_br_b[l].astype(BF16),
                       w_br_c[l].astype(BF16))
        xf = matmul_res(merged, w_out[l].astype(BF16), xf)

        qp, xn = norm_matmul(xf, g_ffn[l], peer_w_q[l].astype(BF16), BF16)
        c1, r1, e2, reach = peer_route(qp, peer_sub_keys[l].astype(BF16))
        xf = peer(xf, xn, c1, r1, e2, reach, peer_u[l].astype(BF16), jnp.transpose(peer_v[l]).astype(BF16))

    return final_norm(xf, g_final).reshape(bsz, seq, d)
```

```python
import functools
import math

import jax
import jax.numpy as jnp
import numpy as np
from jax import lax
from jax.experimental import pallas as pl
from jax.experimental.pallas import tpu as pltpu

F32 = jnp.float32
BF16 = jnp.bfloat16
I32 = jnp.int32

D_MODEL = 2048
HEAD_DIM = 64
A_HEADS, A_KV_HEADS, A_BLOCK, A_WINDOW = 16, 4, 128, 128
B_HEADS, B_BLOCK, B_TOPK = 8, 256, 3
C_HEADS, C_LATENT, IDX_HEADS, IDX_DIM, C_TOPK_MAX = 8, 128, 8, 32, 256
REL_BUCKETS, REL_MAX_DIST = 32, 2048
PEER_HEADS, PEER_KEYS, PEER_QDIM, PEER_TOPK = 8, 128, 256, 16
PEER_N = PEER_KEYS * PEER_KEYS
EPS = 1e-6
NEG = -1e30
INT_MIN = -(2 ** 31)
IN_SPLITS = (A_HEADS * HEAD_DIM, A_KV_HEADS * HEAD_DIM, A_KV_HEADS * HEAD_DIM,
             B_HEADS * HEAD_DIM, B_HEADS * HEAD_DIM, B_HEADS * HEAD_DIM,
             C_HEADS * HEAD_DIM, C_LATENT, IDX_HEADS * IDX_DIM, IDX_DIM, IDX_HEADS,
             D_MODEL, D_MODEL, D_MODEL)
A_COLS = sum(IN_SPLITS[0:3])
B_COLS = sum(IN_SPLITS[3:6])
C_COLS = 1024

TILE = 256
LANES = 128
LOCKSTEP = 4
VMEM_LIMIT = 56 * 1024 * 1024
PEER_VMEM_LIMIT = 58 * 1024 * 1024
NT = (((1,), (1,)), ((), ()))


def _params(sem, vmem=VMEM_LIMIT, flags=None):
    return pltpu.CompilerParams(dimension_semantics=sem, vmem_limit_bytes=vmem, flags=flags)


def _rms(x, g):
    return x * lax.rsqrt(jnp.mean(x * x, axis=-1, keepdims=True) + EPS) * g


def _sortable(v):
    bits = pltpu.bitcast(v, I32)
    return jnp.where(bits < 0, bits ^ jnp.int32(0x7FFFFFFF), bits)


def _unsortable(k):
    return pltpu.bitcast(jnp.where(k < 0, k ^ jnp.int32(0x7FFFFFFF), k), F32)


def _in_proj_kernel(x_ref, g_ref, w_ref, oab_ref, oc_ref, h_ref, *, nab):
    j = pl.program_id(1)

    @pl.when(j == 0)
    def _():
        h_ref[...] = _rms(x_ref[...], g_ref[...]).astype(BF16)

    y = jnp.dot(h_ref[...], w_ref[...], preferred_element_type=F32)

    @pl.when(j < nab)
    def _():
        oab_ref[...] = y.astype(oab_ref.dtype)

    @pl.when(j >= nab)
    def _():
        oc_ref[...] = y.astype(oc_ref.dtype)


def in_proj(x, g, w, tm=512, tn=1024):
    n, d = x.shape
    nab, nc = (A_COLS + B_COLS) // tn, C_COLS // tn
    return pl.pallas_call(
        functools.partial(_in_proj_kernel, nab=nab),
        grid=(n // tm, nab + nc),
        in_specs=[pl.BlockSpec((tm, d), lambda i, j: (i, 0)),
                  pl.BlockSpec((1, d), lambda i, j: (0, 0)),
                  pl.BlockSpec((d, tn), lambda i, j: (0, j))],
        out_specs=[pl.BlockSpec((tm, tn), lambda i, j: (i, jnp.minimum(j, nab - 1))),
                   pl.BlockSpec((tm, tn), lambda i, j: (i, jnp.maximum(j - nab, 0)))],
        out_shape=[jax.ShapeDtypeStruct((n, A_COLS + B_COLS), BF16),
                   jax.ShapeDtypeStruct((n, C_COLS), F32)],
        scratch_shapes=[pltpu.VMEM((tm, d), BF16)],
        compiler_params=_params(("parallel", "arbitrary")),
        name="in_proj",
    )(x, g.reshape(1, d), w)


def _norm_matmul_kernel(x_ref, g_ref, w_ref, o_ref, xn_ref, h_ref):
    @pl.when(pl.program_id(1) == 0)
    def _():
        h = _rms(x_ref[...], g_ref[...]).astype(BF16)
        h_ref[...] = h
        xn_ref[...] = h

    o_ref[...] = jnp.dot(h_ref[...], w_ref[...], preferred_element_type=F32).astype(o_ref.dtype)


def norm_matmul(x, g, w, out_dtype, tm=512, tn=1024):
    n, d = x.shape
    ncol = w.shape[1]
    return pl.pallas_call(
        _norm_matmul_kernel,
        grid=(n // tm, ncol // tn),
        in_specs=[pl.BlockSpec((tm, d), lambda i, j: (i, 0)),
                  pl.BlockSpec((1, d), lambda i, j: (0, 0)),
                  pl.BlockSpec((d, tn), lambda i, j: (0, j))],
        out_specs=[pl.BlockSpec((tm, tn), lambda i, j: (i, j)),
                   pl.BlockSpec((tm, d), lambda i, j: (i, 0))],
        out_shape=[jax.ShapeDtypeStruct((n, ncol), out_dtype),
                   jax.ShapeDtypeStruct((n, d), BF16)],
        scratch_shapes=[pltpu.VMEM((tm, d), BF16)],
        compiler_params=_params(("parallel", "arbitrary")),
        name="norm_matmul",
    )(x, g.reshape(1, d), w)


def _matmul_res_kernel(a_ref, w_ref, r_ref, o_ref):
    o_ref[...] = r_ref[...] + jnp.dot(a_ref[...], w_ref[...], preferred_element_type=F32)


def matmul_res(a, w, res, tm=512, tn=1024):
    n, k = a.shape
    ncol = w.shape[1]
    return pl.pallas_call(
        _matmul_res_kernel,
        grid=(n // tm, ncol // tn),
        in_specs=[pl.BlockSpec((tm, k), lambda i, j: (i, 0)),
                  pl.BlockSpec((k, tn), lambda i, j: (0, j)),
                  pl.BlockSpec((tm, tn), lambda i, j: (i, j))],
        out_specs=pl.BlockSpec((tm, tn), lambda i, j: (i, j)),
        out_shape=jax.ShapeDtypeStruct((n, ncol), F32),
        compiler_params=_params(("parallel", "arbitrary")),
        name="matmul_res",
    )(a, w, res)


def _merge_kernel(x_ref, g_ref, ya_ref, yb_ref, yc_ref, wga_ref, wgb_ref, wgc_ref,
                  wa_ref, wb_ref, wc_ref, o_ref, h_ref):
    @pl.when(pl.program_id(1) == 0)
    def _():
        h_ref[...] = _rms(x_ref[...], g_ref[...]).astype(BF16)

    h = h_ref[...]

    def branch(y_ref, wg_ref, w_ref):
        gate = jnp.dot(h, wg_ref[...], preferred_element_type=F32)
        proj = jnp.dot(y_ref[...], w_ref[...], preferred_element_type=F32)
        return jax.nn.sigmoid(gate) * proj

    merged = branch(ya_ref, wga_ref, wa_ref) + branch(yb_ref, wgb_ref, wb_ref) + branch(yc_ref, wgc_ref, wc_ref)
    o_ref[...] = merged.astype(o_ref.dtype)


def merge(x, g, ya, yb, yc, w_gate, wa, wb, wc, tm=512, tn=512):
    n, d = x.shape
    nj = d // tn
    return pl.pallas_call(
        _merge_kernel,
        grid=(n // tm, nj),
        in_specs=[pl.BlockSpec((tm, d), lambda i, j: (i, 0)),
                  pl.BlockSpec((1, d), lambda i, j: (0, 0)),
                  pl.BlockSpec((tm, ya.shape[1]), lambda i, j: (i, 0)),
                  pl.BlockSpec((tm, yb.shape[1]), lambda i, j: (i, 0)),
                  pl.BlockSpec((tm, yc.shape[1]), lambda i, j: (i, 0)),
                  pl.BlockSpec((d, tn), lambda i, j: (0, j)),
                  pl.BlockSpec((d, tn), lambda i, j: (0, j + nj)),
                  pl.BlockSpec((d, tn), lambda i, j: (0, j + 2 * nj)),
                  pl.BlockSpec((wa.shape[0], tn), lambda i, j: (0, j)),
                  pl.BlockSpec((wb.shape[0], tn), lambda i, j: (0, j)),
                  pl.BlockSpec((wc.shape[0], tn), lambda i, j: (0, j))],
        out_specs=pl.BlockSpec((tm, tn), lambda i, j: (i, j)),
        out_shape=jax.ShapeDtypeStruct((n, d), BF16),
        scratch_shapes=[pltpu.VMEM((tm, d), BF16)],
        compiler_params=_params(("parallel", "arbitrary")),
        name="merge",
    )(x, g.reshape(1, d), ya, yb, yc, w_gate, w_gate, w_gate, wa, wb, wc)


def _rel_bucket(dist):
    d = jnp.maximum(dist, 0)
    max_exact = REL_BUCKETS // 2
    df = jnp.maximum(d, max_exact).astype(F32)
    large = max_exact + (jnp.log(df / max_exact) / math.log(REL_MAX_DIST / max_exact)
                         * (REL_BUCKETS - max_exact)).astype(I32)
    large = jnp.minimum(large, REL_BUCKETS - 1)
    return jnp.where(d < max_exact, d, large)


def _bucket_bias(bt, dist):
    onehot = jax.nn.one_hot(_rel_bucket(dist), REL_BUCKETS, dtype=F32)
    return jnp.einsum("...k,kh->...h", onehot, bt.astype(F32), precision=lax.Precision.HIGHEST)


def _bias_windows(bt, seq):
    dist = (jnp.arange(seq // TILE)[:, None] * TILE - (TILE - 1) + jnp.arange(2 * TILE)[None, :])
    w = _bucket_bias(bt, dist)
    w = jnp.where((dist >= 0)[..., None], w, 0.0)
    return jnp.transpose(w, (2, 0, 1))


def _gen_bias_tiles(win_ref, tiles_ref):
    n_heads, n_rel = win_ref.shape[0], win_ref.shape[1]

    def body(t, carry):
        h = t // n_rel
        r = t % n_rel
        row = win_ref[h, pl.ds(r, 1), :]
        x = jnp.broadcast_to(row, (TILE, 2 * TILE))
        tiles_ref[h, r] = pltpu.roll(x, TILE + 1, 1, stride=1, stride_axis=0)[:, :TILE]
        return carry

    lax.fori_loop(0, n_heads * n_rel, body, 0)


def _swa_kernel(q_ref, kp_ref, kc_ref, vp_ref, vc_ref, bias_ref, sink_ref, o_ref):
    i = pl.program_id(1)
    which = jnp.minimum(i, 1)
    group = A_HEADS // A_KV_HEADS
    outs = []
    for kvh in range(A_KV_HEADS):
        lo, hi = kvh * HEAD_DIM, (kvh + 1) * HEAD_DIM
        kk = jnp.concatenate([kp_ref[0, :, lo:hi], kc_ref[0, :, lo:hi]], axis=0)
        vv = jnp.concatenate([vp_ref[0, :, lo:hi], vc_ref[0, :, lo:hi]], axis=0)
        heads = [kvh * group + g for g in range(group)]
        s = [lax.dot_general(q_ref[0, :, h * HEAD_DIM:(h + 1) * HEAD_DIM], kk, NT, preferred_element_type=F32)
             * (HEAD_DIM ** -0.5) for h in heads]
        s = [s[g] + bias_ref[which, heads[g]] for g in range(group)]
        m = [jnp.maximum(jnp.max(s[g], axis=-1, keepdims=True), sink_ref[heads[g]]) for g in range(group)]
        p = [jnp.exp(s[g] - m[g]) for g in range(group)]
        denom = [jnp.sum(p[g], axis=-1, keepdims=True) + jnp.exp(sink_ref[heads[g]] - m[g]) for g in range(group)]
        o = [jnp.dot(p[g].astype(BF16), vv, preferred_element_type=F32) for g in range(group)]
        outs.extend((o[g] / denom[g]).astype(BF16) for g in range(group))
    o_ref[0] = jnp.concatenate(outs, axis=-1)


def _swa_bias(bt_a):
    qpos = jnp.arange(A_BLOCK)[:, None] + A_BLOCK
    kpos = jnp.arange(2 * A_BLOCK)[None, :]
    dist = qpos - kpos
    band = (dist >= 0) & (dist < A_WINDOW)
    bias = jnp.transpose(_bucket_bias(bt_a, dist), (2, 0, 1))
    general = jnp.where(band[None], bias, NEG)
    first = jnp.where((band & (kpos >= A_BLOCK))[None], bias, NEG)
    return jnp.stack([first, general])


def swa(p_a, bias, sinks):
    bsz, seq, _ = p_a.shape
    qw, kw = A_HEADS * HEAD_DIM, A_KV_HEADS * HEAD_DIM
    kblk, vblk = qw // kw, qw // kw + 1
    return pl.pallas_call(
        _swa_kernel,
        grid=(bsz, seq // A_BLOCK),
        in_specs=[pl.BlockSpec((1, A_BLOCK, qw), lambda b, i: (b, i, 0)),
                  pl.BlockSpec((1, A_BLOCK, kw), lambda b, i: (b, jnp.maximum(i - 1, 0), kblk)),
                  pl.BlockSpec((1, A_BLOCK, kw), lambda b, i: (b, i, kblk)),
                  pl.BlockSpec((1, A_BLOCK, kw), lambda b, i: (b, jnp.maximum(i - 1, 0), vblk)),
                  pl.BlockSpec((1, A_BLOCK, kw), lambda b, i: (b, i, vblk)),
                  pl.BlockSpec(bias.shape, lambda b, i: (0, 0, 0, 0)),
                  pl.BlockSpec(memory_space=pltpu.SMEM)],
        out_specs=pl.BlockSpec((1, A_BLOCK, qw), lambda b, i: (b, i, 0)),
        out_shape=jax.ShapeDtypeStruct((bsz, seq, qw), BF16),
        compiler_params=_params(("parallel", "arbitrary")),
        name="swa",
    )(p_a, p_a, p_a, p_a, p_a, bias, sinks)


def _moba_kernel(q_ref, k_ref, vt_ref, win_ref, o_ref, tiles_ref, kmean_ref, sel_ref):
    b, i = pl.program_id(0), pl.program_id(1)
    n_heads, n_blk = kmean_ref.shape[0], kmean_ref.shape[1]

    @pl.when((b == 0) & (i == 0))
    def _():
        _gen_bias_tiles(win_ref, tiles_ref)

    @pl.when(i == 0)
    def _():
        for h in range(n_heads):
            kmean_ref[h] = jnp.mean(k_ref[0, h].astype(F32), axis=1)

    row = lax.broadcasted_iota(I32, (n_blk, TILE), 0)
    for h in range(n_heads):
        q = q_ref[0, h]
        kmean = kmean_ref[h]
        kmean_hi = kmean.astype(BF16)
        kmean_lo = (kmean - kmean_hi.astype(F32)).astype(BF16)
        gate = (lax.dot_general(kmean_hi, q, NT, preferred_element_type=F32)
                + lax.dot_general(kmean_lo, q, NT, preferred_element_type=F32))
        gate = jnp.where(row < i, gate, NEG)
        sel = jnp.zeros(gate.shape, F32)
        for _ in range(B_TOPK):
            best = jnp.max(gate, axis=0, keepdims=True)
            first = jnp.min(jnp.where(gate == best, row, n_blk), axis=0, keepdims=True)
            pick = row == first
            sel = jnp.where(pick, 1.0, sel)
            gate = jnp.where(pick, -jnp.inf, gate)
        sel_ref[h] = jnp.where(row < i, sel, 0.0)

    key_pos = lax.broadcasted_iota(I32, (TILE, TILE), 0)
    qry_pos = lax.broadcasted_iota(I32, (TILE, TILE), 1)
    causal = jnp.where(qry_pos >= key_pos, 1.0, 0.0)

    def body(j, carry):
        out = []
        for g in range(0, n_heads, LOCKSTEP):
            pair = tuple(range(g, g + LOCKSTEP))
            s = [lax.dot_general(k_ref[0, h, j], q_ref[0, h], NT, preferred_element_type=F32) * (HEAD_DIM ** -0.5)
                 for h in pair]
            s = [s[k] + tiles_ref[pair[k], i - j] for k in range(LOCKSTEP)]
            ok = [jnp.where(j == i, causal, jnp.broadcast_to(sel_ref[h, pl.ds(j, 1), :], (TILE, TILE))) > 0.0
                  for h in pair]
            s = [jnp.where(ok[k], s[k], NEG) for k in range(LOCKSTEP)]
            m_new = [jnp.maximum(carry[pair[k]][0], jnp.max(s[k], axis=0, keepdims=True)) for k in range(LOCKSTEP)]
            alpha = [jnp.exp(carry[pair[k]][0] - m_new[k]) for k in range(LOCKSTEP)]
            p = [jnp.where(ok[k], jnp.exp(s[k] - m_new[k]), 0.0) for k in range(LOCKSTEP)]
            l_new = [alpha[k] * carry[pair[k]][1] + jnp.sum(p[k], axis=0, keepdims=True) for k in range(LOCKSTEP)]
            pv = [jnp.dot(vt_ref[0, pair[k], j], p[k].astype(BF16), preferred_element_type=F32) for k in range(LOCKSTEP)]
            for k in range(LOCKSTEP):
                out.append((m_new[k], l_new[k], alpha[k] * carry[pair[k]][2] + pv[k]))
        return tuple(out)

    init = tuple((jnp.full((1, TILE), NEG, F32), jnp.zeros((1, TILE), F32), jnp.zeros((HEAD_DIM, TILE), F32))
                 for _ in range(n_heads))
    final = lax.fori_loop(0, i + 1, body, init)
    for h in range(n_heads):
        _, l, acc = final[h]
        o_ref[0, h] = (acc / l).astype(o_ref.dtype)


def moba(q, k, vt, windows):
    bsz, n_heads, seq, d = q.shape
    n_blk = seq // TILE
    return pl.pallas_call(
        _moba_kernel,
        grid=(bsz, n_blk),
        in_specs=[pl.BlockSpec((1, n_heads, TILE, d), lambda b, i: (b, 0, i, 0)),
                  pl.BlockSpec((1, n_heads, n_blk, TILE, d), lambda b, i: (b, 0, 0, 0, 0)),
                  pl.BlockSpec((1, n_heads, n_blk, d, TILE), lambda b, i: (b, 0, 0, 0, 0)),
                  pl.BlockSpec(windows.shape, lambda b, i: (0, 0, 0))],
        out_specs=pl.BlockSpec((1, n_heads, d, TILE), lambda b, i: (b, 0, 0, i)),
        out_shape=jax.ShapeDtypeStruct((bsz, n_heads, d, seq), BF16),
        scratch_shapes=[pltpu.VMEM((n_heads, n_blk, TILE, TILE), F32),
                        pltpu.VMEM((n_heads, n_blk, d), F32),
                        pltpu.VMEM((n_heads, n_blk, TILE), F32)],
        compiler_params=_params(("arbitrary", "arbitrary")),
        name="moba",
    )(q, k, vt, windows)


def _dsa_kernel(qc_ref, iq_ref, mq_ref, clat_ref, mk_ref, gain_ref, wuk_ref, wuvt_ref, win_ref, o_ref,
                tiles_ref, cn_ref, cnt_ref, ik_ref, keys_ref, mask_ref, qabs_ref, *, top_k):
    b, c = pl.program_id(0), pl.program_id(1)
    n_blk = cn_ref.shape[0]
    n_kb = c + 1

    @pl.when((b == 0) & (c == 0))
    def _():
        _gen_bias_tiles(win_ref, tiles_ref)

    @pl.when(c == 0)
    def _():
        lat = _rms(clat_ref[0], gain_ref[...])
        for kb in range(n_blk):
            blk = lat[kb * TILE:(kb + 1) * TILE]
            cn_ref[kb] = blk.astype(BF16)
            cnt_ref[kb] = jnp.transpose(blk).astype(BF16)
            ik_ref[kb] = mk_ref[0, kb * TILE:(kb + 1) * TILE, :IDX_DIM].astype(BF16)

    key_in = lax.broadcasted_iota(I32, (TILE, TILE), 0)
    qry_in = lax.broadcasted_iota(I32, (TILE, TILE), 1)
    diag_causal = key_in <= qry_in

    iq = iq_ref[0].astype(BF16)
    iw_t = jnp.transpose(mq_ref[0])[IDX_DIM:IDX_DIM + IDX_HEADS, :]

    def score_block(kb, carry):
        score = jnp.zeros((TILE, TILE), F32)
        ik = ik_ref[kb]
        logits = [lax.dot_general(ik, iq[:, hh * IDX_DIM:(hh + 1) * IDX_DIM], NT, preferred_element_type=F32)
                  for hh in range(IDX_HEADS)]
        for hh in range(IDX_HEADS):
            score = score + jnp.maximum(logits[hh], 0.0) * iw_t[hh:hh + 1, :]
        score = score * ((IDX_DIM ** -0.5) * (IDX_HEADS ** -0.5))
        score = jnp.where(score == 0.0, 0.0, score)
        key = _sortable(score)
        keys_ref[kb] = jnp.where(kb < c, key, jnp.where(diag_causal, key, INT_MIN))
        return carry

    lax.fori_loop(0, n_kb, score_block, 0)

    def count(pred):
        def blk(kb, cnt):
            return cnt + jnp.sum(pred(keys_ref[kb], kb), axis=0, keepdims=True)
        return lax.fori_loop(0, n_kb, blk, jnp.zeros((1, TILE), F32))

    def value_bit(t, tau):
        cand = tau + lax.shift_left(jnp.int32(1), 31 - t)
        cnt = count(lambda key, kb: jnp.where(key >= cand, 1.0, 0.0))
        return jnp.where(cnt >= top_k, cand, tau)

    tau = lax.fori_loop(0, 32, value_bit, jnp.full((1, TILE), INT_MIN, I32))
    need = top_k - count(lambda key, kb: jnp.where(key > tau, 1.0, 0.0))

    def index_bit(t, bound):
        cand = bound + lax.shift_left(jnp.int32(1), 11 - t)
        cnt = count(lambda key, kb: jnp.where(key == tau, jnp.where(kb * TILE + key_in < cand, 1.0, 0.0), 0.0))
        return jnp.where(cnt <= need, cand, bound)

    at_or_above = count(lambda key, kb: jnp.where(key >= tau, 1.0, 0.0))
    surplus = jnp.max(jnp.where(tau > INT_MIN, at_or_above - top_k, 0.0))
    bound = lax.cond(surplus > 0.0,
                     lambda: lax.fori_loop(0, 12, index_bit, jnp.zeros((1, TILE), I32)),
                     lambda: jnp.full((1, TILE), 2 ** 30, I32))

    def mask_block(kb, carry):
        key = keys_ref[kb]
        tied = jnp.where(key == tau, jnp.where(kb * TILE + key_in < bound, 1.0, 0.0), 0.0)
        chosen = jnp.where(key > tau, 1.0, tied)
        mask_ref[kb] = jnp.where(kb < c, chosen, jnp.where(diag_causal, chosen, 0.0))
        return carry

    lax.fori_loop(0, n_kb, mask_block, 0)

    qc = qc_ref[0].astype(BF16)
    for h in range(C_HEADS):
        qabs_ref[h] = lax.dot_general(wuk_ref[h], qc[:, h * HEAD_DIM:(h + 1) * HEAD_DIM], NT,
                                      preferred_element_type=F32).astype(BF16)

    def attend_block(kb, carry):
        ok = mask_ref[kb] > 0.0
        keys_lat, lat_keys = cn_ref[kb], cnt_ref[kb]
        out = []
        for g in range(0, C_HEADS, LOCKSTEP):
            pair = tuple(range(g, g + LOCKSTEP))
            s = [jnp.dot(keys_lat, qabs_ref[h], preferred_element_type=F32) * (HEAD_DIM ** -0.5) for h in pair]
            s = [jnp.where(ok, s[k] + tiles_ref[pair[k], c - kb], NEG) for k in range(LOCKSTEP)]
            m_new = [jnp.maximum(carry[pair[k]][0], jnp.max(s[k], axis=0, keepdims=True)) for k in range(LOCKSTEP)]
            alpha = [jnp.exp(carry[pair[k]][0] - m_new[k]) for k in range(LOCKSTEP)]
            p = [jnp.where(ok, jnp.exp(s[k] - m_new[k]), 0.0) for k in range(LOCKSTEP)]
            l_new = [alpha[k] * carry[pair[k]][1] + jnp.sum(p[k], axis=0, keepdims=True) for k in range(LOCKSTEP)]
            pv = [jnp.dot(lat_keys, p[k].astype(BF16), preferred_element_type=F32) for k in range(LOCKSTEP)]
            for k in range(LOCKSTEP):
                out.append((m_new[k], l_new[k], alpha[k] * carry[pair[k]][2] + pv[k]))
        return tuple(out)

    init = tuple((jnp.full((1, TILE), NEG, F32), jnp.zeros((1, TILE), F32), jnp.zeros((C_LATENT, TILE), F32))
                 for _ in range(C_HEADS))
    final = lax.fori_loop(0, n_kb, attend_block, init)
    for h in range(C_HEADS):
        _, l, acc = final[h]
        oc_t = (acc / l).astype(BF16)
        out_t = jnp.dot(wuvt_ref[h], oc_t, preferred_element_type=F32)
        o_ref[0, h * HEAD_DIM:(h + 1) * HEAD_DIM, :] = out_t.astype(o_ref.dtype)


def dsa(p_c, gain, w_uk, w_uv_t, windows):
    bsz, seq, _ = p_c.shape
    top_k = min(C_TOPK_MAX, seq // 4)
    qw = C_HEADS * HEAD_DIM
    n_blk = seq // TILE
    return pl.pallas_call(
        functools.partial(_dsa_kernel, top_k=top_k),
        grid=(bsz, n_blk),
        in_specs=[pl.BlockSpec((1, TILE, qw), lambda b, c: (b, c, 0)),
                  pl.BlockSpec((1, TILE, 256), lambda b, c: (b, c, 2)),
                  pl.BlockSpec((1, TILE, 128), lambda b, c: (b, c, 7)),
                  pl.BlockSpec((1, seq, 128), lambda b, c: (b, 0, 6)),
                  pl.BlockSpec((1, seq, 128), lambda b, c: (b, 0, 7)),
                  pl.BlockSpec((1, C_LATENT), lambda b, c: (0, 0)),
                  pl.BlockSpec(w_uk.shape, lambda b, c: (0, 0, 0)),
                  pl.BlockSpec(w_uv_t.shape, lambda b, c: (0, 0, 0)),
                  pl.BlockSpec(windows.shape, lambda b, c: (0, 0, 0))],
        out_specs=pl.BlockSpec((1, qw, TILE), lambda b, c: (b, 0, c)),
        out_shape=jax.ShapeDtypeStruct((bsz, qw, seq), BF16),
        scratch_shapes=[pltpu.VMEM((C_HEADS, n_blk, TILE, TILE), F32),
                        pltpu.VMEM((n_blk, TILE, C_LATENT), BF16),
                        pltpu.VMEM((n_blk, C_LATENT, TILE), BF16),
                        pltpu.VMEM((n_blk, TILE, IDX_DIM), BF16),
                        pltpu.VMEM((n_blk, TILE, TILE), I32),
                        pltpu.VMEM((n_blk, TILE, TILE), F32),
                        pltpu.VMEM((C_HEADS, C_LATENT, TILE), BF16)],
        compiler_params=_params(("arbitrary", "arbitrary")),
        name="dsa",
    )(p_c, p_c, p_c, p_c, p_c, gain.reshape(1, C_LATENT), w_uk, w_uv_t, windows)


PEER_TE = 1024
ROUTE_TM = 256
ROUTE_HEADS = 2
ROUTE_ROWS = 72


def _top_values(s, count):
    vals, mult = [], []
    for _ in range(count):
        best = jnp.max(s, axis=0, keepdims=True)
        hit = s == best
        vals.append(best)
        mult.append(jnp.sum(jnp.where(hit, 1.0, 0.0), axis=0, keepdims=True))
        s = jnp.where(hit, -jnp.inf, s)
    return jnp.concatenate(vals, axis=0), jnp.concatenate(mult, axis=0)


def _route_kernel(q_ref, k_ref, c1_ref, r1_ref, e2_ref, reach_ref, keys_ref, wts_ref):
    for hh in range(k_ref.shape[0]):
        _route_head(q_ref[:, hh * PEER_QDIM:(hh + 1) * PEER_QDIM], k_ref.at[hh], c1_ref.at[hh], r1_ref.at[hh],
                    e2_ref.at[hh], reach_ref.at[hh], keys_ref, wts_ref)


def _route_head(q, k_ref, c1_ref, r1_ref, e2_ref, reach_ref, keys_ref, wts_ref):
    tm = q.shape[0]
    half = PEER_QDIM // 2
    s1 = lax.dot_general(k_ref[0], q[:, :half], NT, preferred_element_type=F32)
    s2 = lax.dot_general(k_ref[1], q[:, half:], NT, preferred_element_type=F32)
    t1, w1 = _top_values(s1, PEER_TOPK)
    t2, w2 = _top_values(s2, PEER_TOPK)

    r16 = lax.broadcasted_iota(I32, (16, tm), 0)
    r8 = lax.broadcasted_iota(I32, (8, tm), 0)
    pieces = [
        (t1[0:1] + t2, w1[0:1] * w2),
        (t1[1:2] + t2[:8], w1[1:2] * w2[:8]),
        (t1[2:3] + t2[:8], jnp.where(r8 < 5, w1[2:3] * w2[:8], 0.0)),
        (t1[3:4] + t2[:8], jnp.where(r8 < 4, w1[3:4] * w2[:8], 0.0)),
        (t2[0:1] + t1, jnp.where(r16 >= 4, w2[0:1] * w1, 0.0)),
        (t2[1:2] + t1[:8], jnp.where(r8 >= 4, w2[1:2] * w1[:8], 0.0)),
        (t2[2:3] + t1[:8], jnp.where(r8 == 4, w2[2:3] * w1[:8], 0.0)),
    ]
    cand = jnp.concatenate([p[0] for p in pieces], axis=0)
    wts = jnp.concatenate([p[1] for p in pieces], axis=0)
    keys_ref[...] = _sortable(cand)
    wts_ref[...] = wts

    def value_bit(t, tau):
        c = tau + lax.shift_left(jnp.int32(1), 31 - t)
        cnt = jnp.sum(jnp.where(keys_ref[...] >= c, wts_ref[...], 0.0), axis=0, keepdims=True)
        return jnp.where(cnt >= PEER_TOPK, c, tau)

    tau_key = lax.fori_loop(0, 32, value_bit, jnp.full((1, tm), INT_MIN, I32))
    tau = _unsortable(tau_key)
    top = t1[0:1] + t2[0:1]
    z = jnp.sum(jnp.where(keys_ref[...] >= tau_key, wts * jnp.exp(cand - top), 0.0), axis=0, keepdims=True)

    rank = jnp.zeros(s1.shape, F32)
    reach = jnp.zeros(s2.shape, F32)
    for r in range(PEER_TOPK):
        rank = rank + jnp.where(s1 < t1[r:r + 1], 1.0, 0.0)
        reach = reach + jnp.where(t1[r:r + 1] + s2 >= tau, 1.0, 0.0)
    c1_ref[...] = jnp.exp(s1 - (t1[0:1] + jnp.log(z))).reshape(c1_ref.shape)
    r1_ref[...] = rank.reshape(r1_ref.shape)
    e2_ref[...] = jnp.exp(s2 - t2[0:1]).astype(BF16)
    reach_ref[...] = reach.astype(BF16)


def peer_route(qp, sub_keys):
    n = qp.shape[0]
    tm = ROUTE_TM
    half = PEER_QDIM // 2
    hs = ROUTE_HEADS
    spec = pl.BlockSpec((hs, PEER_KEYS, tm), lambda i, h: (h, 0, i))
    sub = PEER_TE // PEER_KEYS
    chunked = pl.BlockSpec((hs, PEER_KEYS // sub, sub, tm), lambda i, h: (h, 0, 0, i))
    return pl.pallas_call(
        _route_kernel,
        grid=(n // tm, PEER_HEADS // hs),
        in_specs=[pl.BlockSpec((tm, hs * PEER_QDIM), lambda i, h: (i, h)),
                  pl.BlockSpec((hs, 2, PEER_KEYS, half), lambda i, h: (h, 0, 0, 0))],
        out_specs=[chunked, chunked, spec, spec],
        out_shape=[jax.ShapeDtypeStruct((PEER_HEADS, PEER_KEYS // sub, sub, n), F32),
                   jax.ShapeDtypeStruct((PEER_HEADS, PEER_KEYS // sub, sub, n), F32),
                   jax.ShapeDtypeStruct((PEER_HEADS, PEER_KEYS, n), BF16),
                   jax.ShapeDtypeStruct((PEER_HEADS, PEER_KEYS, n), BF16)],
        scratch_shapes=[pltpu.VMEM((ROUTE_ROWS, tm), I32),
                        pltpu.VMEM((ROUTE_ROWS, tm), F32)],
        compiler_params=_params(("parallel", "arbitrary")),
        name="peer_route",
    )(qp, sub_keys)


PACK = 16


def _peer_kernel(x_ref, xn_ref, c1_ref, r1_ref, e2_ref, reach_ref, u_ref, vt_ref, gout_ref, o_ref,
                 acc_ref, xnt_ref, hid_ref, *, norm_out):
    j = pl.program_id(1)
    n_sub = u_ref.shape[0] // PEER_KEYS
    tm = xn_ref.shape[0]

    @pl.when(j == 0)
    def _():
        acc_ref[...] = jnp.zeros_like(acc_ref)
        xnt_ref[...] = jnp.transpose(xn_ref[...].astype(F32)).astype(BF16)

    act = jnp.dot(u_ref[...], xnt_ref[...], preferred_element_type=F32)
    for ab in range(n_sub):
        rows = slice(ab * PEER_KEYS, (ab + 1) * PEER_KEYS)
        rank = [jnp.broadcast_to(r1_ref[h, 0, ab:ab + 1, :], (PACK, tm)).astype(BF16) for h in range(PEER_HEADS)]
        fac = [jnp.broadcast_to(c1_ref[h, 0, ab:ab + 1, :], (PACK, tm)).astype(BF16) for h in range(PEER_HEADS)]
        for tc in range(tm // LANES):
            cols = slice(tc * LANES, (tc + 1) * LANES)
            gate = jnp.zeros((PEER_KEYS, LANES), BF16)
            for h in range(PEER_HEADS):
                chosen = jnp.tile(rank[h][:, cols], (PEER_KEYS // PACK, 1)) < reach_ref[h, :, cols]
                weight = e2_ref[h, :, cols] * jnp.tile(fac[h][:, cols], (PEER_KEYS // PACK, 1))
                gate = gate + jnp.where(chosen, weight, jnp.zeros_like(weight))
            hid_ref[rows, cols] = gate * jax.nn.gelu(act[rows, cols].astype(BF16))
    acc_ref[...] += jnp.dot(vt_ref[...], hid_ref[...], preferred_element_type=F32)

    @pl.when(j == pl.num_programs(1) - 1)
    def _():
        y = x_ref[...] + jnp.transpose(acc_ref[...])
        o_ref[...] = _rms(y, gout_ref[...]) if norm_out else y


def peer(x, xn, c1, r1, e2, reach, u, vt, gain_out, norm_out, tm=512):
    n, d = x.shape
    te = PEER_TE
    n_chunks = PEER_N // te
    n_sub = te // PEER_KEYS
    chunk_spec = pl.BlockSpec((PEER_HEADS, 1, n_sub, tm), lambda i, j: (0, j, 0, i))
    tile_spec = pl.BlockSpec((PEER_HEADS, PEER_KEYS, tm), lambda i, j: (0, 0, i))
    return pl.pallas_call(
        functools.partial(_peer_kernel, norm_out=norm_out),
        grid=(n // tm, n_chunks),
        in_specs=[pl.BlockSpec((tm, d), lambda i, j: (i, 0)),
                  pl.BlockSpec((tm, d), lambda i, j: (i, 0)),
                  chunk_spec, chunk_spec, tile_spec, tile_spec,
                  pl.BlockSpec((te, d), lambda i, j: (j, 0)),
                  pl.BlockSpec((d, te), lambda i, j: (0, j)),
                  pl.BlockSpec((1, d), lambda i, j: (0, 0))],
        out_specs=pl.BlockSpec((tm, d), lambda i, j: (i, 0)),
        out_shape=jax.ShapeDtypeStruct((n, d), F32),
        scratch_shapes=[pltpu.VMEM((d, tm), F32),
                        pltpu.VMEM((d, tm), BF16),
                        pltpu.VMEM((te, tm), BF16)],
        compiler_params=_params(("parallel", "arbitrary"), vmem=PEER_VMEM_LIMIT),
        name="peer",
    )(x, xn, c1, r1, e2, reach, u, vt, gain_out.reshape(1, d))


def kernel(x, rel_bias, g_mix, w_in, a_sinks, c_lat_gain, c_w_uk, c_w_uv, w_br_a, w_br_b, w_br_c, w_out,
           g_ffn, peer_w_q, peer_sub_keys, peer_u, peer_v, g_final):
    bsz, seq, d = x.shape
    n = bsz * seq
    depth = w_in.shape[0]
    n_blk = seq // TILE
    off = [0] + [int(o) for o in np.cumsum(IN_SPLITS)]
    bt_a = rel_bias[:, :A_HEADS]
    bt_b = rel_bias[:, A_HEADS:A_HEADS + B_HEADS]
    bt_c = rel_bias[:, A_HEADS + B_HEADS:]
    bias_a = _swa_bias(bt_a)
    win_b = _bias_windows(bt_b, seq)
    win_c = _bias_windows(bt_c, seq)

    xf = x.reshape(n, d)
    for l in range(depth):
        w = w_in[l]
        pad = jnp.zeros((d, C_COLS - (off[11] - off[6])), F32)
        w_abc = jnp.concatenate([w[:, off[0]:off[6]],
                                 w[:, off[6]:off[7]], w[:, off[8]:off[9]], w[:, off[7]:off[8]],
                                 w[:, off[9]:off[11]], pad], axis=1).astype(BF16)
        w_g = w[:, off[11]:off[14]].astype(BF16)

        p_ab, p_c = in_proj(xf, g_mix[l], w_abc)
        p_a = p_ab.reshape(bsz, seq, -1)
        p_b = p_ab[:, A_COLS:].reshape(bsz, seq, 3, B_HEADS, HEAD_DIM)
        p_c = p_c.reshape(bsz, seq, -1)

        ya = swa(p_a, bias_a, a_sinks[l]).reshape(n, -1)

        qb = jnp.transpose(p_b[:, :, 0], (0, 2, 1, 3))
        kb = jnp.transpose(p_b[:, :, 1], (0, 2, 1, 3)).reshape(bsz, B_HEADS, n_blk, TILE, HEAD_DIM)
        vbt = jnp.transpose(p_b[:, :, 2].reshape(bsz, n_blk, TILE, B_HEADS, HEAD_DIM), (0, 3, 1, 4, 2))
        yb_t = moba(qb, kb, vbt, win_b)
        yb = jnp.transpose(yb_t, (0, 3, 1, 2)).reshape(n, -1)

        w_uk = jnp.transpose(c_w_uk[l], (1, 0, 2)).astype(BF16)
        w_uv_t = jnp.transpose(c_w_uv[l], (1, 2, 0)).astype(BF16)
        yc_t = dsa(p_c, c_lat_gain[l], w_uk, w_uv_t, win_c)
        yc = jnp.transpose(yc_t, (0, 2, 1)).reshape(n, -1)

        merged = merge(xf, g_mix[l], ya, yb, yc, w_g, w_br_a[l].astype(BF16), w_br_b[l].astype(BF16),
                       w_br_c[l].astype(BF16))
        xf = matmul_res(merged, w_out[l].astype(BF16), xf)

        qp, xn = norm_matmul(xf, g_ffn[l], peer_w_q[l].astype(BF16), BF16)
        c1, r1, e2, reach = peer_route(qp, peer_sub_keys[l].astype(BF16))
        xf = peer(xf, xn, c1, r1, e2, reach, peer_u[l].astype(BF16), jnp.transpose(peer_v[l]).astype(BF16),
                  g_final, norm_out=(l == depth - 1))

    return xf.reshape(bsz, seq, d)
```

```python
import functools
import math

import jax
import jax.numpy as jnp
import numpy as np
from jax import lax
from jax.experimental import pallas as pl
from jax.experimental.pallas import tpu as pltpu

F32 = jnp.float32
BF16 = jnp.bfloat16
I32 = jnp.int32

D_MODEL = 2048
HEAD_DIM = 64
A_HEADS, A_KV_HEADS, A_BLOCK, A_WINDOW = 16, 4, 128, 128
B_HEADS, B_BLOCK, B_TOPK = 8, 256, 3
C_HEADS, C_LATENT, IDX_HEADS, IDX_DIM, C_TOPK_MAX = 8, 128, 8, 32, 256
REL_BUCKETS, REL_MAX_DIST = 32, 2048
PEER_HEADS, PEER_KEYS, PEER_QDIM, PEER_TOPK = 8, 128, 256, 16
PEER_N = PEER_KEYS * PEER_KEYS
EPS = 1e-6
NEG = -1e30
INT_MIN = -(2 ** 31)
IN_SPLITS = (A_HEADS * HEAD_DIM, A_KV_HEADS * HEAD_DIM, A_KV_HEADS * HEAD_DIM,
             B_HEADS * HEAD_DIM, B_HEADS * HEAD_DIM, B_HEADS * HEAD_DIM,
             C_HEADS * HEAD_DIM, C_LATENT, IDX_HEADS * IDX_DIM, IDX_DIM, IDX_HEADS,
             D_MODEL, D_MODEL, D_MODEL)
A_COLS = sum(IN_SPLITS[0:3])
B_COLS = sum(IN_SPLITS[3:6])
C_COLS = 1024

TILE = 256
LANES = 128
SUBLANES = 8
LOCKSTEP = 4
VMEM_LIMIT = 56 * 1024 * 1024
PEER_VMEM_LIMIT = 58 * 1024 * 1024
NT = (((1,), (1,)), ((), ()))


def _params(sem, vmem=VMEM_LIMIT, flags=None):
    return pltpu.CompilerParams(dimension_semantics=sem, vmem_limit_bytes=vmem, flags=flags)


def _rms(x, g):
    return x * lax.rsqrt(jnp.mean(x * x, axis=-1, keepdims=True) + EPS) * g


def _sortable(v):
    bits = pltpu.bitcast(v, I32)
    return jnp.where(bits < 0, bits ^ jnp.int32(0x7FFFFFFF), bits)


def _unsortable(k):
    return pltpu.bitcast(jnp.where(k < 0, k ^ jnp.int32(0x7FFFFFFF), k), F32)


def _in_proj_kernel(x_ref, g_ref, w_ref, oab_ref, oc_ref, h_ref, *, nab):
    j = pl.program_id(1)

    @pl.when(j == 0)
    def _():
        h_ref[...] = _rms(x_ref[...], g_ref[...]).astype(BF16)

    y = jnp.dot(h_ref[...], w_ref[...], preferred_element_type=F32)

    @pl.when(j < nab)
    def _():
        oab_ref[...] = y.astype(oab_ref.dtype)

    @pl.when(j >= nab)
    def _():
        oc_ref[...] = y.astype(oc_ref.dtype)


def in_proj(x, g, w, tm=512, tn=1024):
    n, d = x.shape
    nab, nc = (A_COLS + B_COLS) // tn, C_COLS // tn
    return pl.pallas_call(
        functools.partial(_in_proj_kernel, nab=nab),
        grid=(n // tm, nab + nc),
        in_specs=[pl.BlockSpec((tm, d), lambda i, j: (i, 0)),
                  pl.BlockSpec((1, d), lambda i, j: (0, 0)),
                  pl.BlockSpec((d, tn), lambda i, j: (0, j))],
        out_specs=[pl.BlockSpec((tm, tn), lambda i, j: (i, jnp.minimum(j, nab - 1))),
                   pl.BlockSpec((tm, tn), lambda i, j: (i, jnp.maximum(j - nab, 0)))],
        out_shape=[jax.ShapeDtypeStruct((n, A_COLS + B_COLS), BF16),
                   jax.ShapeDtypeStruct((n, C_COLS), F32)],
        scratch_shapes=[pltpu.VMEM((tm, d), BF16)],
        compiler_params=_params(("parallel", "arbitrary")),
        name="in_proj",
    )(x, g.reshape(1, d), w)


def _norm_matmul_kernel(x_ref, g_ref, w_ref, o_ref, xn_ref, h_ref):
    @pl.when(pl.program_id(1) == 0)
    def _():
        h = _rms(x_ref[...], g_ref[...]).astype(BF16)
        h_ref[...] = h
        xn_ref[...] = h

    o_ref[...] = jnp.dot(h_ref[...], w_ref[...], preferred_element_type=F32).astype(o_ref.dtype)


def norm_matmul(x, g, w, out_dtype, tm=512, tn=1024):
    n, d = x.shape
    ncol = w.shape[1]
    return pl.pallas_call(
        _norm_matmul_kernel,
        grid=(n // tm, ncol // tn),
        in_specs=[pl.BlockSpec((tm, d), lambda i, j: (i, 0)),
                  pl.BlockSpec((1, d), lambda i, j: (0, 0)),
                  pl.BlockSpec((d, tn), lambda i, j: (0, j))],
        out_specs=[pl.BlockSpec((tm, tn), lambda i, j: (i, j)),
                   pl.BlockSpec((tm, d), lambda i, j: (i, 0))],
        out_shape=[jax.ShapeDtypeStruct((n, ncol), out_dtype),
                   jax.ShapeDtypeStruct((n, d), BF16)],
        scratch_shapes=[pltpu.VMEM((tm, d), BF16)],
        compiler_params=_params(("parallel", "arbitrary")),
        name="norm_matmul",
    )(x, g.reshape(1, d), w)


def _matmul_res_kernel(a_ref, w_ref, r_ref, o_ref):
    o_ref[...] = r_ref[...] + jnp.dot(a_ref[...], w_ref[...], preferred_element_type=F32)


def matmul_res(a, w, res, tm=512, tn=1024):
    n, k = a.shape
    ncol = w.shape[1]
    return pl.pallas_call(
        _matmul_res_kernel,
        grid=(n // tm, ncol // tn),
        in_specs=[pl.BlockSpec((tm, k), lambda i, j: (i, 0)),
                  pl.BlockSpec((k, tn), lambda i, j: (0, j)),
                  pl.BlockSpec((tm, tn), lambda i, j: (i, j))],
        out_specs=pl.BlockSpec((tm, tn), lambda i, j: (i, j)),
        out_shape=jax.ShapeDtypeStruct((n, ncol), F32),
        compiler_params=_params(("parallel", "arbitrary")),
        name="matmul_res",
    )(a, w, res)


def _merge_kernel(x_ref, g_ref, ya_ref, yb_ref, yc_ref, wga_ref, wgb_ref, wgc_ref,
                  wa_ref, wb_ref, wc_ref, o_ref, h_ref):
    @pl.when(pl.program_id(1) == 0)
    def _():
        h_ref[...] = _rms(x_ref[...], g_ref[...]).astype(BF16)

    h = h_ref[...]

    def branch(y_ref, wg_ref, w_ref):
        gate = jnp.dot(h, wg_ref[...], preferred_element_type=F32)
        proj = jnp.dot(y_ref[...], w_ref[...], preferred_element_type=F32)
        return jax.nn.sigmoid(gate) * proj

    merged = branch(ya_ref, wga_ref, wa_ref) + branch(yb_ref, wgb_ref, wb_ref) + branch(yc_ref, wgc_ref, wc_ref)
    o_ref[...] = merged.astype(o_ref.dtype)


def merge(x, g, ya, yb, yc, w_gate, wa, wb, wc, tm=512, tn=512):
    n, d = x.shape
    nj = d // tn
    return pl.pallas_call(
        _merge_kernel,
        grid=(n // tm, nj),
        in_specs=[pl.BlockSpec((tm, d), lambda i, j: (i, 0)),
                  pl.BlockSpec((1, d), lambda i, j: (0, 0)),
                  pl.BlockSpec((tm, ya.shape[1]), lambda i, j: (i, 0)),
                  pl.BlockSpec((tm, yb.shape[1]), lambda i, j: (i, 0)),
                  pl.BlockSpec((tm, yc.shape[1]), lambda i, j: (i, 0)),
                  pl.BlockSpec((d, tn), lambda i, j: (0, j)),
                  pl.BlockSpec((d, tn), lambda i, j: (0, j + nj)),
                  pl.BlockSpec((d, tn), lambda i, j: (0, j + 2 * nj)),
                  pl.BlockSpec((wa.shape[0], tn), lambda i, j: (0, j)),
                  pl.BlockSpec((wb.shape[0], tn), lambda i, j: (0, j)),
                  pl.BlockSpec((wc.shape[0], tn), lambda i, j: (0, j))],
        out_specs=pl.BlockSpec((tm, tn), lambda i, j: (i, j)),
        out_shape=jax.ShapeDtypeStruct((n, d), BF16),
        scratch_shapes=[pltpu.VMEM((tm, d), BF16)],
        compiler_params=_params(("parallel", "arbitrary")),
        name="merge",
    )(x, g.reshape(1, d), ya, yb, yc, w_gate, w_gate, w_gate, wa, wb, wc)


def _rel_bucket(dist):
    d = jnp.maximum(dist, 0)
    max_exact = REL_BUCKETS // 2
    df = jnp.maximum(d, max_exact).astype(F32)
    large = max_exact + (jnp.log(df / max_exact) / math.log(REL_MAX_DIST / max_exact)
                         * (REL_BUCKETS - max_exact)).astype(I32)
    large = jnp.minimum(large, REL_BUCKETS - 1)
    return jnp.where(d < max_exact, d, large)


def _bucket_bias(bt, dist):
    onehot = jax.nn.one_hot(_rel_bucket(dist), REL_BUCKETS, dtype=F32)
    return jnp.einsum("...k,kh->...h", onehot, bt.astype(F32), precision=lax.Precision.HIGHEST)


def _bias_windows(bt, seq):
    dist = (jnp.arange(seq // TILE)[:, None] * TILE - (TILE - 1) + jnp.arange(2 * TILE)[None, :])
    w = _bucket_bias(bt, dist)
    w = jnp.where((dist >= 0)[..., None], w, 0.0)
    return jnp.transpose(w, (2, 0, 1))


def _gen_bias_tiles(win_ref, tiles_ref):
    n_heads, n_rel = win_ref.shape[0], win_ref.shape[1]

    def body(t, carry):
        h = t // n_rel
        r = t % n_rel
        row = win_ref[h, pl.ds(r, 1), :]
        x = jnp.broadcast_to(row, (TILE, 2 * TILE))
        tiles_ref[h, r] = pltpu.roll(x, TILE + 1, 1, stride=1, stride_axis=0)[:, :TILE]
        return carry

    lax.fori_loop(0, n_heads * n_rel, body, 0)


def _swa_kernel(q_ref, kp_ref, kc_ref, vp_ref, vc_ref, bias_ref, sink_ref, o_ref):
    i = pl.program_id(1)
    which = jnp.minimum(i, 1)
    group = A_HEADS // A_KV_HEADS
    outs = []
    for kvh in range(A_KV_HEADS):
        lo, hi = kvh * HEAD_DIM, (kvh + 1) * HEAD_DIM
        kk = jnp.concatenate([kp_ref[0, :, lo:hi], kc_ref[0, :, lo:hi]], axis=0)
        vv = jnp.concatenate([vp_ref[0, :, lo:hi], vc_ref[0, :, lo:hi]], axis=0)
        for g in range(group):
            h = kvh * group + g
            q = q_ref[0, :, h * HEAD_DIM:(h + 1) * HEAD_DIM]
            s = lax.dot_general(q, kk, NT, preferred_element_type=F32) * (HEAD_DIM ** -0.5)
            s = s + bias_ref[which, h]
            sink = sink_ref[h]
            m = jnp.maximum(jnp.max(s, axis=-1, keepdims=True), sink)
            p = jnp.exp(s - m)
            denom = jnp.sum(p, axis=-1, keepdims=True) + jnp.exp(sink - m)
            o = jnp.dot(p.astype(BF16), vv, preferred_element_type=F32) / denom
            outs.append(o.astype(BF16))
    o_ref[0] = jnp.concatenate(outs, axis=-1)


def _swa_bias(bt_a):
    qpos = jnp.arange(A_BLOCK)[:, None] + A_BLOCK
    kpos = jnp.arange(2 * A_BLOCK)[None, :]
    dist = qpos - kpos
    band = (dist >= 0) & (dist < A_WINDOW)
    bias = jnp.transpose(_bucket_bias(bt_a, dist), (2, 0, 1))
    general = jnp.where(band[None], bias, NEG)
    first = jnp.where((band & (kpos >= A_BLOCK))[None], bias, NEG)
    return jnp.stack([first, general])


def swa(p_a, bias, sinks):
    bsz, seq, _ = p_a.shape
    qw, kw = A_HEADS * HEAD_DIM, A_KV_HEADS * HEAD_DIM
    kblk, vblk = qw // kw, qw // kw + 1
    return pl.pallas_call(
        _swa_kernel,
        grid=(bsz, seq // A_BLOCK),
        in_specs=[pl.BlockSpec((1, A_BLOCK, qw), lambda b, i: (b, i, 0)),
                  pl.BlockSpec((1, A_BLOCK, kw), lambda b, i: (b, jnp.maximum(i - 1, 0), kblk)),
                  pl.BlockSpec((1, A_BLOCK, kw), lambda b, i: (b, i, kblk)),
                  pl.BlockSpec((1, A_BLOCK, kw), lambda b, i: (b, jnp.maximum(i - 1, 0), vblk)),
                  pl.BlockSpec((1, A_BLOCK, kw), lambda b, i: (b, i, vblk)),
                  pl.BlockSpec(bias.shape, lambda b, i: (0, 0, 0, 0)),
                  pl.BlockSpec(memory_space=pltpu.SMEM)],
        out_specs=pl.BlockSpec((1, A_BLOCK, qw), lambda b, i: (b, i, 0)),
        out_shape=jax.ShapeDtypeStruct((bsz, seq, qw), BF16),
        compiler_params=_params(("parallel", "arbitrary")),
        name="swa",
    )(p_a, p_a, p_a, p_a, p_a, bias, sinks)


def _moba_kernel(q_ref, k_ref, vt_ref, win_ref, o_ref, tiles_ref, kmean_ref, sel_ref):
    b, i = pl.program_id(0), pl.program_id(1)
    n_heads, n_blk = kmean_ref.shape[0], kmean_ref.shape[1]

    @pl.when((b == 0) & (i == 0))
    def _():
        _gen_bias_tiles(win_ref, tiles_ref)

    @pl.when(i == 0)
    def _():
        for h in range(n_heads):
            kmean_ref[h] = jnp.mean(k_ref[0, h].astype(F32), axis=1)

    row = lax.broadcasted_iota(I32, (n_blk, TILE), 0)
    for h in range(n_heads):
        q = q_ref[0, h]
        kmean = kmean_ref[h]
        kmean_hi = kmean.astype(BF16)
        kmean_lo = (kmean - kmean_hi.astype(F32)).astype(BF16)
        gate = (lax.dot_general(kmean_hi, q, NT, preferred_element_type=F32)
                + lax.dot_general(kmean_lo, q, NT, preferred_element_type=F32))
        gate = jnp.where(row < i, gate, NEG)
        sel = jnp.zeros(gate.shape, F32)
        for _ in range(B_TOPK):
            best = jnp.max(gate, axis=0, keepdims=True)
            first = jnp.min(jnp.where(gate == best, row, n_blk), axis=0, keepdims=True)
            pick = row == first
            sel = jnp.where(pick, 1.0, sel)
            gate = jnp.where(pick, -jnp.inf, gate)
        sel_ref[h] = jnp.where(row < i, sel, 0.0)

    key_pos = lax.broadcasted_iota(I32, (TILE, TILE), 0)
    qry_pos = lax.broadcasted_iota(I32, (TILE, TILE), 1)
    causal = jnp.where(qry_pos >= key_pos, 1.0, 0.0)

    def body(j, carry):
        out = []
        for g in range(0, n_heads, LOCKSTEP):
            pair = tuple(range(g, g + LOCKSTEP))
            s = [lax.dot_general(k_ref[0, h, j], q_ref[0, h], NT, preferred_element_type=F32) * (HEAD_DIM ** -0.5)
                 for h in pair]
            s = [s[k] + tiles_ref[pair[k], i - j] for k in range(LOCKSTEP)]
            ok = [jnp.where(j == i, causal, jnp.broadcast_to(sel_ref[h, pl.ds(j, 1), :], (TILE, TILE))) > 0.0
                  for h in pair]
            s = [jnp.where(ok[k], s[k], NEG) for k in range(LOCKSTEP)]
            m_new = [jnp.maximum(carry[pair[k]][0], jnp.max(s[k], axis=0, keepdims=True)) for k in range(LOCKSTEP)]
            alpha = [jnp.exp(carry[pair[k]][0] - m_new[k]) for k in range(LOCKSTEP)]
            p = [jnp.where(ok[k], jnp.exp(s[k] - m_new[k]), 0.0) for k in range(LOCKSTEP)]
            l_new = [alpha[k] * carry[pair[k]][1] + jnp.sum(p[k], axis=0, keepdims=True) for k in range(LOCKSTEP)]
            pv = [jnp.dot(vt_ref[0, pair[k], j], p[k].astype(BF16), preferred_element_type=F32) for k in range(LOCKSTEP)]
            for k in range(LOCKSTEP):
                out.append((m_new[k], l_new[k], alpha[k] * carry[pair[k]][2] + pv[k]))
        return tuple(out)

    init = tuple((jnp.full((1, TILE), NEG, F32), jnp.zeros((1, TILE), F32), jnp.zeros((HEAD_DIM, TILE), F32))
                 for _ in range(n_heads))
    final = lax.fori_loop(0, i + 1, body, init)
    for h in range(n_heads):
        _, l, acc = final[h]
        o_ref[0, h] = (acc / l).astype(o_ref.dtype)


def moba(q, k, vt, windows):
    bsz, n_heads, seq, d = q.shape
    n_blk = seq // TILE
    return pl.pallas_call(
        _moba_kernel,
        grid=(bsz, n_blk),
        in_specs=[pl.BlockSpec((1, n_heads, TILE, d), lambda b, i: (b, 0, i, 0)),
                  pl.BlockSpec((1, n_heads, n_blk, TILE, d), lambda b, i: (b, 0, 0, 0, 0)),
                  pl.BlockSpec((1, n_heads, n_blk, d, TILE), lambda b, i: (b, 0, 0, 0, 0)),
                  pl.BlockSpec(windows.shape, lambda b, i: (0, 0, 0))],
        out_specs=pl.BlockSpec((1, n_heads, d, TILE), lambda b, i: (b, 0, 0, i)),
        out_shape=jax.ShapeDtypeStruct((bsz, n_heads, d, seq), BF16),
        scratch_shapes=[pltpu.VMEM((n_heads, n_blk, TILE, TILE), F32),
                        pltpu.VMEM((n_heads, n_blk, d), F32),
                        pltpu.VMEM((n_heads, n_blk, TILE), F32)],
        compiler_params=_params(("arbitrary", "arbitrary")),
        name="moba",
    )(q, k, vt, windows)


def _dsa_kernel(qc_ref, iq_ref, mq_ref, clat_ref, mk_ref, gain_ref, wuk_ref, wuvt_ref, win_ref, o_ref,
                tiles_ref, cn_ref, cnt_ref, ik_ref, keys_ref, mask_ref, qabs_ref, *, top_k):
    b, c = pl.program_id(0), pl.program_id(1)
    n_blk = cn_ref.shape[0]
    n_kb = c + 1

    @pl.when((b == 0) & (c == 0))
    def _():
        _gen_bias_tiles(win_ref, tiles_ref)

    @pl.when(c == 0)
    def _():
        lat = _rms(clat_ref[0], gain_ref[...])
        for kb in range(n_blk):
            blk = lat[kb * TILE:(kb + 1) * TILE]
            cn_ref[kb] = blk.astype(BF16)
            cnt_ref[kb] = jnp.transpose(blk).astype(BF16)
            ik_ref[kb] = mk_ref[0, kb * TILE:(kb + 1) * TILE, :IDX_DIM].astype(BF16)

    key_in = lax.broadcasted_iota(I32, (TILE, TILE), 0)
    qry_in = lax.broadcasted_iota(I32, (TILE, TILE), 1)
    diag_causal = key_in <= qry_in

    iq = iq_ref[0].astype(BF16)
    iw_t = jnp.transpose(mq_ref[0])[IDX_DIM:IDX_DIM + IDX_HEADS, :]

    def score_block(kb, carry):
        score = jnp.zeros((TILE, TILE), F32)
        ik = ik_ref[kb]
        logits = [lax.dot_general(ik, iq[:, hh * IDX_DIM:(hh + 1) * IDX_DIM], NT, preferred_element_type=F32)
                  for hh in range(IDX_HEADS)]
        for hh in range(IDX_HEADS):
            score = score + jnp.maximum(logits[hh], 0.0) * iw_t[hh:hh + 1, :]
        score = score * ((IDX_DIM ** -0.5) * (IDX_HEADS ** -0.5))
        score = jnp.where(score == 0.0, 0.0, score)
        key = _sortable(score)
        keys_ref[kb] = jnp.where(kb < c, key, jnp.where(diag_causal, key, INT_MIN))
        return carry

    lax.fori_loop(0, n_kb, score_block, 0)

    def count(pred):
        def blk(kb, part):
            return part + jnp.sum(pred(keys_ref[kb], kb).reshape(TILE // SUBLANES, SUBLANES, TILE), axis=0)
        part = lax.fori_loop(0, n_kb, blk, jnp.zeros((SUBLANES, TILE), F32))
        return jnp.sum(part, axis=0, keepdims=True)

    def value_bit(t, tau):
        cand = tau + lax.shift_left(jnp.int32(1), 31 - t)
        cnt = count(lambda key, kb: jnp.where(key >= cand, 1.0, 0.0))
        return jnp.where(cnt >= top_k, cand, tau)

    tau = lax.fori_loop(0, 32, value_bit, jnp.full((1, TILE), INT_MIN, I32))
    need = top_k - count(lambda key, kb: jnp.where(key > tau, 1.0, 0.0))

    def index_bit(t, bound):
        cand = bound + lax.shift_left(jnp.int32(1), 11 - t)
        cnt = count(lambda key, kb: jnp.where(key == tau, jnp.where(kb * TILE + key_in < cand, 1.0, 0.0), 0.0))
        return jnp.where(cnt <= need, cand, bound)

    at_or_above = count(lambda key, kb: jnp.where(key >= tau, 1.0, 0.0))
    surplus = jnp.max(jnp.where(tau > INT_MIN, at_or_above - top_k, 0.0))
    bound = lax.cond(surplus > 0.0,
                     lambda: lax.fori_loop(0, 12, index_bit, jnp.zeros((1, TILE), I32)),
                     lambda: jnp.full((1, TILE), 2 ** 30, I32))

    def mask_block(kb, carry):
        key = keys_ref[kb]
        tied = jnp.where(key == tau, jnp.where(kb * TILE + key_in < bound, 1.0, 0.0), 0.0)
        chosen = jnp.where(key > tau, 1.0, tied)
        mask_ref[kb] = jnp.where(kb < c, chosen, jnp.where(diag_causal, chosen, 0.0))
        return carry

    lax.fori_loop(0, n_kb, mask_block, 0)

    qc = qc_ref[0].astype(BF16)
    for h in range(C_HEADS):
        qabs_ref[h] = lax.dot_general(wuk_ref[h], qc[:, h * HEAD_DIM:(h + 1) * HEAD_DIM], NT,
                                      preferred_element_type=F32).astype(BF16)

    def attend_block(kb, carry):
        ok = mask_ref[kb] > 0.0
        keys_lat, lat_keys = cn_ref[kb], cnt_ref[kb]
        out = []
        for g in range(0, C_HEADS, LOCKSTEP):
            pair = tuple(range(g, g + LOCKSTEP))
            s = [jnp.dot(keys_lat, qabs_ref[h], preferred_element_type=F32) * (HEAD_DIM ** -0.5) for h in pair]
            s = [jnp.where(ok, s[k] + tiles_ref[pair[k], c - kb], NEG) for k in range(LOCKSTEP)]
            m_new = [jnp.maximum(carry[pair[k]][0], jnp.max(s[k], axis=0, keepdims=True)) for k in range(LOCKSTEP)]
            alpha = [jnp.exp(carry[pair[k]][0] - m_new[k]) for k in range(LOCKSTEP)]
            p = [jnp.where(ok, jnp.exp(s[k] - m_new[k]), 0.0) for k in range(LOCKSTEP)]
            l_new = [alpha[k] * carry[pair[k]][1] + jnp.sum(p[k], axis=0, keepdims=True) for k in range(LOCKSTEP)]
            pv = [jnp.dot(lat_keys, p[k].astype(BF16), preferred_element_type=F32) for k in range(LOCKSTEP)]
            for k in range(LOCKSTEP):
                out.append((m_new[k], l_new[k], alpha[k] * carry[pair[k]][2] + pv[k]))
        return tuple(out)

    init = tuple((jnp.full((1, TILE), NEG, F32), jnp.zeros((1, TILE), F32), jnp.zeros((C_LATENT, TILE), F32))
                 for _ in range(C_HEADS))
    final = lax.fori_loop(0, n_kb, attend_block, init)
    for h in range(C_HEADS):
        _, l, acc = final[h]
        oc_t = (acc / l).astype(BF16)
        out_t = jnp.dot(wuvt_ref[h], oc_t, preferred_element_type=F32)
        o_ref[0, h * HEAD_DIM:(h + 1) * HEAD_DIM, :] = out_t.astype(o_ref.dtype)


def dsa(p_c, gain, w_uk, w_uv_t, windows):
    bsz, seq, _ = p_c.shape
    top_k = min(C_TOPK_MAX, seq // 4)
    qw = C_HEADS * HEAD_DIM
    n_blk = seq // TILE
    return pl.pallas_call(
        functools.partial(_dsa_kernel, top_k=top_k),
        grid=(bsz, n_blk),
        in_specs=[pl.BlockSpec((1, TILE, qw), lambda b, c: (b, c, 0)),
                  pl.BlockSpec((1, TILE, 256), lambda b, c: (b, c, 2)),
                  pl.BlockSpec((1, TILE, 128), lambda b, c: (b, c, 7)),
                  pl.BlockSpec((1, seq, 128), lambda b, c: (b, 0, 6)),
                  pl.BlockSpec((1, seq, 128), lambda b, c: (b, 0, 7)),
                  pl.BlockSpec((1, C_LATENT), lambda b, c: (0, 0)),
                  pl.BlockSpec(w_uk.shape, lambda b, c: (0, 0, 0)),
                  pl.BlockSpec(w_uv_t.shape, lambda b, c: (0, 0, 0)),
                  pl.BlockSpec(windows.shape, lambda b, c: (0, 0, 0))],
        out_specs=pl.BlockSpec((1, qw, TILE), lambda b, c: (b, 0, c)),
        out_shape=jax.ShapeDtypeStruct((bsz, qw, seq), BF16),
        scratch_shapes=[pltpu.VMEM((C_HEADS, n_blk, TILE, TILE), F32),
                        pltpu.VMEM((n_blk, TILE, C_LATENT), BF16),
                        pltpu.VMEM((n_blk, C_LATENT, TILE), BF16),
                        pltpu.VMEM((n_blk, TILE, IDX_DIM), BF16),
                        pltpu.VMEM((n_blk, TILE, TILE), I32),
                        pltpu.VMEM((n_blk, TILE, TILE), F32),
                        pltpu.VMEM((C_HEADS, C_LATENT, TILE), BF16)],
        compiler_params=_params(("arbitrary", "arbitrary")),
        name="dsa",
    )(p_c, p_c, p_c, p_c, p_c, gain.reshape(1, C_LATENT), w_uk, w_uv_t, windows)


PEER_TE = 1024
ROUTE_TM = 256
ROUTE_HEADS = 2
ROUTE_ROWS = 72


def _top_values(s, count):
    vals = []
    rank = jnp.full(s.shape, float(count), F32)
    for r in range(count):
        best = jnp.max(s, axis=0, keepdims=True)
        hit = s == best
        vals.append(best)
        rank = jnp.where(hit, float(r), rank)
        s = jnp.where(hit, -jnp.inf, s)
    taken = jnp.sum(jnp.where(rank < count, 1.0, 0.0), axis=0, keepdims=True)
    return jnp.concatenate(vals, axis=0), rank, taken


def _multiplicities(rank, count):
    return jnp.concatenate([jnp.sum(jnp.where(rank == float(r), 1.0, 0.0), axis=0, keepdims=True)
                            for r in range(count)], axis=0)


def _route_kernel(q_ref, k_ref, c1_ref, r1_ref, e2_ref, reach_ref, keys_ref, wts_ref):
    for hh in range(k_ref.shape[0]):
        _route_head(q_ref[:, hh * PEER_QDIM:(hh + 1) * PEER_QDIM], k_ref.at[hh], c1_ref.at[hh], r1_ref.at[hh],
                    e2_ref.at[hh], reach_ref.at[hh], keys_ref, wts_ref)


def _route_head(q, k_ref, c1_ref, r1_ref, e2_ref, reach_ref, keys_ref, wts_ref):
    tm = q.shape[0]
    half = PEER_QDIM // 2
    s1 = lax.dot_general(k_ref[0], q[:, :half], NT, preferred_element_type=F32)
    s2 = lax.dot_general(k_ref[1], q[:, half:], NT, preferred_element_type=F32)
    t1, rank, taken1 = _top_values(s1, PEER_TOPK)
    t2, rank2, taken2 = _top_values(s2, PEER_TOPK)
    repeats = jnp.max(taken1 + taken2) > 2.0 * PEER_TOPK
    ones = jnp.ones(t1.shape, F32)
    w1, w2 = lax.cond(repeats,
                      lambda: (_multiplicities(rank, PEER_TOPK), _multiplicities(rank2, PEER_TOPK)),
                      lambda: (ones, ones))

    r16 = lax.broadcasted_iota(I32, (16, tm), 0)
    r8 = lax.broadcasted_iota(I32, (8, tm), 0)
    pieces = [
        (t1[0:1] + t2, w1[0:1] * w2),
        (t1[1:2] + t2[:8], w1[1:2] * w2[:8]),
        (t1[2:3] + t2[:8], jnp.where(r8 < 5, w1[2:3] * w2[:8], 0.0)),
        (t1[3:4] + t2[:8], jnp.where(r8 < 4, w1[3:4] * w2[:8], 0.0)),
        (t2[0:1] + t1, jnp.where(r16 >= 4, w2[0:1] * w1, 0.0)),
        (t2[1:2] + t1[:8], jnp.where(r8 >= 4, w2[1:2] * w1[:8], 0.0)),
        (t2[2:3] + t1[:8], jnp.where(r8 == 4, w2[2:3] * w1[:8], 0.0)),
    ]
    cand = jnp.concatenate([p[0] for p in pieces], axis=0)
    wts = jnp.concatenate([p[1] for p in pieces], axis=0)
    keys_ref[...] = _sortable(cand)
    wts_ref[...] = wts

    def value_bit(t, tau):
        c = tau + lax.shift_left(jnp.int32(1), 31 - t)
        cnt = jnp.sum(jnp.where(keys_ref[...] >= c, wts_ref[...], 0.0), axis=0, keepdims=True)
        return jnp.where(cnt >= PEER_TOPK, c, tau)

    tau_key = lax.fori_loop(0, 32, value_bit, jnp.full((1, tm), INT_MIN, I32))
    tau = _unsortable(tau_key)
    top = t1[0:1] + t2[0:1]
    z = jnp.sum(jnp.where(keys_ref[...] >= tau_key, wts * jnp.exp(cand - top), 0.0), axis=0, keepdims=True)

    reach = jnp.zeros(s2.shape, F32)
    for r in range(PEER_TOPK):
        reach = reach + jnp.where(t1[r:r + 1] + s2 >= tau, 1.0, 0.0)
    c1_ref[...] = jnp.exp(s1 - (t1[0:1] + jnp.log(z))).reshape(c1_ref.shape)
    r1_ref[...] = rank.reshape(r1_ref.shape)
    e2_ref[...] = jnp.exp(s2 - t2[0:1]).astype(BF16)
    reach_ref[...] = reach.astype(BF16)


def peer_route(qp, sub_keys):
    n = qp.shape[0]
    tm = ROUTE_TM
    half = PEER_QDIM // 2
    hs = ROUTE_HEADS
    spec = pl.BlockSpec((hs, PEER_KEYS, tm), lambda i, h: (h, 0, i))
    sub = PEER_TE // PEER_KEYS
    chunked = pl.BlockSpec((hs, PEER_KEYS // sub, sub, tm), lambda i, h: (h, 0, 0, i))
    return pl.pallas_call(
        _route_kernel,
        grid=(n // tm, PEER_HEADS // hs),
        in_specs=[pl.BlockSpec((tm, hs * PEER_QDIM), lambda i, h: (i, h)),
                  pl.BlockSpec((hs, 2, PEER_KEYS, half), lambda i, h: (h, 0, 0, 0))],
        out_specs=[chunked, chunked, spec, spec],
        out_shape=[jax.ShapeDtypeStruct((PEER_HEADS, PEER_KEYS // sub, sub, n), F32),
                   jax.ShapeDtypeStruct((PEER_HEADS, PEER_KEYS // sub, sub, n), F32),
                   jax.ShapeDtypeStruct((PEER_HEADS, PEER_KEYS, n), BF16),
                   jax.ShapeDtypeStruct((PEER_HEADS, PEER_KEYS, n), BF16)],
        scratch_shapes=[pltpu.VMEM((ROUTE_ROWS, tm), I32),
                        pltpu.VMEM((ROUTE_ROWS, tm), F32)],
        compiler_params=_params(("parallel", "arbitrary")),
        name="peer_route",
    )(qp, sub_keys)


PACK = 16


def _peer_kernel(x_ref, xn_ref, c1_ref, r1_ref, e2_ref, reach_ref, u_ref, vt_ref, gout_ref, o_ref,
                 acc_ref, xnt_ref, hid_ref, *, norm_out):
    j = pl.program_id(1)
    n_sub = u_ref.shape[0] // PEER_KEYS
    tm = xn_ref.shape[0]

    @pl.when(j == 0)
    def _():
        acc_ref[...] = jnp.zeros_like(acc_ref)
        xnt_ref[...] = jnp.transpose(xn_ref[...].astype(F32)).astype(BF16)

    act = jnp.dot(u_ref[...], xnt_ref[...], preferred_element_type=F32)
    for ab in range(n_sub):
        rows = slice(ab * PEER_KEYS, (ab + 1) * PEER_KEYS)
        rank = [jnp.broadcast_to(r1_ref[h, 0, ab:ab + 1, :], (PACK, tm)).astype(BF16) for h in range(PEER_HEADS)]
        fac = [jnp.broadcast_to(c1_ref[h, 0, ab:ab + 1, :], (PACK, tm)).astype(BF16) for h in range(PEER_HEADS)]
        for tc in range(tm // LANES):
            cols = slice(tc * LANES, (tc + 1) * LANES)
            gate = jnp.zeros((PEER_KEYS, LANES), BF16)
            for h in range(PEER_HEADS):
                chosen = jnp.tile(rank[h][:, cols], (PEER_KEYS // PACK, 1)) < reach_ref[h, :, cols]
                e2 = e2_ref[h, :, cols]
                gate = gate + jnp.where(chosen, e2, jnp.zeros_like(e2)) * jnp.tile(fac[h][:, cols], (PEER_KEYS // PACK, 1))
            hid_ref[rows, cols] = gate * jax.nn.gelu(act[rows, cols].astype(BF16))
    acc_ref[...] += jnp.dot(vt_ref[...], hid_ref[...], preferred_element_type=F32)

    @pl.when(j == pl.num_programs(1) - 1)
    def _():
        y = x_ref[...] + jnp.transpose(acc_ref[...])
        o_ref[...] = _rms(y, gout_ref[...]) if norm_out else y


def peer(x, xn, c1, r1, e2, reach, u, vt, gain_out, norm_out, tm=512):
    n, d = x.shape
    te = PEER_TE
    n_chunks = PEER_N // te
    n_sub = te // PEER_KEYS
    chunk_spec = pl.BlockSpec((PEER_HEADS, 1, n_sub, tm), lambda i, j: (0, j, 0, i))
    tile_spec = pl.BlockSpec((PEER_HEADS, PEER_KEYS, tm), lambda i, j: (0, 0, i))
    return pl.pallas_call(
        functools.partial(_peer_kernel, norm_out=norm_out),
        grid=(n // tm, n_chunks),
        in_specs=[pl.BlockSpec((tm, d), lambda i, j: (i, 0)),
                  pl.BlockSpec((tm, d), lambda i, j: (i, 0)),
                  chunk_spec, chunk_spec, tile_spec, tile_spec,
                  pl.BlockSpec((te, d), lambda i, j: (j, 0)),
                  pl.BlockSpec((d, te), lambda i, j: (0, j)),
                  pl.BlockSpec((1, d), lambda i, j: (0, 0))],
        out_specs=pl.BlockSpec((tm, d), lambda i, j: (i, 0)),
        out_shape=jax.ShapeDtypeStruct((n, d), F32),
        scratch_shapes=[pltpu.VMEM((d, tm), F32),
                        pltpu.VMEM((d, tm), BF16),
                        pltpu.VMEM((te, tm), BF16)],
        compiler_params=_params(("parallel", "arbitrary"), vmem=PEER_VMEM_LIMIT),
        name="peer",
    )(x, xn, c1, r1, e2, reach, u, vt, gain_out.reshape(1, d))


def kernel(x, rel_bias, g_mix, w_in, a_sinks, c_lat_gain, c_w_uk, c_w_uv, w_br_a, w_br_b, w_br_c, w_out,
           g_ffn, peer_w_q, peer_sub_keys, peer_u, peer_v, g_final):
    bsz, seq, d = x.shape
    n = bsz * seq
    depth = w_in.shape[0]
    n_blk = seq // TILE
    off = [0] + [int(o) for o in np.cumsum(IN_SPLITS)]
    bt_a = rel_bias[:, :A_HEADS]
    bt_b = rel_bias[:, A_HEADS:A_HEADS + B_HEADS]
    bt_c = rel_bias[:, A_HEADS + B_HEADS:]
    bias_a = _swa_bias(bt_a)
    win_b = _bias_windows(bt_b, seq)
    win_c = _bias_windows(bt_c, seq)

    xf = x.reshape(n, d)
    for l in range(depth):
        w = w_in[l]
        pad = jnp.zeros((d, C_COLS - (off[11] - off[6])), F32)
        w_abc = jnp.concatenate([w[:, off[0]:off[6]],
                                 w[:, off[6]:off[7]], w[:, off[8]:off[9]], w[:, off[7]:off[8]],
                                 w[:, off[9]:off[11]], pad], axis=1).astype(BF16)
        w_g = w[:, off[11]:off[14]].astype(BF16)

        p_ab, p_c = in_proj(xf, g_mix[l], w_abc)
        p_a = p_ab.reshape(bsz, seq, -1)
        p_b = p_ab[:, A_COLS:].reshape(bsz, seq, 3, B_HEADS, HEAD_DIM)
        p_c = p_c.reshape(bsz, seq, -1)

        ya = swa(p_a, bias_a, a_sinks[l]).reshape(n, -1)

        qb = jnp.transpose(p_b[:, :, 0], (0, 2, 1, 3))
        kb = jnp.transpose(p_b[:, :, 1], (0, 2, 1, 3)).reshape(bsz, B_HEADS, n_blk, TILE, HEAD_DIM)
        vbt = jnp.transpose(p_b[:, :, 2].reshape(bsz, n_blk, TILE, B_HEADS, HEAD_DIM), (0, 3, 1, 4, 2))
        yb_t = moba(qb, kb, vbt, win_b)
        yb = jnp.transpose(yb_t, (0, 3, 1, 2)).reshape(n, -1)

        w_uk = jnp.transpose(c_w_uk[l], (1, 0, 2)).astype(BF16)
        w_uv_t = jnp.transpose(c_w_uv[l], (1, 2, 0)).astype(BF16)
        yc_t = dsa(p_c, c_lat_gain[l], w_uk, w_uv_t, win_c)
        yc = jnp.transpose(yc_t, (0, 2, 1)).reshape(n, -1)

        merged = merge(xf, g_mix[l], ya, yb, yc, w_g, w_br_a[l].astype(BF16), w_br_b[l].astype(BF16),
                       w_br_c[l].astype(BF16))
        xf = matmul_res(merged, w_out[l].astype(BF16), xf)

        qp, xn = norm_matmul(xf, g_ffn[l], peer_w_q[l].astype(BF16), BF16)
        c1, r1, e2, reach = peer_route(qp, peer_sub_keys[l].astype(BF16))
        xf = peer(xf, xn, c1, r1, e2, reach, peer_u[l].astype(BF16), jnp.transpose(peer_v[l]).astype(BF16),
                  g_final, norm_out=(l == depth - 1))

    return xf.reshape(bsz, seq, d)
```

```python
import functools
import math

import jax
import jax.numpy as jnp
import numpy as np
from jax import lax
from jax.experimental import pallas as pl
from jax.experimental.pallas import tpu as pltpu

F32 = jnp.float32
BF16 = jnp.bfloat16
I32 = jnp.int32

D_MODEL = 2048
HEAD_DIM = 64
A_HEADS, A_KV_HEADS, A_BLOCK, A_WINDOW = 16, 4, 128, 128
B_HEADS, B_BLOCK, B_TOPK = 8, 256, 3
C_HEADS, C_LATENT, IDX_HEADS, IDX_DIM, C_TOPK_MAX = 8, 128, 8, 32, 256
REL_BUCKETS, REL_MAX_DIST = 32, 2048
PEER_HEADS, PEER_KEYS, PEER_QDIM, PEER_TOPK = 8, 128, 256, 16
PEER_N = PEER_KEYS * PEER_KEYS
EPS = 1e-6
NEG = -1e30
INT_MIN = -(2 ** 31)
IN_SPLITS = (A_HEADS * HEAD_DIM, A_KV_HEADS * HEAD_DIM, A_KV_HEADS * HEAD_DIM,
             B_HEADS * HEAD_DIM, B_HEADS * HEAD_DIM, B_HEADS * HEAD_DIM,
             C_HEADS * HEAD_DIM, C_LATENT, IDX_HEADS * IDX_DIM, IDX_DIM, IDX_HEADS,
             D_MODEL, D_MODEL, D_MODEL)
A_COLS = sum(IN_SPLITS[0:3])
B_COLS = sum(IN_SPLITS[3:6])
C_COLS = 1024

TILE = 256
LANES = 128
SUBLANES = 8
LOCKSTEP = 8
VMEM_LIMIT = 56 * 1024 * 1024
PEER_VMEM_LIMIT = 58 * 1024 * 1024
NT = (((1,), (1,)), ((), ()))


def _params(sem, vmem=VMEM_LIMIT, flags=None):
    return pltpu.CompilerParams(dimension_semantics=sem, vmem_limit_bytes=vmem, flags=flags)


def _rms(x, g):
    return x * lax.rsqrt(jnp.mean(x * x, axis=-1, keepdims=True) + EPS) * g


def _sortable(v):
    bits = pltpu.bitcast(v, I32)
    return jnp.where(bits < 0, bits ^ jnp.int32(0x7FFFFFFF), bits)


def _unsortable(k):
    return pltpu.bitcast(jnp.where(k < 0, k ^ jnp.int32(0x7FFFFFFF), k), F32)


def _in_proj_kernel(x_ref, g_ref, w_ref, oab_ref, oc_ref, h_ref, *, nab):
    j = pl.program_id(1)

    @pl.when(j == 0)
    def _():
        h_ref[...] = _rms(x_ref[...], g_ref[...]).astype(BF16)

    y = jnp.dot(h_ref[...], w_ref[...], preferred_element_type=F32)

    @pl.when(j < nab)
    def _():
        oab_ref[...] = y.astype(oab_ref.dtype)

    @pl.when(j >= nab)
    def _():
        oc_ref[...] = y.astype(oc_ref.dtype)


def in_proj(x, g, w, tm=512, tn=1024):
    n, d = x.shape
    nab, nc = (A_COLS + B_COLS) // tn, C_COLS // tn
    return pl.pallas_call(
        functools.partial(_in_proj_kernel, nab=nab),
        grid=(n // tm, nab + nc),
        in_specs=[pl.BlockSpec((tm, d), lambda i, j: (i, 0)),
                  pl.BlockSpec((1, d), lambda i, j: (0, 0)),
                  pl.BlockSpec((d, tn), lambda i, j: (0, j))],
        out_specs=[pl.BlockSpec((tm, tn), lambda i, j: (i, jnp.minimum(j, nab - 1))),
                   pl.BlockSpec((tm, tn), lambda i, j: (i, jnp.maximum(j - nab, 0)))],
        out_shape=[jax.ShapeDtypeStruct((n, A_COLS + B_COLS), BF16),
                   jax.ShapeDtypeStruct((n, C_COLS), F32)],
        scratch_shapes=[pltpu.VMEM((tm, d), BF16)],
        compiler_params=_params(("parallel", "arbitrary")),
        name="in_proj",
    )(x, g.reshape(1, d), w)


def _norm_matmul_kernel(x_ref, g_ref, w_ref, o_ref, xn_ref, h_ref):
    @pl.when(pl.program_id(1) == 0)
    def _():
        h = _rms(x_ref[...], g_ref[...]).astype(BF16)
        h_ref[...] = h
        xn_ref[...] = h

    o_ref[...] = jnp.dot(h_ref[...], w_ref[...], preferred_element_type=F32).astype(o_ref.dtype)


def norm_matmul(x, g, w, out_dtype, tm=512, tn=1024):
    n, d = x.shape
    ncol = w.shape[1]
    return pl.pallas_call(
        _norm_matmul_kernel,
        grid=(n // tm, ncol // tn),
        in_specs=[pl.BlockSpec((tm, d), lambda i, j: (i, 0)),
                  pl.BlockSpec((1, d), lambda i, j: (0, 0)),
                  pl.BlockSpec((d, tn), lambda i, j: (0, j))],
        out_specs=[pl.BlockSpec((tm, tn), lambda i, j: (i, j)),
                   pl.BlockSpec((tm, d), lambda i, j: (i, 0))],
        out_shape=[jax.ShapeDtypeStruct((n, ncol), out_dtype),
                   jax.ShapeDtypeStruct((n, d), BF16)],
        scratch_shapes=[pltpu.VMEM((tm, d), BF16)],
        compiler_params=_params(("parallel", "arbitrary")),
        name="norm_matmul",
    )(x, g.reshape(1, d), w)


def _matmul_res_kernel(a_ref, w_ref, r_ref, o_ref):
    o_ref[...] = r_ref[...] + jnp.dot(a_ref[...], w_ref[...], preferred_element_type=F32)


def matmul_res(a, w, res, tm=512, tn=1024):
    n, k = a.shape
    ncol = w.shape[1]
    return pl.pallas_call(
        _matmul_res_kernel,
        grid=(n // tm, ncol // tn),
        in_specs=[pl.BlockSpec((tm, k), lambda i, j: (i, 0)),
                  pl.BlockSpec((k, tn), lambda i, j: (0, j)),
                  pl.BlockSpec((tm, tn), lambda i, j: (i, j))],
        out_specs=pl.BlockSpec((tm, tn), lambda i, j: (i, j)),
        out_shape=jax.ShapeDtypeStruct((n, ncol), F32),
        compiler_params=_params(("parallel", "arbitrary")),
        name="matmul_res",
    )(a, w, res)


def _merge_kernel(x_ref, g_ref, ya_ref, yb_ref, yc_ref, wga_ref, wgb_ref, wgc_ref,
                  wa_ref, wb_ref, wc_ref, o_ref, h_ref):
    @pl.when(pl.program_id(1) == 0)
    def _():
        h_ref[...] = _rms(x_ref[...], g_ref[...]).astype(BF16)

    h = h_ref[...]

    def branch(y_ref, wg_ref, w_ref):
        gate = jnp.dot(h, wg_ref[...], preferred_element_type=F32)
        proj = jnp.dot(y_ref[...], w_ref[...], preferred_element_type=F32)
        return jax.nn.sigmoid(gate) * proj

    merged = branch(ya_ref, wga_ref, wa_ref) + branch(yb_ref, wgb_ref, wb_ref) + branch(yc_ref, wgc_ref, wc_ref)
    o_ref[...] = merged.astype(o_ref.dtype)


def merge(x, g, ya, yb, yc, w_gate, wa, wb, wc, tm=512, tn=512):
    n, d = x.shape
    nj = d // tn
    return pl.pallas_call(
        _merge_kernel,
        grid=(n // tm, nj),
        in_specs=[pl.BlockSpec((tm, d), lambda i, j: (i, 0)),
                  pl.BlockSpec((1, d), lambda i, j: (0, 0)),
                  pl.BlockSpec((tm, ya.shape[1]), lambda i, j: (i, 0)),
                  pl.BlockSpec((tm, yb.shape[1]), lambda i, j: (i, 0)),
                  pl.BlockSpec((tm, yc.shape[1]), lambda i, j: (i, 0)),
                  pl.BlockSpec((d, tn), lambda i, j: (0, j)),
                  pl.BlockSpec((d, tn), lambda i, j: (0, j + nj)),
                  pl.BlockSpec((d, tn), lambda i, j: (0, j + 2 * nj)),
                  pl.BlockSpec((wa.shape[0], tn), lambda i, j: (0, j)),
                  pl.BlockSpec((wb.shape[0], tn), lambda i, j: (0, j)),
                  pl.BlockSpec((wc.shape[0], tn), lambda i, j: (0, j))],
        out_specs=pl.BlockSpec((tm, tn), lambda i, j: (i, j)),
        out_shape=jax.ShapeDtypeStruct((n, d), BF16),
        scratch_shapes=[pltpu.VMEM((tm, d), BF16)],
        compiler_params=_params(("parallel", "arbitrary")),
        name="merge",
    )(x, g.reshape(1, d), ya, yb, yc, w_gate, w_gate, w_gate, wa, wb, wc)


def _rel_bucket(dist):
    d = jnp.maximum(dist, 0)
    max_exact = REL_BUCKETS // 2
    df = jnp.maximum(d, max_exact).astype(F32)
    large = max_exact + (jnp.log(df / max_exact) / math.log(REL_MAX_DIST / max_exact)
                         * (REL_BUCKETS - max_exact)).astype(I32)
    large = jnp.minimum(large, REL_BUCKETS - 1)
    return jnp.where(d < max_exact, d, large)


def _bucket_bias(bt, dist):
    onehot = jax.nn.one_hot(_rel_bucket(dist), REL_BUCKETS, dtype=F32)
    return jnp.einsum("...k,kh->...h", onehot, bt.astype(F32), precision=lax.Precision.HIGHEST)


def _bias_windows(bt, seq):
    dist = (jnp.arange(seq // TILE)[:, None] * TILE - (TILE - 1) + jnp.arange(2 * TILE)[None, :])
    w = _bucket_bias(bt, dist)
    w = jnp.where((dist >= 0)[..., None], w, 0.0)
    return jnp.transpose(w, (2, 0, 1))


def _gen_bias_tiles(win_ref, tiles_ref):
    n_heads, n_rel = win_ref.shape[0], win_ref.shape[1]

    def body(t, carry):
        h = t // n_rel
        r = t % n_rel
        row = win_ref[h, pl.ds(r, 1), :]
        x = jnp.broadcast_to(row, (TILE, 2 * TILE))
        tiles_ref[h, r] = pltpu.roll(x, TILE + 1, 1, stride=1, stride_axis=0)[:, :TILE]
        return carry

    lax.fori_loop(0, n_heads * n_rel, body, 0)


def _swa_kernel(q_ref, kp_ref, kc_ref, vp_ref, vc_ref, bias_ref, sink_ref, o_ref):
    i = pl.program_id(1)
    which = jnp.minimum(i, 1)
    group = A_HEADS // A_KV_HEADS
    outs = []
    for kvh in range(A_KV_HEADS):
        lo, hi = kvh * HEAD_DIM, (kvh + 1) * HEAD_DIM
        kk = jnp.concatenate([kp_ref[0, :, lo:hi], kc_ref[0, :, lo:hi]], axis=0)
        vv = jnp.concatenate([vp_ref[0, :, lo:hi], vc_ref[0, :, lo:hi]], axis=0)
        for g in range(group):
            h = kvh * group + g
            q = q_ref[0, :, h * HEAD_DIM:(h + 1) * HEAD_DIM]
            s = lax.dot_general(q, kk, NT, preferred_element_type=F32) * (HEAD_DIM ** -0.5)
            s = s + bias_ref[which, h]
            sink = sink_ref[h]
            m = jnp.maximum(jnp.max(s, axis=-1, keepdims=True), sink)
            p = jnp.exp(s - m)
            denom = jnp.sum(p, axis=-1, keepdims=True) + jnp.exp(sink - m)
            o = jnp.dot(p.astype(BF16), vv, preferred_element_type=F32) / denom
            outs.append(o.astype(BF16))
    o_ref[0] = jnp.concatenate(outs, axis=-1)


def _swa_bias(bt_a):
    qpos = jnp.arange(A_BLOCK)[:, None] + A_BLOCK
    kpos = jnp.arange(2 * A_BLOCK)[None, :]
    dist = qpos - kpos
    band = (dist >= 0) & (dist < A_WINDOW)
    bias = jnp.transpose(_bucket_bias(bt_a, dist), (2, 0, 1))
    general = jnp.where(band[None], bias, NEG)
    first = jnp.where((band & (kpos >= A_BLOCK))[None], bias, NEG)
    return jnp.stack([first, general])


def swa(p_a, bias, sinks):
    bsz, seq, _ = p_a.shape
    qw, kw = A_HEADS * HEAD_DIM, A_KV_HEADS * HEAD_DIM
    kblk, vblk = qw // kw, qw // kw + 1
    return pl.pallas_call(
        _swa_kernel,
        grid=(bsz, seq // A_BLOCK),
        in_specs=[pl.BlockSpec((1, A_BLOCK, qw), lambda b, i: (b, i, 0)),
                  pl.BlockSpec((1, A_BLOCK, kw), lambda b, i: (b, jnp.maximum(i - 1, 0), kblk)),
                  pl.BlockSpec((1, A_BLOCK, kw), lambda b, i: (b, i, kblk)),
                  pl.BlockSpec((1, A_BLOCK, kw), lambda b, i: (b, jnp.maximum(i - 1, 0), vblk)),
                  pl.BlockSpec((1, A_BLOCK, kw), lambda b, i: (b, i, vblk)),
                  pl.BlockSpec(bias.shape, lambda b, i: (0, 0, 0, 0)),
                  pl.BlockSpec(memory_space=pltpu.SMEM)],
        out_specs=pl.BlockSpec((1, A_BLOCK, qw), lambda b, i: (b, i, 0)),
        out_shape=jax.ShapeDtypeStruct((bsz, seq, qw), BF16),
        compiler_params=_params(("parallel", "arbitrary")),
        name="swa",
    )(p_a, p_a, p_a, p_a, p_a, bias, sinks)


def _moba_kernel(q_ref, k_ref, vt_ref, win_ref, o_ref, tiles_ref, kmean_ref, sel_ref):
    b, i = pl.program_id(0), pl.program_id(1)
    n_heads, n_blk = kmean_ref.shape[0], kmean_ref.shape[1]

    @pl.when((b == 0) & (i == 0))
    def _():
        _gen_bias_tiles(win_ref, tiles_ref)

    @pl.when(i == 0)
    def _():
        for h in range(n_heads):
            kmean_ref[h] = jnp.mean(k_ref[0, h].astype(F32), axis=1)

    row = lax.broadcasted_iota(I32, (n_blk, TILE), 0)
    for h in range(n_heads):
        q = q_ref[0, h]
        kmean = kmean_ref[h]
        kmean_hi = kmean.astype(BF16)
        kmean_lo = (kmean - kmean_hi.astype(F32)).astype(BF16)
        gate = (lax.dot_general(kmean_hi, q, NT, preferred_element_type=F32)
                + lax.dot_general(kmean_lo, q, NT, preferred_element_type=F32))
        gate = jnp.where(row < i, gate, NEG)
        sel = jnp.zeros(gate.shape, F32)
        for _ in range(B_TOPK):
            best = jnp.max(gate, axis=0, keepdims=True)
            first = jnp.min(jnp.where(gate == best, row, n_blk), axis=0, keepdims=True)
            pick = row == first
            sel = jnp.where(pick, 1.0, sel)
            gate = jnp.where(pick, -jnp.inf, gate)
        sel_ref[h] = jnp.where(row < i, sel, 0.0)

    key_pos = lax.broadcasted_iota(I32, (TILE, TILE), 0)
    qry_pos = lax.broadcasted_iota(I32, (TILE, TILE), 1)
    causal = jnp.where(qry_pos >= key_pos, 1.0, 0.0)

    def body(j, carry):
        out = []
        for g in range(0, n_heads, LOCKSTEP):
            pair = tuple(range(g, g + LOCKSTEP))
            s = [lax.dot_general(k_ref[0, h, j], q_ref[0, h], NT, preferred_element_type=F32) * (HEAD_DIM ** -0.5)
                 for h in pair]
            s = [s[k] + tiles_ref[pair[k], i - j] for k in range(LOCKSTEP)]
            ok = [jnp.where(j == i, causal, jnp.broadcast_to(sel_ref[h, pl.ds(j, 1), :], (TILE, TILE))) > 0.0
                  for h in pair]
            s = [jnp.where(ok[k], s[k], NEG) for k in range(LOCKSTEP)]
            m_new = [jnp.maximum(carry[pair[k]][0], jnp.max(s[k], axis=0, keepdims=True)) for k in range(LOCKSTEP)]
            alpha = [jnp.exp(carry[pair[k]][0] - m_new[k]) for k in range(LOCKSTEP)]
            p = [jnp.where(ok[k], jnp.exp(s[k] - m_new[k]), 0.0) for k in range(LOCKSTEP)]
            l_new = [alpha[k] * carry[pair[k]][1] + jnp.sum(p[k], axis=0, keepdims=True) for k in range(LOCKSTEP)]
            pv = [jnp.dot(vt_ref[0, pair[k], j], p[k].astype(BF16), preferred_element_type=F32) for k in range(LOCKSTEP)]
            for k in range(LOCKSTEP):
                out.append((m_new[k], l_new[k], alpha[k] * carry[pair[k]][2] + pv[k]))
        return tuple(out)

    init = tuple((jnp.full((1, TILE), NEG, F32), jnp.zeros((1, TILE), F32), jnp.zeros((HEAD_DIM, TILE), F32))
                 for _ in range(n_heads))
    final = lax.fori_loop(0, i + 1, body, init)
    for h in range(n_heads):
        _, l, acc = final[h]
        o_ref[0, h] = (acc / l).astype(o_ref.dtype)


def moba(q, k, vt, windows):
    bsz, n_heads, seq, d = q.shape
    n_blk = seq // TILE
    return pl.pallas_call(
        _moba_kernel,
        grid=(bsz, n_blk),
        in_specs=[pl.BlockSpec((1, n_heads, TILE, d), lambda b, i: (b, 0, i, 0)),
                  pl.BlockSpec((1, n_heads, n_blk, TILE, d), lambda b, i: (b, 0, 0, 0, 0)),
                  pl.BlockSpec((1, n_heads, n_blk, d, TILE), lambda b, i: (b, 0, 0, 0, 0)),
                  pl.BlockSpec(windows.shape, lambda b, i: (0, 0, 0))],
        out_specs=pl.BlockSpec((1, n_heads, d, TILE), lambda b, i: (b, 0, 0, i)),
        out_shape=jax.ShapeDtypeStruct((bsz, n_heads, d, seq), BF16),
        scratch_shapes=[pltpu.VMEM((n_heads, n_blk, TILE, TILE), F32),
                        pltpu.VMEM((n_heads, n_blk, d), F32),
                        pltpu.VMEM((n_heads, n_blk, TILE), F32)],
        compiler_params=_params(("arbitrary", "arbitrary")),
        name="moba",
    )(q, k, vt, windows)


def _dsa_kernel(qc_ref, iq_ref, mq_ref, clat_ref, mk_ref, gain_ref, wuk_ref, wuvt_ref, win_ref, o_ref,
                tiles_ref, cn_ref, cnt_ref, ik_ref, keys_ref, mask_ref, qabs_ref, *, top_k):
    b, c = pl.program_id(0), pl.program_id(1)
    n_blk = cn_ref.shape[0]
    n_kb = c + 1

    @pl.when((b == 0) & (c == 0))
    def _():
        _gen_bias_tiles(win_ref, tiles_ref)

    @pl.when(c == 0)
    def _():
        lat = _rms(clat_ref[0], gain_ref[...])
        for kb in range(n_blk):
            blk = lat[kb * TILE:(kb + 1) * TILE]
            cn_ref[kb] = blk.astype(BF16)
            cnt_ref[kb] = jnp.transpose(blk).astype(BF16)
            ik_ref[kb] = mk_ref[0, kb * TILE:(kb + 1) * TILE, :IDX_DIM].astype(BF16)

    key_in = lax.broadcasted_iota(I32, (TILE, TILE), 0)
    qry_in = lax.broadcasted_iota(I32, (TILE, TILE), 1)
    diag_causal = key_in <= qry_in

    iq = iq_ref[0].astype(BF16)
    iw_t = jnp.transpose(mq_ref[0])[IDX_DIM:IDX_DIM + IDX_HEADS, :]

    def score_block(kb, carry):
        score = jnp.zeros((TILE, TILE), F32)
        ik = ik_ref[kb]
        logits = [lax.dot_general(ik, iq[:, hh * IDX_DIM:(hh + 1) * IDX_DIM], NT, preferred_element_type=F32)
                  for hh in range(IDX_HEADS)]
        for hh in range(IDX_HEADS):
            score = score + jnp.maximum(logits[hh], 0.0) * iw_t[hh:hh + 1, :]
        score = score * ((IDX_DIM ** -0.5) * (IDX_HEADS ** -0.5))
        score = jnp.where(score == 0.0, 0.0, score)
        key = _sortable(score)
        keys_ref[kb] = jnp.where(kb < c, key, jnp.where(diag_causal, key, INT_MIN))
        return carry

    lax.fori_loop(0, n_kb, score_block, 0)

    def count(pred):
        def blk(kb, part):
            return part + jnp.sum(pred(keys_ref[kb], kb).reshape(TILE // SUBLANES, SUBLANES, TILE), axis=0)
        part = lax.fori_loop(0, n_kb, blk, jnp.zeros((SUBLANES, TILE), F32))
        return jnp.sum(part, axis=0, keepdims=True)

    def value_bit(t, tau):
        cand = tau + lax.shift_left(jnp.int32(1), 31 - t)
        cnt = count(lambda key, kb: jnp.where(key >= cand, 1.0, 0.0))
        return jnp.where(cnt >= top_k, cand, tau)

    tau = lax.fori_loop(0, 32, value_bit, jnp.full((1, TILE), INT_MIN, I32))
    need = top_k - count(lambda key, kb: jnp.where(key > tau, 1.0, 0.0))

    def index_bit(t, bound):
        cand = bound + lax.shift_left(jnp.int32(1), 11 - t)
        cnt = count(lambda key, kb: jnp.where(key == tau, jnp.where(kb * TILE + key_in < cand, 1.0, 0.0), 0.0))
        return jnp.where(cnt <= need, cand, bound)

    at_or_above = count(lambda key, kb: jnp.where(key >= tau, 1.0, 0.0))
    surplus = jnp.max(jnp.where(tau > INT_MIN, at_or_above - top_k, 0.0))
    bound = lax.cond(surplus > 0.0,
                     lambda: lax.fori_loop(0, 12, index_bit, jnp.zeros((1, TILE), I32)),
                     lambda: jnp.full((1, TILE), 2 ** 30, I32))

    def mask_block(kb, carry):
        key = keys_ref[kb]
        tied = jnp.where(key == tau, jnp.where(kb * TILE + key_in < bound, 1.0, 0.0), 0.0)
        chosen = jnp.where(key > tau, 1.0, tied)
        mask_ref[kb] = jnp.where(kb < c, chosen, jnp.where(diag_causal, chosen, 0.0))
        return carry

    lax.fori_loop(0, n_kb, mask_block, 0)

    qc = qc_ref[0].astype(BF16)
    for h in range(C_HEADS):
        qabs_ref[h] = lax.dot_general(wuk_ref[h], qc[:, h * HEAD_DIM:(h + 1) * HEAD_DIM], NT,
                                      preferred_element_type=F32).astype(BF16)

    def attend_block(kb, carry):
        ok = mask_ref[kb] > 0.0
        keys_lat, lat_keys = cn_ref[kb], cnt_ref[kb]
        out = []
        for g in range(0, C_HEADS, LOCKSTEP):
            pair = tuple(range(g, g + LOCKSTEP))
            s = [jnp.dot(keys_lat, qabs_ref[h], preferred_element_type=F32) * (HEAD_DIM ** -0.5) for h in pair]
            s = [jnp.where(ok, s[k] + tiles_ref[pair[k], c - kb], NEG) for k in range(LOCKSTEP)]
            m_new = [jnp.maximum(carry[pair[k]][0], jnp.max(s[k], axis=0, keepdims=True)) for k in range(LOCKSTEP)]
            alpha = [jnp.exp(carry[pair[k]][0] - m_new[k]) for k in range(LOCKSTEP)]
            p = [jnp.where(ok, jnp.exp(s[k] - m_new[k]), 0.0) for k in range(LOCKSTEP)]
            l_new = [alpha[k] * carry[pair[k]][1] + jnp.sum(p[k], axis=0, keepdims=True) for k in range(LOCKSTEP)]
            pv = [jnp.dot(lat_keys, p[k].astype(BF16), preferred_element_type=F32) for k in range(LOCKSTEP)]
            for k in range(LOCKSTEP):
                out.append((m_new[k], l_new[k], alpha[k] * carry[pair[k]][2] + pv[k]))
        return tuple(out)

    init = tuple((jnp.full((1, TILE), NEG, F32), jnp.zeros((1, TILE), F32), jnp.zeros((C_LATENT, TILE), F32))
                 for _ in range(C_HEADS))
    final = lax.fori_loop(0, n_kb, attend_block, init)
    for h in range(C_HEADS):
        _, l, acc = final[h]
        oc_t = (acc / l).astype(BF16)
        out_t = jnp.dot(wuvt_ref[h], oc_t, preferred_element_type=F32)
        o_ref[0, h * HEAD_DIM:(h + 1) * HEAD_DIM, :] = out_t.astype(o_ref.dtype)


def dsa(p_c, gain, w_uk, w_uv_t, windows):
    bsz, seq, _ = p_c.shape
    top_k = min(C_TOPK_MAX, seq // 4)
    qw = C_HEADS * HEAD_DIM
    n_blk = seq // TILE
    return pl.pallas_call(
        functools.partial(_dsa_kernel, top_k=top_k),
        grid=(bsz, n_blk),
        in_specs=[pl.BlockSpec((1, TILE, qw), lambda b, c: (b, c, 0)),
                  pl.BlockSpec((1, TILE, 256), lambda b, c: (b, c, 2)),
                  pl.BlockSpec((1, TILE, 128), lambda b, c: (b, c, 7)),
                  pl.BlockSpec((1, seq, 128), lambda b, c: (b, 0, 6)),
                  pl.BlockSpec((1, seq, 128), lambda b, c: (b, 0, 7)),
                  pl.BlockSpec((1, C_LATENT), lambda b, c: (0, 0)),
                  pl.BlockSpec(w_uk.shape, lambda b, c: (0, 0, 0)),
                  pl.BlockSpec(w_uv_t.shape, lambda b, c: (0, 0, 0)),
                  pl.BlockSpec(windows.shape, lambda b, c: (0, 0, 0))],
        out_specs=pl.BlockSpec((1, qw, TILE), lambda b, c: (b, 0, c)),
        out_shape=jax.ShapeDtypeStruct((bsz, qw, seq), BF16),
        scratch_shapes=[pltpu.VMEM((C_HEADS, n_blk, TILE, TILE), F32),
                        pltpu.VMEM((n_blk, TILE, C_LATENT), BF16),
                        pltpu.VMEM((n_blk, C_LATENT, TILE), BF16),
                        pltpu.VMEM((n_blk, TILE, IDX_DIM), BF16),
                        pltpu.VMEM((n_blk, TILE, TILE), I32),
                        pltpu.VMEM((n_blk, TILE, TILE), F32),
                        pltpu.VMEM((C_HEADS, C_LATENT, TILE), BF16)],
        compiler_params=_params(("arbitrary", "arbitrary")),
        name="dsa",
    )(p_c, p_c, p_c, p_c, p_c, gain.reshape(1, C_LATENT), w_uk, w_uv_t, windows)


PEER_TE = 1024
ROUTE_TM = 256
ROUTE_HEADS = 2
ROUTE_ROWS = 72


def _top_values(s, count):
    vals = []
    rank = jnp.full(s.shape, float(count), F32)
    for r in range(count):
        best = jnp.max(s, axis=0, keepdims=True)
        hit = s == best
        vals.append(best)
        rank = jnp.where(hit, float(r), rank)
        s = jnp.where(hit, -jnp.inf, s)
    taken = jnp.sum(jnp.where(rank < count, 1.0, 0.0), axis=0, keepdims=True)
    return jnp.concatenate(vals, axis=0), rank, taken


def _multiplicities(rank, count):
    return jnp.concatenate([jnp.sum(jnp.where(rank == float(r), 1.0, 0.0), axis=0, keepdims=True)
                            for r in range(count)], axis=0)


def _route_kernel(q_ref, k_ref, c1_ref, r1_ref, e2_ref, reach_ref, keys_ref, wts_ref, s2_ref, t1_ref):
    heads = range(k_ref.shape[0])
    tm = q_ref.shape[0]
    for hh in heads:
        _route_prepare(q_ref[:, hh * PEER_QDIM:(hh + 1) * PEER_QDIM], k_ref.at[hh], c1_ref.at[hh], r1_ref.at[hh],
                       e2_ref.at[hh], keys_ref.at[hh], wts_ref.at[hh], s2_ref.at[hh], t1_ref.at[hh])

    def value_bit(t, taus):
        out = []
        for hh in heads:
            c = taus[hh] + lax.shift_left(jnp.int32(1), 31 - t)
            cnt = jnp.sum(jnp.where(keys_ref[hh] >= c, wts_ref[hh], 0.0), axis=0, keepdims=True)
            out.append(jnp.where(cnt >= PEER_TOPK, c, taus[hh]))
        return tuple(out)

    taus = lax.fori_loop(0, 32, value_bit, tuple(jnp.full((1, tm), INT_MIN, I32) for _ in heads))
    for hh in heads:
        _route_finish(taus[hh], c1_ref.at[hh], reach_ref.at[hh], keys_ref.at[hh], wts_ref.at[hh], s2_ref.at[hh],
                      t1_ref.at[hh])


def _route_prepare(q, k_ref, c1_ref, r1_ref, e2_ref, keys_ref, wts_ref, s2_ref, t1_ref):
    tm = q.shape[0]
    half = PEER_QDIM // 2
    s1 = lax.dot_general(k_ref[0], q[:, :half], NT, preferred_element_type=F32)
    s2 = lax.dot_general(k_ref[1], q[:, half:], NT, preferred_element_type=F32)
    t1, rank, taken1 = _top_values(s1, PEER_TOPK)
    t2, rank2, taken2 = _top_values(s2, PEER_TOPK)
    repeats = jnp.max(taken1 + taken2) > 2.0 * PEER_TOPK
    ones = jnp.ones(t1.shape, F32)
    w1, w2 = lax.cond(repeats,
                      lambda: (_multiplicities(rank, PEER_TOPK), _multiplicities(rank2, PEER_TOPK)),
                      lambda: (ones, ones))

    r16 = lax.broadcasted_iota(I32, (16, tm), 0)
    r8 = lax.broadcasted_iota(I32, (8, tm), 0)
    pieces = [
        (t1[0:1] + t2, w1[0:1] * w2),
        (t1[1:2] + t2[:8], w1[1:2] * w2[:8]),
        (t1[2:3] + t2[:8], jnp.where(r8 < 5, w1[2:3] * w2[:8], 0.0)),
        (t1[3:4] + t2[:8], jnp.where(r8 < 4, w1[3:4] * w2[:8], 0.0)),
        (t2[0:1] + t1, jnp.where(r16 >= 4, w2[0:1] * w1, 0.0)),
        (t2[1:2] + t1[:8], jnp.where(r8 >= 4, w2[1:2] * w1[:8], 0.0)),
        (t2[2:3] + t1[:8], jnp.where(r8 == 4, w2[2:3] * w1[:8], 0.0)),
    ]
    keys_ref[...] = _sortable(jnp.concatenate([p[0] for p in pieces], axis=0))
    wts_ref[...] = jnp.concatenate([p[1] for p in pieces], axis=0)
    c1_ref[...] = s1.reshape(c1_ref.shape)
    r1_ref[...] = rank.reshape(r1_ref.shape)
    e2_ref[...] = jnp.exp(s2 - t2[0:1]).astype(BF16)
    s2_ref[...] = s2
    t1_ref[...] = t1


def _route_finish(tau_key, c1_ref, reach_ref, keys_ref, wts_ref, s2_ref, t1_ref):
    tau = _unsortable(tau_key)
    cand = _unsortable(keys_ref[...])
    top = cand[0:1]
    z = jnp.sum(jnp.where(keys_ref[...] >= tau_key, wts_ref[...] * jnp.exp(cand - top), 0.0), axis=0, keepdims=True)
    s2 = s2_ref[...]
    reach = jnp.zeros(s2.shape, F32)
    for r in range(PEER_TOPK):
        reach = reach + jnp.where(t1_ref[r:r + 1, :] + s2 >= tau, 1.0, 0.0)
    reach_ref[...] = reach.astype(BF16)
    c1_ref[...] = jnp.exp(c1_ref[...] - (t1_ref[0:1, :] + jnp.log(z)))


def peer_route(qp, sub_keys):
    n = qp.shape[0]
    tm = ROUTE_TM
    half = PEER_QDIM // 2
    hs = ROUTE_HEADS
    spec = pl.BlockSpec((hs, PEER_KEYS, tm), lambda i, h: (h, 0, i))
    sub = PEER_TE // PEER_KEYS
    chunked = pl.BlockSpec((hs, PEER_KEYS // sub, sub, tm), lambda i, h: (h, 0, 0, i))
    return pl.pallas_call(
        _route_kernel,
        grid=(n // tm, PEER_HEADS // hs),
        in_specs=[pl.BlockSpec((tm, hs * PEER_QDIM), lambda i, h: (i, h)),
                  pl.BlockSpec((hs, 2, PEER_KEYS, half), lambda i, h: (h, 0, 0, 0))],
        out_specs=[chunked, chunked, spec, spec],
        out_shape=[jax.ShapeDtypeStruct((PEER_HEADS, PEER_KEYS // sub, sub, n), F32),
                   jax.ShapeDtypeStruct((PEER_HEADS, PEER_KEYS // sub, sub, n), F32),
                   jax.ShapeDtypeStruct((PEER_HEADS, PEER_KEYS, n), BF16),
                   jax.ShapeDtypeStruct((PEER_HEADS, PEER_KEYS, n), BF16)],
        scratch_shapes=[pltpu.VMEM((hs, ROUTE_ROWS, tm), I32),
                        pltpu.VMEM((hs, ROUTE_ROWS, tm), F32),
                        pltpu.VMEM((hs, PEER_KEYS, tm), F32),
                        pltpu.VMEM((hs, PEER_TOPK, tm), F32)],
        compiler_params=_params(("parallel", "arbitrary")),
        name="peer_route",
    )(qp, sub_keys)


PACK = 16


def _peer_kernel(x_ref, xn_ref, c1_ref, r1_ref, e2_ref, reach_ref, u_ref, vt_ref, gout_ref, o_ref,
                 acc_ref, xnt_ref, hid_ref, *, norm_out):
    j = pl.program_id(1)
    n_sub = u_ref.shape[0] // PEER_KEYS
    tm = xn_ref.shape[0]

    @pl.when(j == 0)
    def _():
        acc_ref[...] = jnp.zeros_like(acc_ref)
        xnt_ref[...] = jnp.transpose(xn_ref[...].astype(F32)).astype(BF16)

    act = jnp.dot(u_ref[...], xnt_ref[...], preferred_element_type=F32)
    for ab in range(n_sub):
        rows = slice(ab * PEER_KEYS, (ab + 1) * PEER_KEYS)
        rank = [jnp.broadcast_to(r1_ref[h, 0, ab:ab + 1, :], (PACK, tm)).astype(BF16) for h in range(PEER_HEADS)]
        fac = [jnp.broadcast_to(c1_ref[h, 0, ab:ab + 1, :], (PACK, tm)).astype(BF16) for h in range(PEER_HEADS)]
        for tc in range(tm // LANES):
            cols = slice(tc * LANES, (tc + 1) * LANES)
            gate = jnp.zeros((PEER_KEYS, LANES), BF16)
            for h in range(PEER_HEADS):
                chosen = jnp.tile(rank[h][:, cols], (PEER_KEYS // PACK, 1)) < reach_ref[h, :, cols]
                e2 = e2_ref[h, :, cols]
                gate = gate + jnp.where(chosen, e2, jnp.zeros_like(e2)) * jnp.tile(fac[h][:, cols], (PEER_KEYS // PACK, 1))
            hid_ref[rows, cols] = gate * jax.nn.gelu(act[rows, cols].astype(BF16))
    acc_ref[...] += jnp.dot(vt_ref[...], hid_ref[...], preferred_element_type=F32)

    @pl.when(j == pl.num_programs(1) - 1)
    def _():
        y = x_ref[...] + jnp.transpose(acc_ref[...])
        o_ref[...] = _rms(y, gout_ref[...]) if norm_out else y


def peer(x, xn, c1, r1, e2, reach, u, vt, gain_out, norm_out, tm=512):
    n, d = x.shape
    te = PEER_TE
    n_chunks = PEER_N // te
    n_sub = te // PEER_KEYS
    chunk_spec = pl.BlockSpec((PEER_HEADS, 1, n_sub, tm), lambda i, j: (0, j, 0, i))
    tile_spec = pl.BlockSpec((PEER_HEADS, PEER_KEYS, tm), lambda i, j: (0, 0, i))
    return pl.pallas_call(
        functools.partial(_peer_kernel, norm_out=norm_out),
        grid=(n // tm, n_chunks),
        in_specs=[pl.BlockSpec((tm, d), lambda i, j: (i, 0)),
                  pl.BlockSpec((tm, d), lambda i, j: (i, 0)),
                  chunk_spec, chunk_spec, tile_spec, tile_spec,
                  pl.BlockSpec((te, d), lambda i, j: (j, 0)),
                  pl.BlockSpec((d, te), lambda i, j: (0, j)),
                  pl.BlockSpec((1, d), lambda i, j: (0, 0))],
        out_specs=pl.BlockSpec((tm, d), lambda i, j: (i, 0)),
        out_shape=jax.ShapeDtypeStruct((n, d), F32),
        scratch_shapes=[pltpu.VMEM((d, tm), F32),
                        pltpu.VMEM((d, tm), BF16),
                        pltpu.VMEM((te, tm), BF16)],
        compiler_params=_params(("parallel", "arbitrary"), vmem=PEER_VMEM_LIMIT),
        name="peer",
    )(x, xn, c1, r1, e2, reach, u, vt, gain_out.reshape(1, d))


def kernel(x, rel_bias, g_mix, w_in, a_sinks, c_lat_gain, c_w_uk, c_w_uv, w_br_a, w_br_b, w_br_c, w_out,
           g_ffn, peer_w_q, peer_sub_keys, peer_u, peer_v, g_final):
    bsz, seq, d = x.shape
    n = bsz * seq
    depth = w_in.shape[0]
    n_blk = seq // TILE
    off = [0] + [int(o) for o in np.cumsum(IN_SPLITS)]
    bt_a = rel_bias[:, :A_HEADS]
    bt_b = rel_bias[:, A_HEADS:A_HEADS + B_HEADS]
    bt_c = rel_bias[:, A_HEADS + B_HEADS:]
    bias_a = _swa_bias(bt_a)
    win_b = _bias_windows(bt_b, seq)
    win_c = _bias_windows(bt_c, seq)

    xf = x.reshape(n, d)
    for l in range(depth):
        w = w_in[l]
        pad = jnp.zeros((d, C_COLS - (off[11] - off[6])), F32)
        w_abc = jnp.concatenate([w[:, off[0]:off[6]],
                                 w[:, off[6]:off[7]], w[:, off[8]:off[9]], w[:, off[7]:off[8]],
                                 w[:, off[9]:off[11]], pad], axis=1).astype(BF16)
        w_g = w[:, off[11]:off[14]].astype(BF16)

        p_ab, p_c = in_proj(xf, g_mix[l], w_abc)
        p_a = p_ab.reshape(bsz, seq, -1)
        p_b = p_ab[:, A_COLS:].reshape(bsz, seq, 3, B_HEADS, HEAD_DIM)
        p_c = p_c.reshape(bsz, seq, -1)

        ya = swa(p_a, bias_a, a_sinks[l]).reshape(n, -1)

        qb = jnp.transpose(p_b[:, :, 0], (0, 2, 1, 3))
        kb = jnp.transpose(p_b[:, :, 1], (0, 2, 1, 3)).reshape(bsz, B_HEADS, n_blk, TILE, HEAD_DIM)
        vbt = jnp.transpose(p_b[:, :, 2].reshape(bsz, n_blk, TILE, B_HEADS, HEAD_DIM), (0, 3, 1, 4, 2))
        yb_t = moba(qb, kb, vbt, win_b)
        yb = jnp.transpose(yb_t, (0, 3, 1, 2)).reshape(n, -1)

        w_uk = jnp.transpose(c_w_uk[l], (1, 0, 2)).astype(BF16)
        w_uv_t = jnp.transpose(c_w_uv[l], (1, 2, 0)).astype(BF16)
        yc_t = dsa(p_c, c_lat_gain[l], w_uk, w_uv_t, win_c)
        yc = jnp.transpose(yc_t, (0, 2, 1)).reshape(n, -1)

        merged = merge(xf, g_mix[l], ya, yb, yc, w_g, w_br_a[l].astype(BF16), w_br_b[l].astype(BF16),
                       w_br_c[l].astype(BF16))
        xf = matmul_res(merged, w_out[l].astype(BF16), xf)

        qp, xn = norm_matmul(xf, g_ffn[l], peer_w_q[l].astype(BF16), BF16)
        c1, r1, e2, reach = peer_route(qp, peer_sub_keys[l].astype(BF16))
        xf = peer(xf, xn, c1, r1, e2, reach, peer_u[l].astype(BF16), jnp.transpose(peer_v[l]).astype(BF16),
                  g_final, norm_out=(l == depth - 1))

    return xf.reshape(bsz, seq, d)
```

```python
import functools
import math

import jax
import jax.numpy as jnp
import numpy as np
from jax import lax
from jax.experimental import pallas as pl
from jax.experimental.pallas import tpu as pltpu

F32 = jnp.float32
BF16 = jnp.bfloat16
I32 = jnp.int32

D_MODEL = 2048
HEAD_DIM = 64
A_HEADS, A_KV_HEADS, A_BLOCK, A_WINDOW = 16, 4, 128, 128
B_HEADS, B_BLOCK, B_TOPK = 8, 256, 3
C_HEADS, C_LATENT, IDX_HEADS, IDX_DIM, C_TOPK_MAX = 8, 128, 8, 32, 256
REL_BUCKETS, REL_MAX_DIST = 32, 2048
PEER_HEADS, PEER_KEYS, PEER_QDIM, PEER_TOPK = 8, 128, 256, 16
PEER_N = PEER_KEYS * PEER_KEYS
EPS = 1e-6
NEG = -1e30
INT_MIN = -(2 ** 31)
IN_SPLITS = (A_HEADS * HEAD_DIM, A_KV_HEADS * HEAD_DIM, A_KV_HEADS * HEAD_DIM,
             B_HEADS * HEAD_DIM, B_HEADS * HEAD_DIM, B_HEADS * HEAD_DIM,
             C_HEADS * HEAD_DIM, C_LATENT, IDX_HEADS * IDX_DIM, IDX_DIM, IDX_HEADS,
             D_MODEL, D_MODEL, D_MODEL)
A_COLS = sum(IN_SPLITS[0:3])
B_COLS = sum(IN_SPLITS[3:6])
C_COLS = 1024

TILE = 256
LANES = 128
SUBLANES = 8
LOCKSTEP = 8
VMEM_LIMIT = 56 * 1024 * 1024
PEER_VMEM_LIMIT = 58 * 1024 * 1024
NT = (((1,), (1,)), ((), ()))


def _params(sem, vmem=VMEM_LIMIT, flags=None):
    return pltpu.CompilerParams(dimension_semantics=sem, vmem_limit_bytes=vmem, flags=flags)


def _rms(x, g):
    return x * lax.rsqrt(jnp.mean(x * x, axis=-1, keepdims=True) + EPS) * g


def _sortable(v):
    bits = pltpu.bitcast(v, I32)
    return jnp.where(bits < 0, bits ^ jnp.int32(0x7FFFFFFF), bits)


def _unsortable(k):
    return pltpu.bitcast(jnp.where(k < 0, k ^ jnp.int32(0x7FFFFFFF), k), F32)


def _in_proj_kernel(x_ref, g_ref, w_ref, oab_ref, oc_ref, h_ref, *, nab):
    j = pl.program_id(1)

    @pl.when(j == 0)
    def _():
        h_ref[...] = _rms(x_ref[...], g_ref[...]).astype(BF16)

    y = jnp.dot(h_ref[...], w_ref[...], preferred_element_type=F32)

    @pl.when(j < nab)
    def _():
        oab_ref[...] = y.astype(oab_ref.dtype)

    @pl.when(j >= nab)
    def _():
        oc_ref[...] = y.astype(oc_ref.dtype)


def in_proj(x, g, w, tm=512, tn=1024):
    n, d = x.shape
    nab, nc = (A_COLS + B_COLS) // tn, C_COLS // tn
    return pl.pallas_call(
        functools.partial(_in_proj_kernel, nab=nab),
        grid=(n // tm, nab + nc),
        in_specs=[pl.BlockSpec((tm, d), lambda i, j: (i, 0)),
                  pl.BlockSpec((1, d), lambda i, j: (0, 0)),
                  pl.BlockSpec((d, tn), lambda i, j: (0, j))],
        out_specs=[pl.BlockSpec((tm, tn), lambda i, j: (i, jnp.minimum(j, nab - 1))),
                   pl.BlockSpec((tm, tn), lambda i, j: (i, jnp.maximum(j - nab, 0)))],
        out_shape=[jax.ShapeDtypeStruct((n, A_COLS + B_COLS), BF16),
                   jax.ShapeDtypeStruct((n, C_COLS), F32)],
        scratch_shapes=[pltpu.VMEM((tm, d), BF16)],
        compiler_params=_params(("parallel", "arbitrary")),
        name="in_proj",
    )(x, g.reshape(1, d), w)


def _norm_matmul_kernel(x_ref, g_ref, w_ref, o_ref, xn_ref, h_ref):
    @pl.when(pl.program_id(1) == 0)
    def _():
        h = _rms(x_ref[...], g_ref[...]).astype(BF16)
        h_ref[...] = h
        xn_ref[...] = h

    o_ref[...] = jnp.dot(h_ref[...], w_ref[...], preferred_element_type=F32).astype(o_ref.dtype)


def norm_matmul(x, g, w, out_dtype, tm=512, tn=1024):
    n, d = x.shape
    ncol = w.shape[1]
    return pl.pallas_call(
        _norm_matmul_kernel,
        grid=(n // tm, ncol // tn),
        in_specs=[pl.BlockSpec((tm, d), lambda i, j: (i, 0)),
                  pl.BlockSpec((1, d), lambda i, j: (0, 0)),
                  pl.BlockSpec((d, tn), lambda i, j: (0, j))],
        out_specs=[pl.BlockSpec((tm, tn), lambda i, j: (i, j)),
                   pl.BlockSpec((tm, d), lambda i, j: (i, 0))],
        out_shape=[jax.ShapeDtypeStruct((n, ncol), out_dtype),
                   jax.ShapeDtypeStruct((n, d), BF16)],
        scratch_shapes=[pltpu.VMEM((tm, d), BF16)],
        compiler_params=_params(("parallel", "arbitrary")),
        name="norm_matmul",
    )(x, g.reshape(1, d), w)


def _matmul_res_kernel(a_ref, w_ref, r_ref, o_ref):
    o_ref[...] = r_ref[...] + jnp.dot(a_ref[...], w_ref[...], preferred_element_type=F32)


def matmul_res(a, w, res, tm=512, tn=1024):
    n, k = a.shape
    ncol = w.shape[1]
    return pl.pallas_call(
        _matmul_res_kernel,
        grid=(n // tm, ncol // tn),
        in_specs=[pl.BlockSpec((tm, k), lambda i, j: (i, 0)),
                  pl.BlockSpec((k, tn), lambda i, j: (0, j)),
                  pl.BlockSpec((tm, tn), lambda i, j: (i, j))],
        out_specs=pl.BlockSpec((tm, tn), lambda i, j: (i, j)),
        out_shape=jax.ShapeDtypeStruct((n, ncol), F32),
        compiler_params=_params(("parallel", "arbitrary")),
        name="matmul_res",
    )(a, w, res)


def _merge_kernel(x_ref, g_ref, ya_ref, yb_ref, yc_ref, wga_ref, wgb_ref, wgc_ref,
                  wa_ref, wb_ref, wc_ref, o_ref, h_ref):
    @pl.when(pl.program_id(1) == 0)
    def _():
        h_ref[...] = _rms(x_ref[...], g_ref[...]).astype(BF16)

    h = h_ref[...]

    def branch(y_ref, wg_ref, w_ref):
        gate = jnp.dot(h, wg_ref[...], preferred_element_type=F32)
        proj = jnp.dot(y_ref[...], w_ref[...], preferred_element_type=F32)
        return jax.nn.sigmoid(gate) * proj

    merged = branch(ya_ref, wga_ref, wa_ref) + branch(yb_ref, wgb_ref, wb_ref) + branch(yc_ref, wgc_ref, wc_ref)
    o_ref[...] = merged.astype(o_ref.dtype)


def merge(x, g, ya, yb, yc, w_gate, wa, wb, wc, tm=512, tn=512):
    n, d = x.shape
    nj = d // tn
    return pl.pallas_call(
        _merge_kernel,
        grid=(n // tm, nj),
        in_specs=[pl.BlockSpec((tm, d), lambda i, j: (i, 0)),
                  pl.BlockSpec((1, d), lambda i, j: (0, 0)),
                  pl.BlockSpec((tm, ya.shape[1]), lambda i, j: (i, 0)),
                  pl.BlockSpec((tm, yb.shape[1]), lambda i, j: (i, 0)),
                  pl.BlockSpec((tm, yc.shape[1]), lambda i, j: (i, 0)),
                  pl.BlockSpec((d, tn), lambda i, j: (0, j)),
                  pl.BlockSpec((d, tn), lambda i, j: (0, j + nj)),
                  pl.BlockSpec((d, tn), lambda i, j: (0, j + 2 * nj)),
                  pl.BlockSpec((wa.shape[0], tn), lambda i, j: (0, j)),
                  pl.BlockSpec((wb.shape[0], tn), lambda i, j: (0, j)),
                  pl.BlockSpec((wc.shape[0], tn), lambda i, j: (0, j))],
        out_specs=pl.BlockSpec((tm, tn), lambda i, j: (i, j)),
        out_shape=jax.ShapeDtypeStruct((n, d), BF16),
        scratch_shapes=[pltpu.VMEM((tm, d), BF16)],
        compiler_params=_params(("parallel", "arbitrary")),
        name="merge",
    )(x, g.reshape(1, d), ya, yb, yc, w_gate, w_gate, w_gate, wa, wb, wc)


def _rel_bucket(dist):
    d = jnp.maximum(dist, 0)
    max_exact = REL_BUCKETS // 2
    df = jnp.maximum(d, max_exact).astype(F32)
    large = max_exact + (jnp.log(df / max_exact) / math.log(REL_MAX_DIST / max_exact)
                         * (REL_BUCKETS - max_exact)).astype(I32)
    large = jnp.minimum(large, REL_BUCKETS - 1)
    return jnp.where(d < max_exact, d, large)


def _bucket_bias(bt, dist):
    onehot = jax.nn.one_hot(_rel_bucket(dist), REL_BUCKETS, dtype=F32)
    return jnp.einsum("...k,kh->...h", onehot, bt.astype(F32), precision=lax.Precision.HIGHEST)


def _bias_windows(bt, seq):
    dist = (jnp.arange(seq // TILE)[:, None] * TILE - (TILE - 1) + jnp.arange(2 * TILE)[None, :])
    w = _bucket_bias(bt, dist)
    w = jnp.where((dist >= 0)[..., None], w, 0.0)
    return jnp.transpose(w, (2, 0, 1))


def _gen_bias_tiles(win_ref, tiles_ref):
    n_heads, n_rel = win_ref.shape[0], win_ref.shape[1]

    def body(t, carry):
        h = t // n_rel
        r = t % n_rel
        row = win_ref[h, pl.ds(r, 1), :]
        x = jnp.broadcast_to(row, (TILE, 2 * TILE))
        tiles_ref[h, r] = pltpu.roll(x, TILE + 1, 1, stride=1, stride_axis=0)[:, :TILE]
        return carry

    lax.fori_loop(0, n_heads * n_rel, body, 0)


def _swa_kernel(q_ref, kp_ref, kc_ref, vp_ref, vc_ref, bias_ref, sink_ref, o_ref):
    i = pl.program_id(1)
    which = jnp.minimum(i, 1)
    group = A_HEADS // A_KV_HEADS
    outs = []
    for kvh in range(A_KV_HEADS):
        lo, hi = kvh * HEAD_DIM, (kvh + 1) * HEAD_DIM
        kk = jnp.concatenate([kp_ref[0, :, lo:hi], kc_ref[0, :, lo:hi]], axis=0)
        vv = jnp.concatenate([vp_ref[0, :, lo:hi], vc_ref[0, :, lo:hi]], axis=0)
        for g in range(group):
            h = kvh * group + g
            q = q_ref[0, :, h * HEAD_DIM:(h + 1) * HEAD_DIM]
            s = lax.dot_general(q, kk, NT, preferred_element_type=F32) * (HEAD_DIM ** -0.5)
            s = s + bias_ref[which, h]
            sink = sink_ref[h]
            m = jnp.maximum(jnp.max(s, axis=-1, keepdims=True), sink)
            p = jnp.exp(s - m)
            denom = jnp.sum(p, axis=-1, keepdims=True) + jnp.exp(sink - m)
            o = jnp.dot(p.astype(BF16), vv, preferred_element_type=F32) / denom
            outs.append(o.astype(BF16))
    o_ref[0] = jnp.concatenate(outs, axis=-1)


def _swa_bias(bt_a):
    qpos = jnp.arange(A_BLOCK)[:, None] + A_BLOCK
    kpos = jnp.arange(2 * A_BLOCK)[None, :]
    dist = qpos - kpos
    band = (dist >= 0) & (dist < A_WINDOW)
    bias = jnp.transpose(_bucket_bias(bt_a, dist), (2, 0, 1))
    general = jnp.where(band[None], bias, NEG)
    first = jnp.where((band & (kpos >= A_BLOCK))[None], bias, NEG)
    return jnp.stack([first, general])


def swa(p_a, bias, sinks):
    bsz, seq, _ = p_a.shape
    qw, kw = A_HEADS * HEAD_DIM, A_KV_HEADS * HEAD_DIM
    kblk, vblk = qw // kw, qw // kw + 1
    return pl.pallas_call(
        _swa_kernel,
        grid=(bsz, seq // A_BLOCK),
        in_specs=[pl.BlockSpec((1, A_BLOCK, qw), lambda b, i: (b, i, 0)),
                  pl.BlockSpec((1, A_BLOCK, kw), lambda b, i: (b, jnp.maximum(i - 1, 0), kblk)),
                  pl.BlockSpec((1, A_BLOCK, kw), lambda b, i: (b, i, kblk)),
                  pl.BlockSpec((1, A_BLOCK, kw), lambda b, i: (b, jnp.maximum(i - 1, 0), vblk)),
                  pl.BlockSpec((1, A_BLOCK, kw), lambda b, i: (b, i, vblk)),
                  pl.BlockSpec(bias.shape, lambda b, i: (0, 0, 0, 0)),
                  pl.BlockSpec(memory_space=pltpu.SMEM)],
        out_specs=pl.BlockSpec((1, A_BLOCK, qw), lambda b, i: (b, i, 0)),
        out_shape=jax.ShapeDtypeStruct((bsz, seq, qw), BF16),
        compiler_params=_params(("parallel", "arbitrary")),
        name="swa",
    )(p_a, p_a, p_a, p_a, p_a, bias, sinks)


def _moba_kernel(q_ref, k_ref, vt_ref, win_ref, o_ref, tiles_ref, kmean_ref, sel_ref):
    b, i = pl.program_id(0), pl.program_id(1)
    n_heads, n_blk = kmean_ref.shape[0], kmean_ref.shape[1]

    @pl.when((b == 0) & (i == 0))
    def _():
        _gen_bias_tiles(win_ref, tiles_ref)

    @pl.when(i == 0)
    def _():
        for h in range(n_heads):
            kmean_ref[h] = jnp.mean(k_ref[0, h].astype(F32), axis=1)

    row = lax.broadcasted_iota(I32, (n_blk, TILE), 0)
    for h in range(n_heads):
        q = q_ref[0, h]
        kmean = kmean_ref[h]
        kmean_hi = kmean.astype(BF16)
        kmean_lo = (kmean - kmean_hi.astype(F32)).astype(BF16)
        gate = (lax.dot_general(kmean_hi, q, NT, preferred_element_type=F32)
                + lax.dot_general(kmean_lo, q, NT, preferred_element_type=F32))
        gate = jnp.where(row < i, gate, NEG)
        sel = jnp.zeros(gate.shape, F32)
        for _ in range(B_TOPK):
            best = jnp.max(gate, axis=0, keepdims=True)
            first = jnp.min(jnp.where(gate == best, row, n_blk), axis=0, keepdims=True)
            pick = row == first
            sel = jnp.where(pick, 1.0, sel)
            gate = jnp.where(pick, -jnp.inf, gate)
        sel_ref[h] = jnp.where(row < i, sel, 0.0)

    key_pos = lax.broadcasted_iota(I32, (TILE, TILE), 0)
    qry_pos = lax.broadcasted_iota(I32, (TILE, TILE), 1)
    causal = jnp.where(qry_pos >= key_pos, 1.0, 0.0)

    def body(j, carry):
        out = []
        for g in range(0, n_heads, LOCKSTEP):
            pair = tuple(range(g, g + LOCKSTEP))
            s = [lax.dot_general(k_ref[0, h, j], q_ref[0, h], NT, preferred_element_type=F32) * (HEAD_DIM ** -0.5)
                 for h in pair]
            s = [s[k] + tiles_ref[pair[k], i - j] for k in range(LOCKSTEP)]
            ok = [jnp.where(j == i, causal, jnp.broadcast_to(sel_ref[h, pl.ds(j, 1), :], (TILE, TILE))) > 0.0
                  for h in pair]
            s = [jnp.where(ok[k], s[k], NEG) for k in range(LOCKSTEP)]
            m_new = [jnp.maximum(carry[pair[k]][0], jnp.max(s[k], axis=0, keepdims=True)) for k in range(LOCKSTEP)]
            alpha = [jnp.exp(carry[pair[k]][0] - m_new[k]) for k in range(LOCKSTEP)]
            p = [jnp.where(ok[k], jnp.exp(s[k] - m_new[k]), 0.0) for k in range(LOCKSTEP)]
            l_new = [alpha[k] * carry[pair[k]][1] + jnp.sum(p[k], axis=0, keepdims=True) for k in range(LOCKSTEP)]
            pv = [jnp.dot(vt_ref[0, pair[k], j], p[k].astype(BF16), preferred_element_type=F32) for k in range(LOCKSTEP)]
            for k in range(LOCKSTEP):
                out.append((m_new[k], l_new[k], alpha[k] * carry[pair[k]][2] + pv[k]))
        return tuple(out)

    init = tuple((jnp.full((1, TILE), NEG, F32), jnp.zeros((1, TILE), F32), jnp.zeros((HEAD_DIM, TILE), F32))
                 for _ in range(n_heads))
    final = lax.fori_loop(0, i + 1, body, init)
    for h in range(n_heads):
        _, l, acc = final[h]
        o_ref[0, h] = (acc / l).astype(o_ref.dtype)


def moba(q, k, vt, windows):
    bsz, n_heads, seq, d = q.shape
    n_blk = seq // TILE
    return pl.pallas_call(
        _moba_kernel,
        grid=(bsz, n_blk),
        in_specs=[pl.BlockSpec((1, n_heads, TILE, d), lambda b, i: (b, 0, i, 0)),
                  pl.BlockSpec((1, n_heads, n_blk, TILE, d), lambda b, i: (b, 0, 0, 0, 0)),
                  pl.BlockSpec((1, n_heads, n_blk, d, TILE), lambda b, i: (b, 0, 0, 0, 0)),
                  pl.BlockSpec(windows.shape, lambda b, i: (0, 0, 0))],
        out_specs=pl.BlockSpec((1, n_heads, d, TILE), lambda b, i: (b, 0, 0, i)),
        out_shape=jax.ShapeDtypeStruct((bsz, n_heads, d, seq), BF16),
        scratch_shapes=[pltpu.VMEM((n_heads, n_blk, TILE, TILE), F32),
                        pltpu.VMEM((n_heads, n_blk, d), F32),
                        pltpu.VMEM((n_heads, n_blk, TILE), F32)],
        compiler_params=_params(("arbitrary", "arbitrary")),
        name="moba",
    )(q, k, vt, windows)


def _dsa_kernel(qc_ref, iq_ref, mq_ref, clat_ref, mk_ref, gain_ref, wuk_ref, wuvt_ref, win_ref, o_ref,
                tiles_ref, cn_ref, cnt_ref, ik_ref, keys_ref, mask_ref, qabs_ref, *, top_k):
    b, c = pl.program_id(0), pl.program_id(1)
    n_blk = cn_ref.shape[0]
    n_kb = c + 1

    @pl.when((b == 0) & (c == 0))
    def _():
        _gen_bias_tiles(win_ref, tiles_ref)

    @pl.when(c == 0)
    def _():
        lat = _rms(clat_ref[0], gain_ref[...])
        for kb in range(n_blk):
            blk = lat[kb * TILE:(kb + 1) * TILE]
            cn_ref[kb] = blk.astype(BF16)
            cnt_ref[kb] = jnp.transpose(blk).astype(BF16)
            ik_ref[kb] = mk_ref[0, kb * TILE:(kb + 1) * TILE, :IDX_DIM].astype(BF16)

    key_in = lax.broadcasted_iota(I32, (TILE, TILE), 0)
    qry_in = lax.broadcasted_iota(I32, (TILE, TILE), 1)
    diag_causal = key_in <= qry_in

    iq = iq_ref[0].astype(BF16)
    iw_t = jnp.transpose(mq_ref[0])[IDX_DIM:IDX_DIM + IDX_HEADS, :]

    def score_block(kb, carry):
        score = jnp.zeros((TILE, TILE), F32)
        ik = ik_ref[kb]
        logits = [lax.dot_general(ik, iq[:, hh * IDX_DIM:(hh + 1) * IDX_DIM], NT, preferred_element_type=F32)
                  for hh in range(IDX_HEADS)]
        for hh in range(IDX_HEADS):
            score = score + jnp.maximum(logits[hh], 0.0) * iw_t[hh:hh + 1, :]
        score = score * ((IDX_DIM ** -0.5) * (IDX_HEADS ** -0.5))
        score = jnp.where(score == 0.0, 0.0, score)
        key = _sortable(score)
        keys_ref[kb] = jnp.where(kb < c, key, jnp.where(diag_causal, key, INT_MIN))
        return carry

    lax.fori_loop(0, n_kb, score_block, 0)

    def count(pred):
        def blk(kb, part):
            return part + jnp.sum(pred(keys_ref[kb], kb).reshape(TILE // SUBLANES, SUBLANES, TILE), axis=0)
        part = lax.fori_loop(0, n_kb, blk, jnp.zeros((SUBLANES, TILE), F32))
        return jnp.sum(part, axis=0, keepdims=True)

    def value_bit(t, tau):
        cand = tau + lax.shift_left(jnp.int32(1), 31 - t)
        cnt = count(lambda key, kb: jnp.where(key >= cand, 1.0, 0.0))
        return jnp.where(cnt >= top_k, cand, tau)

    tau = lax.fori_loop(0, 32, value_bit, jnp.full((1, TILE), INT_MIN, I32))
    need = top_k - count(lambda key, kb: jnp.where(key > tau, 1.0, 0.0))

    def index_bit(t, bound):
        cand = bound + lax.shift_left(jnp.int32(1), 11 - t)
        cnt = count(lambda key, kb: jnp.where(key == tau, jnp.where(kb * TILE + key_in < cand, 1.0, 0.0), 0.0))
        return jnp.where(cnt <= need, cand, bound)

    at_or_above = count(lambda key, kb: jnp.where(key >= tau, 1.0, 0.0))
    surplus = jnp.max(jnp.where(tau > INT_MIN, at_or_above - top_k, 0.0))
    bound = lax.cond(surplus > 0.0,
                     lambda: lax.fori_loop(0, 12, index_bit, jnp.zeros((1, TILE), I32)),
                     lambda: jnp.full((1, TILE), 2 ** 30, I32))

    def mask_block(kb, carry):
        key = keys_ref[kb]
        tied = jnp.where(key == tau, jnp.where(kb * TILE + key_in < bound, 1.0, 0.0), 0.0)
        chosen = jnp.where(key > tau, 1.0, tied)
        mask_ref[kb] = jnp.where(kb < c, chosen, jnp.where(diag_causal, chosen, 0.0))
        return carry

    lax.fori_loop(0, n_kb, mask_block, 0)

    qc = qc_ref[0].astype(BF16)
    for h in range(C_HEADS):
        qabs_ref[h] = lax.dot_general(wuk_ref[h], qc[:, h * HEAD_DIM:(h + 1) * HEAD_DIM], NT,
                                      preferred_element_type=F32).astype(BF16)

    def attend_block(kb, carry):
        ok = mask_ref[kb] > 0.0
        keys_lat, lat_keys = cn_ref[kb], cnt_ref[kb]
        out = []
        for g in range(0, C_HEADS, LOCKSTEP):
            pair = tuple(range(g, g + LOCKSTEP))
            s = [jnp.dot(keys_lat, qabs_ref[h], preferred_element_type=F32) * (HEAD_DIM ** -0.5) for h in pair]
            s = [jnp.where(ok, s[k] + tiles_ref[pair[k], c - kb], NEG) for k in range(LOCKSTEP)]
            m_new = [jnp.maximum(carry[pair[k]][0], jnp.max(s[k], axis=0, keepdims=True)) for k in range(LOCKSTEP)]
            alpha = [jnp.exp(carry[pair[k]][0] - m_new[k]) for k in range(LOCKSTEP)]
            p = [jnp.where(ok, jnp.exp(s[k] - m_new[k]), 0.0) for k in range(LOCKSTEP)]
            l_new = [alpha[k] * carry[pair[k]][1] + jnp.sum(p[k], axis=0, keepdims=True) for k in range(LOCKSTEP)]
            pv = [jnp.dot(lat_keys, p[k].astype(BF16), preferred_element_type=F32) for k in range(LOCKSTEP)]
            for k in range(LOCKSTEP):
                out.append((m_new[k], l_new[k], alpha[k] * carry[pair[k]][2] + pv[k]))
        return tuple(out)

    init = tuple((jnp.full((1, TILE), NEG, F32), jnp.zeros((1, TILE), F32), jnp.zeros((C_LATENT, TILE), F32))
                 for _ in range(C_HEADS))
    final = lax.fori_loop(0, n_kb, attend_block, init)
    for h in range(C_HEADS):
        _, l, acc = final[h]
        oc_t = (acc / l).astype(BF16)
        out_t = jnp.dot(wuvt_ref[h], oc_t, preferred_element_type=F32)
        o_ref[0, h * HEAD_DIM:(h + 1) * HEAD_DIM, :] = out_t.astype(o_ref.dtype)


def dsa(p_c, gain, w_uk, w_uv_t, windows):
    bsz, seq, _ = p_c.shape
    top_k = min(C_TOPK_MAX, seq // 4)
    qw = C_HEADS * HEAD_DIM
    n_blk = seq // TILE
    return pl.pallas_call(
        functools.partial(_dsa_kernel, top_k=top_k),
        grid=(bsz, n_blk),
        in_specs=[pl.BlockSpec((1, TILE, qw), lambda b, c: (b, c, 0)),
                  pl.BlockSpec((1, TILE, 256), lambda b, c: (b, c, 2)),
                  pl.BlockSpec((1, TILE, 128), lambda b, c: (b, c, 7)),
                  pl.BlockSpec((1, seq, 128), lambda b, c: (b, 0, 6)),
                  pl.BlockSpec((1, seq, 128), lambda b, c: (b, 0, 7)),
                  pl.BlockSpec((1, C_LATENT), lambda b, c: (0, 0)),
                  pl.BlockSpec(w_uk.shape, lambda b, c: (0, 0, 0)),
                  pl.BlockSpec(w_uv_t.shape, lambda b, c: (0, 0, 0)),
                  pl.BlockSpec(windows.shape, lambda b, c: (0, 0, 0))],
        out_specs=pl.BlockSpec((1, qw, TILE), lambda b, c: (b, 0, c)),
        out_shape=jax.ShapeDtypeStruct((bsz, qw, seq), BF16),
        scratch_shapes=[pltpu.VMEM((C_HEADS, n_blk, TILE, TILE), F32),
                        pltpu.VMEM((n_blk, TILE, C_LATENT), BF16),
                        pltpu.VMEM((n_blk, C_LATENT, TILE), BF16),
                        pltpu.VMEM((n_blk, TILE, IDX_DIM), BF16),
                        pltpu.VMEM((n_blk, TILE, TILE), I32),
                        pltpu.VMEM((n_blk, TILE, TILE), F32),
                        pltpu.VMEM((C_HEADS, C_LATENT, TILE), BF16)],
        compiler_params=_params(("arbitrary", "arbitrary")),
        name="dsa",
    )(p_c, p_c, p_c, p_c, p_c, gain.reshape(1, C_LATENT), w_uk, w_uv_t, windows)


PEER_TE = 1024
ROUTE_TM = 256
ROUTE_HEADS = 4
ROUTE_ROWS = 72


def _top_values(arrays, count):
    n = len(arrays)
    vals = [[] for _ in range(n)]
    rank = [jnp.full(a.shape, float(count), F32) for a in arrays]
    rest = list(arrays)
    for r in range(count):
        best = [jnp.max(rest[k], axis=0, keepdims=True) for k in range(n)]
        hit = [rest[k] == best[k] for k in range(n)]
        rank = [jnp.where(hit[k], float(r), rank[k]) for k in range(n)]
        rest = [jnp.where(hit[k], -jnp.inf, rest[k]) for k in range(n)]
        for k in range(n):
            vals[k].append(best[k])
    taken = [jnp.sum(jnp.where(rank[k] < count, 1.0, 0.0), axis=0, keepdims=True) for k in range(n)]
    return [(jnp.concatenate(vals[k], axis=0), rank[k], taken[k]) for k in range(n)]


def _multiplicities(rank, count):
    return jnp.concatenate([jnp.sum(jnp.where(rank == float(r), 1.0, 0.0), axis=0, keepdims=True)
                            for r in range(count)], axis=0)


def _route_kernel(q_ref, k_ref, c1_ref, r1_ref, e2_ref, reach_ref, keys_ref, wts_ref, s2_ref, t1_ref):
    heads = range(k_ref.shape[0])
    tm = q_ref.shape[0]
    for hh in heads:
        _route_prepare(q_ref[:, hh * PEER_QDIM:(hh + 1) * PEER_QDIM], k_ref.at[hh], c1_ref.at[hh], r1_ref.at[hh],
                       e2_ref.at[hh], keys_ref.at[hh], wts_ref.at[hh], s2_ref.at[hh], t1_ref.at[hh])

    def value_bit(t, taus):
        out = []
        for hh in heads:
            c = taus[hh] + lax.shift_left(jnp.int32(1), 31 - t)
            cnt = jnp.sum(jnp.where(keys_ref[hh] >= c, wts_ref[hh], 0.0), axis=0, keepdims=True)
            out.append(jnp.where(cnt >= PEER_TOPK, c, taus[hh]))
        return tuple(out)

    taus = lax.fori_loop(0, 32, value_bit, tuple(jnp.full((1, tm), INT_MIN, I32) for _ in heads))
    for hh in heads:
        _route_finish(taus[hh], c1_ref.at[hh], reach_ref.at[hh], keys_ref.at[hh], wts_ref.at[hh], s2_ref.at[hh],
                      t1_ref.at[hh])


def _route_prepare(q, k_ref, c1_ref, r1_ref, e2_ref, keys_ref, wts_ref, s2_ref, t1_ref):
    tm = q.shape[0]
    half = PEER_QDIM // 2
    s1 = lax.dot_general(k_ref[0], q[:, :half], NT, preferred_element_type=F32)
    s2 = lax.dot_general(k_ref[1], q[:, half:], NT, preferred_element_type=F32)
    (t1, rank, taken1), (t2, rank2, taken2) = _top_values([s1, s2], PEER_TOPK)
    repeats = jnp.max(taken1 + taken2) > 2.0 * PEER_TOPK
    ones = jnp.ones(t1.shape, F32)
    w1, w2 = lax.cond(repeats,
                      lambda: (_multiplicities(rank, PEER_TOPK), _multiplicities(rank2, PEER_TOPK)),
                      lambda: (ones, ones))

    r16 = lax.broadcasted_iota(I32, (16, tm), 0)
    r8 = lax.broadcasted_iota(I32, (8, tm), 0)
    pieces = [
        (t1[0:1] + t2, w1[0:1] * w2),
        (t1[1:2] + t2[:8], w1[1:2] * w2[:8]),
        (t1[2:3] + t2[:8], jnp.where(r8 < 5, w1[2:3] * w2[:8], 0.0)),
        (t1[3:4] + t2[:8], jnp.where(r8 < 4, w1[3:4] * w2[:8], 0.0)),
        (t2[0:1] + t1, jnp.where(r16 >= 4, w2[0:1] * w1, 0.0)),
        (t2[1:2] + t1[:8], jnp.where(r8 >= 4, w2[1:2] * w1[:8], 0.0)),
        (t2[2:3] + t1[:8], jnp.where(r8 == 4, w2[2:3] * w1[:8], 0.0)),
    ]
    keys_ref[...] = _sortable(jnp.concatenate([p[0] for p in pieces], axis=0))
    wts_ref[...] = jnp.concatenate([p[1] for p in pieces], axis=0)
    c1_ref[...] = s1.reshape(c1_ref.shape)
    r1_ref[...] = rank.reshape(r1_ref.shape)
    e2_ref[...] = jnp.exp(s2 - t2[0:1]).astype(BF16)
    s2_ref[...] = s2
    t1_ref[...] = t1


def _route_finish(tau_key, c1_ref, reach_ref, keys_ref, wts_ref, s2_ref, t1_ref):
    tau = _unsortable(tau_key)
    cand = _unsortable(keys_ref[...])
    top = cand[0:1]
    z = jnp.sum(jnp.where(keys_ref[...] >= tau_key, wts_ref[...] * jnp.exp(cand - top), 0.0), axis=0, keepdims=True)
    s2 = s2_ref[...]
    reach = jnp.zeros(s2.shape, F32)
    for r in range(PEER_TOPK):
        reach = reach + jnp.where(t1_ref[r:r + 1, :] + s2 >= tau, 1.0, 0.0)
    reach_ref[...] = reach.astype(BF16)
    c1_ref[...] = jnp.exp(c1_ref[...] - (t1_ref[0:1, :] + jnp.log(z)))


def peer_route(qp, sub_keys):
    n = qp.shape[0]
    tm = ROUTE_TM
    half = PEER_QDIM // 2
    hs = ROUTE_HEADS
    spec = pl.BlockSpec((hs, PEER_KEYS, tm), lambda i, h: (h, 0, i))
    sub = PEER_TE // PEER_KEYS
    chunked = pl.BlockSpec((hs, PEER_KEYS // sub, sub, tm), lambda i, h: (h, 0, 0, i))
    return pl.pallas_call(
        _route_kernel,
        grid=(n // tm, PEER_HEADS // hs),
        in_specs=[pl.BlockSpec((tm, hs * PEER_QDIM), lambda i, h: (i, h)),
                  pl.BlockSpec((hs, 2, PEER_KEYS, half), lambda i, h: (h, 0, 0, 0))],
        out_specs=[chunked, chunked, spec, spec],
        out_shape=[jax.ShapeDtypeStruct((PEER_HEADS, PEER_KEYS // sub, sub, n), F32),
                   jax.ShapeDtypeStruct((PEER_HEADS, PEER_KEYS // sub, sub, n), F32),
                   jax.ShapeDtypeStruct((PEER_HEADS, PEER_KEYS, n), BF16),
                   jax.ShapeDtypeStruct((PEER_HEADS, PEER_KEYS, n), BF16)],
        scratch_shapes=[pltpu.VMEM((hs, ROUTE_ROWS, tm), I32),
                        pltpu.VMEM((hs, ROUTE_ROWS, tm), F32),
                        pltpu.VMEM((hs, PEER_KEYS, tm), F32),
                        pltpu.VMEM((hs, PEER_TOPK, tm), F32)],
        compiler_params=_params(("parallel", "arbitrary")),
        name="peer_route",
    )(qp, sub_keys)


PACK = 16


def _peer_kernel(x_ref, xn_ref, c1_ref, r1_ref, e2_ref, reach_ref, u_ref, vt_ref, gout_ref, o_ref,
                 acc_ref, xnt_ref, hid_ref, *, norm_out):
    j = pl.program_id(1)
    n_sub = u_ref.shape[0] // PEER_KEYS
    tm = xn_ref.shape[0]

    @pl.when(j == 0)
    def _():
        acc_ref[...] = jnp.zeros_like(acc_ref)
        xnt_ref[...] = jnp.transpose(xn_ref[...].astype(F32)).astype(BF16)

    act = jnp.dot(u_ref[...], xnt_ref[...], preferred_element_type=F32)
    for ab in range(n_sub):
        rows = slice(ab * PEER_KEYS, (ab + 1) * PEER_KEYS)
        rank = [jnp.broadcast_to(r1_ref[h, 0, ab:ab + 1, :], (PACK, tm)).astype(BF16) for h in range(PEER_HEADS)]
        fac = [jnp.broadcast_to(c1_ref[h, 0, ab:ab + 1, :], (PACK, tm)).astype(BF16) for h in range(PEER_HEADS)]
        for tc in range(tm // LANES):
            cols = slice(tc * LANES, (tc + 1) * LANES)
            gate = jnp.zeros((PEER_KEYS, LANES), BF16)
            for h in range(PEER_HEADS):
                chosen = jnp.tile(rank[h][:, cols], (PEER_KEYS // PACK, 1)) < reach_ref[h, :, cols]
                e2 = e2_ref[h, :, cols]
                gate = gate + jnp.where(chosen, e2, jnp.zeros_like(e2)) * jnp.tile(fac[h][:, cols], (PEER_KEYS // PACK, 1))
            hid_ref[rows, cols] = gate * jax.nn.gelu(act[rows, cols].astype(BF16))
    acc_ref[...] += jnp.dot(vt_ref[...], hid_ref[...], preferred_element_type=F32)

    @pl.when(j == pl.num_programs(1) - 1)
    def _():
        y = x_ref[...] + jnp.transpose(acc_ref[...])
        o_ref[...] = _rms(y, gout_ref[...]) if norm_out else y


def peer(x, xn, c1, r1, e2, reach, u, vt, gain_out, norm_out, tm=512):
    n, d = x.shape
    te = PEER_TE
    n_chunks = PEER_N // te
    n_sub = te // PEER_KEYS
    chunk_spec = pl.BlockSpec((PEER_HEADS, 1, n_sub, tm), lambda i, j: (0, j, 0, i))
    tile_spec = pl.BlockSpec((PEER_HEADS, PEER_KEYS, tm), lambda i, j: (0, 0, i))
    return pl.pallas_call(
        functools.partial(_peer_kernel, norm_out=norm_out),
        grid=(n // tm, n_chunks),
        in_specs=[pl.BlockSpec((tm, d), lambda i, j: (i, 0)),
                  pl.BlockSpec((tm, d), lambda i, j: (i, 0)),
                  chunk_spec, chunk_spec, tile_spec, tile_spec,
                  pl.BlockSpec((te, d), lambda i, j: (j, 0)),
                  pl.BlockSpec((d, te), lambda i, j: (0, j)),
                  pl.BlockSpec((1, d), lambda i, j: (0, 0))],
        out_specs=pl.BlockSpec((tm, d), lambda i, j: (i, 0)),
        out_shape=jax.ShapeDtypeStruct((n, d), F32),
        scratch_shapes=[pltpu.VMEM((d, tm), F32),
                        pltpu.VMEM((d, tm), BF16),
                        pltpu.VMEM((te, tm), BF16)],
        compiler_params=_params(("parallel", "arbitrary"), vmem=PEER_VMEM_LIMIT),
        name="peer",
    )(x, xn, c1, r1, e2, reach, u, vt, gain_out.reshape(1, d))


def kernel(x, rel_bias, g_mix, w_in, a_sinks, c_lat_gain, c_w_uk, c_w_uv, w_br_a, w_br_b, w_br_c, w_out,
           g_ffn, peer_w_q, peer_sub_keys, peer_u, peer_v, g_final):
    bsz, seq, d = x.shape
    n = bsz * seq
    depth = w_in.shape[0]
    n_blk = seq // TILE
    off = [0] + [int(o) for o in np.cumsum(IN_SPLITS)]
    bt_a = rel_bias[:, :A_HEADS]
    bt_b = rel_bias[:, A_HEADS:A_HEADS + B_HEADS]
    bt_c = rel_bias[:, A_HEADS + B_HEADS:]
    bias_a = _swa_bias(bt_a)
    win_b = _bias_windows(bt_b, seq)
    win_c = _bias_windows(bt_c, seq)

    xf = x.reshape(n, d)
    for l in range(depth):
        w = w_in[l]
        pad = jnp.zeros((d, C_COLS - (off[11] - off[6])), F32)
        w_abc = jnp.concatenate([w[:, off[0]:off[6]],
                                 w[:, off[6]:off[7]], w[:, off[8]:off[9]], w[:, off[7]:off[8]],
                                 w[:, off[9]:off[11]], pad], axis=1).astype(BF16)
        w_g = w[:, off[11]:off[14]].astype(BF16)

        p_ab, p_c = in_proj(xf, g_mix[l], w_abc)
        p_a = p_ab.reshape(bsz, seq, -1)
        p_b = p_ab[:, A_COLS:].reshape(bsz, seq, 3, B_HEADS, HEAD_DIM)
        p_c = p_c.reshape(bsz, seq, -1)

        ya = swa(p_a, bias_a, a_sinks[l]).reshape(n, -1)

        qb = jnp.transpose(p_b[:, :, 0], (0, 2, 1, 3))
        kb = jnp.transpose(p_b[:, :, 1], (0, 2, 1, 3)).reshape(bsz, B_HEADS, n_blk, TILE, HEAD_DIM)
        vbt = jnp.transpose(p_b[:, :, 2].reshape(bsz, n_blk, TILE, B_HEADS, HEAD_DIM), (0, 3, 1, 4, 2))
        yb_t = moba(qb, kb, vbt, win_b)
        yb = jnp.transpose(yb_t, (0, 3, 1, 2)).reshape(n, -1)

        w_uk = jnp.transpose(c_w_uk[l], (1, 0, 2)).astype(BF16)
        w_uv_t = jnp.transpose(c_w_uv[l], (1, 2, 0)).astype(BF16)
        yc_t = dsa(p_c, c_lat_gain[l], w_uk, w_uv_t, win_c)
        yc = jnp.transpose(yc_t, (0, 2, 1)).reshape(n, -1)

        merged = merge(xf, g_mix[l], ya, yb, yc, w_g, w_br_a[l].astype(BF16), w_br_b[l].astype(BF16),
                       w_br_c[l].astype(BF16))
        xf = matmul_res(merged, w_out[l].astype(BF16), xf)

        qp, xn = norm_matmul(xf, g_ffn[l], peer_w_q[l].astype(BF16), BF16)
        c1, r1, e2, reach = peer_route(qp, peer_sub_keys[l].astype(BF16))
        xf = peer(xf, xn, c1, r1, e2, reach, peer_u[l].astype(BF16), jnp.transpose(peer_v[l]).astype(BF16),
                  g_final, norm_out=(l == depth - 1))

    return xf.reshape(bsz, seq, d)
```

```python
import functools
import math

import jax
import jax.numpy as jnp
import numpy as np
from jax import lax
from jax.experimental import pallas as pl
from jax.experimental.pallas import tpu as pltpu

F32 = jnp.float32
BF16 = jnp.bfloat16
I32 = jnp.int32

D_MODEL = 2048
HEAD_DIM = 64
A_HEADS, A_KV_HEADS, A_BLOCK, A_WINDOW = 16, 4, 128, 128
B_HEADS, B_BLOCK, B_TOPK = 8, 256, 3
C_HEADS, C_LATENT, IDX_HEADS, IDX_DIM, C_TOPK_MAX = 8, 128, 8, 32, 256
REL_BUCKETS, REL_MAX_DIST = 32, 2048
PEER_HEADS, PEER_KEYS, PEER_QDIM, PEER_TOPK = 8, 128, 256, 16
PEER_N = PEER_KEYS * PEER_KEYS
EPS = 1e-6
NEG = -1e30
INT_MIN = -(2 ** 31)
IN_SPLITS = (A_HEADS * HEAD_DIM, A_KV_HEADS * HEAD_DIM, A_KV_HEADS * HEAD_DIM,
             B_HEADS * HEAD_DIM, B_HEADS * HEAD_DIM, B_HEADS * HEAD_DIM,
             C_HEADS * HEAD_DIM, C_LATENT, IDX_HEADS * IDX_DIM, IDX_DIM, IDX_HEADS,
             D_MODEL, D_MODEL, D_MODEL)
A_COLS = sum(IN_SPLITS[0:3])
B_COLS = sum(IN_SPLITS[3:6])
C_COLS = 1024

TILE = 256
LANES = 128
SUBLANES = 8
LOCKSTEP = 8
VMEM_LIMIT = 56 * 1024 * 1024
PEER_VMEM_LIMIT = 58 * 1024 * 1024
NT = (((1,), (1,)), ((), ()))


def _params(sem, vmem=VMEM_LIMIT, flags=None):
    return pltpu.CompilerParams(dimension_semantics=sem, vmem_limit_bytes=vmem, flags=flags)


def _rms(x, g):
    return x * lax.rsqrt(jnp.mean(x * x, axis=-1, keepdims=True) + EPS) * g


def _sortable(v):
    bits = pltpu.bitcast(v, I32)
    return jnp.where(bits < 0, bits ^ jnp.int32(0x7FFFFFFF), bits)


def _unsortable(k):
    return pltpu.bitcast(jnp.where(k < 0, k ^ jnp.int32(0x7FFFFFFF), k), F32)


def _in_proj_kernel(x_ref, g_ref, w_ref, oab_ref, oc_ref, h_ref, *, nab):
    j = pl.program_id(1)

    @pl.when(j == 0)
    def _():
        h_ref[...] = _rms(x_ref[...], g_ref[...]).astype(BF16)

    y = jnp.dot(h_ref[...], w_ref[...], preferred_element_type=F32)

    @pl.when(j < nab)
    def _():
        oab_ref[...] = y.astype(oab_ref.dtype)

    @pl.when(j >= nab)
    def _():
        oc_ref[...] = y.astype(oc_ref.dtype)


def in_proj(x, g, w, tm=512, tn=1024):
    n, d = x.shape
    nab, nc = (A_COLS + B_COLS) // tn, C_COLS // tn
    return pl.pallas_call(
        functools.partial(_in_proj_kernel, nab=nab),
        grid=(n // tm, nab + nc),
        in_specs=[pl.BlockSpec((tm, d), lambda i, j: (i, 0)),
                  pl.BlockSpec((1, d), lambda i, j: (0, 0)),
                  pl.BlockSpec((d, tn), lambda i, j: (0, j))],
        out_specs=[pl.BlockSpec((tm, tn), lambda i, j: (i, jnp.minimum(j, nab - 1))),
                   pl.BlockSpec((tm, tn), lambda i, j: (i, jnp.maximum(j - nab, 0)))],
        out_shape=[jax.ShapeDtypeStruct((n, A_COLS + B_COLS), BF16),
                   jax.ShapeDtypeStruct((n, C_COLS), F32)],
        scratch_shapes=[pltpu.VMEM((tm, d), BF16)],
        compiler_params=_params(("parallel", "arbitrary")),
        name="in_proj",
    )(x, g.reshape(1, d), w)


def _norm_matmul_kernel(x_ref, g_ref, w_ref, o_ref, xn_ref, h_ref):
    @pl.when(pl.program_id(1) == 0)
    def _():
        h = _rms(x_ref[...], g_ref[...]).astype(BF16)
        h_ref[...] = h
        xn_ref[...] = h

    o_ref[...] = jnp.dot(h_ref[...], w_ref[...], preferred_element_type=F32).astype(o_ref.dtype)


def norm_matmul(x, g, w, out_dtype, tm=512, tn=1024):
    n, d = x.shape
    ncol = w.shape[1]
    return pl.pallas_call(
        _norm_matmul_kernel,
        grid=(n // tm, ncol // tn),
        in_specs=[pl.BlockSpec((tm, d), lambda i, j: (i, 0)),
                  pl.BlockSpec((1, d), lambda i, j: (0, 0)),
                  pl.BlockSpec((d, tn), lambda i, j: (0, j))],
        out_specs=[pl.BlockSpec((tm, tn), lambda i, j: (i, j)),
                   pl.BlockSpec((tm, d), lambda i, j: (i, 0))],
        out_shape=[jax.ShapeDtypeStruct((n, ncol), out_dtype),
                   jax.ShapeDtypeStruct((n, d), BF16)],
        scratch_shapes=[pltpu.VMEM((tm, d), BF16)],
        compiler_params=_params(("parallel", "arbitrary")),
        name="norm_matmul",
    )(x, g.reshape(1, d), w)


def _matmul_res_kernel(a_ref, w_ref, r_ref, o_ref):
    o_ref[...] = r_ref[...] + jnp.dot(a_ref[...], w_ref[...], preferred_element_type=F32)


def matmul_res(a, w, res, tm=512, tn=1024):
    n, k = a.shape
    ncol = w.shape[1]
    return pl.pallas_call(
        _matmul_res_kernel,
        grid=(n // tm, ncol // tn),
        in_specs=[pl.BlockSpec((tm, k), lambda i, j: (i, 0)),
                  pl.BlockSpec((k, tn), lambda i, j: (0, j)),
                  pl.BlockSpec((tm, tn), lambda i, j: (i, j))],
        out_specs=pl.BlockSpec((tm, tn), lambda i, j: (i, j)),
        out_shape=jax.ShapeDtypeStruct((n, ncol), F32),
        compiler_params=_params(("parallel", "arbitrary")),
        name="matmul_res",
    )(a, w, res)


def _merge_kernel(x_ref, g_ref, ya_ref, yb_ref, yc_ref, wga_ref, wgb_ref, wgc_ref,
                  wa_ref, wb_ref, wc_ref, o_ref, h_ref):
    @pl.when(pl.program_id(1) == 0)
    def _():
        h_ref[...] = _rms(x_ref[...], g_ref[...]).astype(BF16)

    h = h_ref[...]

    def branch(y_ref, wg_ref, w_ref):
        gate = jnp.dot(h, wg_ref[...], preferred_element_type=F32)
        proj = jnp.dot(y_ref[...], w_ref[...], preferred_element_type=F32)
        return jax.nn.sigmoid(gate) * proj

    merged = branch(ya_ref, wga_ref, wa_ref) + branch(yb_ref, wgb_ref, wb_ref) + branch(yc_ref, wgc_ref, wc_ref)
    o_ref[...] = merged.astype(o_ref.dtype)


def merge(x, g, ya, yb, yc, w_gate, wa, wb, wc, tm=512, tn=512):
    n, d = x.shape
    nj = d // tn
    return pl.pallas_call(
        _merge_kernel,
        grid=(n // tm, nj),
        in_specs=[pl.BlockSpec((tm, d), lambda i, j: (i, 0)),
                  pl.BlockSpec((1, d), lambda i, j: (0, 0)),
                  pl.BlockSpec((tm, ya.shape[1]), lambda i, j: (i, 0)),
                  pl.BlockSpec((tm, yb.shape[1]), lambda i, j: (i, 0)),
                  pl.BlockSpec((tm, yc.shape[1]), lambda i, j: (i, 0)),
                  pl.BlockSpec((d, tn), lambda i, j: (0, j)),
                  pl.BlockSpec((d, tn), lambda i, j: (0, j + nj)),
                  pl.BlockSpec((d, tn), lambda i, j: (0, j + 2 * nj)),
                  pl.BlockSpec((wa.shape[0], tn), lambda i, j: (0, j)),
                  pl.BlockSpec((wb.shape[0], tn), lambda i, j: (0, j)),
                  pl.BlockSpec((wc.shape[0], tn), lambda i, j: (0, j))],
        out_specs=pl.BlockSpec((tm, tn), lambda i, j: (i, j)),
        out_shape=jax.ShapeDtypeStruct((n, d), BF16),
        scratch_shapes=[pltpu.VMEM((tm, d), BF16)],
        compiler_params=_params(("parallel", "arbitrary")),
        name="merge",
    )(x, g.reshape(1, d), ya, yb, yc, w_gate, w_gate, w_gate, wa, wb, wc)


def _rel_bucket(dist):
    d = jnp.maximum(dist, 0)
    max_exact = REL_BUCKETS // 2
    df = jnp.maximum(d, max_exact).astype(F32)
    large = max_exact + (jnp.log(df / max_exact) / math.log(REL_MAX_DIST / max_exact)
                         * (REL_BUCKETS - max_exact)).astype(I32)
    large = jnp.minimum(large, REL_BUCKETS - 1)
    return jnp.where(d < max_exact, d, large)


def _bucket_bias(bt, dist):
    onehot = jax.nn.one_hot(_rel_bucket(dist), REL_BUCKETS, dtype=F32)
    return jnp.einsum("...k,kh->...h", onehot, bt.astype(F32), precision=lax.Precision.HIGHEST)


def _bias_windows(bt, seq):
    dist = (jnp.arange(seq // TILE)[:, None] * TILE - (TILE - 1) + jnp.arange(2 * TILE)[None, :])
    w = _bucket_bias(bt, dist)
    w = jnp.where((dist >= 0)[..., None], w, 0.0)
    return jnp.transpose(w, (2, 0, 1))


def _gen_bias_tiles(win_ref, tiles_ref):
    n_heads, n_rel = win_ref.shape[0], win_ref.shape[1]

    def body(t, carry):
        h = t // n_rel
        r = t % n_rel
        row = win_ref[h, pl.ds(r, 1), :]
        x = jnp.broadcast_to(row, (TILE, 2 * TILE))
        tiles_ref[h, r] = pltpu.roll(x, TILE + 1, 1, stride=1, stride_axis=0)[:, :TILE]
        return carry

    lax.fori_loop(0, n_heads * n_rel, body, 0)


def _swa_kernel(q_ref, kp_ref, kc_ref, vp_ref, vc_ref, bias_ref, sink_ref, o_ref):
    i = pl.program_id(1)
    which = jnp.minimum(i, 1)
    group = A_HEADS // A_KV_HEADS
    outs = []
    for kvh in range(A_KV_HEADS):
        lo, hi = kvh * HEAD_DIM, (kvh + 1) * HEAD_DIM
        kk = jnp.concatenate([kp_ref[0, :, lo:hi], kc_ref[0, :, lo:hi]], axis=0)
        vv = jnp.concatenate([vp_ref[0, :, lo:hi], vc_ref[0, :, lo:hi]], axis=0)
        for g in range(group):
            h = kvh * group + g
            q = q_ref[0, :, h * HEAD_DIM:(h + 1) * HEAD_DIM]
            s = lax.dot_general(q, kk, NT, preferred_element_type=F32) * (HEAD_DIM ** -0.5)
            s = s + bias_ref[which, h]
            sink = sink_ref[h]
            m = jnp.maximum(jnp.max(s, axis=-1, keepdims=True), sink)
            p = jnp.exp(s - m)
            denom = jnp.sum(p, axis=-1, keepdims=True) + jnp.exp(sink - m)
            o = jnp.dot(p.astype(BF16), vv, preferred_element_type=F32) / denom
            outs.append(o.astype(BF16))
    o_ref[0] = jnp.concatenate(outs, axis=-1)


def _swa_bias(bt_a):
    qpos = jnp.arange(A_BLOCK)[:, None] + A_BLOCK
    kpos = jnp.arange(2 * A_BLOCK)[None, :]
    dist = qpos - kpos
    band = (dist >= 0) & (dist < A_WINDOW)
    bias = jnp.transpose(_bucket_bias(bt_a, dist), (2, 0, 1))
    general = jnp.where(band[None], bias, NEG)
    first = jnp.where((band & (kpos >= A_BLOCK))[None], bias, NEG)
    return jnp.stack([first, general])


def swa(p_a, bias, sinks):
    bsz, seq, _ = p_a.shape
    qw, kw = A_HEADS * HEAD_DIM, A_KV_HEADS * HEAD_DIM
    kblk, vblk = qw // kw, qw // kw + 1
    return pl.pallas_call(
        _swa_kernel,
        grid=(bsz, seq // A_BLOCK),
        in_specs=[pl.BlockSpec((1, A_BLOCK, qw), lambda b, i: (b, i, 0)),
                  pl.BlockSpec((1, A_BLOCK, kw), lambda b, i: (b, jnp.maximum(i - 1, 0), kblk)),
                  pl.BlockSpec((1, A_BLOCK, kw), lambda b, i: (b, i, kblk)),
                  pl.BlockSpec((1, A_BLOCK, kw), lambda b, i: (b, jnp.maximum(i - 1, 0), vblk)),
                  pl.BlockSpec((1, A_BLOCK, kw), lambda b, i: (b, i, vblk)),
                  pl.BlockSpec(bias.shape, lambda b, i: (0, 0, 0, 0)),
                  pl.BlockSpec(memory_space=pltpu.SMEM)],
        out_specs=pl.BlockSpec((1, A_BLOCK, qw), lambda b, i: (b, i, 0)),
        out_shape=jax.ShapeDtypeStruct((bsz, seq, qw), BF16),
        compiler_params=_params(("parallel", "arbitrary")),
        name="swa",
    )(p_a, p_a, p_a, p_a, p_a, bias, sinks)


def _moba_kernel(q_ref, k_ref, vt_ref, win_ref, o_ref, tiles_ref, kmean_ref, sel_ref):
    b, i = pl.program_id(0), pl.program_id(1)
    n_heads, n_blk = kmean_ref.shape[0], kmean_ref.shape[1]

    @pl.when((b == 0) & (i == 0))
    def _():
        _gen_bias_tiles(win_ref, tiles_ref)

    @pl.when(i == 0)
    def _():
        for h in range(n_heads):
            kmean_ref[h] = jnp.mean(k_ref[0, h].astype(F32), axis=1)

    row = lax.broadcasted_iota(I32, (n_blk, TILE), 0)
    for h in range(n_heads):
        q = q_ref[0, h]
        kmean = kmean_ref[h]
        kmean_hi = kmean.astype(BF16)
        kmean_lo = (kmean - kmean_hi.astype(F32)).astype(BF16)
        gate = (lax.dot_general(kmean_hi, q, NT, preferred_element_type=F32)
                + lax.dot_general(kmean_lo, q, NT, preferred_element_type=F32))
        gate = jnp.where(row < i, gate, NEG)
        sel = jnp.zeros(gate.shape, F32)
        for _ in range(B_TOPK):
            best = jnp.max(gate, axis=0, keepdims=True)
            first = jnp.min(jnp.where(gate == best, row, n_blk), axis=0, keepdims=True)
            pick = row == first
            sel = jnp.where(pick, 1.0, sel)
            gate = jnp.where(pick, -jnp.inf, gate)
        sel_ref[h] = jnp.where(row < i, sel, 0.0)

    key_pos = lax.broadcasted_iota(I32, (TILE, TILE), 0)
    qry_pos = lax.broadcasted_iota(I32, (TILE, TILE), 1)
    causal = jnp.where(qry_pos >= key_pos, 1.0, 0.0)

    def body(j, carry):
        out = []
        for g in range(0, n_heads, LOCKSTEP):
            pair = tuple(range(g, g + LOCKSTEP))
            s = [lax.dot_general(k_ref[0, h, j], q_ref[0, h], NT, preferred_element_type=F32) * (HEAD_DIM ** -0.5)
                 for h in pair]
            s = [s[k] + tiles_ref[pair[k], i - j] for k in range(LOCKSTEP)]
            ok = [jnp.where(j == i, causal, jnp.broadcast_to(sel_ref[h, pl.ds(j, 1), :], (TILE, TILE))) > 0.0
                  for h in pair]
            s = [jnp.where(ok[k], s[k], NEG) for k in range(LOCKSTEP)]
            m_new = [jnp.maximum(carry[pair[k]][0], jnp.max(s[k], axis=0, keepdims=True)) for k in range(LOCKSTEP)]
            alpha = [jnp.exp(carry[pair[k]][0] - m_new[k]) for k in range(LOCKSTEP)]
            p = [jnp.where(ok[k], jnp.exp(s[k] - m_new[k]), 0.0) for k in range(LOCKSTEP)]
            l_new = [alpha[k] * carry[pair[k]][1] + jnp.sum(p[k], axis=0, keepdims=True) for k in range(LOCKSTEP)]
            pv = [jnp.dot(vt_ref[0, pair[k], j], p[k].astype(BF16), preferred_element_type=F32) for k in range(LOCKSTEP)]
            for k in range(LOCKSTEP):
                out.append((m_new[k], l_new[k], alpha[k] * carry[pair[k]][2] + pv[k]))
        return tuple(out)

    init = tuple((jnp.full((1, TILE), NEG, F32), jnp.zeros((1, TILE), F32), jnp.zeros((HEAD_DIM, TILE), F32))
                 for _ in range(n_heads))
    final = lax.fori_loop(0, i + 1, body, init)
    for h in range(n_heads):
        _, l, acc = final[h]
        o_ref[0, h] = (acc / l).astype(o_ref.dtype)


def moba(q, k, vt, windows):
    bsz, n_heads, seq, d = q.shape
    n_blk = seq // TILE
    return pl.pallas_call(
        _moba_kernel,
        grid=(bsz, n_blk),
        in_specs=[pl.BlockSpec((1, n_heads, TILE, d), lambda b, i: (b, 0, i, 0)),
                  pl.BlockSpec((1, n_heads, n_blk, TILE, d), lambda b, i: (b, 0, 0, 0, 0)),
                  pl.BlockSpec((1, n_heads, n_blk, d, TILE), lambda b, i: (b, 0, 0, 0, 0)),
                  pl.BlockSpec(windows.shape, lambda b, i: (0, 0, 0))],
        out_specs=pl.BlockSpec((1, n_heads, d, TILE), lambda b, i: (b, 0, 0, i)),
        out_shape=jax.ShapeDtypeStruct((bsz, n_heads, d, seq), BF16),
        scratch_shapes=[pltpu.VMEM((n_heads, n_blk, TILE, TILE), F32),
                        pltpu.VMEM((n_heads, n_blk, d), F32),
                        pltpu.VMEM((n_heads, n_blk, TILE), F32)],
        compiler_params=_params(("arbitrary", "arbitrary")),
        name="moba",
    )(q, k, vt, windows)


def _dsa_kernel(qc_ref, iq_ref, mq_ref, clat_ref, mk_ref, gain_ref, wuk_ref, wuvt_ref, win_ref, o_ref,
                tiles_ref, cn_ref, cnt_ref, ik_ref, keys_ref, mask_ref, qabs_ref, *, top_k):
    b, c = pl.program_id(0), pl.program_id(1)
    n_blk = cn_ref.shape[0]
    n_kb = c + 1

    @pl.when((b == 0) & (c == 0))
    def _():
        _gen_bias_tiles(win_ref, tiles_ref)

    @pl.when(c == 0)
    def _():
        lat = _rms(clat_ref[0], gain_ref[...])
        for kb in range(n_blk):
            blk = lat[kb * TILE:(kb + 1) * TILE]
            cn_ref[kb] = blk.astype(BF16)
            cnt_ref[kb] = jnp.transpose(blk).astype(BF16)
            ik_ref[kb] = mk_ref[0, kb * TILE:(kb + 1) * TILE, :IDX_DIM].astype(BF16)

    key_in = lax.broadcasted_iota(I32, (TILE, TILE), 0)
    qry_in = lax.broadcasted_iota(I32, (TILE, TILE), 1)
    diag_causal = key_in <= qry_in

    iq = iq_ref[0].astype(BF16)
    iw_t = jnp.transpose(mq_ref[0])[IDX_DIM:IDX_DIM + IDX_HEADS, :]

    def score_block(kb, carry):
        score = jnp.zeros((TILE, TILE), F32)
        ik = ik_ref[kb]
        logits = [lax.dot_general(ik, iq[:, hh * IDX_DIM:(hh + 1) * IDX_DIM], NT, preferred_element_type=F32)
                  for hh in range(IDX_HEADS)]
        for hh in range(IDX_HEADS):
            score = score + jnp.maximum(logits[hh], 0.0) * iw_t[hh:hh + 1, :]
        score = score * ((IDX_DIM ** -0.5) * (IDX_HEADS ** -0.5))
        score = jnp.where(score == 0.0, 0.0, score)
        key = _sortable(score)
        keys_ref[kb] = jnp.where(kb < c, key, jnp.where(diag_causal, key, INT_MIN))
        return carry

    lax.fori_loop(0, n_kb, score_block, 0)

    def count(pred):
        def blk(kb, part):
            return part + jnp.sum(pred(keys_ref[kb], kb).reshape(TILE // SUBLANES, SUBLANES, TILE), axis=0)
        part = lax.fori_loop(0, n_kb, blk, jnp.zeros((SUBLANES, TILE), F32))
        return jnp.sum(part, axis=0, keepdims=True)

    def value_bit(t, tau):
        cand = tau + lax.shift_left(jnp.int32(1), 31 - t)
        cnt = count(lambda key, kb: jnp.where(key >= cand, 1.0, 0.0))
        return jnp.where(cnt >= top_k, cand, tau)

    tau = lax.fori_loop(0, 32, value_bit, jnp.full((1, TILE), INT_MIN, I32))
    need = top_k - count(lambda key, kb: jnp.where(key > tau, 1.0, 0.0))

    def index_bit(t, bound):
        cand = bound + lax.shift_left(jnp.int32(1), 11 - t)
        cnt = count(lambda key, kb: jnp.where(key == tau, jnp.where(kb * TILE + key_in < cand, 1.0, 0.0), 0.0))
        return jnp.where(cnt <= need, cand, bound)

    at_or_above = count(lambda key, kb: jnp.where(key >= tau, 1.0, 0.0))
    surplus = jnp.max(jnp.where(tau > INT_MIN, at_or_above - top_k, 0.0))
    bound = lax.cond(surplus > 0.0,
                     lambda: lax.fori_loop(0, 12, index_bit, jnp.zeros((1, TILE), I32)),
                     lambda: jnp.full((1, TILE), 2 ** 30, I32))

    def mask_block(kb, carry):
        key = keys_ref[kb]
        tied = jnp.where(key == tau, jnp.where(kb * TILE + key_in < bound, 1.0, 0.0), 0.0)
        chosen = jnp.where(key > tau, 1.0, tied)
        mask_ref[kb] = jnp.where(kb < c, chosen, jnp.where(diag_causal, chosen, 0.0))
        return carry

    lax.fori_loop(0, n_kb, mask_block, 0)

    qc = qc_ref[0].astype(BF16)
    for h in range(C_HEADS):
        qabs_ref[h] = lax.dot_general(wuk_ref[h], qc[:, h * HEAD_DIM:(h + 1) * HEAD_DIM], NT,
                                      preferred_element_type=F32).astype(BF16)

    def attend_block(kb, carry):
        ok = mask_ref[kb] > 0.0
        keys_lat, lat_keys = cn_ref[kb], cnt_ref[kb]
        out = []
        for g in range(0, C_HEADS, LOCKSTEP):
            pair = tuple(range(g, g + LOCKSTEP))
            s = [jnp.dot(keys_lat, qabs_ref[h], preferred_element_type=F32) * (HEAD_DIM ** -0.5) for h in pair]
            s = [jnp.where(ok, s[k] + tiles_ref[pair[k], c - kb], NEG) for k in range(LOCKSTEP)]
            m_new = [jnp.maximum(carry[pair[k]][0], jnp.max(s[k], axis=0, keepdims=True)) for k in range(LOCKSTEP)]
            alpha = [jnp.exp(carry[pair[k]][0] - m_new[k]) for k in range(LOCKSTEP)]
            p = [jnp.where(ok, jnp.exp(s[k] - m_new[k]), 0.0) for k in range(LOCKSTEP)]
            l_new = [alpha[k] * carry[pair[k]][1] + jnp.sum(p[k], axis=0, keepdims=True) for k in range(LOCKSTEP)]
            pv = [jnp.dot(lat_keys, p[k].astype(BF16), preferred_element_type=F32) for k in range(LOCKSTEP)]
            for k in range(LOCKSTEP):
                out.append((m_new[k], l_new[k], alpha[k] * carry[pair[k]][2] + pv[k]))
        return tuple(out)

    init = tuple((jnp.full((1, TILE), NEG, F32), jnp.zeros((1, TILE), F32), jnp.zeros((C_LATENT, TILE), F32))
                 for _ in range(C_HEADS))
    final = lax.fori_loop(0, n_kb, attend_block, init)
    for h in range(C_HEADS):
        _, l, acc = final[h]
        oc_t = (acc / l).astype(BF16)
        out_t = jnp.dot(wuvt_ref[h], oc_t, preferred_element_type=F32)
        o_ref[0, h * HEAD_DIM:(h + 1) * HEAD_DIM, :] = out_t.astype(o_ref.dtype)


def dsa(p_c, gain, w_uk, w_uv_t, windows):
    bsz, seq, _ = p_c.shape
    top_k = min(C_TOPK_MAX, seq // 4)
    qw = C_HEADS * HEAD_DIM
    n_blk = seq // TILE
    return pl.pallas_call(
        functools.partial(_dsa_kernel, top_k=top_k),
        grid=(bsz, n_blk),
        in_specs=[pl.BlockSpec((1, TILE, qw), lambda b, c: (b, c, 0)),
                  pl.BlockSpec((1, TILE, 256), lambda b, c: (b, c, 2)),
                  pl.BlockSpec((1, TILE, 128), lambda b, c: (b, c, 7)),
                  pl.BlockSpec((1, seq, 128), lambda b, c: (b, 0, 6)),
                  pl.BlockSpec((1, seq, 128), lambda b, c: (b, 0, 7)),
                  pl.BlockSpec((1, C_LATENT), lambda b, c: (0, 0)),
                  pl.BlockSpec(w_uk.shape, lambda b, c: (0, 0, 0)),
                  pl.BlockSpec(w_uv_t.shape, lambda b, c: (0, 0, 0)),
                  pl.BlockSpec(windows.shape, lambda b, c: (0, 0, 0))],
        out_specs=pl.BlockSpec((1, qw, TILE), lambda b, c: (b, 0, c)),
        out_shape=jax.ShapeDtypeStruct((bsz, qw, seq), BF16),
        scratch_shapes=[pltpu.VMEM((C_HEADS, n_blk, TILE, TILE), F32),
                        pltpu.VMEM((n_blk, TILE, C_LATENT), BF16),
                        pltpu.VMEM((n_blk, C_LATENT, TILE), BF16),
                        pltpu.VMEM((n_blk, TILE, IDX_DIM), BF16),
                        pltpu.VMEM((n_blk, TILE, TILE), I32),
                        pltpu.VMEM((n_blk, TILE, TILE), F32),
                        pltpu.VMEM((C_HEADS, C_LATENT, TILE), BF16)],
        compiler_params=_params(("arbitrary", "arbitrary")),
        name="dsa",
    )(p_c, p_c, p_c, p_c, p_c, gain.reshape(1, C_LATENT), w_uk, w_uv_t, windows)


PEER_TE = 512
ROUTE_TM = 256
ROUTE_HEADS = 4
ROUTE_ROWS = 72


def _top_values(arrays, count):
    n = len(arrays)
    vals = [[] for _ in range(n)]
    rank = [jnp.full(a.shape, float(count), F32) for a in arrays]
    rest = list(arrays)
    for r in range(count):
        best = [jnp.max(rest[k], axis=0, keepdims=True) for k in range(n)]
        hit = [rest[k] == best[k] for k in range(n)]
        rank = [jnp.where(hit[k], float(r), rank[k]) for k in range(n)]
        rest = [jnp.where(hit[k], -jnp.inf, rest[k]) for k in range(n)]
        for k in range(n):
            vals[k].append(best[k])
    taken = [jnp.sum(jnp.where(rank[k] < count, 1.0, 0.0), axis=0, keepdims=True) for k in range(n)]
    return [(jnp.concatenate(vals[k], axis=0), rank[k], taken[k]) for k in range(n)]


def _multiplicities(rank, count):
    return jnp.concatenate([jnp.sum(jnp.where(rank == float(r), 1.0, 0.0), axis=0, keepdims=True)
                            for r in range(count)], axis=0)


def _route_kernel(q_ref, k_ref, c1_ref, r1_ref, e2_ref, reach_ref, keys_ref, wts_ref, s2_ref, t1_ref):
    heads = range(k_ref.shape[0])
    tm = q_ref.shape[0]
    for hh in heads:
        _route_prepare(q_ref[:, hh * PEER_QDIM:(hh + 1) * PEER_QDIM], k_ref.at[hh], c1_ref.at[hh], r1_ref.at[hh],
                       e2_ref.at[hh], keys_ref.at[hh], wts_ref.at[hh], s2_ref.at[hh], t1_ref.at[hh])

    def value_bit(t, taus):
        out = []
        for hh in heads:
            c = taus[hh] + lax.shift_left(jnp.int32(1), 31 - t)
            cnt = jnp.sum(jnp.where(keys_ref[hh] >= c, wts_ref[hh], 0.0), axis=0, keepdims=True)
            out.append(jnp.where(cnt >= PEER_TOPK, c, taus[hh]))
        return tuple(out)

    taus = lax.fori_loop(0, 32, value_bit, tuple(jnp.full((1, tm), INT_MIN, I32) for _ in heads))
    for hh in heads:
        _route_finish(taus[hh], c1_ref.at[hh], reach_ref.at[hh], keys_ref.at[hh], wts_ref.at[hh], s2_ref.at[hh],
                      t1_ref.at[hh])


def _route_prepare(q, k_ref, c1_ref, r1_ref, e2_ref, keys_ref, wts_ref, s2_ref, t1_ref):
    tm = q.shape[0]
    half = PEER_QDIM // 2
    s1 = lax.dot_general(k_ref[0], q[:, :half], NT, preferred_element_type=F32)
    s2 = lax.dot_general(k_ref[1], q[:, half:], NT, preferred_element_type=F32)
    (t1, rank, taken1), (t2, rank2, taken2) = _top_values([s1, s2], PEER_TOPK)
    repeats = jnp.max(taken1 + taken2) > 2.0 * PEER_TOPK
    ones = jnp.ones(t1.shape, F32)
    w1, w2 = lax.cond(repeats,
                      lambda: (_multiplicities(rank, PEER_TOPK), _multiplicities(rank2, PEER_TOPK)),
                      lambda: (ones, ones))

    r16 = lax.broadcasted_iota(I32, (16, tm), 0)
    r8 = lax.broadcasted_iota(I32, (8, tm), 0)
    pieces = [
        (t1[0:1] + t2, w1[0:1] * w2),
        (t1[1:2] + t2[:8], w1[1:2] * w2[:8]),
        (t1[2:3] + t2[:8], jnp.where(r8 < 5, w1[2:3] * w2[:8], 0.0)),
        (t1[3:4] + t2[:8], jnp.where(r8 < 4, w1[3:4] * w2[:8], 0.0)),
        (t2[0:1] + t1, jnp.where(r16 >= 4, w2[0:1] * w1, 0.0)),
        (t2[1:2] + t1[:8], jnp.where(r8 >= 4, w2[1:2] * w1[:8], 0.0)),
        (t2[2:3] + t1[:8], jnp.where(r8 == 4, w2[2:3] * w1[:8], 0.0)),
    ]
    keys_ref[...] = _sortable(jnp.concatenate([p[0] for p in pieces], axis=0))
    wts_ref[...] = jnp.concatenate([p[1] for p in pieces], axis=0)
    c1_ref[...] = s1.reshape(c1_ref.shape)
    r1_ref[...] = rank.reshape(r1_ref.shape)
    e2_ref[...] = jnp.exp(s2 - t2[0:1]).astype(BF16)
    s2_ref[...] = s2
    t1_ref[...] = t1


def _route_finish(tau_key, c1_ref, reach_ref, keys_ref, wts_ref, s2_ref, t1_ref):
    tau = _unsortable(tau_key)
    cand = _unsortable(keys_ref[...])
    top = cand[0:1]
    z = jnp.sum(jnp.where(keys_ref[...] >= tau_key, wts_ref[...] * jnp.exp(cand - top), 0.0), axis=0, keepdims=True)
    s2 = s2_ref[...]
    reach = jnp.zeros(s2.shape, F32)
    for r in range(PEER_TOPK):
        reach = reach + jnp.where(t1_ref[r:r + 1, :] + s2 >= tau, 1.0, 0.0)
    reach_ref[...] = reach.astype(BF16)
    c1_ref[...] = jnp.exp(c1_ref[...] - (t1_ref[0:1, :] + jnp.log(z)))


def peer_route(qp, sub_keys):
    n = qp.shape[0]
    tm = ROUTE_TM
    half = PEER_QDIM // 2
    hs = ROUTE_HEADS
    spec = pl.BlockSpec((hs, PEER_KEYS, tm), lambda i, h: (h, 0, i))
    sub = PEER_TE // PEER_KEYS
    chunked = pl.BlockSpec((hs, PEER_KEYS // sub, sub, tm), lambda i, h: (h, 0, 0, i))
    return pl.pallas_call(
        _route_kernel,
        grid=(n // tm, PEER_HEADS // hs),
        in_specs=[pl.BlockSpec((tm, hs * PEER_QDIM), lambda i, h: (i, h)),
                  pl.BlockSpec((hs, 2, PEER_KEYS, half), lambda i, h: (h, 0, 0, 0))],
        out_specs=[chunked, chunked, spec, spec],
        out_shape=[jax.ShapeDtypeStruct((PEER_HEADS, PEER_KEYS // sub, sub, n), F32),
                   jax.ShapeDtypeStruct((PEER_HEADS, PEER_KEYS // sub, sub, n), F32),
                   jax.ShapeDtypeStruct((PEER_HEADS, PEER_KEYS, n), BF16),
                   jax.ShapeDtypeStruct((PEER_HEADS, PEER_KEYS, n), BF16)],
        scratch_shapes=[pltpu.VMEM((hs, ROUTE_ROWS, tm), I32),
                        pltpu.VMEM((hs, ROUTE_ROWS, tm), F32),
                        pltpu.VMEM((hs, PEER_KEYS, tm), F32),
                        pltpu.VMEM((hs, PEER_TOPK, tm), F32)],
        compiler_params=_params(("parallel", "arbitrary")),
        name="peer_route",
    )(qp, sub_keys)


PACK = 16


def _peer_kernel(x_ref, xn_ref, c1_ref, r1_ref, e2_ref, reach_ref, u_hbm, vt_hbm, gout_ref, o_ref,
                 acc_ref, xnt_ref, e2c_ref, reachc_ref, act0_ref, act1_ref, hid0_ref, hid1_ref,
                 ubuf_ref, vbuf_ref, sem_ref, *, norm_out):
    i, j = pl.program_id(0), pl.program_id(1)
    n_i, n_j = pl.num_programs(0), pl.num_programs(1)
    n_chunks = n_j - 2
    te = ubuf_ref.shape[1]
    n_sub = te // PEER_KEYS
    tm = xn_ref.shape[0]

    def table_copies(step, slot):
        cu = jnp.minimum(step, n_chunks - 1)
        cv = jnp.clip(step - 2, 0, n_chunks - 1)
        return (pltpu.make_async_copy(u_hbm.at[pl.ds(cu * te, te), :], ubuf_ref.at[slot], sem_ref.at[0, slot]),
                pltpu.make_async_copy(vt_hbm.at[cv], vbuf_ref.at[slot], sem_ref.at[1, slot]))

    @pl.when((i == 0) & (j == 0))
    def _():
        for cp in table_copies(0, 0):
            cp.start()

    @pl.when(j == 0)
    def _():
        acc_ref[...] = jnp.zeros_like(acc_ref)
        act1_ref[...] = jnp.zeros_like(act1_ref)
        hid0_ref[...] = jnp.zeros_like(hid0_ref)
        xnt_ref[...] = jnp.transpose(xn_ref[...].astype(F32)).astype(BF16)
        e2c_ref[...] = e2_ref[...]
        reachc_ref[...] = reach_ref[...]

    def step(slot, act_w, act_r, hid_w, hid_r):
        for cp in table_copies(j, slot):
            cp.wait()

        @pl.when((i < n_i - 1) | (j < n_j - 1))
        def _():
            nxt = jnp.where(j == n_j - 1, 0, j + 1)
            for cp in table_copies(nxt, 1 - slot):
                cp.start()

        rank = [[jnp.broadcast_to(r1_ref[h, 0, ab:ab + 1, :], (PACK, tm)).astype(BF16) for h in range(PEER_HEADS)]
                for ab in range(n_sub)]
        fac = [[jnp.broadcast_to(c1_ref[h, 0, ab:ab + 1, :], (PACK, tm)).astype(BF16) for h in range(PEER_HEADS)]
               for ab in range(n_sub)]
        for ab in range(n_sub):
            rows = slice(ab * PEER_KEYS, (ab + 1) * PEER_KEYS)
            for tc in range(tm // LANES):
                cols = slice(tc * LANES, (tc + 1) * LANES)
                gate = jnp.zeros((PEER_KEYS, LANES), BF16)
                for h in range(PEER_HEADS):
                    chosen = jnp.tile(rank[ab][h][:, cols], (PEER_KEYS // PACK, 1)) < reachc_ref[h, :, cols]
                    e2 = e2c_ref[h, :, cols]
                    gate = gate + (jnp.where(chosen, e2, jnp.zeros_like(e2))
                                   * jnp.tile(fac[ab][h][:, cols], (PEER_KEYS // PACK, 1)))
                hid_w[rows, cols] = gate * jax.nn.gelu(act_r[rows, cols].astype(BF16))
        act_w[...] = jnp.dot(ubuf_ref[slot], xnt_ref[...], preferred_element_type=F32)
        acc_ref[...] += jnp.dot(vbuf_ref[slot], hid_r[...], preferred_element_type=F32)

    @pl.when(j % 2 == 0)
    def _():
        step(0, act0_ref, act1_ref, hid1_ref, hid0_ref)

    @pl.when(j % 2 == 1)
    def _():
        step(1, act1_ref, act0_ref, hid0_ref, hid1_ref)

    @pl.when(j == n_j - 1)
    def _():
        y = x_ref[...] + jnp.transpose(acc_ref[...])
        o_ref[...] = _rms(y, gout_ref[...]) if norm_out else y


def peer(x, xn, c1, r1, e2, reach, u, vt, gain_out, norm_out, tm=512):
    n, d = x.shape
    te = PEER_TE
    n_chunks = PEER_N // te
    n_sub = te // PEER_KEYS
    assert n_chunks % 2 == 0
    chunk_spec = pl.BlockSpec((PEER_HEADS, 1, n_sub, tm), lambda i, j: (0, jnp.clip(j - 1, 0, n_chunks - 1), 0, i))
    tile_spec = pl.BlockSpec((PEER_HEADS, PEER_KEYS, tm), lambda i, j: (0, 0, i))
    return pl.pallas_call(
        functools.partial(_peer_kernel, norm_out=norm_out),
        grid=(n // tm, n_chunks + 2),
        in_specs=[pl.BlockSpec((tm, d), lambda i, j: (i, 0)),
                  pl.BlockSpec((tm, d), lambda i, j: (i, 0)),
                  chunk_spec, chunk_spec, tile_spec, tile_spec,
                  pl.BlockSpec(memory_space=pl.ANY),
                  pl.BlockSpec(memory_space=pl.ANY),
                  pl.BlockSpec((1, d), lambda i, j: (0, 0))],
        out_specs=pl.BlockSpec((tm, d), lambda i, j: (i, 0)),
        out_shape=jax.ShapeDtypeStruct((n, d), F32),
        scratch_shapes=[pltpu.VMEM((d, tm), F32),
                        pltpu.VMEM((d, tm), BF16),
                        pltpu.VMEM((PEER_HEADS, PEER_KEYS, tm), BF16),
                        pltpu.VMEM((PEER_HEADS, PEER_KEYS, tm), BF16),
                        pltpu.VMEM((te, tm), F32),
                        pltpu.VMEM((te, tm), F32),
                        pltpu.VMEM((te, tm), BF16),
                        pltpu.VMEM((te, tm), BF16),
                        pltpu.VMEM((2, te, d), BF16),
                        pltpu.VMEM((2, d, te), BF16),
                        pltpu.SemaphoreType.DMA((2, 2))],
        compiler_params=_params(("arbitrary", "arbitrary"), vmem=PEER_VMEM_LIMIT),
        name="peer",
    )(x, xn, c1, r1, e2, reach, u, vt, gain_out.reshape(1, d))


def kernel(x, rel_bias, g_mix, w_in, a_sinks, c_lat_gain, c_w_uk, c_w_uv, w_br_a, w_br_b, w_br_c, w_out,
           g_ffn, peer_w_q, peer_sub_keys, peer_u, peer_v, g_final):
    bsz, seq, d = x.shape
    n = bsz * seq
    depth = w_in.shape[0]
    n_blk = seq // TILE
    off = [0] + [int(o) for o in np.cumsum(IN_SPLITS)]
    bt_a = rel_bias[:, :A_HEADS]
    bt_b = rel_bias[:, A_HEADS:A_HEADS + B_HEADS]
    bt_c = rel_bias[:, A_HEADS + B_HEADS:]
    bias_a = _swa_bias(bt_a)
    win_b = _bias_windows(bt_b, seq)
    win_c = _bias_windows(bt_c, seq)

    xf = x.reshape(n, d)
    for l in range(depth):
        w = w_in[l]
        pad = jnp.zeros((d, C_COLS - (off[11] - off[6])), F32)
        w_abc = jnp.concatenate([w[:, off[0]:off[6]],
                                 w[:, off[6]:off[7]], w[:, off[8]:off[9]], w[:, off[7]:off[8]],
                                 w[:, off[9]:off[11]], pad], axis=1).astype(BF16)
        w_g = w[:, off[11]:off[14]].astype(BF16)

        p_ab, p_c = in_proj(xf, g_mix[l], w_abc)
        p_a = p_ab.reshape(bsz, seq, -1)
        p_b = p_ab[:, A_COLS:].reshape(bsz, seq, 3, B_HEADS, HEAD_DIM)
        p_c = p_c.reshape(bsz, seq, -1)

        ya = swa(p_a, bias_a, a_sinks[l]).reshape(n, -1)

        qb = jnp.transpose(p_b[:, :, 0], (0, 2, 1, 3))
        kb = jnp.transpose(p_b[:, :, 1], (0, 2, 1, 3)).reshape(bsz, B_HEADS, n_blk, TILE, HEAD_DIM)
        vbt = jnp.transpose(p_b[:, :, 2].reshape(bsz, n_blk, TILE, B_HEADS, HEAD_DIM), (0, 3, 1, 4, 2))
        yb_t = moba(qb, kb, vbt, win_b)
        yb = jnp.transpose(yb_t, (0, 3, 1, 2)).reshape(n, -1)

        w_uk = jnp.transpose(c_w_uk[l], (1, 0, 2)).astype(BF16)
        w_uv_t = jnp.transpose(c_w_uv[l], (1, 2, 0)).astype(BF16)
        yc_t = dsa(p_c, c_lat_gain[l], w_uk, w_uv_t, win_c)
        yc = jnp.transpose(yc_t, (0, 2, 1)).reshape(n, -1)

        merged = merge(xf, g_mix[l], ya, yb, yc, w_g, w_br_a[l].astype(BF16), w_br_b[l].astype(BF16),
                       w_br_c[l].astype(BF16))
        xf = matmul_res(merged, w_out[l].astype(BF16), xf)

        qp, xn = norm_matmul(xf, g_ffn[l], peer_w_q[l].astype(BF16), BF16)
        c1, r1, e2, reach = peer_route(qp, peer_sub_keys[l].astype(BF16))
        vt = jnp.transpose(peer_v[l].reshape(PEER_N // PEER_TE, PEER_TE, d), (0, 2, 1)).astype(BF16)
        xf = peer(xf, xn, c1, r1, e2, reach, peer_u[l].astype(BF16), vt, g_final, norm_out=(l == depth - 1))

    return xf.reshape(bsz, seq, d)
```

```python
import functools
import math

import jax
import jax.numpy as jnp
import numpy as np
from jax import lax
from jax.experimental import pallas as pl
from jax.experimental.pallas import tpu as pltpu

F32 = jnp.float32
BF16 = jnp.bfloat16
I32 = jnp.int32

D_MODEL = 2048
HEAD_DIM = 64
A_HEADS, A_KV_HEADS, A_BLOCK, A_WINDOW = 16, 4, 128, 128
B_HEADS, B_BLOCK, B_TOPK = 8, 256, 3
C_HEADS, C_LATENT, IDX_HEADS, IDX_DIM, C_TOPK_MAX = 8, 128, 8, 32, 256
REL_BUCKETS, REL_MAX_DIST = 32, 2048
PEER_HEADS, PEER_KEYS, PEER_QDIM, PEER_TOPK = 8, 128, 256, 16
PEER_N = PEER_KEYS * PEER_KEYS
EPS = 1e-6
NEG = -1e30
INT_MIN = -(2 ** 31)
IN_SPLITS = (A_HEADS * HEAD_DIM, A_KV_HEADS * HEAD_DIM, A_KV_HEADS * HEAD_DIM,
             B_HEADS * HEAD_DIM, B_HEADS * HEAD_DIM, B_HEADS * HEAD_DIM,
             C_HEADS * HEAD_DIM, C_LATENT, IDX_HEADS * IDX_DIM, IDX_DIM, IDX_HEADS,
             D_MODEL, D_MODEL, D_MODEL)
A_COLS = sum(IN_SPLITS[0:3])
B_COLS = sum(IN_SPLITS[3:6])
C_COLS = 1024

TILE = 256
LANES = 128
SUBLANES = 8
LOCKSTEP = 8
VMEM_LIMIT = 56 * 1024 * 1024
PEER_VMEM_LIMIT = 58 * 1024 * 1024
NT = (((1,), (1,)), ((), ()))


def _params(sem, vmem=VMEM_LIMIT, flags=None):
    return pltpu.CompilerParams(dimension_semantics=sem, vmem_limit_bytes=vmem, flags=flags)


def _rms(x, g):
    return x * lax.rsqrt(jnp.mean(x * x, axis=-1, keepdims=True) + EPS) * g


def _sortable(v):
    bits = pltpu.bitcast(v, I32)
    return jnp.where(bits < 0, bits ^ jnp.int32(0x7FFFFFFF), bits)


def _unsortable(k):
    return pltpu.bitcast(jnp.where(k < 0, k ^ jnp.int32(0x7FFFFFFF), k), F32)


def _in_proj_kernel(x_ref, g_ref, w_ref, oab_ref, oc_ref, h_ref, *, nab):
    j = pl.program_id(1)

    @pl.when(j == 0)
    def _():
        h_ref[...] = _rms(x_ref[...], g_ref[...]).astype(BF16)

    y = jnp.dot(h_ref[...], w_ref[...], preferred_element_type=F32)

    @pl.when(j < nab)
    def _():
        oab_ref[...] = y.astype(oab_ref.dtype)

    @pl.when(j >= nab)
    def _():
        oc_ref[...] = y.astype(oc_ref.dtype)


def in_proj(x, g, w, tm=512, tn=1024):
    n, d = x.shape
    nab, nc = (A_COLS + B_COLS) // tn, C_COLS // tn
    return pl.pallas_call(
        functools.partial(_in_proj_kernel, nab=nab),
        grid=(n // tm, nab + nc),
        in_specs=[pl.BlockSpec((tm, d), lambda i, j: (i, 0)),
                  pl.BlockSpec((1, d), lambda i, j: (0, 0)),
                  pl.BlockSpec((d, tn), lambda i, j: (0, j))],
        out_specs=[pl.BlockSpec((tm, tn), lambda i, j: (i, jnp.minimum(j, nab - 1))),
                   pl.BlockSpec((tm, tn), lambda i, j: (i, jnp.maximum(j - nab, 0)))],
        out_shape=[jax.ShapeDtypeStruct((n, A_COLS + B_COLS), BF16),
                   jax.ShapeDtypeStruct((n, C_COLS), F32)],
        scratch_shapes=[pltpu.VMEM((tm, d), BF16)],
        compiler_params=_params(("parallel", "arbitrary")),
        name="in_proj",
    )(x, g.reshape(1, d), w)


def _norm_matmul_kernel(x_ref, g_ref, w_ref, o_ref, xn_ref, h_ref):
    @pl.when(pl.program_id(1) == 0)
    def _():
        h = _rms(x_ref[...], g_ref[...]).astype(BF16)
        h_ref[...] = h
        xn_ref[...] = h

    o_ref[...] = jnp.dot(h_ref[...], w_ref[...], preferred_element_type=F32).astype(o_ref.dtype)


def norm_matmul(x, g, w, out_dtype, tm=512, tn=1024):
    n, d = x.shape
    ncol = w.shape[1]
    return pl.pallas_call(
        _norm_matmul_kernel,
        grid=(n // tm, ncol // tn),
        in_specs=[pl.BlockSpec((tm, d), lambda i, j: (i, 0)),
                  pl.BlockSpec((1, d), lambda i, j: (0, 0)),
                  pl.BlockSpec((d, tn), lambda i, j: (0, j))],
        out_specs=[pl.BlockSpec((tm, tn), lambda i, j: (i, j)),
                   pl.BlockSpec((tm, d), lambda i, j: (i, 0))],
        out_shape=[jax.ShapeDtypeStruct((n, ncol), out_dtype),
                   jax.ShapeDtypeStruct((n, d), BF16)],
        scratch_shapes=[pltpu.VMEM((tm, d), BF16)],
        compiler_params=_params(("parallel", "arbitrary")),
        name="norm_matmul",
    )(x, g.reshape(1, d), w)


def _mix_out_kernel(x_ref, g_ref, ya_ref, yb_ref, yc_ref, wga_ref, wgb_ref, wgc_ref,
                    wa_ref, wb_ref, wc_ref, wo_ref, xres_ref, o_ref, h_ref, mg_ref, *, nj):
    j = pl.program_id(1)

    @pl.when(j == 0)
    def _():
        h_ref[...] = _rms(x_ref[...], g_ref[...]).astype(BF16)

    @pl.when(j < nj)
    def _():
        h = h_ref[...]

        def branch(y_ref, wg_ref, w_ref):
            gate = jnp.dot(h, wg_ref[...], preferred_element_type=F32)
            proj = jnp.dot(y_ref[...], w_ref[...], preferred_element_type=F32)
            return jax.nn.sigmoid(gate) * proj

        merged = (branch(ya_ref, wga_ref, wa_ref) + branch(yb_ref, wgb_ref, wb_ref)
                  + branch(yc_ref, wgc_ref, wc_ref))
        mg_ref[j] = merged.astype(BF16)

    @pl.when(j >= nj)
    def _():
        tn = mg_ref.shape[2]
        out = xres_ref[...]
        for k in range(nj):
            out = out + jnp.dot(mg_ref[k], wo_ref[k * tn:(k + 1) * tn, :], preferred_element_type=F32)
        o_ref[...] = out


def mix_out(x, g, ya, yb, yc, w_gate, wa, wb, wc, w_out, tm=512, tn=512):
    n, d = x.shape
    nj = d // tn

    def first(i, j):
        return (0, jnp.minimum(j, nj - 1))

    def second(i, j):
        return (0, jnp.maximum(j - nj, 0))

    return pl.pallas_call(
        functools.partial(_mix_out_kernel, nj=nj),
        grid=(n // tm, 2 * nj),
        in_specs=[pl.BlockSpec((tm, d), lambda i, j: (i, 0)),
                  pl.BlockSpec((1, d), lambda i, j: (0, 0)),
                  pl.BlockSpec((tm, ya.shape[1]), lambda i, j: (i, 0)),
                  pl.BlockSpec((tm, yb.shape[1]), lambda i, j: (i, 0)),
                  pl.BlockSpec((tm, yc.shape[1]), lambda i, j: (i, 0)),
                  pl.BlockSpec((d, tn), first),
                  pl.BlockSpec((d, tn), lambda i, j: (0, jnp.minimum(j, nj - 1) + nj)),
                  pl.BlockSpec((d, tn), lambda i, j: (0, jnp.minimum(j, nj - 1) + 2 * nj)),
                  pl.BlockSpec((wa.shape[0], tn), first),
                  pl.BlockSpec((wb.shape[0], tn), first),
                  pl.BlockSpec((wc.shape[0], tn), first),
                  pl.BlockSpec((d, tn), second),
                  pl.BlockSpec((tm, tn), lambda i, j: (i, jnp.maximum(j - nj, 0)))],
        out_specs=pl.BlockSpec((tm, tn), lambda i, j: (i, jnp.maximum(j - nj, 0))),
        out_shape=jax.ShapeDtypeStruct((n, d), F32),
        scratch_shapes=[pltpu.VMEM((tm, d), BF16),
                        pltpu.VMEM((nj, tm, tn), BF16)],
        compiler_params=_params(("parallel", "arbitrary")),
        name="mix_out",
    )(x, g.reshape(1, d), ya, yb, yc, w_gate, w_gate, w_gate, wa, wb, wc, w_out, x)


def _rel_bucket(dist):
    d = jnp.maximum(dist, 0)
    max_exact = REL_BUCKETS // 2
    df = jnp.maximum(d, max_exact).astype(F32)
    large = max_exact + (jnp.log(df / max_exact) / math.log(REL_MAX_DIST / max_exact)
                         * (REL_BUCKETS - max_exact)).astype(I32)
    large = jnp.minimum(large, REL_BUCKETS - 1)
    return jnp.where(d < max_exact, d, large)


def _bucket_bias(bt, dist):
    onehot = jax.nn.one_hot(_rel_bucket(dist), REL_BUCKETS, dtype=F32)
    return jnp.einsum("...k,kh->...h", onehot, bt.astype(F32), precision=lax.Precision.HIGHEST)


def _bias_windows(bt, seq):
    dist = (jnp.arange(seq // TILE)[:, None] * TILE - (TILE - 1) + jnp.arange(2 * TILE)[None, :])
    w = _bucket_bias(bt, dist)
    w = jnp.where((dist >= 0)[..., None], w, 0.0)
    return jnp.transpose(w, (2, 0, 1))


def _gen_bias_tiles(win_ref, tiles_ref):
    n_heads, n_rel = win_ref.shape[0], win_ref.shape[1]

    def body(t, carry):
        h = t // n_rel
        r = t % n_rel
        row = win_ref[h, pl.ds(r, 1), :]
        x = jnp.broadcast_to(row, (TILE, 2 * TILE))
        tiles_ref[h, r] = pltpu.roll(x, TILE + 1, 1, stride=1, stride_axis=0)[:, :TILE]
        return carry

    lax.fori_loop(0, n_heads * n_rel, body, 0)


def _swa_kernel(q_ref, kp_ref, kc_ref, vp_ref, vc_ref, bias_ref, sink_ref, o_ref):
    i = pl.program_id(1)
    which = jnp.minimum(i, 1)
    group = A_HEADS // A_KV_HEADS
    outs = []
    for kvh in range(A_KV_HEADS):
        lo, hi = kvh * HEAD_DIM, (kvh + 1) * HEAD_DIM
        kk = jnp.concatenate([kp_ref[0, :, lo:hi], kc_ref[0, :, lo:hi]], axis=0)
        vv = jnp.concatenate([vp_ref[0, :, lo:hi], vc_ref[0, :, lo:hi]], axis=0)
        for g in range(group):
            h = kvh * group + g
            q = q_ref[0, :, h * HEAD_DIM:(h + 1) * HEAD_DIM]
            s = lax.dot_general(q, kk, NT, preferred_element_type=F32) * (HEAD_DIM ** -0.5)
            s = s + bias_ref[which, h]
            sink = sink_ref[h]
            m = jnp.maximum(jnp.max(s, axis=-1, keepdims=True), sink)
            p = jnp.exp(s - m)
            denom = jnp.sum(p, axis=-1, keepdims=True) + jnp.exp(sink - m)
            o = jnp.dot(p.astype(BF16), vv, preferred_element_type=F32) / denom
            outs.append(o.astype(BF16))
    o_ref[0] = jnp.concatenate(outs, axis=-1)


def _swa_bias(bt_a):
    qpos = jnp.arange(A_BLOCK)[:, None] + A_BLOCK
    kpos = jnp.arange(2 * A_BLOCK)[None, :]
    dist = qpos - kpos
    band = (dist >= 0) & (dist < A_WINDOW)
    bias = jnp.transpose(_bucket_bias(bt_a, dist), (2, 0, 1))
    general = jnp.where(band[None], bias, NEG)
    first = jnp.where((band & (kpos >= A_BLOCK))[None], bias, NEG)
    return jnp.stack([first, general])


def swa(p_a, bias, sinks):
    bsz, seq, _ = p_a.shape
    qw, kw = A_HEADS * HEAD_DIM, A_KV_HEADS * HEAD_DIM
    kblk, vblk = qw // kw, qw // kw + 1
    return pl.pallas_call(
        _swa_kernel,
        grid=(bsz, seq // A_BLOCK),
        in_specs=[pl.BlockSpec((1, A_BLOCK, qw), lambda b, i: (b, i, 0)),
                  pl.BlockSpec((1, A_BLOCK, kw), lambda b, i: (b, jnp.maximum(i - 1, 0), kblk)),
                  pl.BlockSpec((1, A_BLOCK, kw), lambda b, i: (b, i, kblk)),
                  pl.BlockSpec((1, A_BLOCK, kw), lambda b, i: (b, jnp.maximum(i - 1, 0), vblk)),
                  pl.BlockSpec((1, A_BLOCK, kw), lambda b, i: (b, i, vblk)),
                  pl.BlockSpec(bias.shape, lambda b, i: (0, 0, 0, 0)),
                  pl.BlockSpec(memory_space=pltpu.SMEM)],
        out_specs=pl.BlockSpec((1, A_BLOCK, qw), lambda b, i: (b, i, 0)),
        out_shape=jax.ShapeDtypeStruct((bsz, seq, qw), BF16),
        compiler_params=_params(("parallel", "arbitrary")),
        name="swa",
    )(p_a, p_a, p_a, p_a, p_a, bias, sinks)


def _moba_kernel(q_ref, k_ref, vt_ref, win_ref, o_ref, tiles_ref, kmean_ref, sel_ref):
    b, i = pl.program_id(0), pl.program_id(1)
    n_heads, n_blk = kmean_ref.shape[0], kmean_ref.shape[1]

    @pl.when((b == 0) & (i == 0))
    def _():
        _gen_bias_tiles(win_ref, tiles_ref)

    @pl.when(i == 0)
    def _():
        for h in range(n_heads):
            kmean_ref[h] = jnp.mean(k_ref[0, h].astype(F32), axis=1)

    row = lax.broadcasted_iota(I32, (n_blk, TILE), 0)
    for h in range(n_heads):
        q = q_ref[0, h]
        kmean = kmean_ref[h]
        kmean_hi = kmean.astype(BF16)
        kmean_lo = (kmean - kmean_hi.astype(F32)).astype(BF16)
        gate = (lax.dot_general(kmean_hi, q, NT, preferred_element_type=F32)
                + lax.dot_general(kmean_lo, q, NT, preferred_element_type=F32))
        gate = jnp.where(row < i, gate, NEG)
        sel = jnp.zeros(gate.shape, F32)
        for _ in range(B_TOPK):
            best = jnp.max(gate, axis=0, keepdims=True)
            first = jnp.min(jnp.where(gate == best, row, n_blk), axis=0, keepdims=True)
            pick = row == first
            sel = jnp.where(pick, 1.0, sel)
            gate = jnp.where(pick, -jnp.inf, gate)
        sel_ref[h] = jnp.where(row < i, sel, 0.0)

    key_pos = lax.broadcasted_iota(I32, (TILE, TILE), 0)
    qry_pos = lax.broadcasted_iota(I32, (TILE, TILE), 1)
    causal = jnp.where(qry_pos >= key_pos, 1.0, 0.0)

    def body(j, carry):
        out = []
        for g in range(0, n_heads, LOCKSTEP):
            pair = tuple(range(g, g + LOCKSTEP))
            s = [lax.dot_general(k_ref[0, h, j], q_ref[0, h], NT, preferred_element_type=F32) * (HEAD_DIM ** -0.5)
                 for h in pair]
            s = [s[k] + tiles_ref[pair[k], i - j] for k in range(LOCKSTEP)]
            ok = [jnp.where(j == i, causal, jnp.broadcast_to(sel_ref[h, pl.ds(j, 1), :], (TILE, TILE))) > 0.0
                  for h in pair]
            s = [jnp.where(ok[k], s[k], NEG) for k in range(LOCKSTEP)]
            m_new = [jnp.maximum(carry[pair[k]][0], jnp.max(s[k], axis=0, keepdims=True)) for k in range(LOCKSTEP)]
            alpha = [jnp.exp(carry[pair[k]][0] - m_new[k]) for k in range(LOCKSTEP)]
            p = [jnp.where(ok[k], jnp.exp(s[k] - m_new[k]), 0.0) for k in range(LOCKSTEP)]
            l_new = [alpha[k] * carry[pair[k]][1] + jnp.sum(p[k], axis=0, keepdims=True) for k in range(LOCKSTEP)]
            pv = [jnp.dot(vt_ref[0, pair[k], j], p[k].astype(BF16), preferred_element_type=F32) for k in range(LOCKSTEP)]
            for k in range(LOCKSTEP):
                out.append((m_new[k], l_new[k], alpha[k] * carry[pair[k]][2] + pv[k]))
        return tuple(out)

    init = tuple((jnp.full((1, TILE), NEG, F32), jnp.zeros((1, TILE), F32), jnp.zeros((HEAD_DIM, TILE), F32))
                 for _ in range(n_heads))
    final = lax.fori_loop(0, i + 1, body, init)
    for h in range(n_heads):
        _, l, acc = final[h]
        o_ref[0, h] = (acc / l).astype(o_ref.dtype)


def moba(q, k, vt, windows):
    bsz, n_heads, seq, d = q.shape
    n_blk = seq // TILE
    return pl.pallas_call(
        _moba_kernel,
        grid=(bsz, n_blk),
        in_specs=[pl.BlockSpec((1, n_heads, TILE, d), lambda b, i: (b, 0, i, 0)),
                  pl.BlockSpec((1, n_heads, n_blk, TILE, d), lambda b, i: (b, 0, 0, 0, 0)),
                  pl.BlockSpec((1, n_heads, n_blk, d, TILE), lambda b, i: (b, 0, 0, 0, 0)),
                  pl.BlockSpec(windows.shape, lambda b, i: (0, 0, 0))],
        out_specs=pl.BlockSpec((1, n_heads, d, TILE), lambda b, i: (b, 0, 0, i)),
        out_shape=jax.ShapeDtypeStruct((bsz, n_heads, d, seq), BF16),
        scratch_shapes=[pltpu.VMEM((n_heads, n_blk, TILE, TILE), F32),
                        pltpu.VMEM((n_heads, n_blk, d), F32),
                        pltpu.VMEM((n_heads, n_blk, TILE), F32)],
        compiler_params=_params(("arbitrary", "arbitrary")),
        name="moba",
    )(q, k, vt, windows)


def _dsa_kernel(qc_ref, iq_ref, mq_ref, clat_ref, mk_ref, gain_ref, wuk_ref, wuvt_ref, win_ref, o_ref,
                tiles_ref, cn_ref, cnt_ref, ik_ref, keys_ref, mask_ref, qabs_ref, *, top_k):
    b, c = pl.program_id(0), pl.program_id(1)
    n_blk = cn_ref.shape[0]
    n_kb = c + 1

    @pl.when((b == 0) & (c == 0))
    def _():
        _gen_bias_tiles(win_ref, tiles_ref)

    @pl.when(c == 0)
    def _():
        lat = _rms(clat_ref[0], gain_ref[...])
        for kb in range(n_blk):
            blk = lat[kb * TILE:(kb + 1) * TILE]
            cn_ref[kb] = blk.astype(BF16)
            cnt_ref[kb] = jnp.transpose(blk).astype(BF16)
            ik_ref[kb] = mk_ref[0, kb * TILE:(kb + 1) * TILE, :IDX_DIM].astype(BF16)

    key_in = lax.broadcasted_iota(I32, (TILE, TILE), 0)
    qry_in = lax.broadcasted_iota(I32, (TILE, TILE), 1)
    diag_causal = key_in <= qry_in

    iq = iq_ref[0].astype(BF16)
    iw_t = jnp.transpose(mq_ref[0])[IDX_DIM:IDX_DIM + IDX_HEADS, :]

    def score_block(kb, carry):
        score = jnp.zeros((TILE, TILE), F32)
        ik = ik_ref[kb]
        logits = [lax.dot_general(ik, iq[:, hh * IDX_DIM:(hh + 1) * IDX_DIM], NT, preferred_element_type=F32)
                  for hh in range(IDX_HEADS)]
        for hh in range(IDX_HEADS):
            score = score + jnp.maximum(logits[hh], 0.0) * iw_t[hh:hh + 1, :]
        score = score * ((IDX_DIM ** -0.5) * (IDX_HEADS ** -0.5))
        score = jnp.where(score == 0.0, 0.0, score)
        key = _sortable(score)
        keys_ref[kb] = jnp.where(kb < c, key, jnp.where(diag_causal, key, INT_MIN))
        return carry

    lax.fori_loop(0, n_kb, score_block, 0)

    def count(pred):
        def blk(kb, part):
            return part + jnp.sum(pred(keys_ref[kb], kb).reshape(TILE // SUBLANES, SUBLANES, TILE), axis=0)
        part = lax.fori_loop(0, n_kb, blk, jnp.zeros((SUBLANES, TILE), F32))
        return jnp.sum(part, axis=0, keepdims=True)

    def value_bit(t, tau):
        cand = tau + lax.shift_left(jnp.int32(1), 31 - t)
        cnt = count(lambda key, kb: jnp.where(key >= cand, 1.0, 0.0))
        return jnp.where(cnt >= top_k, cand, tau)

    tau = lax.fori_loop(0, 32, value_bit, jnp.full((1, TILE), INT_MIN, I32))
    need = top_k - count(lambda key, kb: jnp.where(key > tau, 1.0, 0.0))

    def index_bit(t, bound):
        cand = bound + lax.shift_left(jnp.int32(1), 11 - t)
        cnt = count(lambda key, kb: jnp.where(key == tau, jnp.where(kb * TILE + key_in < cand, 1.0, 0.0), 0.0))
        return jnp.where(cnt <= need, cand, bound)

    at_or_above = count(lambda key, kb: jnp.where(key >= tau, 1.0, 0.0))
    surplus = jnp.max(jnp.where(tau > INT_MIN, at_or_above - top_k, 0.0))
    bound = lax.cond(surplus > 0.0,
                     lambda: lax.fori_loop(0, 12, index_bit, jnp.zeros((1, TILE), I32)),
                     lambda: jnp.full((1, TILE), 2 ** 30, I32))

    def mask_block(kb, carry):
        key = keys_ref[kb]
        tied = jnp.where(key == tau, jnp.where(kb * TILE + key_in < bound, 1.0, 0.0), 0.0)
        chosen = jnp.where(key > tau, 1.0, tied)
        mask_ref[kb] = jnp.where(kb < c, chosen, jnp.where(diag_causal, chosen, 0.0))
        return carry

    lax.fori_loop(0, n_kb, mask_block, 0)

    qc = qc_ref[0].astype(BF16)
    for h in range(C_HEADS):
        qabs_ref[h] = lax.dot_general(wuk_ref[h], qc[:, h * HEAD_DIM:(h + 1) * HEAD_DIM], NT,
                                      preferred_element_type=F32).astype(BF16)

    def attend_block(kb, carry):
        ok = mask_ref[kb] > 0.0
        keys_lat, lat_keys = cn_ref[kb], cnt_ref[kb]
        out = []
        for g in range(0, C_HEADS, LOCKSTEP):
            pair = tuple(range(g, g + LOCKSTEP))
            s = [jnp.dot(keys_lat, qabs_ref[h], preferred_element_type=F32) * (HEAD_DIM ** -0.5) for h in pair]
            s = [jnp.where(ok, s[k] + tiles_ref[pair[k], c - kb], NEG) for k in range(LOCKSTEP)]
            m_new = [jnp.maximum(carry[pair[k]][0], jnp.max(s[k], axis=0, keepdims=True)) for k in range(LOCKSTEP)]
            alpha = [jnp.exp(carry[pair[k]][0] - m_new[k]) for k in range(LOCKSTEP)]
            p = [jnp.where(ok, jnp.exp(s[k] - m_new[k]), 0.0) for k in range(LOCKSTEP)]
            l_new = [alpha[k] * carry[pair[k]][1] + jnp.sum(p[k], axis=0, keepdims=True) for k in range(LOCKSTEP)]
            pv = [jnp.dot(lat_keys, p[k].astype(BF16), preferred_element_type=F32) for k in range(LOCKSTEP)]
            for k in range(LOCKSTEP):
                out.append((m_new[k], l_new[k], alpha[k] * carry[pair[k]][2] + pv[k]))
        return tuple(out)

    init = tuple((jnp.full((1, TILE), NEG, F32), jnp.zeros((1, TILE), F32), jnp.zeros((C_LATENT, TILE), F32))
                 for _ in range(C_HEADS))
    final = lax.fori_loop(0, n_kb, attend_block, init)
    for h in range(C_HEADS):
        _, l, acc = final[h]
        oc_t = (acc / l).astype(BF16)
        out_t = jnp.dot(wuvt_ref[h], oc_t, preferred_element_type=F32)
        o_ref[0, h * HEAD_DIM:(h + 1) * HEAD_DIM, :] = out_t.astype(o_ref.dtype)


def dsa(p_c, gain, w_uk, w_uv_t, windows):
    bsz, seq, _ = p_c.shape
    top_k = min(C_TOPK_MAX, seq // 4)
    qw = C_HEADS * HEAD_DIM
    n_blk = seq // TILE
    return pl.pallas_call(
        functools.partial(_dsa_kernel, top_k=top_k),
        grid=(bsz, n_blk),
        in_specs=[pl.BlockSpec((1, TILE, qw), lambda b, c: (b, c, 0)),
                  pl.BlockSpec((1, TILE, 256), lambda b, c: (b, c, 2)),
                  pl.BlockSpec((1, TILE, 128), lambda b, c: (b, c, 7)),
                  pl.BlockSpec((1, seq, 128), lambda b, c: (b, 0, 6)),
                  pl.BlockSpec((1, seq, 128), lambda b, c: (b, 0, 7)),
                  pl.BlockSpec((1, C_LATENT), lambda b, c: (0, 0)),
                  pl.BlockSpec(w_uk.shape, lambda b, c: (0, 0, 0)),
                  pl.BlockSpec(w_uv_t.shape, lambda b, c: (0, 0, 0)),
                  pl.BlockSpec(windows.shape, lambda b, c: (0, 0, 0))],
        out_specs=pl.BlockSpec((1, qw, TILE), lambda b, c: (b, 0, c)),
        out_shape=jax.ShapeDtypeStruct((bsz, qw, seq), BF16),
        scratch_shapes=[pltpu.VMEM((C_HEADS, n_blk, TILE, TILE), F32),
                        pltpu.VMEM((n_blk, TILE, C_LATENT), BF16),
                        pltpu.VMEM((n_blk, C_LATENT, TILE), BF16),
                        pltpu.VMEM((n_blk, TILE, IDX_DIM), BF16),
                        pltpu.VMEM((n_blk, TILE, TILE), I32),
                        pltpu.VMEM((n_blk, TILE, TILE), F32),
                        pltpu.VMEM((C_HEADS, C_LATENT, TILE), BF16)],
        compiler_params=_params(("arbitrary", "arbitrary")),
        name="dsa",
    )(p_c, p_c, p_c, p_c, p_c, gain.reshape(1, C_LATENT), w_uk, w_uv_t, windows)


PEER_TE = 1024
ROUTE_TM = 256
ROUTE_HEADS = 4
ROUTE_ROWS = 72


def _top_values(arrays, count):
    n = len(arrays)
    vals = [[] for _ in range(n)]
    rank = [jnp.full(a.shape, float(count), F32) for a in arrays]
    rest = list(arrays)
    for r in range(count):
        best = [jnp.max(rest[k], axis=0, keepdims=True) for k in range(n)]
        hit = [rest[k] == best[k] for k in range(n)]
        rank = [jnp.where(hit[k], float(r), rank[k]) for k in range(n)]
        rest = [jnp.where(hit[k], -jnp.inf, rest[k]) for k in range(n)]
        for k in range(n):
            vals[k].append(best[k])
    taken = [jnp.sum(jnp.where(rank[k] < count, 1.0, 0.0), axis=0, keepdims=True) for k in range(n)]
    return [(jnp.concatenate(vals[k], axis=0), rank[k], taken[k]) for k in range(n)]


def _multiplicities(rank, count):
    return jnp.concatenate([jnp.sum(jnp.where(rank == float(r), 1.0, 0.0), axis=0, keepdims=True)
                            for r in range(count)], axis=0)


def _route_kernel(q_ref, k_ref, c1_ref, r1_ref, e2_ref, reach_ref, keys_ref, wts_ref, s2_ref, t1_ref):
    heads = range(k_ref.shape[0])
    tm = q_ref.shape[0]
    for hh in heads:
        _route_prepare(q_ref[:, hh * PEER_QDIM:(hh + 1) * PEER_QDIM], k_ref.at[hh], c1_ref.at[hh], r1_ref.at[hh],
                       e2_ref.at[hh], keys_ref.at[hh], wts_ref.at[hh], s2_ref.at[hh], t1_ref.at[hh])

    def value_bit(t, taus):
        out = []
        for hh in heads:
            c = taus[hh] + lax.shift_left(jnp.int32(1), 31 - t)
            cnt = jnp.sum(jnp.where(keys_ref[hh] >= c, wts_ref[hh], 0.0), axis=0, keepdims=True)
            out.append(jnp.where(cnt >= PEER_TOPK, c, taus[hh]))
        return tuple(out)

    taus = lax.fori_loop(0, 32, value_bit, tuple(jnp.full((1, tm), INT_MIN, I32) for _ in heads))
    for hh in heads:
        _route_finish(taus[hh], c1_ref.at[hh], reach_ref.at[hh], keys_ref.at[hh], wts_ref.at[hh], s2_ref.at[hh],
                      t1_ref.at[hh])


def _route_prepare(q, k_ref, c1_ref, r1_ref, e2_ref, keys_ref, wts_ref, s2_ref, t1_ref):
    tm = q.shape[0]
    half = PEER_QDIM // 2
    s1 = lax.dot_general(k_ref[0], q[:, :half], NT, preferred_element_type=F32)
    s2 = lax.dot_general(k_ref[1], q[:, half:], NT, preferred_element_type=F32)
    (t1, rank, taken1), (t2, rank2, taken2) = _top_values([s1, s2], PEER_TOPK)
    repeats = jnp.max(taken1 + taken2) > 2.0 * PEER_TOPK
    ones = jnp.ones(t1.shape, F32)
    w1, w2 = lax.cond(repeats,
                      lambda: (_multiplicities(rank, PEER_TOPK), _multiplicities(rank2, PEER_TOPK)),
                      lambda: (ones, ones))

    r16 = lax.broadcasted_iota(I32, (16, tm), 0)
    r8 = lax.broadcasted_iota(I32, (8, tm), 0)
    pieces = [
        (t1[0:1] + t2, w1[0:1] * w2),
        (t1[1:2] + t2[:8], w1[1:2] * w2[:8]),
        (t1[2:3] + t2[:8], jnp.where(r8 < 5, w1[2:3] * w2[:8], 0.0)),
        (t1[3:4] + t2[:8], jnp.where(r8 < 4, w1[3:4] * w2[:8], 0.0)),
        (t2[0:1] + t1, jnp.where(r16 >= 4, w2[0:1] * w1, 0.0)),
        (t2[1:2] + t1[:8], jnp.where(r8 >= 4, w2[1:2] * w1[:8], 0.0)),
        (t2[2:3] + t1[:8], jnp.where(r8 == 4, w2[2:3] * w1[:8], 0.0)),
    ]
    keys_ref[...] = _sortable(jnp.concatenate([p[0] for p in pieces], axis=0))
    wts_ref[...] = jnp.concatenate([p[1] for p in pieces], axis=0)
    c1_ref[...] = s1.reshape(c1_ref.shape)
    r1_ref[...] = rank.reshape(r1_ref.shape)
    e2_ref[...] = jnp.exp(s2 - t2[0:1]).astype(BF16)
    s2_ref[...] = s2
    t1_ref[...] = t1


def _route_finish(tau_key, c1_ref, reach_ref, keys_ref, wts_ref, s2_ref, t1_ref):
    tau = _unsortable(tau_key)
    cand = _unsortable(keys_ref[...])
    top = cand[0:1]
    z = jnp.sum(jnp.where(keys_ref[...] >= tau_key, wts_ref[...] * jnp.exp(cand - top), 0.0), axis=0, keepdims=True)
    s2 = s2_ref[...]
    reach = jnp.zeros(s2.shape, F32)
    for r in range(PEER_TOPK):
        reach = reach + jnp.where(t1_ref[r:r + 1, :] + s2 >= tau, 1.0, 0.0)
    reach_ref[...] = reach.astype(BF16)
    c1_ref[...] = jnp.exp(c1_ref[...] - (t1_ref[0:1, :] + jnp.log(z)))


def peer_route(qp, sub_keys):
    n = qp.shape[0]
    tm = ROUTE_TM
    half = PEER_QDIM // 2
    hs = ROUTE_HEADS
    spec = pl.BlockSpec((hs, PEER_KEYS, tm), lambda i, h: (h, 0, i))
    sub = PEER_TE // PEER_KEYS
    chunked = pl.BlockSpec((hs, PEER_KEYS // sub, sub, tm), lambda i, h: (h, 0, 0, i))
    return pl.pallas_call(
        _route_kernel,
        grid=(n // tm, PEER_HEADS // hs),
        in_specs=[pl.BlockSpec((tm, hs * PEER_QDIM), lambda i, h: (i, h)),
                  pl.BlockSpec((hs, 2, PEER_KEYS, half), lambda i, h: (h, 0, 0, 0))],
        out_specs=[chunked, chunked, spec, spec],
        out_shape=[jax.ShapeDtypeStruct((PEER_HEADS, PEER_KEYS // sub, sub, n), F32),
                   jax.ShapeDtypeStruct((PEER_HEADS, PEER_KEYS // sub, sub, n), F32),
                   jax.ShapeDtypeStruct((PEER_HEADS, PEER_KEYS, n), BF16),
                   jax.ShapeDtypeStruct((PEER_HEADS, PEER_KEYS, n), BF16)],
        scratch_shapes=[pltpu.VMEM((hs, ROUTE_ROWS, tm), I32),
                        pltpu.VMEM((hs, ROUTE_ROWS, tm), F32),
                        pltpu.VMEM((hs, PEER_KEYS, tm), F32),
                        pltpu.VMEM((hs, PEER_TOPK, tm), F32)],
        compiler_params=_params(("parallel", "arbitrary")),
        name="peer_route",
    )(qp, sub_keys)


PACK = 16


def _peer_kernel(x_ref, xn_ref, c1_ref, r1_ref, e2_ref, reach_ref, u_ref, vt_ref, gout_ref, o_ref,
                 acc_ref, xnt_ref, hid_ref, *, norm_out):
    j = pl.program_id(1)
    n_sub = u_ref.shape[0] // PEER_KEYS
    tm = xn_ref.shape[0]

    @pl.when(j == 0)
    def _():
        acc_ref[...] = jnp.zeros_like(acc_ref)
        xnt_ref[...] = jnp.transpose(xn_ref[...].astype(F32)).astype(BF16)

    act = jnp.dot(u_ref[...], xnt_ref[...], preferred_element_type=F32)
    for ab in range(n_sub):
        rows = slice(ab * PEER_KEYS, (ab + 1) * PEER_KEYS)
        rank = [jnp.broadcast_to(r1_ref[h, 0, ab:ab + 1, :], (PACK, tm)).astype(BF16) for h in range(PEER_HEADS)]
        fac = [jnp.broadcast_to(c1_ref[h, 0, ab:ab + 1, :], (PACK, tm)).astype(BF16) for h in range(PEER_HEADS)]
        for tc in range(tm // LANES):
            cols = slice(tc * LANES, (tc + 1) * LANES)
            gate = jnp.zeros((PEER_KEYS, LANES), BF16)
            for h in range(PEER_HEADS):
                chosen = jnp.tile(rank[h][:, cols], (PEER_KEYS // PACK, 1)) < reach_ref[h, :, cols]
                e2 = e2_ref[h, :, cols]
                gate = gate + jnp.where(chosen, e2, jnp.zeros_like(e2)) * jnp.tile(fac[h][:, cols], (PEER_KEYS // PACK, 1))
            hid_ref[rows, cols] = gate * jax.nn.gelu(act[rows, cols].astype(BF16))
    acc_ref[...] += jnp.dot(vt_ref[...], hid_ref[...], preferred_element_type=F32)

    @pl.when(j == pl.num_programs(1) - 1)
    def _():
        y = x_ref[...] + jnp.transpose(acc_ref[...])
        o_ref[...] = _rms(y, gout_ref[...]) if norm_out else y


def peer(x, xn, c1, r1, e2, reach, u, vt, gain_out, norm_out, tm=512):
    n, d = x.shape
    te = PEER_TE
    n_chunks = PEER_N // te
    n_sub = te // PEER_KEYS
    chunk_spec = pl.BlockSpec((PEER_HEADS, 1, n_sub, tm), lambda i, j: (0, j, 0, i))
    tile_spec = pl.BlockSpec((PEER_HEADS, PEER_KEYS, tm), lambda i, j: (0, 0, i))
    return pl.pallas_call(
        functools.partial(_peer_kernel, norm_out=norm_out),
        grid=(n // tm, n_chunks),
        in_specs=[pl.BlockSpec((tm, d), lambda i, j: (i, 0)),
                  pl.BlockSpec((tm, d), lambda i, j: (i, 0)),
                  chunk_spec, chunk_spec, tile_spec, tile_spec,
                  pl.BlockSpec((te, d), lambda i, j: (j, 0)),
                  pl.BlockSpec((d, te), lambda i, j: (0, j)),
                  pl.BlockSpec((1, d), lambda i, j: (0, 0))],
        out_specs=pl.BlockSpec((tm, d), lambda i, j: (i, 0)),
        out_shape=jax.ShapeDtypeStruct((n, d), F32),
        scratch_shapes=[pltpu.VMEM((d, tm), F32),
                        pltpu.VMEM((d, tm), BF16),
                        pltpu.VMEM((te, tm), BF16)],
        compiler_params=_params(("parallel", "arbitrary"), vmem=PEER_VMEM_LIMIT),
        name="peer",
    )(x, xn, c1, r1, e2, reach, u, vt, gain_out.reshape(1, d))


def kernel(x, rel_bias, g_mix, w_in, a_sinks, c_lat_gain, c_w_uk, c_w_uv, w_br_a, w_br_b, w_br_c, w_out,
           g_ffn, peer_w_q, peer_sub_keys, peer_u, peer_v, g_final):
    bsz, seq, d = x.shape
    n = bsz * seq
    depth = w_in.shape[0]
    n_blk = seq // TILE
    off = [0] + [int(o) for o in np.cumsum(IN_SPLITS)]
    bt_a = rel_bias[:, :A_HEADS]
    bt_b = rel_bias[:, A_HEADS:A_HEADS + B_HEADS]
    bt_c = rel_bias[:, A_HEADS + B_HEADS:]
    bias_a = _swa_bias(bt_a)
    win_b = _bias_windows(bt_b, seq)
    win_c = _bias_windows(bt_c, seq)

    xf = x.reshape(n, d)
    for l in range(depth):
        w = w_in[l]
        pad = jnp.zeros((d, C_COLS - (off[11] - off[6])), F32)
        w_abc = jnp.concatenate([w[:, off[0]:off[6]],
                                 w[:, off[6]:off[7]], w[:, off[8]:off[9]], w[:, off[7]:off[8]],
                                 w[:, off[9]:off[11]], pad], axis=1).astype(BF16)
        w_g = w[:, off[11]:off[14]].astype(BF16)

        p_ab, p_c = in_proj(xf, g_mix[l], w_abc)
        p_a = p_ab.reshape(bsz, seq, -1)
        p_b = p_ab[:, A_COLS:].reshape(bsz, seq, 3, B_HEADS, HEAD_DIM)
        p_c = p_c.reshape(bsz, seq, -1)

        ya = swa(p_a, bias_a, a_sinks[l]).reshape(n, -1)

        qb = jnp.transpose(p_b[:, :, 0], (0, 2, 1, 3))
        kb = jnp.transpose(p_b[:, :, 1], (0, 2, 1, 3)).reshape(bsz, B_HEADS, n_blk, TILE, HEAD_DIM)
        vbt = jnp.transpose(p_b[:, :, 2].reshape(bsz, n_blk, TILE, B_HEADS, HEAD_DIM), (0, 3, 1, 4, 2))
        yb_t = moba(qb, kb, vbt, win_b)
        yb = jnp.transpose(yb_t, (0, 3, 1, 2)).reshape(n, -1)

        w_uk = jnp.transpose(c_w_uk[l], (1, 0, 2)).astype(BF16)
        w_uv_t = jnp.transpose(c_w_uv[l], (1, 2, 0)).astype(BF16)
        yc_t = dsa(p_c, c_lat_gain[l], w_uk, w_uv_t, win_c)
        yc = jnp.transpose(yc_t, (0, 2, 1)).reshape(n, -1)

        xf = mix_out(xf, g_mix[l], ya, yb, yc, w_g, w_br_a[l].astype(BF16), w_br_b[l].astype(BF16),
                     w_br_c[l].astype(BF16), w_out[l].astype(BF16))

        qp, xn = norm_matmul(xf, g_ffn[l], peer_w_q[l].astype(BF16), BF16)
        c1, r1, e2, reach = peer_route(qp, peer_sub_keys[l].astype(BF16))
        xf = peer(xf, xn, c1, r1, e2, reach, peer_u[l].astype(BF16), jnp.transpose(peer_v[l]).astype(BF16),
                  g_final, norm_out=(l == depth - 1))

    return xf.reshape(bsz, seq, d)
```

```python
import functools
import math

import jax
import jax.numpy as jnp
import numpy as np
from jax import lax
from jax.experimental import pallas as pl
from jax.experimental.pallas import tpu as pltpu

F32 = jnp.float32
BF16 = jnp.bfloat16
I32 = jnp.int32

D_MODEL = 2048
HEAD_DIM = 64
A_HEADS, A_KV_HEADS, A_BLOCK, A_WINDOW = 16, 4, 128, 128
B_HEADS, B_BLOCK, B_TOPK = 8, 256, 3
C_HEADS, C_LATENT, IDX_HEADS, IDX_DIM, C_TOPK_MAX = 8, 128, 8, 32, 256
REL_BUCKETS, REL_MAX_DIST = 32, 2048
PEER_HEADS, PEER_KEYS, PEER_QDIM, PEER_TOPK = 8, 128, 256, 16
PEER_N = PEER_KEYS * PEER_KEYS
EPS = 1e-6
NEG = -1e30
INT_MIN = -(2 ** 31)
IN_SPLITS = (A_HEADS * HEAD_DIM, A_KV_HEADS * HEAD_DIM, A_KV_HEADS * HEAD_DIM,
             B_HEADS * HEAD_DIM, B_HEADS * HEAD_DIM, B_HEADS * HEAD_DIM,
             C_HEADS * HEAD_DIM, C_LATENT, IDX_HEADS * IDX_DIM, IDX_DIM, IDX_HEADS,
             D_MODEL, D_MODEL, D_MODEL)
A_COLS = sum(IN_SPLITS[0:3])
B_COLS = sum(IN_SPLITS[3:6])
C_COLS = 1024

TILE = 256
LANES = 128
SUBLANES = 8
LOCKSTEP = 8
VMEM_LIMIT = 56 * 1024 * 1024
PEER_VMEM_LIMIT = 58 * 1024 * 1024
NT = (((1,), (1,)), ((), ()))


def _params(sem, vmem=VMEM_LIMIT, flags=None):
    return pltpu.CompilerParams(dimension_semantics=sem, vmem_limit_bytes=vmem, flags=flags)


def _rms(x, g):
    return x * lax.rsqrt(jnp.mean(x * x, axis=-1, keepdims=True) + EPS) * g


def _sortable(v):
    bits = pltpu.bitcast(v, I32)
    return jnp.where(bits < 0, bits ^ jnp.int32(0x7FFFFFFF), bits)


def _unsortable(k):
    return pltpu.bitcast(jnp.where(k < 0, k ^ jnp.int32(0x7FFFFFFF), k), F32)


def _in_proj_kernel(x_ref, g_ref, w_ref, oab_ref, oc_ref, h_ref, *, nab):
    j = pl.program_id(1)

    @pl.when(j == 0)
    def _():
        h_ref[...] = _rms(x_ref[...], g_ref[...]).astype(BF16)

    y = jnp.dot(h_ref[...], w_ref[...], preferred_element_type=F32)

    @pl.when(j < nab)
    def _():
        oab_ref[...] = y.astype(oab_ref.dtype)

    @pl.when(j >= nab)
    def _():
        oc_ref[...] = y.astype(oc_ref.dtype)


def in_proj(x, g, w, tm=1024, tn=1024):
    n, d = x.shape
    nab, nc = (A_COLS + B_COLS) // tn, C_COLS // tn
    return pl.pallas_call(
        functools.partial(_in_proj_kernel, nab=nab),
        grid=(n // tm, nab + nc),
        in_specs=[pl.BlockSpec((tm, d), lambda i, j: (i, 0)),
                  pl.BlockSpec((1, d), lambda i, j: (0, 0)),
                  pl.BlockSpec((d, tn), lambda i, j: (0, j))],
        out_specs=[pl.BlockSpec((tm, tn), lambda i, j: (i, jnp.minimum(j, nab - 1))),
                   pl.BlockSpec((tm, tn), lambda i, j: (i, jnp.maximum(j - nab, 0)))],
        out_shape=[jax.ShapeDtypeStruct((n, A_COLS + B_COLS), BF16),
                   jax.ShapeDtypeStruct((n, C_COLS), F32)],
        scratch_shapes=[pltpu.VMEM((tm, d), BF16)],
        compiler_params=_params(("parallel", "arbitrary")),
        name="in_proj",
    )(x, g.reshape(1, d), w)


def _norm_matmul_kernel(x_ref, g_ref, w_ref, o_ref, xn_ref, h_ref):
    @pl.when(pl.program_id(1) == 0)
    def _():
        h = _rms(x_ref[...], g_ref[...]).astype(BF16)
        h_ref[...] = h
        xn_ref[...] = h

    o_ref[...] = jnp.dot(h_ref[...], w_ref[...], preferred_element_type=F32).astype(o_ref.dtype)


def norm_matmul(x, g, w, out_dtype, tm=1024, tn=1024):
    n, d = x.shape
    ncol = w.shape[1]
    return pl.pallas_call(
        _norm_matmul_kernel,
        grid=(n // tm, ncol // tn),
        in_specs=[pl.BlockSpec((tm, d), lambda i, j: (i, 0)),
                  pl.BlockSpec((1, d), lambda i, j: (0, 0)),
                  pl.BlockSpec((d, tn), lambda i, j: (0, j))],
        out_specs=[pl.BlockSpec((tm, tn), lambda i, j: (i, j)),
                   pl.BlockSpec((tm, d), lambda i, j: (i, 0))],
        out_shape=[jax.ShapeDtypeStruct((n, ncol), out_dtype),
                   jax.ShapeDtypeStruct((n, d), BF16)],
        scratch_shapes=[pltpu.VMEM((tm, d), BF16)],
        compiler_params=_params(("parallel", "arbitrary")),
        name="norm_matmul",
    )(x, g.reshape(1, d), w)


def _matmul_res_kernel(a_ref, w_ref, r_ref, o_ref):
    o_ref[...] = r_ref[...] + jnp.dot(a_ref[...], w_ref[...], preferred_element_type=F32)


def matmul_res(a, w, res, tm=1024, tn=1024):
    n, k = a.shape
    ncol = w.shape[1]
    return pl.pallas_call(
        _matmul_res_kernel,
        grid=(n // tm, ncol // tn),
        in_specs=[pl.BlockSpec((tm, k), lambda i, j: (i, 0)),
                  pl.BlockSpec((k, tn), lambda i, j: (0, j)),
                  pl.BlockSpec((tm, tn), lambda i, j: (i, j))],
        out_specs=pl.BlockSpec((tm, tn), lambda i, j: (i, j)),
        out_shape=jax.ShapeDtypeStruct((n, ncol), F32),
        compiler_params=_params(("parallel", "arbitrary")),
        name="matmul_res",
    )(a, w, res)


def _merge_kernel(x_ref, g_ref, ya_ref, yb_ref, yc_ref, wga_ref, wgb_ref, wgc_ref,
                  wa_ref, wb_ref, wc_ref, o_ref, h_ref):
    @pl.when(pl.program_id(1) == 0)
    def _():
        h_ref[...] = _rms(x_ref[...], g_ref[...]).astype(BF16)

    h = h_ref[...]

    def branch(y_ref, wg_ref, w_ref):
        gate = jnp.dot(h, wg_ref[...], preferred_element_type=F32)
        proj = jnp.dot(y_ref[...], w_ref[...], preferred_element_type=F32)
        return jax.nn.sigmoid(gate) * proj

    merged = branch(ya_ref, wga_ref, wa_ref) + branch(yb_ref, wgb_ref, wb_ref) + branch(yc_ref, wgc_ref, wc_ref)
    o_ref[...] = merged.astype(o_ref.dtype)


def merge(x, g, ya, yb, yc, w_gate, wa, wb, wc, tm=512, tn=512):
    n, d = x.shape
    nj = d // tn
    return pl.pallas_call(
        _merge_kernel,
        grid=(n // tm, nj),
        in_specs=[pl.BlockSpec((tm, d), lambda i, j: (i, 0)),
                  pl.BlockSpec((1, d), lambda i, j: (0, 0)),
                  pl.BlockSpec((tm, ya.shape[1]), lambda i, j: (i, 0)),
                  pl.BlockSpec((tm, yb.shape[1]), lambda i, j: (i, 0)),
                  pl.BlockSpec((tm, yc.shape[1]), lambda i, j: (i, 0)),
                  pl.BlockSpec((d, tn), lambda i, j: (0, j)),
                  pl.BlockSpec((d, tn), lambda i, j: (0, j + nj)),
                  pl.BlockSpec((d, tn), lambda i, j: (0, j + 2 * nj)),
                  pl.BlockSpec((wa.shape[0], tn), lambda i, j: (0, j)),
                  pl.BlockSpec((wb.shape[0], tn), lambda i, j: (0, j)),
                  pl.BlockSpec((wc.shape[0], tn), lambda i, j: (0, j))],
        out_specs=pl.BlockSpec((tm, tn), lambda i, j: (i, j)),
        out_shape=jax.ShapeDtypeStruct((n, d), BF16),
        scratch_shapes=[pltpu.VMEM((tm, d), BF16)],
        compiler_params=_params(("parallel", "arbitrary")),
        name="merge",
    )(x, g.reshape(1, d), ya, yb, yc, w_gate, w_gate, w_gate, wa, wb, wc)


def _rel_bucket(dist):
    d = jnp.maximum(dist, 0)
    max_exact = REL_BUCKETS // 2
    df = jnp.maximum(d, max_exact).astype(F32)
    large = max_exact + (jnp.log(df / max_exact) / math.log(REL_MAX_DIST / max_exact)
                         * (REL_BUCKETS - max_exact)).astype(I32)
    large = jnp.minimum(large, REL_BUCKETS - 1)
    return jnp.where(d < max_exact, d, large)


def _bucket_bias(bt, dist):
    onehot = jax.nn.one_hot(_rel_bucket(dist), REL_BUCKETS, dtype=F32)
    return jnp.einsum("...k,kh->...h", onehot, bt.astype(F32), precision=lax.Precision.HIGHEST)


def _bias_windows(bt, seq):
    dist = (jnp.arange(seq // TILE)[:, None] * TILE - (TILE - 1) + jnp.arange(2 * TILE)[None, :])
    w = _bucket_bias(bt, dist)
    w = jnp.where((dist >= 0)[..., None], w, 0.0)
    return jnp.transpose(w, (2, 0, 1))


def _gen_bias_tiles(win_ref, tiles_ref):
    n_heads, n_rel = win_ref.shape[0], win_ref.shape[1]

    def body(t, carry):
        h = t // n_rel
        r = t % n_rel
        row = win_ref[h, pl.ds(r, 1), :]
        x = jnp.broadcast_to(row, (TILE, 2 * TILE))
        tiles_ref[h, r] = pltpu.roll(x, TILE + 1, 1, stride=1, stride_axis=0)[:, :TILE]
        return carry

    lax.fori_loop(0, n_heads * n_rel, body, 0)


def _swa_kernel(q_ref, kp_ref, kc_ref, vp_ref, vc_ref, bias_ref, sink_ref, o_ref):
    i = pl.program_id(1)
    which = jnp.minimum(i, 1)
    group = A_HEADS // A_KV_HEADS
    outs = []
    for kvh in range(A_KV_HEADS):
        lo, hi = kvh * HEAD_DIM, (kvh + 1) * HEAD_DIM
        kk = jnp.concatenate([kp_ref[0, :, lo:hi], kc_ref[0, :, lo:hi]], axis=0)
        vv = jnp.concatenate([vp_ref[0, :, lo:hi], vc_ref[0, :, lo:hi]], axis=0)
        for g in range(group):
            h = kvh * group + g
            q = q_ref[0, :, h * HEAD_DIM:(h + 1) * HEAD_DIM]
            s = lax.dot_general(q, kk, NT, preferred_element_type=F32) * (HEAD_DIM ** -0.5)
            s = s + bias_ref[which, h]
            sink = sink_ref[h]
            m = jnp.maximum(jnp.max(s, axis=-1, keepdims=True), sink)
            p = jnp.exp(s - m)
            denom = jnp.sum(p, axis=-1, keepdims=True) + jnp.exp(sink - m)
            o = jnp.dot(p.astype(BF16), vv, preferred_element_type=F32) / denom
            outs.append(o.astype(BF16))
    o_ref[0] = jnp.concatenate(outs, axis=-1)


def _swa_bias(bt_a):
    qpos = jnp.arange(A_BLOCK)[:, None] + A_BLOCK
    kpos = jnp.arange(2 * A_BLOCK)[None, :]
    dist = qpos - kpos
    band = (dist >= 0) & (dist < A_WINDOW)
    bias = jnp.transpose(_bucket_bias(bt_a, dist), (2, 0, 1))
    general = jnp.where(band[None], bias, NEG)
    first = jnp.where((band & (kpos >= A_BLOCK))[None], bias, NEG)
    return jnp.stack([first, general])


def swa(p_a, bias, sinks):
    bsz, seq, _ = p_a.shape
    qw, kw = A_HEADS * HEAD_DIM, A_KV_HEADS * HEAD_DIM
    kblk, vblk = qw // kw, qw // kw + 1
    return pl.pallas_call(
        _swa_kernel,
        grid=(bsz, seq // A_BLOCK),
        in_specs=[pl.BlockSpec((1, A_BLOCK, qw), lambda b, i: (b, i, 0)),
                  pl.BlockSpec((1, A_BLOCK, kw), lambda b, i: (b, jnp.maximum(i - 1, 0), kblk)),
                  pl.BlockSpec((1, A_BLOCK, kw), lambda b, i: (b, i, kblk)),
                  pl.BlockSpec((1, A_BLOCK, kw), lambda b, i: (b, jnp.maximum(i - 1, 0), vblk)),
                  pl.BlockSpec((1, A_BLOCK, kw), lambda b, i: (b, i, vblk)),
                  pl.BlockSpec(bias.shape, lambda b, i: (0, 0, 0, 0)),
                  pl.BlockSpec(memory_space=pltpu.SMEM)],
        out_specs=pl.BlockSpec((1, A_BLOCK, qw), lambda b, i: (b, i, 0)),
        out_shape=jax.ShapeDtypeStruct((bsz, seq, qw), BF16),
        compiler_params=_params(("parallel", "arbitrary")),
        name="swa",
    )(p_a, p_a, p_a, p_a, p_a, bias, sinks)


def _moba_kernel(q_ref, k_ref, vt_ref, win_ref, o_ref, tiles_ref, kmean_ref, sel_ref):
    b, i = pl.program_id(0), pl.program_id(1)
    n_heads, n_blk = kmean_ref.shape[0], kmean_ref.shape[1]

    @pl.when((b == 0) & (i == 0))
    def _():
        _gen_bias_tiles(win_ref, tiles_ref)

    @pl.when(i == 0)
    def _():
        for h in range(n_heads):
            kmean_ref[h] = jnp.mean(k_ref[0, h].astype(F32), axis=1)

    row = lax.broadcasted_iota(I32, (n_blk, TILE), 0)
    for h in range(n_heads):
        q = q_ref[0, h]
        kmean = kmean_ref[h]
        kmean_hi = kmean.astype(BF16)
        kmean_lo = (kmean - kmean_hi.astype(F32)).astype(BF16)
        gate = (lax.dot_general(kmean_hi, q, NT, preferred_element_type=F32)
                + lax.dot_general(kmean_lo, q, NT, preferred_element_type=F32))
        gate = jnp.where(row < i, gate, NEG)
        sel = jnp.zeros(gate.shape, F32)
        for _ in range(B_TOPK):
            best = jnp.max(gate, axis=0, keepdims=True)
            first = jnp.min(jnp.where(gate == best, row, n_blk), axis=0, keepdims=True)
            pick = row == first
            sel = jnp.where(pick, 1.0, sel)
            gate = jnp.where(pick, -jnp.inf, gate)
        sel_ref[h] = jnp.where(row < i, sel, 0.0)

    key_pos = lax.broadcasted_iota(I32, (TILE, TILE), 0)
    qry_pos = lax.broadcasted_iota(I32, (TILE, TILE), 1)
    causal = jnp.where(qry_pos >= key_pos, 1.0, 0.0)

    def body(j, carry):
        out = []
        for g in range(0, n_heads, LOCKSTEP):
            pair = tuple(range(g, g + LOCKSTEP))
            s = [lax.dot_general(k_ref[0, h, j], q_ref[0, h], NT, preferred_element_type=F32) * (HEAD_DIM ** -0.5)
                 for h in pair]
            s = [s[k] + tiles_ref[pair[k], i - j] for k in range(LOCKSTEP)]
            ok = [jnp.where(j == i, causal, jnp.broadcast_to(sel_ref[h, pl.ds(j, 1), :], (TILE, TILE))) > 0.0
                  for h in pair]
            s = [jnp.where(ok[k], s[k], NEG) for k in range(LOCKSTEP)]
            m_new = [jnp.maximum(carry[pair[k]][0], jnp.max(s[k], axis=0, keepdims=True)) for k in range(LOCKSTEP)]
            alpha = [jnp.exp(carry[pair[k]][0] - m_new[k]) for k in range(LOCKSTEP)]
            p = [jnp.where(ok[k], jnp.exp(s[k] - m_new[k]), 0.0) for k in range(LOCKSTEP)]
            l_new = [alpha[k] * carry[pair[k]][1] + jnp.sum(p[k], axis=0, keepdims=True) for k in range(LOCKSTEP)]
            pv = [jnp.dot(vt_ref[0, pair[k], j], p[k].astype(BF16), preferred_element_type=F32) for k in range(LOCKSTEP)]
            for k in range(LOCKSTEP):
                out.append((m_new[k], l_new[k], alpha[k] * carry[pair[k]][2] + pv[k]))
        return tuple(out)

    init = tuple((jnp.full((1, TILE), NEG, F32), jnp.zeros((1, TILE), F32), jnp.zeros((HEAD_DIM, TILE), F32))
                 for _ in range(n_heads))
    final = lax.fori_loop(0, i + 1, body, init)
    for h in range(n_heads):
        _, l, acc = final[h]
        o_ref[0, h] = (acc / l).astype(o_ref.dtype)


def moba(q, k, vt, windows):
    bsz, n_heads, seq, d = q.shape
    n_blk = seq // TILE
    return pl.pallas_call(
        _moba_kernel,
        grid=(bsz, n_blk),
        in_specs=[pl.BlockSpec((1, n_heads, TILE, d), lambda b, i: (b, 0, i, 0)),
                  pl.BlockSpec((1, n_heads, n_blk, TILE, d), lambda b, i: (b, 0, 0, 0, 0)),
                  pl.BlockSpec((1, n_heads, n_blk, d, TILE), lambda b, i: (b, 0, 0, 0, 0)),
                  pl.BlockSpec(windows.shape, lambda b, i: (0, 0, 0))],
        out_specs=pl.BlockSpec((1, n_heads, d, TILE), lambda b, i: (b, 0, 0, i)),
        out_shape=jax.ShapeDtypeStruct((bsz, n_heads, d, seq), BF16),
        scratch_shapes=[pltpu.VMEM((n_heads, n_blk, TILE, TILE), F32),
                        pltpu.VMEM((n_heads, n_blk, d), F32),
                        pltpu.VMEM((n_heads, n_blk, TILE), F32)],
        compiler_params=_params(("arbitrary", "arbitrary")),
        name="moba",
    )(q, k, vt, windows)


def _dsa_kernel(qc_ref, iq_ref, mq_ref, clat_ref, mk_ref, gain_ref, wuk_ref, wuvt_ref, win_ref, o_ref,
                tiles_ref, cn_ref, cnt_ref, ik_ref, keys_ref, mask_ref, qabs_ref, *, top_k):
    b, c = pl.program_id(0), pl.program_id(1)
    n_blk = cn_ref.shape[0]
    n_kb = c + 1

    @pl.when((b == 0) & (c == 0))
    def _():
        _gen_bias_tiles(win_ref, tiles_ref)

    @pl.when(c == 0)
    def _():
        lat = _rms(clat_ref[0], gain_ref[...])
        for kb in range(n_blk):
            blk = lat[kb * TILE:(kb + 1) * TILE]
            cn_ref[kb] = blk.astype(BF16)
            cnt_ref[kb] = jnp.transpose(blk).astype(BF16)
            ik_ref[kb] = mk_ref[0, kb * TILE:(kb + 1) * TILE, :IDX_DIM].astype(BF16)

    key_in = lax.broadcasted_iota(I32, (TILE, TILE), 0)
    qry_in = lax.broadcasted_iota(I32, (TILE, TILE), 1)
    diag_causal = key_in <= qry_in

    iq = iq_ref[0].astype(BF16)
    iw_t = jnp.transpose(mq_ref[0])[IDX_DIM:IDX_DIM + IDX_HEADS, :]

    def score_block(kb, carry):
        score = jnp.zeros((TILE, TILE), F32)
        ik = ik_ref[kb]
        logits = [lax.dot_general(ik, iq[:, hh * IDX_DIM:(hh + 1) * IDX_DIM], NT, preferred_element_type=F32)
                  for hh in range(IDX_HEADS)]
        for hh in range(IDX_HEADS):
            score = score + jnp.maximum(logits[hh], 0.0) * iw_t[hh:hh + 1, :]
        score = score * ((IDX_DIM ** -0.5) * (IDX_HEADS ** -0.5))
        score = jnp.where(score == 0.0, 0.0, score)
        key = _sortable(score)
        keys_ref[kb] = jnp.where(kb < c, key, jnp.where(diag_causal, key, INT_MIN))
        return carry

    lax.fori_loop(0, n_kb, score_block, 0)

    def count(pred):
        def blk(kb, part):
            return part + jnp.sum(pred(keys_ref[kb], kb).reshape(TILE // SUBLANES, SUBLANES, TILE), axis=0)
        part = lax.fori_loop(0, n_kb, blk, jnp.zeros((SUBLANES, TILE), F32))
        return jnp.sum(part, axis=0, keepdims=True)

    def value_bit(t, tau):
        cand = tau + lax.shift_left(jnp.int32(1), 31 - t)
        cnt = count(lambda key, kb: jnp.where(key >= cand, 1.0, 0.0))
        return jnp.where(cnt >= top_k, cand, tau)

    tau = lax.fori_loop(0, 32, value_bit, jnp.full((1, TILE), INT_MIN, I32))
    need = top_k - count(lambda key, kb: jnp.where(key > tau, 1.0, 0.0))

    def index_bit(t, bound):
        cand = bound + lax.shift_left(jnp.int32(1), 11 - t)
        cnt = count(lambda key, kb: jnp.where(key == tau, jnp.where(kb * TILE + key_in < cand, 1.0, 0.0), 0.0))
        return jnp.where(cnt <= need, cand, bound)

    at_or_above = count(lambda key, kb: jnp.where(key >= tau, 1.0, 0.0))
    surplus = jnp.max(jnp.where(tau > INT_MIN, at_or_above - top_k, 0.0))
    bound = lax.cond(surplus > 0.0,
                     lambda: lax.fori_loop(0, 12, index_bit, jnp.zeros((1, TILE), I32)),
                     lambda: jnp.full((1, TILE), 2 ** 30, I32))

    def mask_block(kb, carry):
        key = keys_ref[kb]
        tied = jnp.where(key == tau, jnp.where(kb * TILE + key_in < bound, 1.0, 0.0), 0.0)
        chosen = jnp.where(key > tau, 1.0, tied)
        mask_ref[kb] = jnp.where(kb < c, chosen, jnp.where(diag_causal, chosen, 0.0))
        return carry

    lax.fori_loop(0, n_kb, mask_block, 0)

    qc = qc_ref[0].astype(BF16)
    for h in range(C_HEADS):
        qabs_ref[h] = lax.dot_general(wuk_ref[h], qc[:, h * HEAD_DIM:(h + 1) * HEAD_DIM], NT,
                                      preferred_element_type=F32).astype(BF16)

    def attend_block(kb, carry):
        ok = mask_ref[kb] > 0.0
        keys_lat, lat_keys = cn_ref[kb], cnt_ref[kb]
        out = []
        for g in range(0, C_HEADS, LOCKSTEP):
            pair = tuple(range(g, g + LOCKSTEP))
            s = [jnp.dot(keys_lat, qabs_ref[h], preferred_element_type=F32) * (HEAD_DIM ** -0.5) for h in pair]
            s = [jnp.where(ok, s[k] + tiles_ref[pair[k], c - kb], NEG) for k in range(LOCKSTEP)]
            m_new = [jnp.maximum(carry[pair[k]][0], jnp.max(s[k], axis=0, keepdims=True)) for k in range(LOCKSTEP)]
            alpha = [jnp.exp(carry[pair[k]][0] - m_new[k]) for k in range(LOCKSTEP)]
            p = [jnp.where(ok, jnp.exp(s[k] - m_new[k]), 0.0) for k in range(LOCKSTEP)]
            l_new = [alpha[k] * carry[pair[k]][1] + jnp.sum(p[k], axis=0, keepdims=True) for k in range(LOCKSTEP)]
            pv = [jnp.dot(lat_keys, p[k].astype(BF16), preferred_element_type=F32) for k in range(LOCKSTEP)]
            for k in range(LOCKSTEP):
                out.append((m_new[k], l_new[k], alpha[k] * carry[pair[k]][2] + pv[k]))
        return tuple(out)

    init = tuple((jnp.full((1, TILE), NEG, F32), jnp.zeros((1, TILE), F32), jnp.zeros((C_LATENT, TILE), F32))
                 for _ in range(C_HEADS))
    final = lax.fori_loop(0, n_kb, attend_block, init)
    for h in range(C_HEADS):
        _, l, acc = final[h]
        oc_t = (acc / l).astype(BF16)
        out_t = jnp.dot(wuvt_ref[h], oc_t, preferred_element_type=F32)
        o_ref[0, h * HEAD_DIM:(h + 1) * HEAD_DIM, :] = out_t.astype(o_ref.dtype)


def dsa(p_c, gain, w_uk, w_uv_t, windows):
    bsz, seq, _ = p_c.shape
    top_k = min(C_TOPK_MAX, seq // 4)
    qw = C_HEADS * HEAD_DIM
    n_blk = seq // TILE
    return pl.pallas_call(
        functools.partial(_dsa_kernel, top_k=top_k),
        grid=(bsz, n_blk),
        in_specs=[pl.BlockSpec((1, TILE, qw), lambda b, c: (b, c, 0)),
                  pl.BlockSpec((1, TILE, 256), lambda b, c: (b, c, 2)),
                  pl.BlockSpec((1, TILE, 128), lambda b, c: (b, c, 7)),
                  pl.BlockSpec((1, seq, 128), lambda b, c: (b, 0, 6)),
                  pl.BlockSpec((1, seq, 128), lambda b, c: (b, 0, 7)),
                  pl.BlockSpec((1, C_LATENT), lambda b, c: (0, 0)),
                  pl.BlockSpec(w_uk.shape, lambda b, c: (0, 0, 0)),
                  pl.BlockSpec(w_uv_t.shape, lambda b, c: (0, 0, 0)),
                  pl.BlockSpec(windows.shape, lambda b, c: (0, 0, 0))],
        out_specs=pl.BlockSpec((1, qw, TILE), lambda b, c: (b, 0, c)),
        out_shape=jax.ShapeDtypeStruct((bsz, qw, seq), BF16),
        scratch_shapes=[pltpu.VMEM((C_HEADS, n_blk, TILE, TILE), F32),
                        pltpu.VMEM((n_blk, TILE, C_LATENT), BF16),
                        pltpu.VMEM((n_blk, C_LATENT, TILE), BF16),
                        pltpu.VMEM((n_blk, TILE, IDX_DIM), BF16),
                        pltpu.VMEM((n_blk, TILE, TILE), I32),
                        pltpu.VMEM((n_blk, TILE, TILE), F32),
                        pltpu.VMEM((C_HEADS, C_LATENT, TILE), BF16)],
        compiler_params=_params(("arbitrary", "arbitrary")),
        name="dsa",
    )(p_c, p_c, p_c, p_c, p_c, gain.reshape(1, C_LATENT), w_uk, w_uv_t, windows)


PEER_TE = 1024
ROUTE_TM = 256
ROUTE_HEADS = 4
ROUTE_ROWS = 72


def _top_values(arrays, count):
    n = len(arrays)
    vals = [[] for _ in range(n)]
    rank = [jnp.full(a.shape, float(count), F32) for a in arrays]
    rest = list(arrays)
    for r in range(count):
        best = [jnp.max(rest[k], axis=0, keepdims=True) for k in range(n)]
        hit = [rest[k] == best[k] for k in range(n)]
        rank = [jnp.where(hit[k], float(r), rank[k]) for k in range(n)]
        rest = [jnp.where(hit[k], -jnp.inf, rest[k]) for k in range(n)]
        for k in range(n):
            vals[k].append(best[k])
    taken = [jnp.sum(jnp.where(rank[k] < count, 1.0, 0.0), axis=0, keepdims=True) for k in range(n)]
    return [(jnp.concatenate(vals[k], axis=0), rank[k], taken[k]) for k in range(n)]


def _multiplicities(rank, count):
    return jnp.concatenate([jnp.sum(jnp.where(rank == float(r), 1.0, 0.0), axis=0, keepdims=True)
                            for r in range(count)], axis=0)


def _route_kernel(q_ref, k_ref, c1_ref, r1_ref, e2_ref, reach_ref, keys_ref, wts_ref, s2_ref, t1_ref):
    heads = range(k_ref.shape[0])
    tm = q_ref.shape[0]
    for hh in heads:
        _route_prepare(q_ref[:, hh * PEER_QDIM:(hh + 1) * PEER_QDIM], k_ref.at[hh], c1_ref.at[hh], r1_ref.at[hh],
                       e2_ref.at[hh], keys_ref.at[hh], wts_ref.at[hh], s2_ref.at[hh], t1_ref.at[hh])

    def value_bit(t, taus):
        out = []
        for hh in heads:
            c = taus[hh] + lax.shift_left(jnp.int32(1), 31 - t)
            cnt = jnp.sum(jnp.where(keys_ref[hh] >= c, wts_ref[hh], 0.0), axis=0, keepdims=True)
            out.append(jnp.where(cnt >= PEER_TOPK, c, taus[hh]))
        return tuple(out)

    taus = lax.fori_loop(0, 32, value_bit, tuple(jnp.full((1, tm), INT_MIN, I32) for _ in heads))
    for hh in heads:
        _route_finish(taus[hh], c1_ref.at[hh], reach_ref.at[hh], keys_ref.at[hh], wts_ref.at[hh], s2_ref.at[hh],
                      t1_ref.at[hh])


def _route_prepare(q, k_ref, c1_ref, r1_ref, e2_ref, keys_ref, wts_ref, s2_ref, t1_ref):
    tm = q.shape[0]
    half = PEER_QDIM // 2
    s1 = lax.dot_general(k_ref[0], q[:, :half], NT, preferred_element_type=F32)
    s2 = lax.dot_general(k_ref[1], q[:, half:], NT, preferred_element_type=F32)
    (t1, rank, taken1), (t2, rank2, taken2) = _top_values([s1, s2], PEER_TOPK)
    repeats = jnp.max(taken1 + taken2) > 2.0 * PEER_TOPK
    ones = jnp.ones(t1.shape, F32)
    w1, w2 = lax.cond(repeats,
                      lambda: (_multiplicities(rank, PEER_TOPK), _multiplicities(rank2, PEER_TOPK)),
                      lambda: (ones, ones))

    r16 = lax.broadcasted_iota(I32, (16, tm), 0)
    r8 = lax.broadcasted_iota(I32, (8, tm), 0)
    pieces = [
        (t1[0:1] + t2, w1[0:1] * w2),
        (t1[1:2] + t2[:8], w1[1:2] * w2[:8]),
        (t1[2:3] + t2[:8], jnp.where(r8 < 5, w1[2:3] * w2[:8], 0.0)),
        (t1[3:4] + t2[:8], jnp.where(r8 < 4, w1[3:4] * w2[:8], 0.0)),
        (t2[0:1] + t1, jnp.where(r16 >= 4, w2[0:1] * w1, 0.0)),
        (t2[1:2] + t1[:8], jnp.where(r8 >= 4, w2[1:2] * w1[:8], 0.0)),
        (t2[2:3] + t1[:8], jnp.where(r8 == 4, w2[2:3] * w1[:8], 0.0)),
    ]
    keys_ref[...] = _sortable(jnp.concatenate([p[0] for p in pieces], axis=0))
    wts_ref[...] = jnp.concatenate([p[1] for p in pieces], axis=0)
    c1_ref[...] = s1.reshape(c1_ref.shape)
    r1_ref[...] = rank.reshape(r1_ref.shape)
    e2_ref[...] = jnp.exp(s2 - t2[0:1]).astype(BF16)
    s2_ref[...] = s2
    t1_ref[...] = t1


def _route_finish(tau_key, c1_ref, reach_ref, keys_ref, wts_ref, s2_ref, t1_ref):
    tau = _unsortable(tau_key)
    cand = _unsortable(keys_ref[...])
    top = cand[0:1]
    z = jnp.sum(jnp.where(keys_ref[...] >= tau_key, wts_ref[...] * jnp.exp(cand - top), 0.0), axis=0, keepdims=True)
    s2 = s2_ref[...]
    reach = jnp.zeros(s2.shape, F32)
    for r in range(PEER_TOPK):
        reach = reach + jnp.where(t1_ref[r:r + 1, :] + s2 >= tau, 1.0, 0.0)
    reach_ref[...] = reach.astype(BF16)
    c1_ref[...] = jnp.exp(c1_ref[...] - (t1_ref[0:1, :] + jnp.log(z)))


def peer_route(qp, sub_keys):
    n = qp.shape[0]
    tm = ROUTE_TM
    half = PEER_QDIM // 2
    hs = ROUTE_HEADS
    spec = pl.BlockSpec((hs, PEER_KEYS, tm), lambda i, h: (h, 0, i))
    sub = PEER_TE // PEER_KEYS
    chunked = pl.BlockSpec((hs, PEER_KEYS // sub, sub, tm), lambda i, h: (h, 0, 0, i))
    return pl.pallas_call(
        _route_kernel,
        grid=(n // tm, PEER_HEADS // hs),
        in_specs=[pl.BlockSpec((tm, hs * PEER_QDIM), lambda i, h: (i, h)),
                  pl.BlockSpec((hs, 2, PEER_KEYS, half), lambda i, h: (h, 0, 0, 0))],
        out_specs=[chunked, chunked, spec, spec],
        out_shape=[jax.ShapeDtypeStruct((PEER_HEADS, PEER_KEYS // sub, sub, n), F32),
                   jax.ShapeDtypeStruct((PEER_HEADS, PEER_KEYS // sub, sub, n), F32),
                   jax.ShapeDtypeStruct((PEER_HEADS, PEER_KEYS, n), BF16),
                   jax.ShapeDtypeStruct((PEER_HEADS, PEER_KEYS, n), BF16)],
        scratch_shapes=[pltpu.VMEM((hs, ROUTE_ROWS, tm), I32),
                        pltpu.VMEM((hs, ROUTE_ROWS, tm), F32),
                        pltpu.VMEM((hs, PEER_KEYS, tm), F32),
                        pltpu.VMEM((hs, PEER_TOPK, tm), F32)],
        compiler_params=_params(("parallel", "arbitrary")),
        name="peer_route",
    )(qp, sub_keys)


PACK = 16


def _peer_kernel(x_ref, xn_ref, c1_ref, r1_ref, e2_ref, reach_ref, u_ref, vt_ref, gout_ref, o_ref,
                 acc_ref, xnt_ref, hid_ref, *, norm_out):
    j = pl.program_id(1)
    n_sub = u_ref.shape[0] // PEER_KEYS
    tm = xn_ref.shape[0]

    @pl.when(j == 0)
    def _():
        acc_ref[...] = jnp.zeros_like(acc_ref)
        xnt_ref[...] = jnp.transpose(xn_ref[...].astype(F32)).astype(BF16)

    act = jnp.dot(u_ref[...], xnt_ref[...], preferred_element_type=F32)
    for ab in range(n_sub):
        rows = slice(ab * PEER_KEYS, (ab + 1) * PEER_KEYS)
        rank = [jnp.broadcast_to(r1_ref[h, 0, ab:ab + 1, :], (PACK, tm)).astype(BF16) for h in range(PEER_HEADS)]
        fac = [jnp.broadcast_to(c1_ref[h, 0, ab:ab + 1, :], (PACK, tm)).astype(BF16) for h in range(PEER_HEADS)]
        for tc in range(tm // LANES):
            cols = slice(tc * LANES, (tc + 1) * LANES)
            gate = jnp.zeros((PEER_KEYS, LANES), BF16)
            for h in range(PEER_HEADS):
                chosen = jnp.tile(rank[h][:, cols], (PEER_KEYS // PACK, 1)) < reach_ref[h, :, cols]
                e2 = e2_ref[h, :, cols]
                gate = gate + jnp.where(chosen, e2, jnp.zeros_like(e2)) * jnp.tile(fac[h][:, cols], (PEER_KEYS // PACK, 1))
            hid_ref[rows, cols] = gate * jax.nn.gelu(act[rows, cols].astype(BF16))
    acc_ref[...] += jnp.dot(vt_ref[...], hid_ref[...], preferred_element_type=F32)

    @pl.when(j == pl.num_programs(1) - 1)
    def _():
        y = x_ref[...] + jnp.transpose(acc_ref[...])
        o_ref[...] = _rms(y, gout_ref[...]) if norm_out else y


def peer(x, xn, c1, r1, e2, reach, u, vt, gain_out, norm_out, tm=512):
    n, d = x.shape
    te = PEER_TE
    n_chunks = PEER_N // te
    n_sub = te // PEER_KEYS
    chunk_spec = pl.BlockSpec((PEER_HEADS, 1, n_sub, tm), lambda i, j: (0, j, 0, i))
    tile_spec = pl.BlockSpec((PEER_HEADS, PEER_KEYS, tm), lambda i, j: (0, 0, i))
    return pl.pallas_call(
        functools.partial(_peer_kernel, norm_out=norm_out),
        grid=(n // tm, n_chunks),
        in_specs=[pl.BlockSpec((tm, d), lambda i, j: (i, 0)),
                  pl.BlockSpec((tm, d), lambda i, j: (i, 0)),
                  chunk_spec, chunk_spec, tile_spec, tile_spec,
                  pl.BlockSpec((te, d), lambda i, j: (j, 0)),
                  pl.BlockSpec((d, te), lambda i, j: (0, j)),
                  pl.BlockSpec((1, d), lambda i, j: (0, 0))],
        out_specs=pl.BlockSpec((tm, d), lambda i, j: (i, 0)),
        out_shape=jax.ShapeDtypeStruct((n, d), F32),
        scratch_shapes=[pltpu.VMEM((d, tm), F32),
                        pltpu.VMEM((d, tm), BF16),
                        pltpu.VMEM((te, tm), BF16)],
        compiler_params=_params(("parallel", "arbitrary"), vmem=PEER_VMEM_LIMIT),
        name="peer",
    )(x, xn, c1, r1, e2, reach, u, vt, gain_out.reshape(1, d))


def kernel(x, rel_bias, g_mix, w_in, a_sinks, c_lat_gain, c_w_uk, c_w_uv, w_br_a, w_br_b, w_br_c, w_out,
           g_ffn, peer_w_q, peer_sub_keys, peer_u, peer_v, g_final):
    bsz, seq, d = x.shape
    n = bsz * seq
    depth = w_in.shape[0]
    n_blk = seq // TILE
    off = [0] + [int(o) for o in np.cumsum(IN_SPLITS)]
    bt_a = rel_bias[:, :A_HEADS]
    bt_b = rel_bias[:, A_HEADS:A_HEADS + B_HEADS]
    bt_c = rel_bias[:, A_HEADS + B_HEADS:]
    bias_a = _swa_bias(bt_a)
    win_b = _bias_windows(bt_b, seq)
    win_c = _bias_windows(bt_c, seq)

    xf = x.reshape(n, d)
    for l in range(depth):
        w = w_in[l]
        pad = jnp.zeros((d, C_COLS - (off[11] - off[6])), F32)
        w_abc = jnp.concatenate([w[:, off[0]:off[6]],
                                 w[:, off[6]:off[7]], w[:, off[8]:off[9]], w[:, off[7]:off[8]],
                                 w[:, off[9]:off[11]], pad], axis=1).astype(BF16)
        w_g = w[:, off[11]:off[14]].astype(BF16)

        p_ab, p_c = in_proj(xf, g_mix[l], w_abc)
        p_a = p_ab.reshape(bsz, seq, -1)
        p_b = p_ab[:, A_COLS:].reshape(bsz, seq, 3, B_HEADS, HEAD_DIM)
        p_c = p_c.reshape(bsz, seq, -1)

        ya = swa(p_a, bias_a, a_sinks[l]).reshape(n, -1)

        qb = jnp.transpose(p_b[:, :, 0], (0, 2, 1, 3))
        kb = jnp.transpose(p_b[:, :, 1], (0, 2, 1, 3)).reshape(bsz, B_HEADS, n_blk, TILE, HEAD_DIM)
        vbt = jnp.transpose(p_b[:, :, 2].reshape(bsz, n_blk, TILE, B_HEADS, HEAD_DIM), (0, 3, 1, 4, 2))
        yb_t = moba(qb, kb, vbt, win_b)
        yb = jnp.transpose(yb_t, (0, 3, 1, 2)).reshape(n, -1)

        w_uk = jnp.transpose(c_w_uk[l], (1, 0, 2)).astype(BF16)
        w_uv_t = jnp.transpose(c_w_uv[l], (1, 2, 0)).astype(BF16)
        yc_t = dsa(p_c, c_lat_gain[l], w_uk, w_uv_t, win_c)
        yc = jnp.transpose(yc_t, (0, 2, 1)).reshape(n, -1)

        merged = merge(xf, g_mix[l], ya, yb, yc, w_g, w_br_a[l].astype(BF16), w_br_b[l].astype(BF16),
                       w_br_c[l].astype(BF16))
        xf = matmul_res(merged, w_out[l].astype(BF16), xf)

        qp, xn = norm_matmul(xf, g_ffn[l], peer_w_q[l].astype(BF16), BF16)
        c1, r1, e2, reach = peer_route(qp, peer_sub_keys[l].astype(BF16))
        xf = peer(xf, xn, c1, r1, e2, reach, peer_u[l].astype(BF16), jnp.transpose(peer_v[l]).astype(BF16),
                  g_final, norm_out=(l == depth - 1))

    return xf.reshape(bsz, seq, d)
```

```python
import functools
import math

import jax
import jax.numpy as jnp
import numpy as np
from jax import lax
from jax.experimental import pallas as pl
from jax.experimental.pallas import tpu as pltpu

F32 = jnp.float32
BF16 = jnp.bfloat16
I32 = jnp.int32

D_MODEL = 2048
HEAD_DIM = 64
A_HEADS, A_KV_HEADS, A_BLOCK, A_WINDOW = 16, 4, 128, 128
B_HEADS, B_BLOCK, B_TOPK = 8, 256, 3
C_HEADS, C_LATENT, IDX_HEADS, IDX_DIM, C_TOPK_MAX = 8, 128, 8, 32, 256
REL_BUCKETS, REL_MAX_DIST = 32, 2048
PEER_HEADS, PEER_KEYS, PEER_QDIM, PEER_TOPK = 8, 128, 256, 16
PEER_N = PEER_KEYS * PEER_KEYS
EPS = 1e-6
NEG = -1e30
INT_MIN = -(2 ** 31)
IN_SPLITS = (A_HEADS * HEAD_DIM, A_KV_HEADS * HEAD_DIM, A_KV_HEADS * HEAD_DIM,
             B_HEADS * HEAD_DIM, B_HEADS * HEAD_DIM, B_HEADS * HEAD_DIM,
             C_HEADS * HEAD_DIM, C_LATENT, IDX_HEADS * IDX_DIM, IDX_DIM, IDX_HEADS,
             D_MODEL, D_MODEL, D_MODEL)
A_COLS = sum(IN_SPLITS[0:3])
B_COLS = sum(IN_SPLITS[3:6])
C_COLS = 1024

TILE = 256
LANES = 128
SUBLANES = 8
LOCKSTEP = 8
VMEM_LIMIT = 56 * 1024 * 1024
PEER_VMEM_LIMIT = 58 * 1024 * 1024
NT = (((1,), (1,)), ((), ()))


def _params(sem, vmem=VMEM_LIMIT, flags=None):
    return pltpu.CompilerParams(dimension_semantics=sem, vmem_limit_bytes=vmem, flags=flags)


def _rms(x, g):
    return x * lax.rsqrt(jnp.mean(x * x, axis=-1, keepdims=True) + EPS) * g


def _sortable(v):
    bits = pltpu.bitcast(v, I32)
    return jnp.where(bits < 0, bits ^ jnp.int32(0x7FFFFFFF), bits)


def _unsortable(k):
    return pltpu.bitcast(jnp.where(k < 0, k ^ jnp.int32(0x7FFFFFFF), k), F32)


def _in_proj_kernel(x_ref, g_ref, w_ref, oab_ref, oc_ref, h_ref, *, nab):
    j = pl.program_id(1)

    @pl.when(j == 0)
    def _():
        h_ref[...] = _rms(x_ref[...], g_ref[...]).astype(BF16)

    y = jnp.dot(h_ref[...], w_ref[...], preferred_element_type=F32)

    @pl.when(j < nab)
    def _():
        oab_ref[...] = y.astype(oab_ref.dtype)

    @pl.when(j >= nab)
    def _():
        oc_ref[...] = y.astype(oc_ref.dtype)


def in_proj(x, g, w, tm=1024, tn=1024):
    n, d = x.shape
    nab, nc = (A_COLS + B_COLS) // tn, C_COLS // tn
    return pl.pallas_call(
        functools.partial(_in_proj_kernel, nab=nab),
        grid=(n // tm, nab + nc),
        in_specs=[pl.BlockSpec((tm, d), lambda i, j: (i, 0)),
                  pl.BlockSpec((1, d), lambda i, j: (0, 0)),
                  pl.BlockSpec((d, tn), lambda i, j: (0, j))],
        out_specs=[pl.BlockSpec((tm, tn), lambda i, j: (i, jnp.minimum(j, nab - 1))),
                   pl.BlockSpec((tm, tn), lambda i, j: (i, jnp.maximum(j - nab, 0)))],
        out_shape=[jax.ShapeDtypeStruct((n, A_COLS + B_COLS), BF16),
                   jax.ShapeDtypeStruct((n, C_COLS), F32)],
        scratch_shapes=[pltpu.VMEM((tm, d), BF16)],
        compiler_params=_params(("parallel", "arbitrary")),
        name="in_proj",
    )(x, g.reshape(1, d), w)


def _norm_matmul_kernel(x_ref, g_ref, w_ref, o_ref, xn_ref, h_ref):
    @pl.when(pl.program_id(1) == 0)
    def _():
        h = _rms(x_ref[...], g_ref[...]).astype(BF16)
        h_ref[...] = h
        xn_ref[...] = h

    o_ref[...] = jnp.dot(h_ref[...], w_ref[...], preferred_element_type=F32).astype(o_ref.dtype)


def norm_matmul(x, g, w, out_dtype, tm=1024, tn=1024):
    n, d = x.shape
    ncol = w.shape[1]
    return pl.pallas_call(
        _norm_matmul_kernel,
        grid=(n // tm, ncol // tn),
        in_specs=[pl.BlockSpec((tm, d), lambda i, j: (i, 0)),
                  pl.BlockSpec((1, d), lambda i, j: (0, 0)),
                  pl.BlockSpec((d, tn), lambda i, j: (0, j))],
        out_specs=[pl.BlockSpec((tm, tn), lambda i, j: (i, j)),
                   pl.BlockSpec((tm, d), lambda i, j: (i, 0))],
        out_shape=[jax.ShapeDtypeStruct((n, ncol), out_dtype),
                   jax.ShapeDtypeStruct((n, d), BF16)],
        scratch_shapes=[pltpu.VMEM((tm, d), BF16)],
        compiler_params=_params(("parallel", "arbitrary")),
        name="norm_matmul",
    )(x, g.reshape(1, d), w)


def _matmul_res_kernel(a_ref, w_ref, r_ref, o_ref):
    o_ref[...] = r_ref[...] + jnp.dot(a_ref[...], w_ref[...], preferred_element_type=F32)


def matmul_res(a, w, res, tm=1024, tn=1024):
    n, k = a.shape
    ncol = w.shape[1]
    return pl.pallas_call(
        _matmul_res_kernel,
        grid=(n // tm, ncol // tn),
        in_specs=[pl.BlockSpec((tm, k), lambda i, j: (i, 0)),
                  pl.BlockSpec((k, tn), lambda i, j: (0, j)),
                  pl.BlockSpec((tm, tn), lambda i, j: (i, j))],
        out_specs=pl.BlockSpec((tm, tn), lambda i, j: (i, j)),
        out_shape=jax.ShapeDtypeStruct((n, ncol), F32),
        compiler_params=_params(("parallel", "arbitrary")),
        name="matmul_res",
    )(a, w, res)


def _merge_kernel(x_ref, g_ref, ya_ref, yb_ref, yc_ref, wga_ref, wgb_ref, wgc_ref,
                  wa_ref, wb_ref, wc_ref, o_ref, h_ref):
    @pl.when(pl.program_id(1) == 0)
    def _():
        h_ref[...] = _rms(x_ref[...], g_ref[...]).astype(BF16)

    h = h_ref[...]

    def branch(y_ref, wg_ref, w_ref):
        gate = jnp.dot(h, wg_ref[...], preferred_element_type=F32)
        proj = jnp.dot(y_ref[...], w_ref[...], preferred_element_type=F32)
        return jax.nn.sigmoid(gate) * proj

    merged = branch(ya_ref, wga_ref, wa_ref) + branch(yb_ref, wgb_ref, wb_ref) + branch(yc_ref, wgc_ref, wc_ref)
    o_ref[...] = merged.astype(o_ref.dtype)


def merge(x, g, ya, yb, yc, w_gate, wa, wb, wc, tm=1024, tn=512):
    n, d = x.shape
    nj = d // tn
    return pl.pallas_call(
        _merge_kernel,
        grid=(n // tm, nj),
        in_specs=[pl.BlockSpec((tm, d), lambda i, j: (i, 0)),
                  pl.BlockSpec((1, d), lambda i, j: (0, 0)),
                  pl.BlockSpec((tm, ya.shape[1]), lambda i, j: (i, 0)),
                  pl.BlockSpec((tm, yb.shape[1]), lambda i, j: (i, 0)),
                  pl.BlockSpec((tm, yc.shape[1]), lambda i, j: (i, 0)),
                  pl.BlockSpec((d, tn), lambda i, j: (0, j)),
                  pl.BlockSpec((d, tn), lambda i, j: (0, j + nj)),
                  pl.BlockSpec((d, tn), lambda i, j: (0, j + 2 * nj)),
                  pl.BlockSpec((wa.shape[0], tn), lambda i, j: (0, j)),
                  pl.BlockSpec((wb.shape[0], tn), lambda i, j: (0, j)),
                  pl.BlockSpec((wc.shape[0], tn), lambda i, j: (0, j))],
        out_specs=pl.BlockSpec((tm, tn), lambda i, j: (i, j)),
        out_shape=jax.ShapeDtypeStruct((n, d), BF16),
        scratch_shapes=[pltpu.VMEM((tm, d), BF16)],
        compiler_params=_params(("parallel", "arbitrary")),
        name="merge",
    )(x, g.reshape(1, d), ya, yb, yc, w_gate, w_gate, w_gate, wa, wb, wc)


def _rel_bucket(dist):
    d = jnp.maximum(dist, 0)
    max_exact = REL_BUCKETS // 2
    df = jnp.maximum(d, max_exact).astype(F32)
    large = max_exact + (jnp.log(df / max_exact) / math.log(REL_MAX_DIST / max_exact)
                         * (REL_BUCKETS - max_exact)).astype(I32)
    large = jnp.minimum(large, REL_BUCKETS - 1)
    return jnp.where(d < max_exact, d, large)


def _bucket_bias(bt, dist):
    onehot = jax.nn.one_hot(_rel_bucket(dist), REL_BUCKETS, dtype=F32)
    return jnp.einsum("...k,kh->...h", onehot, bt.astype(F32), precision=lax.Precision.HIGHEST)


def _bias_windows(bt, seq):
    dist = (jnp.arange(seq // TILE)[:, None] * TILE - (TILE - 1) + jnp.arange(2 * TILE)[None, :])
    w = _bucket_bias(bt, dist)
    w = jnp.where((dist >= 0)[..., None], w, 0.0)
    return jnp.transpose(w, (2, 0, 1))


def _gen_bias_tiles(win_ref, tiles_ref):
    n_heads, n_rel = win_ref.shape[0], win_ref.shape[1]

    def body(t, carry):
        h = t // n_rel
        r = t % n_rel
        row = win_ref[h, pl.ds(r, 1), :]
        x = jnp.broadcast_to(row, (TILE, 2 * TILE))
        tiles_ref[h, r] = pltpu.roll(x, TILE + 1, 1, stride=1, stride_axis=0)[:, :TILE]
        return carry

    lax.fori_loop(0, n_heads * n_rel, body, 0)


def _swa_kernel(q_ref, kp_ref, kc_ref, vp_ref, vc_ref, bias_ref, sink_ref, o_ref):
    i = pl.program_id(1)
    which = jnp.minimum(i, 1)
    group = A_HEADS // A_KV_HEADS
    outs = []
    for kvh in range(A_KV_HEADS):
        lo, hi = kvh * HEAD_DIM, (kvh + 1) * HEAD_DIM
        kk = jnp.concatenate([kp_ref[0, :, lo:hi], kc_ref[0, :, lo:hi]], axis=0)
        vv = jnp.concatenate([vp_ref[0, :, lo:hi], vc_ref[0, :, lo:hi]], axis=0)
        for g in range(group):
            h = kvh * group + g
            q = q_ref[0, :, h * HEAD_DIM:(h + 1) * HEAD_DIM]
            s = lax.dot_general(q, kk, NT, preferred_element_type=F32) * (HEAD_DIM ** -0.5)
            s = s + bias_ref[which, h]
            sink = sink_ref[h]
            m = jnp.maximum(jnp.max(s, axis=-1, keepdims=True), sink)
            p = jnp.exp(s - m)
            denom = jnp.sum(p, axis=-1, keepdims=True) + jnp.exp(sink - m)
            o = jnp.dot(p.astype(BF16), vv, preferred_element_type=F32) / denom
            outs.append(o.astype(BF16))
    o_ref[0] = jnp.concatenate(outs, axis=-1)


def _swa_bias(bt_a):
    qpos = jnp.arange(A_BLOCK)[:, None] + A_BLOCK
    kpos = jnp.arange(2 * A_BLOCK)[None, :]
    dist = qpos - kpos
    band = (dist >= 0) & (dist < A_WINDOW)
    bias = jnp.transpose(_bucket_bias(bt_a, dist), (2, 0, 1))
    general = jnp.where(band[None], bias, NEG)
    first = jnp.where((band & (kpos >= A_BLOCK))[None], bias, NEG)
    return jnp.stack([first, general])


def swa(p_a, bias, sinks):
    bsz, seq, _ = p_a.shape
    qw, kw = A_HEADS * HEAD_DIM, A_KV_HEADS * HEAD_DIM
    kblk, vblk = qw // kw, qw // kw + 1
    return pl.pallas_call(
        _swa_kernel,
        grid=(bsz, seq // A_BLOCK),
        in_specs=[pl.BlockSpec((1, A_BLOCK, qw), lambda b, i: (b, i, 0)),
                  pl.BlockSpec((1, A_BLOCK, kw), lambda b, i: (b, jnp.maximum(i - 1, 0), kblk)),
                  pl.BlockSpec((1, A_BLOCK, kw), lambda b, i: (b, i, kblk)),
                  pl.BlockSpec((1, A_BLOCK, kw), lambda b, i: (b, jnp.maximum(i - 1, 0), vblk)),
                  pl.BlockSpec((1, A_BLOCK, kw), lambda b, i: (b, i, vblk)),
                  pl.BlockSpec(bias.shape, lambda b, i: (0, 0, 0, 0)),
                  pl.BlockSpec(memory_space=pltpu.SMEM)],
        out_specs=pl.BlockSpec((1, A_BLOCK, qw), lambda b, i: (b, i, 0)),
        out_shape=jax.ShapeDtypeStruct((bsz, seq, qw), BF16),
        compiler_params=_params(("parallel", "arbitrary")),
        name="swa",
    )(p_a, p_a, p_a, p_a, p_a, bias, sinks)


def _moba_kernel(q_ref, k_ref, vt_ref, win_ref, o_ref, tiles_ref, kmean_ref, sel_ref):
    b, i = pl.program_id(0), pl.program_id(1)
    n_heads, n_blk = kmean_ref.shape[0], kmean_ref.shape[1]

    @pl.when((b == 0) & (i == 0))
    def _():
        _gen_bias_tiles(win_ref, tiles_ref)

    @pl.when(i == 0)
    def _():
        for h in range(n_heads):
            kmean_ref[h] = jnp.mean(k_ref[0, h].astype(F32), axis=1)

    row = lax.broadcasted_iota(I32, (n_blk, TILE), 0)
    for h in range(n_heads):
        q = q_ref[0, h]
        kmean = kmean_ref[h]
        kmean_hi = kmean.astype(BF16)
        kmean_lo = (kmean - kmean_hi.astype(F32)).astype(BF16)
        gate = (lax.dot_general(kmean_hi, q, NT, preferred_element_type=F32)
                + lax.dot_general(kmean_lo, q, NT, preferred_element_type=F32))
        gate = jnp.where(row < i, gate, NEG)
        sel = jnp.zeros(gate.shape, F32)
        for _ in range(B_TOPK):
            best = jnp.max(gate, axis=0, keepdims=True)
            first = jnp.min(jnp.where(gate == best, row, n_blk), axis=0, keepdims=True)
            pick = row == first
            sel = jnp.where(pick, 1.0, sel)
            gate = jnp.where(pick, -jnp.inf, gate)
        sel_ref[h] = jnp.where(row < i, sel, 0.0)

    key_pos = lax.broadcasted_iota(I32, (TILE, TILE), 0)
    qry_pos = lax.broadcasted_iota(I32, (TILE, TILE), 1)
    causal = jnp.where(qry_pos >= key_pos, 1.0, 0.0)

    def body(j, carry):
        out = []
        for g in range(0, n_heads, LOCKSTEP):
            pair = tuple(range(g, g + LOCKSTEP))
            s = [lax.dot_general(k_ref[0, h, j], q_ref[0, h], NT, preferred_element_type=F32) * (HEAD_DIM ** -0.5)
                 for h in pair]
            s = [s[k] + tiles_ref[pair[k], i - j] for k in range(LOCKSTEP)]
            ok = [jnp.where(j == i, causal, jnp.broadcast_to(sel_ref[h, pl.ds(j, 1), :], (TILE, TILE))) > 0.0
                  for h in pair]
            s = [jnp.where(ok[k], s[k], NEG) for k in range(LOCKSTEP)]
            m_new = [jnp.maximum(carry[pair[k]][0], jnp.max(s[k], axis=0, keepdims=True)) for k in range(LOCKSTEP)]
            alpha = [jnp.exp(carry[pair[k]][0] - m_new[k]) for k in range(LOCKSTEP)]
            p = [jnp.where(ok[k], jnp.exp(s[k] - m_new[k]), 0.0) for k in range(LOCKSTEP)]
            l_new = [alpha[k] * carry[pair[k]][1] + jnp.sum(p[k], axis=0, keepdims=True) for k in range(LOCKSTEP)]
            pv = [jnp.dot(vt_ref[0, pair[k], j], p[k].astype(BF16), preferred_element_type=F32) for k in range(LOCKSTEP)]
            for k in range(LOCKSTEP):
                out.append((m_new[k], l_new[k], alpha[k] * carry[pair[k]][2] + pv[k]))
        return tuple(out)

    init = tuple((jnp.full((1, TILE), NEG, F32), jnp.zeros((1, TILE), F32), jnp.zeros((HEAD_DIM, TILE), F32))
                 for _ in range(n_heads))
    final = lax.fori_loop(0, i + 1, body, init)
    for h in range(n_heads):
        _, l, acc = final[h]
        o_ref[0, h] = (acc / l).astype(o_ref.dtype)


def moba(q, k, vt, windows):
    bsz, n_heads, seq, d = q.shape
    n_blk = seq // TILE
    return pl.pallas_call(
        _moba_kernel,
        grid=(bsz, n_blk),
        in_specs=[pl.BlockSpec((1, n_heads, TILE, d), lambda b, i: (b, 0, i, 0)),
                  pl.BlockSpec((1, n_heads, n_blk, TILE, d), lambda b, i: (b, 0, 0, 0, 0)),
                  pl.BlockSpec((1, n_heads, n_blk, d, TILE), lambda b, i: (b, 0, 0, 0, 0)),
                  pl.BlockSpec(windows.shape, lambda b, i: (0, 0, 0))],
        out_specs=pl.BlockSpec((1, n_heads, d, TILE), lambda b, i: (b, 0, 0, i)),
        out_shape=jax.ShapeDtypeStruct((bsz, n_heads, d, seq), BF16),
        scratch_shapes=[pltpu.VMEM((n_heads, n_blk, TILE, TILE), F32),
                        pltpu.VMEM((n_heads, n_blk, d), F32),
                        pltpu.VMEM((n_heads, n_blk, TILE), F32)],
        compiler_params=_params(("arbitrary", "arbitrary")),
        name="moba",
    )(q, k, vt, windows)


def _dsa_kernel(qc_ref, iq_ref, mq_ref, clat_ref, mk_ref, gain_ref, wuk_ref, wuvt_ref, win_ref, o_ref,
                tiles_ref, cn_ref, cnt_ref, ik_ref, keys_ref, mask_ref, qabs_ref, *, top_k):
    b, c = pl.program_id(0), pl.program_id(1)
    n_blk = cn_ref.shape[0]
    n_kb = c + 1

    @pl.when((b == 0) & (c == 0))
    def _():
        _gen_bias_tiles(win_ref, tiles_ref)

    @pl.when(c == 0)
    def _():
        lat = _rms(clat_ref[0], gain_ref[...])
        for kb in range(n_blk):
            blk = lat[kb * TILE:(kb + 1) * TILE]
            cn_ref[kb] = blk.astype(BF16)
            cnt_ref[kb] = jnp.transpose(blk).astype(BF16)
            ik_ref[kb] = mk_ref[0, kb * TILE:(kb + 1) * TILE, :IDX_DIM].astype(BF16)

    key_in = lax.broadcasted_iota(I32, (TILE, TILE), 0)
    qry_in = lax.broadcasted_iota(I32, (TILE, TILE), 1)
    diag_causal = key_in <= qry_in

    iq = iq_ref[0].astype(BF16)
    iw_t = jnp.transpose(mq_ref[0])[IDX_DIM:IDX_DIM + IDX_HEADS, :]

    def score_block(kb, carry):
        score = jnp.zeros((TILE, TILE), F32)
        ik = ik_ref[kb]
        logits = [lax.dot_general(ik, iq[:, hh * IDX_DIM:(hh + 1) * IDX_DIM], NT, preferred_element_type=F32)
                  for hh in range(IDX_HEADS)]
        for hh in range(IDX_HEADS):
            score = score + jnp.maximum(logits[hh], 0.0) * iw_t[hh:hh + 1, :]
        score = score * ((IDX_DIM ** -0.5) * (IDX_HEADS ** -0.5))
        score = jnp.where(score == 0.0, 0.0, score)
        key = _sortable(score)
        keys_ref[kb] = jnp.where(kb < c, key, jnp.where(diag_causal, key, INT_MIN))
        return carry

    lax.fori_loop(0, n_kb, score_block, 0)

    def count(pred):
        def blk(kb, part):
            return part + jnp.sum(pred(keys_ref[kb], kb).reshape(TILE // SUBLANES, SUBLANES, TILE), axis=0)
        part = lax.fori_loop(0, n_kb, blk, jnp.zeros((SUBLANES, TILE), F32))
        return jnp.sum(part, axis=0, keepdims=True)

    def value_bit(t, tau):
        cand = tau + lax.shift_left(jnp.int32(1), 31 - t)
        cnt = count(lambda key, kb: jnp.where(key >= cand, 1.0, 0.0))
        return jnp.where(cnt >= top_k, cand, tau)

    tau = lax.fori_loop(0, 32, value_bit, jnp.full((1, TILE), INT_MIN, I32))
    need = top_k - count(lambda key, kb: jnp.where(key > tau, 1.0, 0.0))

    def index_bit(t, bound):
        cand = bound + lax.shift_left(jnp.int32(1), 11 - t)
        cnt = count(lambda key, kb: jnp.where(key == tau, jnp.where(kb * TILE + key_in < cand, 1.0, 0.0), 0.0))
        return jnp.where(cnt <= need, cand, bound)

    at_or_above = count(lambda key, kb: jnp.where(key >= tau, 1.0, 0.0))
    surplus = jnp.max(jnp.where(tau > INT_MIN, at_or_above - top_k, 0.0))
    bound = lax.cond(surplus > 0.0,
                     lambda: lax.fori_loop(0, 12, index_bit, jnp.zeros((1, TILE), I32)),
                     lambda: jnp.full((1, TILE), 2 ** 30, I32))

    def mask_block(kb, carry):
        key = keys_ref[kb]
        tied = jnp.where(key == tau, jnp.where(kb * TILE + key_in < bound, 1.0, 0.0), 0.0)
        chosen = jnp.where(key > tau, 1.0, tied)
        mask_ref[kb] = jnp.where(kb < c, chosen, jnp.where(diag_causal, chosen, 0.0))
        return carry

    lax.fori_loop(0, n_kb, mask_block, 0)

    qc = qc_ref[0].astype(BF16)
    for h in range(C_HEADS):
        qabs_ref[h] = lax.dot_general(wuk_ref[h], qc[:, h * HEAD_DIM:(h + 1) * HEAD_DIM], NT,
                                      preferred_element_type=F32).astype(BF16)

    def attend_block(kb, carry):
        ok = mask_ref[kb] > 0.0
        keys_lat, lat_keys = cn_ref[kb], cnt_ref[kb]
        out = []
        for g in range(0, C_HEADS, LOCKSTEP):
            pair = tuple(range(g, g + LOCKSTEP))
            s = [jnp.dot(keys_lat, qabs_ref[h], preferred_element_type=F32) * (HEAD_DIM ** -0.5) for h in pair]
            s = [jnp.where(ok, s[k] + tiles_ref[pair[k], c - kb], NEG) for k in range(LOCKSTEP)]
            m_new = [jnp.maximum(carry[pair[k]][0], jnp.max(s[k], axis=0, keepdims=True)) for k in range(LOCKSTEP)]
            alpha = [jnp.exp(carry[pair[k]][0] - m_new[k]) for k in range(LOCKSTEP)]
            p = [jnp.where(ok, jnp.exp(s[k] - m_new[k]), 0.0) for k in range(LOCKSTEP)]
            l_new = [alpha[k] * carry[pair[k]][1] + jnp.sum(p[k], axis=0, keepdims=True) for k in range(LOCKSTEP)]
            pv = [jnp.dot(lat_keys, p[k].astype(BF16), preferred_element_type=F32) for k in range(LOCKSTEP)]
            for k in range(LOCKSTEP):
                out.append((m_new[k], l_new[k], alpha[k] * carry[pair[k]][2] + pv[k]))
        return tuple(out)

    init = tuple((jnp.full((1, TILE), NEG, F32), jnp.zeros((1, TILE), F32), jnp.zeros((C_LATENT, TILE), F32))
                 for _ in range(C_HEADS))
    final = lax.fori_loop(0, n_kb, attend_block, init)
    for h in range(C_HEADS):
        _, l, acc = final[h]
        oc_t = (acc / l).astype(BF16)
        out_t = jnp.dot(wuvt_ref[h], oc_t, preferred_element_type=F32)
        o_ref[0, h * HEAD_DIM:(h + 1) * HEAD_DIM, :] = out_t.astype(o_ref.dtype)


def dsa(p_c, gain, w_uk, w_uv_t, windows):
    bsz, seq, _ = p_c.shape
    top_k = min(C_TOPK_MAX, seq // 4)
    qw = C_HEADS * HEAD_DIM
    n_blk = seq // TILE
    return pl.pallas_call(
        functools.partial(_dsa_kernel, top_k=top_k),
        grid=(bsz, n_blk),
        in_specs=[pl.BlockSpec((1, TILE, qw), lambda b, c: (b, c, 0)),
                  pl.BlockSpec((1, TILE, 256), lambda b, c: (b, c, 2)),
                  pl.BlockSpec((1, TILE, 128), lambda b, c: (b, c, 7)),
                  pl.BlockSpec((1, seq, 128), lambda b, c: (b, 0, 6)),
                  pl.BlockSpec((1, seq, 128), lambda b, c: (b, 0, 7)),
                  pl.BlockSpec((1, C_LATENT), lambda b, c: (0, 0)),
                  pl.BlockSpec(w_uk.shape, lambda b, c: (0, 0, 0)),
                  pl.BlockSpec(w_uv_t.shape, lambda b, c: (0, 0, 0)),
                  pl.BlockSpec(windows.shape, lambda b, c: (0, 0, 0))],
        out_specs=pl.BlockSpec((1, qw, TILE), lambda b, c: (b, 0, c)),
        out_shape=jax.ShapeDtypeStruct((bsz, qw, seq), BF16),
        scratch_shapes=[pltpu.VMEM((C_HEADS, n_blk, TILE, TILE), F32),
                        pltpu.VMEM((n_blk, TILE, C_LATENT), BF16),
                        pltpu.VMEM((n_blk, C_LATENT, TILE), BF16),
                        pltpu.VMEM((n_blk, TILE, IDX_DIM), BF16),
                        pltpu.VMEM((n_blk, TILE, TILE), I32),
                        pltpu.VMEM((n_blk, TILE, TILE), F32),
                        pltpu.VMEM((C_HEADS, C_LATENT, TILE), BF16)],
        compiler_params=_params(("arbitrary", "arbitrary")),
        name="dsa",
    )(p_c, p_c, p_c, p_c, p_c, gain.reshape(1, C_LATENT), w_uk, w_uv_t, windows)


PEER_TE = 1024
ROUTE_TM = 256
ROUTE_HEADS = 4
ROUTE_ROWS = 72


def _top_values(arrays, count):
    n = len(arrays)
    vals = [[] for _ in range(n)]
    rank = [jnp.full(a.shape, float(count), F32) for a in arrays]
    rest = list(arrays)
    for r in range(count):
        best = [jnp.max(rest[k], axis=0, keepdims=True) for k in range(n)]
        hit = [rest[k] == best[k] for k in range(n)]
        rank = [jnp.where(hit[k], float(r), rank[k]) for k in range(n)]
        rest = [jnp.where(hit[k], -jnp.inf, rest[k]) for k in range(n)]
        for k in range(n):
            vals[k].append(best[k])
    taken = [jnp.sum(jnp.where(rank[k] < count, 1.0, 0.0), axis=0, keepdims=True) for k in range(n)]
    return [(jnp.concatenate(vals[k], axis=0), rank[k], taken[k]) for k in range(n)]


def _multiplicities(rank, count):
    return jnp.concatenate([jnp.sum(jnp.where(rank == float(r), 1.0, 0.0), axis=0, keepdims=True)
                            for r in range(count)], axis=0)


def _route_kernel(q_ref, k_ref, c1_ref, r1_ref, e2_ref, reach_ref, keys_ref, wts_ref, s2_ref, t1_ref):
    heads = range(k_ref.shape[0])
    tm = q_ref.shape[0]
    for hh in heads:
        _route_prepare(q_ref[:, hh * PEER_QDIM:(hh + 1) * PEER_QDIM], k_ref.at[hh], c1_ref.at[hh], r1_ref.at[hh],
                       e2_ref.at[hh], keys_ref.at[hh], wts_ref.at[hh], s2_ref.at[hh], t1_ref.at[hh])

    def value_bit(t, taus):
        out = []
        for hh in heads:
            c = taus[hh] + lax.shift_left(jnp.int32(1), 31 - t)
            cnt = jnp.sum(jnp.where(keys_ref[hh] >= c, wts_ref[hh], 0.0), axis=0, keepdims=True)
            out.append(jnp.where(cnt >= PEER_TOPK, c, taus[hh]))
        return tuple(out)

    taus = lax.fori_loop(0, 32, value_bit, tuple(jnp.full((1, tm), INT_MIN, I32) for _ in heads))
    for hh in heads:
        _route_finish(taus[hh], c1_ref.at[hh], reach_ref.at[hh], keys_ref.at[hh], wts_ref.at[hh], s2_ref.at[hh],
                      t1_ref.at[hh])


def _route_prepare(q, k_ref, c1_ref, r1_ref, e2_ref, keys_ref, wts_ref, s2_ref, t1_ref):
    tm = q.shape[0]
    half = PEER_QDIM // 2
    s1 = lax.dot_general(k_ref[0], q[:, :half], NT, preferred_element_type=F32)
    s2 = lax.dot_general(k_ref[1], q[:, half:], NT, preferred_element_type=F32)
    (t1, rank, taken1), (t2, rank2, taken2) = _top_values([s1, s2], PEER_TOPK)
    repeats = jnp.max(taken1 + taken2) > 2.0 * PEER_TOPK
    ones = jnp.ones(t1.shape, F32)
    w1, w2 = lax.cond(repeats,
                      lambda: (_multiplicities(rank, PEER_TOPK), _multiplicities(rank2, PEER_TOPK)),
                      lambda: (ones, ones))

    r16 = lax.broadcasted_iota(I32, (16, tm), 0)
    r8 = lax.broadcasted_iota(I32, (8, tm), 0)
    pieces = [
        (t1[0:1] + t2, w1[0:1] * w2),
        (t1[1:2] + t2[:8], w1[1:2] * w2[:8]),
        (t1[2:3] + t2[:8], jnp.where(r8 < 5, w1[2:3] * w2[:8], 0.0)),
        (t1[3:4] + t2[:8], jnp.where(r8 < 4, w1[3:4] * w2[:8], 0.0)),
        (t2[0:1] + t1, jnp.where(r16 >= 4, w2[0:1] * w1, 0.0)),
        (t2[1:2] + t1[:8], jnp.where(r8 >= 4, w2[1:2] * w1[:8], 0.0)),
        (t2[2:3] + t1[:8], jnp.where(r8 == 4, w2[2:3] * w1[:8], 0.0)),
    ]
    keys_ref[...] = _sortable(jnp.concatenate([p[0] for p in pieces], axis=0))
    wts_ref[...] = jnp.concatenate([p[1] for p in pieces], axis=0)
    c1_ref[...] = s1.reshape(c1_ref.shape)
    r1_ref[...] = rank.reshape(r1_ref.shape)
    e2_ref[...] = jnp.exp(s2 - t2[0:1]).astype(BF16)
    s2_ref[...] = s2
    t1_ref[...] = t1


def _route_finish(tau_key, c1_ref, reach_ref, keys_ref, wts_ref, s2_ref, t1_ref):
    tau = _unsortable(tau_key)
    cand = _unsortable(keys_ref[...])
    top = cand[0:1]
    z = jnp.sum(jnp.where(keys_ref[...] >= tau_key, wts_ref[...] * jnp.exp(cand - top), 0.0), axis=0, keepdims=True)
    s2 = s2_ref[...]
    reach = jnp.zeros(s2.shape, F32)
    for r in range(PEER_TOPK):
        reach = reach + jnp.where(t1_ref[r:r + 1, :] + s2 >= tau, 1.0, 0.0)
    reach_ref[...] = reach.astype(BF16)
    c1_ref[...] = jnp.exp(c1_ref[...] - (t1_ref[0:1, :] + jnp.log(z)))


def peer_route(qp, sub_keys):
    n = qp.shape[0]
    tm = ROUTE_TM
    half = PEER_QDIM // 2
    hs = ROUTE_HEADS
    spec = pl.BlockSpec((hs, PEER_KEYS, tm), lambda i, h: (h, 0, i))
    sub = PEER_TE // PEER_KEYS
    chunked = pl.BlockSpec((hs, PEER_KEYS // sub, sub, tm), lambda i, h: (h, 0, 0, i))
    return pl.pallas_call(
        _route_kernel,
        grid=(n // tm, PEER_HEADS // hs),
        in_specs=[pl.BlockSpec((tm, hs * PEER_QDIM), lambda i, h: (i, h)),
                  pl.BlockSpec((hs, 2, PEER_KEYS, half), lambda i, h: (h, 0, 0, 0))],
        out_specs=[chunked, chunked, spec, spec],
        out_shape=[jax.ShapeDtypeStruct((PEER_HEADS, PEER_KEYS // sub, sub, n), F32),
                   jax.ShapeDtypeStruct((PEER_HEADS, PEER_KEYS // sub, sub, n), F32),
                   jax.ShapeDtypeStruct((PEER_HEADS, PEER_KEYS, n), BF16),
                   jax.ShapeDtypeStruct((PEER_HEADS, PEER_KEYS, n), BF16)],
        scratch_shapes=[pltpu.VMEM((hs, ROUTE_ROWS, tm), I32),
                        pltpu.VMEM((hs, ROUTE_ROWS, tm), F32),
                        pltpu.VMEM((hs, PEER_KEYS, tm), F32),
                        pltpu.VMEM((hs, PEER_TOPK, tm), F32)],
        compiler_params=_params(("parallel", "arbitrary")),
        name="peer_route",
    )(qp, sub_keys)


PACK = 16


def _peer_kernel(x_ref, xn_ref, c1_ref, r1_ref, e2_ref, reach_ref, u_ref, vt_ref, gout_ref, o_ref,
                 acc_ref, xnt_ref, hid_ref, *, norm_out):
    j = pl.program_id(1)
    n_sub = u_ref.shape[0] // PEER_KEYS
    tm = xn_ref.shape[0]

    @pl.when(j == 0)
    def _():
        acc_ref[...] = jnp.zeros_like(acc_ref)
        xnt_ref[...] = jnp.transpose(xn_ref[...].astype(F32)).astype(BF16)

    act = jnp.dot(u_ref[...], xnt_ref[...], preferred_element_type=F32)
    for ab in range(n_sub):
        rows = slice(ab * PEER_KEYS, (ab + 1) * PEER_KEYS)
        rank = [jnp.broadcast_to(r1_ref[h, 0, ab:ab + 1, :], (PACK, tm)).astype(BF16) for h in range(PEER_HEADS)]
        fac = [jnp.broadcast_to(c1_ref[h, 0, ab:ab + 1, :], (PACK, tm)).astype(BF16) for h in range(PEER_HEADS)]
        for tc in range(tm // LANES):
            cols = slice(tc * LANES, (tc + 1) * LANES)
            gate = jnp.zeros((PEER_KEYS, LANES), BF16)
            for h in range(PEER_HEADS):
                chosen = jnp.tile(rank[h][:, cols], (PEER_KEYS // PACK, 1)) < reach_ref[h, :, cols]
                e2 = e2_ref[h, :, cols]
                gate = gate + jnp.where(chosen, e2, jnp.zeros_like(e2)) * jnp.tile(fac[h][:, cols], (PEER_KEYS // PACK, 1))
            hid_ref[rows, cols] = gate * jax.nn.gelu(act[rows, cols].astype(BF16))
    acc_ref[...] += jnp.dot(vt_ref[...], hid_ref[...], preferred_element_type=F32)

    @pl.when(j == pl.num_programs(1) - 1)
    def _():
        y = x_ref[...] + jnp.transpose(acc_ref[...])
        o_ref[...] = _rms(y, gout_ref[...]) if norm_out else y


def peer(x, xn, c1, r1, e2, reach, u, vt, gain_out, norm_out, tm=512):
    n, d = x.shape
    te = PEER_TE
    n_chunks = PEER_N // te
    n_sub = te // PEER_KEYS
    chunk_spec = pl.BlockSpec((PEER_HEADS, 1, n_sub, tm), lambda i, j: (0, j, 0, i))
    tile_spec = pl.BlockSpec((PEER_HEADS, PEER_KEYS, tm), lambda i, j: (0, 0, i))
    return pl.pallas_call(
        functools.partial(_peer_kernel, norm_out=norm_out),
        grid=(n // tm, n_chunks),
        in_specs=[pl.BlockSpec((tm, d), lambda i, j: (i, 0)),
                  pl.BlockSpec((tm, d), lambda i, j: (i, 0)),
                  chunk_spec, chunk_spec, tile_spec, tile_spec,
                  pl.BlockSpec((te, d), lambda i, j: (j, 0)),
                  pl.BlockSpec((d, te), lambda i, j: (0, j)),
                  pl.BlockSpec((1, d), lambda i, j: (0, 0))],
        out_specs=pl.BlockSpec((tm, d), lambda i, j: (i, 0)),
        out_shape=jax.ShapeDtypeStruct((n, d), F32),
        scratch_shapes=[pltpu.VMEM((d, tm), F32),
                        pltpu.VMEM((d, tm), BF16),
                        pltpu.VMEM((te, tm), BF16)],
        compiler_params=_params(("parallel", "arbitrary"), vmem=PEER_VMEM_LIMIT),
        name="peer",
    )(x, xn, c1, r1, e2, reach, u, vt, gain_out.reshape(1, d))


def kernel(x, rel_bias, g_mix, w_in, a_sinks, c_lat_gain, c_w_uk, c_w_uv, w_br_a, w_br_b, w_br_c, w_out,
           g_ffn, peer_w_q, peer_sub_keys, peer_u, peer_v, g_final):
    bsz, seq, d = x.shape
    n = bsz * seq
    depth = w_in.shape[0]
    n_blk = seq // TILE
    off = [0] + [int(o) for o in np.cumsum(IN_SPLITS)]
    bt_a = rel_bias[:, :A_HEADS]
    bt_b = rel_bias[:, A_HEADS:A_HEADS + B_HEADS]
    bt_c = rel_bias[:, A_HEADS + B_HEADS:]
    bias_a = _swa_bias(bt_a)
    win_b = _bias_windows(bt_b, seq)
    win_c = _bias_windows(bt_c, seq)

    xf = x.reshape(n, d)
    for l in range(depth):
        w = w_in[l]
        pad = jnp.zeros((d, C_COLS - (off[11] - off[6])), F32)
        w_abc = jnp.concatenate([w[:, off[0]:off[6]],
                                 w[:, off[6]:off[7]], w[:, off[8]:off[9]], w[:, off[7]:off[8]],
                                 w[:, off[9]:off[11]], pad], axis=1).astype(BF16)
        w_g = w[:, off[11]:off[14]].astype(BF16)

        p_ab, p_c = in_proj(xf, g_mix[l], w_abc)
        p_a = p_ab.reshape(bsz, seq, -1)
        p_b = p_ab[:, A_COLS:].reshape(bsz, seq, 3, B_HEADS, HEAD_DIM)
        p_c = p_c.reshape(bsz, seq, -1)

        ya = swa(p_a, bias_a, a_sinks[l]).reshape(n, -1)

        qb = jnp.transpose(p_b[:, :, 0], (0, 2, 1, 3))
        kb = jnp.transpose(p_b[:, :, 1], (0, 2, 1, 3)).reshape(bsz, B_HEADS, n_blk, TILE, HEAD_DIM)
        vbt = jnp.transpose(p_b[:, :, 2].reshape(bsz, n_blk, TILE, B_HEADS, HEAD_DIM), (0, 3, 1, 4, 2))
        yb_t = moba(qb, kb, vbt, win_b)
        yb = jnp.transpose(yb_t, (0, 3, 1, 2)).reshape(n, -1)

        w_uk = jnp.transpose(c_w_uk[l], (1, 0, 2)).astype(BF16)
        w_uv_t = jnp.transpose(c_w_uv[l], (1, 2, 0)).astype(BF16)
        yc_t = dsa(p_c, c_lat_gain[l], w_uk, w_uv_t, win_c)
        yc = jnp.transpose(yc_t, (0, 2, 1)).reshape(n, -1)

        merged = merge(xf, g_mix[l], ya, yb, yc, w_g, w_br_a[l].astype(BF16), w_br_b[l].astype(BF16),
                       w_br_c[l].astype(BF16))
        xf = matmul_res(merged, w_out[l].astype(BF16), xf)

        qp, xn = norm_matmul(xf, g_ffn[l], peer_w_q[l].astype(BF16), BF16)
        c1, r1, e2, reach = peer_route(qp, peer_sub_keys[l].astype(BF16))
        xf = peer(xf, xn, c1, r1, e2, reach, peer_u[l].astype(BF16), jnp.transpose(peer_v[l]).astype(BF16),
                  g_final, norm_out=(l == depth - 1))

    return xf.reshape(bsz, seq, d)
```

```python
import functools
import math

import jax
import jax.numpy as jnp
import numpy as np
from jax import lax
from jax.experimental import pallas as pl
from jax.experimental.pallas import tpu as pltpu

F32 = jnp.float32
BF16 = jnp.bfloat16
I32 = jnp.int32

D_MODEL = 2048
HEAD_DIM = 64
A_HEADS, A_KV_HEADS, A_BLOCK, A_WINDOW = 16, 4, 128, 128
B_HEADS, B_BLOCK, B_TOPK = 8, 256, 3
C_HEADS, C_LATENT, IDX_HEADS, IDX_DIM, C_TOPK_MAX = 8, 128, 8, 32, 256
REL_BUCKETS, REL_MAX_DIST = 32, 2048
PEER_HEADS, PEER_KEYS, PEER_QDIM, PEER_TOPK = 8, 128, 256, 16
PEER_N = PEER_KEYS * PEER_KEYS
EPS = 1e-6
NEG = -1e30
INT_MIN = -(2 ** 31)
IN_SPLITS = (A_HEADS * HEAD_DIM, A_KV_HEADS * HEAD_DIM, A_KV_HEADS * HEAD_DIM,
             B_HEADS * HEAD_DIM, B_HEADS * HEAD_DIM, B_HEADS * HEAD_DIM,
             C_HEADS * HEAD_DIM, C_LATENT, IDX_HEADS * IDX_DIM, IDX_DIM, IDX_HEADS,
             D_MODEL, D_MODEL, D_MODEL)
A_COLS = sum(IN_SPLITS[0:3])
B_COLS = sum(IN_SPLITS[3:6])
C_COLS = 1024

TILE = 256
LANES = 128
SUBLANES = 8
LOCKSTEP = 8
VMEM_LIMIT = 56 * 1024 * 1024
PEER_VMEM_LIMIT = 58 * 1024 * 1024
NT = (((1,), (1,)), ((), ()))


def _params(sem, vmem=VMEM_LIMIT, flags=None):
    return pltpu.CompilerParams(dimension_semantics=sem, vmem_limit_bytes=vmem, flags=flags)


def _rms(x, g):
    return x * lax.rsqrt(jnp.mean(x * x, axis=-1, keepdims=True) + EPS) * g


def _sortable(v):
    bits = pltpu.bitcast(v, I32)
    return jnp.where(bits < 0, bits ^ jnp.int32(0x7FFFFFFF), bits)


def _unsortable(k):
    return pltpu.bitcast(jnp.where(k < 0, k ^ jnp.int32(0x7FFFFFFF), k), F32)


def _in_proj_kernel(x_ref, g_ref, w_ref, oab_ref, oc_ref, h_ref, *, nab):
    j = pl.program_id(1)

    @pl.when(j == 0)
    def _():
        h_ref[...] = _rms(x_ref[...], g_ref[...]).astype(BF16)

    y = jnp.dot(h_ref[...], w_ref[...], preferred_element_type=F32)

    @pl.when(j < nab)
    def _():
        oab_ref[...] = y.astype(oab_ref.dtype)

    @pl.when(j >= nab)
    def _():
        oc_ref[...] = y.astype(oc_ref.dtype)


def in_proj(x, g, w, tm=1024, tn=1024):
    n, d = x.shape
    nab, nc = (A_COLS + B_COLS) // tn, C_COLS // tn
    return pl.pallas_call(
        functools.partial(_in_proj_kernel, nab=nab),
        grid=(n // tm, nab + nc),
        in_specs=[pl.BlockSpec((tm, d), lambda i, j: (i, 0)),
                  pl.BlockSpec((1, d), lambda i, j: (0, 0)),
                  pl.BlockSpec((d, tn), lambda i, j: (0, j))],
        out_specs=[pl.BlockSpec((tm, tn), lambda i, j: (i, jnp.minimum(j, nab - 1))),
                   pl.BlockSpec((tm, tn), lambda i, j: (i, jnp.maximum(j - nab, 0)))],
        out_shape=[jax.ShapeDtypeStruct((n, A_COLS + B_COLS), BF16),
                   jax.ShapeDtypeStruct((n, C_COLS), F32)],
        scratch_shapes=[pltpu.VMEM((tm, d), BF16)],
        compiler_params=_params(("parallel", "arbitrary")),
        name="in_proj",
    )(x, g.reshape(1, d), w)


def _norm_matmul_kernel(x_ref, g_ref, w_ref, o_ref, xn_ref, h_ref):
    @pl.when(pl.program_id(1) == 0)
    def _():
        h = _rms(x_ref[...], g_ref[...]).astype(BF16)
        h_ref[...] = h
        xn_ref[...] = h

    o_ref[...] = jnp.dot(h_ref[...], w_ref[...], preferred_element_type=F32).astype(o_ref.dtype)


def norm_matmul(x, g, w, out_dtype, tm=1024, tn=1024):
    n, d = x.shape
    ncol = w.shape[1]
    return pl.pallas_call(
        _norm_matmul_kernel,
        grid=(n // tm, ncol // tn),
        in_specs=[pl.BlockSpec((tm, d), lambda i, j: (i, 0)),
                  pl.BlockSpec((1, d), lambda i, j: (0, 0)),
                  pl.BlockSpec((d, tn), lambda i, j: (0, j))],
        out_specs=[pl.BlockSpec((tm, tn), lambda i, j: (i, j)),
                   pl.BlockSpec((tm, d), lambda i, j: (i, 0))],
        out_shape=[jax.ShapeDtypeStruct((n, ncol), out_dtype),
                   jax.ShapeDtypeStruct((n, d), BF16)],
        scratch_shapes=[pltpu.VMEM((tm, d), BF16)],
        compiler_params=_params(("parallel", "arbitrary")),
        name="norm_matmul",
    )(x, g.reshape(1, d), w)


def _matmul_res_kernel(a_ref, w_ref, r_ref, o_ref):
    o_ref[...] = r_ref[...] + jnp.dot(a_ref[...], w_ref[...], preferred_element_type=F32)


def matmul_res(a, w, res, tm=1024, tn=1024):
    n, k = a.shape
    ncol = w.shape[1]
    return pl.pallas_call(
        _matmul_res_kernel,
        grid=(n // tm, ncol // tn),
        in_specs=[pl.BlockSpec((tm, k), lambda i, j: (i, 0)),
                  pl.BlockSpec((k, tn), lambda i, j: (0, j)),
                  pl.BlockSpec((tm, tn), lambda i, j: (i, j))],
        out_specs=pl.BlockSpec((tm, tn), lambda i, j: (i, j)),
        out_shape=jax.ShapeDtypeStruct((n, ncol), F32),
        compiler_params=_params(("parallel", "arbitrary")),
        name="matmul_res",
    )(a, w, res)


def _merge_kernel(x_ref, g_ref, ya_ref, yb_ref, yc_ref, wga_ref, wgb_ref, wgc_ref,
                  wa_ref, wb_ref, wc_ref, o_ref, h_ref):
    @pl.when(pl.program_id(1) == 0)
    def _():
        h_ref[...] = _rms(x_ref[...], g_ref[...]).astype(BF16)

    h = h_ref[...]

    def branch(y_ref, wg_ref, w_ref):
        gate = jnp.dot(h, wg_ref[...], preferred_element_type=F32)
        proj = jnp.dot(y_ref[...], w_ref[...], preferred_element_type=F32)
        return jax.nn.sigmoid(gate) * proj

    merged = branch(ya_ref, wga_ref, wa_ref) + branch(yb_ref, wgb_ref, wb_ref) + branch(yc_ref, wgc_ref, wc_ref)
    o_ref[...] = merged.astype(o_ref.dtype)


def merge(x, g, ya, yb, yc, w_gate, wa, wb, wc, tm=1024, tn=512):
    n, d = x.shape
    nj = d // tn
    return pl.pallas_call(
        _merge_kernel,
        grid=(n // tm, nj),
        in_specs=[pl.BlockSpec((tm, d), lambda i, j: (i, 0)),
                  pl.BlockSpec((1, d), lambda i, j: (0, 0)),
                  pl.BlockSpec((tm, ya.shape[1]), lambda i, j: (i, 0)),
                  pl.BlockSpec((tm, yb.shape[1]), lambda i, j: (i, 0)),
                  pl.BlockSpec((tm, yc.shape[1]), lambda i, j: (i, 0)),
                  pl.BlockSpec((d, tn), lambda i, j: (0, j)),
                  pl.BlockSpec((d, tn), lambda i, j: (0, j + nj)),
                  pl.BlockSpec((d, tn), lambda i, j: (0, j + 2 * nj)),
                  pl.BlockSpec((wa.shape[0], tn), lambda i, j: (0, j)),
                  pl.BlockSpec((wb.shape[0], tn), lambda i, j: (0, j)),
                  pl.BlockSpec((wc.shape[0], tn), lambda i, j: (0, j))],
        out_specs=pl.BlockSpec((tm, tn), lambda i, j: (i, j)),
        out_shape=jax.ShapeDtypeStruct((n, d), BF16),
        scratch_shapes=[pltpu.VMEM((tm, d), BF16)],
        compiler_params=_params(("parallel", "arbitrary")),
        name="merge",
    )(x, g.reshape(1, d), ya, yb, yc, w_gate, w_gate, w_gate, wa, wb, wc)


def _rel_bucket(dist):
    d = jnp.maximum(dist, 0)
    max_exact = REL_BUCKETS // 2
    df = jnp.maximum(d, max_exact).astype(F32)
    large = max_exact + (jnp.log(df / max_exact) / math.log(REL_MAX_DIST / max_exact)
                         * (REL_BUCKETS - max_exact)).astype(I32)
    large = jnp.minimum(large, REL_BUCKETS - 1)
    return jnp.where(d < max_exact, d, large)


def _bucket_bias(bt, dist):
    onehot = jax.nn.one_hot(_rel_bucket(dist), REL_BUCKETS, dtype=F32)
    return jnp.einsum("...k,kh->...h", onehot, bt.astype(F32), precision=lax.Precision.HIGHEST)


def _bias_windows(bt, seq):
    dist = (jnp.arange(seq // TILE)[:, None] * TILE - (TILE - 1) + jnp.arange(2 * TILE)[None, :])
    w = _bucket_bias(bt, dist)
    w = jnp.where((dist >= 0)[..., None], w, 0.0)
    return jnp.transpose(w, (2, 0, 1))


def _gen_bias_tiles(win_ref, tiles_ref):
    n_heads, n_rel = win_ref.shape[0], win_ref.shape[1]

    def body(t, carry):
        h = t // n_rel
        r = t % n_rel
        row = win_ref[h, pl.ds(r, 1), :]
        x = jnp.broadcast_to(row, (TILE, 2 * TILE))
        tiles_ref[h, r] = pltpu.roll(x, TILE + 1, 1, stride=1, stride_axis=0)[:, :TILE]
        return carry

    lax.fori_loop(0, n_heads * n_rel, body, 0)


def _swa_kernel(q_ref, kp_ref, kc_ref, vp_ref, vc_ref, bias_ref, sink_ref, o_ref):
    i = pl.program_id(1)
    which = jnp.minimum(i, 1)
    group = A_HEADS // A_KV_HEADS
    outs = []
    for kvh in range(A_KV_HEADS):
        lo, hi = kvh * HEAD_DIM, (kvh + 1) * HEAD_DIM
        kk = jnp.concatenate([kp_ref[0, :, lo:hi], kc_ref[0, :, lo:hi]], axis=0)
        vv = jnp.concatenate([vp_ref[0, :, lo:hi], vc_ref[0, :, lo:hi]], axis=0)
        for g in range(group):
            h = kvh * group + g
            q = q_ref[0, :, h * HEAD_DIM:(h + 1) * HEAD_DIM]
            s = lax.dot_general(q, kk, NT, preferred_element_type=F32) * (HEAD_DIM ** -0.5)
            s = s + bias_ref[which, h]
            sink = sink_ref[h]
            m = jnp.maximum(jnp.max(s, axis=-1, keepdims=True), sink)
            p = jnp.exp(s - m)
            denom = jnp.sum(p, axis=-1, keepdims=True) + jnp.exp(sink - m)
            o = jnp.dot(p.astype(BF16), vv, preferred_element_type=F32) / denom
            outs.append(o.astype(BF16))
    o_ref[0] = jnp.concatenate(outs, axis=-1)


def _swa_bias(bt_a):
    qpos = jnp.arange(A_BLOCK)[:, None] + A_BLOCK
    kpos = jnp.arange(2 * A_BLOCK)[None, :]
    dist = qpos - kpos
    band = (dist >= 0) & (dist < A_WINDOW)
    bias = jnp.transpose(_bucket_bias(bt_a, dist), (2, 0, 1))
    general = jnp.where(band[None], bias, NEG)
    first = jnp.where((band & (kpos >= A_BLOCK))[None], bias, NEG)
    return jnp.stack([first, general])


def swa(p_a, bias, sinks):
    bsz, seq, _ = p_a.shape
    qw, kw = A_HEADS * HEAD_DIM, A_KV_HEADS * HEAD_DIM
    kblk, vblk = qw // kw, qw // kw + 1
    return pl.pallas_call(
        _swa_kernel,
        grid=(bsz, seq // A_BLOCK),
        in_specs=[pl.BlockSpec((1, A_BLOCK, qw), lambda b, i: (b, i, 0)),
                  pl.BlockSpec((1, A_BLOCK, kw), lambda b, i: (b, jnp.maximum(i - 1, 0), kblk)),
                  pl.BlockSpec((1, A_BLOCK, kw), lambda b, i: (b, i, kblk)),
                  pl.BlockSpec((1, A_BLOCK, kw), lambda b, i: (b, jnp.maximum(i - 1, 0), vblk)),
                  pl.BlockSpec((1, A_BLOCK, kw), lambda b, i: (b, i, vblk)),
                  pl.BlockSpec(bias.shape, lambda b, i: (0, 0, 0, 0)),
                  pl.BlockSpec(memory_space=pltpu.SMEM)],
        out_specs=pl.BlockSpec((1, A_BLOCK, qw), lambda b, i: (b, i, 0)),
        out_shape=jax.ShapeDtypeStruct((bsz, seq, qw), BF16),
        compiler_params=_params(("parallel", "arbitrary")),
        name="swa",
    )(p_a, p_a, p_a, p_a, p_a, bias, sinks)


def _moba_kernel(q_ref, k_ref, vt_ref, win_ref, o_ref, tiles_ref, kmean_ref, sel_ref):
    b, i = pl.program_id(0), pl.program_id(1)
    n_heads, n_blk = kmean_ref.shape[0], kmean_ref.shape[1]

    @pl.when((b == 0) & (i == 0))
    def _():
        _gen_bias_tiles(win_ref, tiles_ref)

    @pl.when(i == 0)
    def _():
        for h in range(n_heads):
            kmean_ref[h] = jnp.mean(k_ref[0, h].astype(F32), axis=1)

    row = lax.broadcasted_iota(I32, (n_blk, TILE), 0)
    for h in range(n_heads):
        q = q_ref[0, h]
        kmean = kmean_ref[h]
        kmean_hi = kmean.astype(BF16)
        kmean_lo = (kmean - kmean_hi.astype(F32)).astype(BF16)
        gate = (lax.dot_general(kmean_hi, q, NT, preferred_element_type=F32)
                + lax.dot_general(kmean_lo, q, NT, preferred_element_type=F32))
        gate = jnp.where(row < i, gate, NEG)
        sel = jnp.zeros(gate.shape, F32)
        for _ in range(B_TOPK):
            best = jnp.max(gate, axis=0, keepdims=True)
            first = jnp.min(jnp.where(gate == best, row, n_blk), axis=0, keepdims=True)
            pick = row == first
            sel = jnp.where(pick, 1.0, sel)
            gate = jnp.where(pick, -jnp.inf, gate)
        sel_ref[h] = jnp.where(row < i, sel, 0.0)

    key_pos = lax.broadcasted_iota(I32, (TILE, TILE), 0)
    qry_pos = lax.broadcasted_iota(I32, (TILE, TILE), 1)
    causal = jnp.where(qry_pos >= key_pos, 1.0, 0.0)

    def body(j, carry):
        out = []
        for g in range(0, n_heads, LOCKSTEP):
            pair = tuple(range(g, g + LOCKSTEP))
            s = [lax.dot_general(k_ref[0, h, j], q_ref[0, h], NT, preferred_element_type=F32) * (HEAD_DIM ** -0.5)
                 for h in pair]
            s = [s[k] + tiles_ref[pair[k], i - j] for k in range(LOCKSTEP)]
            ok = [jnp.where(j == i, causal, jnp.broadcast_to(sel_ref[h, pl.ds(j, 1), :], (TILE, TILE))) > 0.0
                  for h in pair]
            s = [jnp.where(ok[k], s[k], NEG) for k in range(LOCKSTEP)]
            m_new = [jnp.maximum(carry[pair[k]][0], jnp.max(s[k], axis=0, keepdims=True)) for k in range(LOCKSTEP)]
            alpha = [jnp.exp(carry[pair[k]][0] - m_new[k]) for k in range(LOCKSTEP)]
            p = [jnp.where(ok[k], jnp.exp(s[k] - m_new[k]), 0.0) for k in range(LOCKSTEP)]
            l_new = [alpha[k] * carry[pair[k]][1] + jnp.sum(p[k], axis=0, keepdims=True) for k in range(LOCKSTEP)]
            pv = [jnp.dot(vt_ref[0, pair[k], j], p[k].astype(BF16), preferred_element_type=F32) for k in range(LOCKSTEP)]
            for k in range(LOCKSTEP):
                out.append((m_new[k], l_new[k], alpha[k] * carry[pair[k]][2] + pv[k]))
        return tuple(out)

    init = tuple((jnp.full((1, TILE), NEG, F32), jnp.zeros((1, TILE), F32), jnp.zeros((HEAD_DIM, TILE), F32))
                 for _ in range(n_heads))
    final = lax.fori_loop(0, i + 1, body, init)
    for h in range(n_heads):
        _, l, acc = final[h]
        o_ref[0, h] = (acc / l).astype(o_ref.dtype)


def moba(q, k, vt, windows):
    bsz, n_heads, seq, d = q.shape
    n_blk = seq // TILE
    return pl.pallas_call(
        _moba_kernel,
        grid=(bsz, n_blk),
        in_specs=[pl.BlockSpec((1, n_heads, TILE, d), lambda b, i: (b, 0, i, 0)),
                  pl.BlockSpec((1, n_heads, n_blk, TILE, d), lambda b, i: (b, 0, 0, 0, 0)),
                  pl.BlockSpec((1, n_heads, n_blk, d, TILE), lambda b, i: (b, 0, 0, 0, 0)),
                  pl.BlockSpec(windows.shape, lambda b, i: (0, 0, 0))],
        out_specs=pl.BlockSpec((1, n_heads, d, TILE), lambda b, i: (b, 0, 0, i)),
        out_shape=jax.ShapeDtypeStruct((bsz, n_heads, d, seq), BF16),
        scratch_shapes=[pltpu.VMEM((n_heads, n_blk, TILE, TILE), F32),
                        pltpu.VMEM((n_heads, n_blk, d), F32),
                        pltpu.VMEM((n_heads, n_blk, TILE), F32)],
        compiler_params=_params(("arbitrary", "arbitrary")),
        name="moba",
    )(q, k, vt, windows)


def _dsa_kernel(qc_ref, iq_ref, mq_ref, clat_ref, mk_ref, gain_ref, wuk_ref, wuvt_ref, win_ref, o_ref,
                tiles_ref, cn_ref, cnt_ref, ik_ref, keys_ref, mask_ref, qabs_ref, *, top_k):
    b, c = pl.program_id(0), pl.program_id(1)
    n_blk = cn_ref.shape[0]
    n_kb = c + 1

    @pl.when((b == 0) & (c == 0))
    def _():
        _gen_bias_tiles(win_ref, tiles_ref)

    @pl.when(c == 0)
    def _():
        lat = _rms(clat_ref[0], gain_ref[...])
        for kb in range(n_blk):
            blk = lat[kb * TILE:(kb + 1) * TILE]
            cn_ref[kb] = blk.astype(BF16)
            cnt_ref[kb] = jnp.transpose(blk).astype(BF16)
            ik_ref[kb] = mk_ref[0, kb * TILE:(kb + 1) * TILE, :IDX_DIM].astype(BF16)

    key_in = lax.broadcasted_iota(I32, (TILE, TILE), 0)
    qry_in = lax.broadcasted_iota(I32, (TILE, TILE), 1)
    diag_causal = key_in <= qry_in

    iq = iq_ref[0].astype(BF16)
    iw_t = jnp.transpose(mq_ref[0])[IDX_DIM:IDX_DIM + IDX_HEADS, :]

    def score_block(kb, carry):
        score = jnp.zeros((TILE, TILE), F32)
        ik = ik_ref[kb]
        logits = [lax.dot_general(ik, iq[:, hh * IDX_DIM:(hh + 1) * IDX_DIM], NT, preferred_element_type=F32)
                  for hh in range(IDX_HEADS)]
        for hh in range(IDX_HEADS):
            score = score + jnp.maximum(logits[hh], 0.0) * iw_t[hh:hh + 1, :]
        score = score * ((IDX_DIM ** -0.5) * (IDX_HEADS ** -0.5))
        score = jnp.where(score == 0.0, 0.0, score)
        key = _sortable(score)
        keys_ref[kb] = jnp.where(kb < c, key, jnp.where(diag_causal, key, INT_MIN))
        return carry

    lax.fori_loop(0, n_kb, score_block, 0)

    def count(pred):
        def blk(kb, part):
            return part + jnp.sum(pred(keys_ref[kb], kb).reshape(TILE // SUBLANES, SUBLANES, TILE), axis=0)
        part = lax.fori_loop(0, n_kb, blk, jnp.zeros((SUBLANES, TILE), F32))
        return jnp.sum(part, axis=0, keepdims=True)

    def value_bit(t, tau):
        cand = tau + lax.shift_left(jnp.int32(1), 31 - t)
        cnt = count(lambda key, kb: jnp.where(key >= cand, 1.0, 0.0))
        return jnp.where(cnt >= top_k, cand, tau)

    tau = lax.fori_loop(0, 32, value_bit, jnp.full((1, TILE), INT_MIN, I32))
    need = top_k - count(lambda key, kb: jnp.where(key > tau, 1.0, 0.0))

    def index_bit(t, bound):
        cand = bound + lax.shift_left(jnp.int32(1), 11 - t)
        cnt = count(lambda key, kb: jnp.where(key == tau, jnp.where(kb * TILE + key_in < cand, 1.0, 0.0), 0.0))
        return jnp.where(cnt <= need, cand, bound)

    at_or_above = count(lambda key, kb: jnp.where(key >= tau, 1.0, 0.0))
    surplus = jnp.max(jnp.where(tau > INT_MIN, at_or_above - top_k, 0.0))
    bound = lax.cond(surplus > 0.0,
                     lambda: lax.fori_loop(0, 12, index_bit, jnp.zeros((1, TILE), I32)),
                     lambda: jnp.full((1, TILE), 2 ** 30, I32))

    def mask_block(kb, carry):
        key = keys_ref[kb]
        tied = jnp.where(key == tau, jnp.where(kb * TILE + key_in < bound, 1.0, 0.0), 0.0)
        chosen = jnp.where(key > tau, 1.0, tied)
        mask_ref[kb] = jnp.where(kb < c, chosen, jnp.where(diag_causal, chosen, 0.0))
        return carry

    lax.fori_loop(0, n_kb, mask_block, 0)

    qc = qc_ref[0].astype(BF16)
    for h in range(C_HEADS):
        qabs_ref[h] = lax.dot_general(wuk_ref[h], qc[:, h * HEAD_DIM:(h + 1) * HEAD_DIM], NT,
                                      preferred_element_type=F32).astype(BF16)

    def attend_block(kb, carry):
        ok = mask_ref[kb] > 0.0
        keys_lat, lat_keys = cn_ref[kb], cnt_ref[kb]
        out = []
        for g in range(0, C_HEADS, LOCKSTEP):
            pair = tuple(range(g, g + LOCKSTEP))
            s = [jnp.dot(keys_lat, qabs_ref[h], preferred_element_type=F32) * (HEAD_DIM ** -0.5) for h in pair]
            s = [jnp.where(ok, s[k] + tiles_ref[pair[k], c - kb], NEG) for k in range(LOCKSTEP)]
            m_new = [jnp.maximum(carry[pair[k]][0], jnp.max(s[k], axis=0, keepdims=True)) for k in range(LOCKSTEP)]
            alpha = [jnp.exp(carry[pair[k]][0] - m_new[k]) for k in range(LOCKSTEP)]
            p = [jnp.where(ok, jnp.exp(s[k] - m_new[k]), 0.0) for k in range(LOCKSTEP)]
            l_new = [alpha[k] * carry[pair[k]][1] + jnp.sum(p[k], axis=0, keepdims=True) for k in range(LOCKSTEP)]
            pv = [jnp.dot(lat_keys, p[k].astype(BF16), preferred_element_type=F32) for k in range(LOCKSTEP)]
            for k in range(LOCKSTEP):
                out.append((m_new[k], l_new[k], alpha[k] * carry[pair[k]][2] + pv[k]))
        return tuple(out)

    init = tuple((jnp.full((1, TILE), NEG, F32), jnp.zeros((1, TILE), F32), jnp.zeros((C_LATENT, TILE), F32))
                 for _ in range(C_HEADS))
    final = lax.fori_loop(0, n_kb, attend_block, init)
    for h in range(C_HEADS):
        _, l, acc = final[h]
        oc_t = (acc / l).astype(BF16)
        out_t = jnp.dot(wuvt_ref[h], oc_t, preferred_element_type=F32)
        o_ref[0, h * HEAD_DIM:(h + 1) * HEAD_DIM, :] = out_t.astype(o_ref.dtype)


def dsa(p_c, gain, w_uk, w_uv_t, windows):
    bsz, seq, _ = p_c.shape
    top_k = min(C_TOPK_MAX, seq // 4)
    qw = C_HEADS * HEAD_DIM
    n_blk = seq // TILE
    return pl.pallas_call(
        functools.partial(_dsa_kernel, top_k=top_k),
        grid=(bsz, n_blk),
        in_specs=[pl.BlockSpec((1, TILE, qw), lambda b, c: (b, c, 0)),
                  pl.BlockSpec((1, TILE, 256), lambda b, c: (b, c, 2)),
                  pl.BlockSpec((1, TILE, 128), lambda b, c: (b, c, 7)),
                  pl.BlockSpec((1, seq, 128), lambda b, c: (b, 0, 6)),
                  pl.BlockSpec((1, seq, 128), lambda b, c: (b, 0, 7)),
                  pl.BlockSpec((1, C_LATENT), lambda b, c: (0, 0)),
                  pl.BlockSpec(w_uk.shape, lambda b, c: (0, 0, 0)),
                  pl.BlockSpec(w_uv_t.shape, lambda b, c: (0, 0, 0)),
                  pl.BlockSpec(windows.shape, lambda b, c: (0, 0, 0))],
        out_specs=pl.BlockSpec((1, qw, TILE), lambda b, c: (b, 0, c)),
        out_shape=jax.ShapeDtypeStruct((bsz, qw, seq), BF16),
        scratch_shapes=[pltpu.VMEM((C_HEADS, n_blk, TILE, TILE), F32),
                        pltpu.VMEM((n_blk, TILE, C_LATENT), BF16),
                        pltpu.VMEM((n_blk, C_LATENT, TILE), BF16),
                        pltpu.VMEM((n_blk, TILE, IDX_DIM), BF16),
                        pltpu.VMEM((n_blk, TILE, TILE), I32),
                        pltpu.VMEM((n_blk, TILE, TILE), F32),
                        pltpu.VMEM((C_HEADS, C_LATENT, TILE), BF16)],
        compiler_params=_params(("arbitrary", "arbitrary")),
        name="dsa",
    )(p_c, p_c, p_c, p_c, p_c, gain.reshape(1, C_LATENT), w_uk, w_uv_t, windows)


PEER_TE = 1024
ROUTE_TM = 256
ROUTE_HEADS = 8
ROUTE_ROWS = 72


def _top_values(arrays, count):
    n = len(arrays)
    vals = [[] for _ in range(n)]
    rank = [jnp.full(a.shape, float(count), F32) for a in arrays]
    rest = list(arrays)
    for r in range(count):
        best = [jnp.max(rest[k], axis=0, keepdims=True) for k in range(n)]
        hit = [rest[k] == best[k] for k in range(n)]
        rank = [jnp.where(hit[k], float(r), rank[k]) for k in range(n)]
        rest = [jnp.where(hit[k], -jnp.inf, rest[k]) for k in range(n)]
        for k in range(n):
            vals[k].append(best[k])
    taken = [jnp.sum(jnp.where(rank[k] < count, 1.0, 0.0), axis=0, keepdims=True) for k in range(n)]
    return [(jnp.concatenate(vals[k], axis=0), rank[k], taken[k]) for k in range(n)]


def _multiplicities(rank, count):
    return jnp.concatenate([jnp.sum(jnp.where(rank == float(r), 1.0, 0.0), axis=0, keepdims=True)
                            for r in range(count)], axis=0)


def _route_kernel(q_ref, k_ref, c1_ref, r1_ref, e2_ref, reach_ref, keys_ref, wts_ref, s2_ref, t1_ref):
    heads = range(k_ref.shape[0])
    tm = q_ref.shape[0]
    for hh in heads:
        _route_prepare(q_ref[:, hh * PEER_QDIM:(hh + 1) * PEER_QDIM], k_ref.at[hh], c1_ref.at[hh], r1_ref.at[hh],
                       e2_ref.at[hh], keys_ref.at[hh], wts_ref.at[hh], s2_ref.at[hh], t1_ref.at[hh])

    def value_bit(t, taus):
        out = []
        for hh in heads:
            c = taus[hh] + lax.shift_left(jnp.int32(1), 31 - t)
            cnt = jnp.sum(jnp.where(keys_ref[hh] >= c, wts_ref[hh], 0.0), axis=0, keepdims=True)
            out.append(jnp.where(cnt >= PEER_TOPK, c, taus[hh]))
        return tuple(out)

    taus = lax.fori_loop(0, 32, value_bit, tuple(jnp.full((1, tm), INT_MIN, I32) for _ in heads))
    for hh in heads:
        _route_finish(taus[hh], c1_ref.at[hh], reach_ref.at[hh], keys_ref.at[hh], wts_ref.at[hh], s2_ref.at[hh],
                      t1_ref.at[hh])


def _route_prepare(q, k_ref, c1_ref, r1_ref, e2_ref, keys_ref, wts_ref, s2_ref, t1_ref):
    tm = q.shape[0]
    half = PEER_QDIM // 2
    s1 = lax.dot_general(k_ref[0], q[:, :half], NT, preferred_element_type=F32)
    s2 = lax.dot_general(k_ref[1], q[:, half:], NT, preferred_element_type=F32)
    (t1, rank, taken1), (t2, rank2, taken2) = _top_values([s1, s2], PEER_TOPK)
    repeats = jnp.max(taken1 + taken2) > 2.0 * PEER_TOPK
    ones = jnp.ones(t1.shape, F32)
    w1, w2 = lax.cond(repeats,
                      lambda: (_multiplicities(rank, PEER_TOPK), _multiplicities(rank2, PEER_TOPK)),
                      lambda: (ones, ones))

    r16 = lax.broadcasted_iota(I32, (16, tm), 0)
    r8 = lax.broadcasted_iota(I32, (8, tm), 0)
    pieces = [
        (t1[0:1] + t2, w1[0:1] * w2),
        (t1[1:2] + t2[:8], w1[1:2] * w2[:8]),
        (t1[2:3] + t2[:8], jnp.where(r8 < 5, w1[2:3] * w2[:8], 0.0)),
        (t1[3:4] + t2[:8], jnp.where(r8 < 4, w1[3:4] * w2[:8], 0.0)),
        (t2[0:1] + t1, jnp.where(r16 >= 4, w2[0:1] * w1, 0.0)),
        (t2[1:2] + t1[:8], jnp.where(r8 >= 4, w2[1:2] * w1[:8], 0.0)),
        (t2[2:3] + t1[:8], jnp.where(r8 == 4, w2[2:3] * w1[:8], 0.0)),
    ]
    keys_ref[...] = _sortable(jnp.concatenate([p[0] for p in pieces], axis=0))
    wts_ref[...] = jnp.concatenate([p[1] for p in pieces], axis=0)
    c1_ref[...] = s1.reshape(c1_ref.shape)
    r1_ref[...] = rank.reshape(r1_ref.shape)
    e2_ref[...] = jnp.exp(s2 - t2[0:1]).astype(BF16)
    s2_ref[...] = s2
    t1_ref[...] = t1


def _route_finish(tau_key, c1_ref, reach_ref, keys_ref, wts_ref, s2_ref, t1_ref):
    tau = _unsortable(tau_key)
    cand = _unsortable(keys_ref[...])
    top = cand[0:1]
    z = jnp.sum(jnp.where(keys_ref[...] >= tau_key, wts_ref[...] * jnp.exp(cand - top), 0.0), axis=0, keepdims=True)
    s2 = s2_ref[...]
    reach = jnp.zeros(s2.shape, F32)
    for r in range(PEER_TOPK):
        reach = reach + jnp.where(t1_ref[r:r + 1, :] + s2 >= tau, 1.0, 0.0)
    reach_ref[...] = reach.astype(BF16)
    c1_ref[...] = jnp.exp(c1_ref[...] - (t1_ref[0:1, :] + jnp.log(z)))


def peer_route(qp, sub_keys):
    n = qp.shape[0]
    tm = ROUTE_TM
    half = PEER_QDIM // 2
    hs = ROUTE_HEADS
    spec = pl.BlockSpec((hs, PEER_KEYS, tm), lambda i, h: (h, 0, i))
    sub = PEER_TE // PEER_KEYS
    chunked = pl.BlockSpec((hs, PEER_KEYS // sub, sub, tm), lambda i, h: (h, 0, 0, i))
    return pl.pallas_call(
        _route_kernel,
        grid=(n // tm, PEER_HEADS // hs),
        in_specs=[pl.BlockSpec((tm, hs * PEER_QDIM), lambda i, h: (i, h)),
                  pl.BlockSpec((hs, 2, PEER_KEYS, half), lambda i, h: (h, 0, 0, 0))],
        out_specs=[chunked, chunked, spec, spec],
        out_shape=[jax.ShapeDtypeStruct((PEER_HEADS, PEER_KEYS // sub, sub, n), F32),
                   jax.ShapeDtypeStruct((PEER_HEADS, PEER_KEYS // sub, sub, n), F32),
                   jax.ShapeDtypeStruct((PEER_HEADS, PEER_KEYS, n), BF16),
                   jax.ShapeDtypeStruct((PEER_HEADS, PEER_KEYS, n), BF16)],
        scratch_shapes=[pltpu.VMEM((hs, ROUTE_ROWS, tm), I32),
                        pltpu.VMEM((hs, ROUTE_ROWS, tm), F32),
                        pltpu.VMEM((hs, PEER_KEYS, tm), F32),
                        pltpu.VMEM((hs, PEER_TOPK, tm), F32)],
        compiler_params=_params(("parallel", "arbitrary")),
        name="peer_route",
    )(qp, sub_keys)


PACK = 16


def _peer_kernel(x_ref, xn_ref, c1_ref, r1_ref, e2_ref, reach_ref, u_ref, vt_ref, gout_ref, o_ref,
                 acc_ref, xnt_ref, hid_ref, *, norm_out):
    j = pl.program_id(1)
    n_sub = u_ref.shape[0] // PEER_KEYS
    tm = xn_ref.shape[0]

    @pl.when(j == 0)
    def _():
        acc_ref[...] = jnp.zeros_like(acc_ref)
        xnt_ref[...] = jnp.transpose(xn_ref[...].astype(F32)).astype(BF16)

    act = jnp.dot(u_ref[...], xnt_ref[...], preferred_element_type=F32)
    for ab in range(n_sub):
        rows = slice(ab * PEER_KEYS, (ab + 1) * PEER_KEYS)
        rank = [jnp.broadcast_to(r1_ref[h, 0, ab:ab + 1, :], (PACK, tm)).astype(BF16) for h in range(PEER_HEADS)]
        fac = [jnp.broadcast_to(c1_ref[h, 0, ab:ab + 1, :], (PACK, tm)).astype(BF16) for h in range(PEER_HEADS)]
        for tc in range(tm // LANES):
            cols = slice(tc * LANES, (tc + 1) * LANES)
            gate = jnp.zeros((PEER_KEYS, LANES), BF16)
            for h in range(PEER_HEADS):
                chosen = jnp.tile(rank[h][:, cols], (PEER_KEYS // PACK, 1)) < reach_ref[h, :, cols]
                e2 = e2_ref[h, :, cols]
                gate = gate + jnp.where(chosen, e2, jnp.zeros_like(e2)) * jnp.tile(fac[h][:, cols], (PEER_KEYS // PACK, 1))
            hid_ref[rows, cols] = gate * jax.nn.gelu(act[rows, cols].astype(BF16))
    acc_ref[...] += jnp.dot(vt_ref[...], hid_ref[...], preferred_element_type=F32)

    @pl.when(j == pl.num_programs(1) - 1)
    def _():
        y = x_ref[...] + jnp.transpose(acc_ref[...])
        o_ref[...] = _rms(y, gout_ref[...]) if norm_out else y


def peer(x, xn, c1, r1, e2, reach, u, vt, gain_out, norm_out, tm=512):
    n, d = x.shape
    te = PEER_TE
    n_chunks = PEER_N // te
    n_sub = te // PEER_KEYS
    chunk_spec = pl.BlockSpec((PEER_HEADS, 1, n_sub, tm), lambda i, j: (0, j, 0, i))
    tile_spec = pl.BlockSpec((PEER_HEADS, PEER_KEYS, tm), lambda i, j: (0, 0, i))
    return pl.pallas_call(
        functools.partial(_peer_kernel, norm_out=norm_out),
        grid=(n // tm, n_chunks),
        in_specs=[pl.BlockSpec((tm, d), lambda i, j: (i, 0)),
                  pl.BlockSpec((tm, d), lambda i, j: (i, 0)),
                  chunk_spec, chunk_spec, tile_spec, tile_spec,
                  pl.BlockSpec((te, d), lambda i, j: (j, 0)),
                  pl.BlockSpec((d, te), lambda i, j: (0, j)),
                  pl.BlockSpec((1, d), lambda i, j: (0, 0))],
        out_specs=pl.BlockSpec((tm, d), lambda i, j: (i, 0)),
        out_shape=jax.ShapeDtypeStruct((n, d), F32),
        scratch_shapes=[pltpu.VMEM((d, tm), F32),
                        pltpu.VMEM((d, tm), BF16),
                        pltpu.VMEM((te, tm), BF16)],
        compiler_params=_params(("parallel", "arbitrary"), vmem=PEER_VMEM_LIMIT),
        name="peer",
    )(x, xn, c1, r1, e2, reach, u, vt, gain_out.reshape(1, d))


def kernel(x, rel_bias, g_mix, w_in, a_sinks, c_lat_gain, c_w_uk, c_w_uv, w_br_a, w_br_b, w_br_c, w_out,
           g_ffn, peer_w_q, peer_sub_keys, peer_u, peer_v, g_final):
    bsz, seq, d = x.shape
    n = bsz * seq
    depth = w_in.shape[0]
    n_blk = seq // TILE
    off = [0] + [int(o) for o in np.cumsum(IN_SPLITS)]
    bt_a = rel_bias[:, :A_HEADS]
    bt_b = rel_bias[:, A_HEADS:A_HEADS + B_HEADS]
    bt_c = rel_bias[:, A_HEADS + B_HEADS:]
    bias_a = _swa_bias(bt_a)
    win_b = _bias_windows(bt_b, seq)
    win_c = _bias_windows(bt_c, seq)

    xf = x.reshape(n, d)
    for l in range(depth):
        w = w_in[l]
        pad = jnp.zeros((d, C_COLS - (off[11] - off[6])), F32)
        w_abc = jnp.concatenate([w[:, off[0]:off[6]],
                                 w[:, off[6]:off[7]], w[:, off[8]:off[9]], w[:, off[7]:off[8]],
                                 w[:, off[9]:off[11]], pad], axis=1).astype(BF16)
        w_g = w[:, off[11]:off[14]].astype(BF16)

        p_ab, p_c = in_proj(xf, g_mix[l], w_abc)
        p_a = p_ab.reshape(bsz, seq, -1)
        p_b = p_ab[:, A_COLS:].reshape(bsz, seq, 3, B_HEADS, HEAD_DIM)
        p_c = p_c.reshape(bsz, seq, -1)

        ya = swa(p_a, bias_a, a_sinks[l]).reshape(n, -1)

        qb = jnp.transpose(p_b[:, :, 0], (0, 2, 1, 3))
        kb = jnp.transpose(p_b[:, :, 1], (0, 2, 1, 3)).reshape(bsz, B_HEADS, n_blk, TILE, HEAD_DIM)
        vbt = jnp.transpose(p_b[:, :, 2].reshape(bsz, n_blk, TILE, B_HEADS, HEAD_DIM), (0, 3, 1, 4, 2))
        yb_t = moba(qb, kb, vbt, win_b)
        yb = jnp.transpose(yb_t, (0, 3, 1, 2)).reshape(n, -1)

        w_uk = jnp.transpose(c_w_uk[l], (1, 0, 2)).astype(BF16)
        w_uv_t = jnp.transpose(c_w_uv[l], (1, 2, 0)).astype(BF16)
        yc_t = dsa(p_c, c_lat_gain[l], w_uk, w_uv_t, win_c)
        yc = jnp.transpose(yc_t, (0, 2, 1)).reshape(n, -1)

        merged = merge(xf, g_mix[l], ya, yb, yc, w_g, w_br_a[l].astype(BF16), w_br_b[l].astype(BF16),
                       w_br_c[l].astype(BF16))
        xf = matmul_res(merged, w_out[l].astype(BF16), xf)

        qp, xn = norm_matmul(xf, g_ffn[l], peer_w_q[l].astype(BF16), BF16)
        c1, r1, e2, reach = peer_route(qp, peer_sub_keys[l].astype(BF16))
        xf = peer(xf, xn, c1, r1, e2, reach, peer_u[l].astype(BF16), jnp.transpose(peer_v[l]).astype(BF16),
                  g_final, norm_out=(l == depth - 1))

    return xf.reshape(bsz, seq, d)
```

```python
import functools
import math

import jax
import jax.numpy as jnp
import numpy as np
from jax import lax
from jax.experimental import pallas as pl
from jax.experimental.pallas import tpu as pltpu

F32 = jnp.float32
BF16 = jnp.bfloat16
I32 = jnp.int32

D_MODEL = 2048
HEAD_DIM = 64
A_HEADS, A_KV_HEADS, A_BLOCK, A_WINDOW = 16, 4, 128, 128
B_HEADS, B_BLOCK, B_TOPK = 8, 256, 3
C_HEADS, C_LATENT, IDX_HEADS, IDX_DIM, C_TOPK_MAX = 8, 128, 8, 32, 256
REL_BUCKETS, REL_MAX_DIST = 32, 2048
PEER_HEADS, PEER_KEYS, PEER_QDIM, PEER_TOPK = 8, 128, 256, 16
PEER_N = PEER_KEYS * PEER_KEYS
EPS = 1e-6
NEG = -1e30
INT_MIN = -(2 ** 31)
IN_SPLITS = (A_HEADS * HEAD_DIM, A_KV_HEADS * HEAD_DIM, A_KV_HEADS * HEAD_DIM,
             B_HEADS * HEAD_DIM, B_HEADS * HEAD_DIM, B_HEADS * HEAD_DIM,
             C_HEADS * HEAD_DIM, C_LATENT, IDX_HEADS * IDX_DIM, IDX_DIM, IDX_HEADS,
             D_MODEL, D_MODEL, D_MODEL)
A_COLS = sum(IN_SPLITS[0:3])
B_COLS = sum(IN_SPLITS[3:6])
C_COLS = 1024

TILE = 256
LANES = 128
SUBLANES = 8
LOCKSTEP = 8
VMEM_LIMIT = 56 * 1024 * 1024
PEER_VMEM_LIMIT = 58 * 1024 * 1024
NT = (((1,), (1,)), ((), ()))


def _params(sem, vmem=VMEM_LIMIT, flags=None):
    return pltpu.CompilerParams(dimension_semantics=sem, vmem_limit_bytes=vmem, flags=flags)


def _rms(x, g):
    return x * lax.rsqrt(jnp.mean(x * x, axis=-1, keepdims=True) + EPS) * g


def _sortable(v):
    bits = pltpu.bitcast(v, I32)
    return jnp.where(bits < 0, bits ^ jnp.int32(0x7FFFFFFF), bits)


def _unsortable(k):
    return pltpu.bitcast(jnp.where(k < 0, k ^ jnp.int32(0x7FFFFFFF), k), F32)


def _in_proj_kernel(x_ref, g_ref, w_ref, oab_ref, oc_ref, h_ref, *, nab):
    j = pl.program_id(1)

    @pl.when(j == 0)
    def _():
        h_ref[...] = _rms(x_ref[...], g_ref[...]).astype(BF16)

    y = jnp.dot(h_ref[...], w_ref[...], preferred_element_type=F32)

    @pl.when(j < nab)
    def _():
        oab_ref[...] = y.astype(oab_ref.dtype)

    @pl.when(j >= nab)
    def _():
        oc_ref[...] = y.astype(oc_ref.dtype)


def in_proj(x, g, w, tm=1024, tn=1024):
    n, d = x.shape
    nab, nc = (A_COLS + B_COLS) // tn, C_COLS // tn
    return pl.pallas_call(
        functools.partial(_in_proj_kernel, nab=nab),
        grid=(n // tm, nab + nc),
        in_specs=[pl.BlockSpec((tm, d), lambda i, j: (i, 0)),
                  pl.BlockSpec((1, d), lambda i, j: (0, 0)),
                  pl.BlockSpec((d, tn), lambda i, j: (0, j))],
        out_specs=[pl.BlockSpec((tm, tn), lambda i, j: (i, jnp.minimum(j, nab - 1))),
                   pl.BlockSpec((tm, tn), lambda i, j: (i, jnp.maximum(j - nab, 0)))],
        out_shape=[jax.ShapeDtypeStruct((n, A_COLS + B_COLS), BF16),
                   jax.ShapeDtypeStruct((n, C_COLS), F32)],
        scratch_shapes=[pltpu.VMEM((tm, d), BF16)],
        compiler_params=_params(("parallel", "arbitrary")),
        name="in_proj",
    )(x, g.reshape(1, d), w)


def _norm_matmul_kernel(x_ref, g_ref, w_ref, o_ref, xn_ref, h_ref):
    @pl.when(pl.program_id(1) == 0)
    def _():
        h = _rms(x_ref[...], g_ref[...]).astype(BF16)
        h_ref[...] = h
        xn_ref[...] = h

    o_ref[...] = jnp.dot(h_ref[...], w_ref[...], preferred_element_type=F32).astype(o_ref.dtype)


def norm_matmul(x, g, w, out_dtype, tm=1024, tn=1024):
    n, d = x.shape
    ncol = w.shape[1]
    return pl.pallas_call(
        _norm_matmul_kernel,
        grid=(n // tm, ncol // tn),
        in_specs=[pl.BlockSpec((tm, d), lambda i, j: (i, 0)),
                  pl.BlockSpec((1, d), lambda i, j: (0, 0)),
                  pl.BlockSpec((d, tn), lambda i, j: (0, j))],
        out_specs=[pl.BlockSpec((tm, tn), lambda i, j: (i, j)),
                   pl.BlockSpec((tm, d), lambda i, j: (i, 0))],
        out_shape=[jax.ShapeDtypeStruct((n, ncol), out_dtype),
                   jax.ShapeDtypeStruct((n, d), BF16)],
        scratch_shapes=[pltpu.VMEM((tm, d), BF16)],
        compiler_params=_params(("parallel", "arbitrary")),
        name="norm_matmul",
    )(x, g.reshape(1, d), w)


def _matmul_res_kernel(a_ref, w_ref, r_ref, o_ref):
    o_ref[...] = r_ref[...] + jnp.dot(a_ref[...], w_ref[...], preferred_element_type=F32)


def matmul_res(a, w, res, tm=1024, tn=1024):
    n, k = a.shape
    ncol = w.shape[1]
    return pl.pallas_call(
        _matmul_res_kernel,
        grid=(n // tm, ncol // tn),
        in_specs=[pl.BlockSpec((tm, k), lambda i, j: (i, 0)),
                  pl.BlockSpec((k, tn), lambda i, j: (0, j)),
                  pl.BlockSpec((tm, tn), lambda i, j: (i, j))],
        out_specs=pl.BlockSpec((tm, tn), lambda i, j: (i, j)),
        out_shape=jax.ShapeDtypeStruct((n, ncol), F32),
        compiler_params=_params(("parallel", "arbitrary")),
        name="matmul_res",
    )(a, w, res)


def _merge_kernel(x_ref, g_ref, ya_ref, yb_ref, yc_ref, wga_ref, wgb_ref, wgc_ref,
                  wa_ref, wb_ref, wc_ref, o_ref, h_ref):
    @pl.when(pl.program_id(1) == 0)
    def _():
        h_ref[...] = _rms(x_ref[...], g_ref[...]).astype(BF16)

    h = h_ref[...]

    def branch(y_ref, wg_ref, w_ref):
        gate = jnp.dot(h, wg_ref[...], preferred_element_type=F32)
        proj = jnp.dot(y_ref[...], w_ref[...], preferred_element_type=F32)
        return jax.nn.sigmoid(gate) * proj

    merged = branch(ya_ref, wga_ref, wa_ref) + branch(yb_ref, wgb_ref, wb_ref) + branch(yc_ref, wgc_ref, wc_ref)
    o_ref[...] = merged.astype(o_ref.dtype)


def merge(x, g, ya, yb, yc, w_gate, wa, wb, wc, tm=1024, tn=512):
    n, d = x.shape
    nj = d // tn
    return pl.pallas_call(
        _merge_kernel,
        grid=(n // tm, nj),
        in_specs=[pl.BlockSpec((tm, d), lambda i, j: (i, 0)),
                  pl.BlockSpec((1, d), lambda i, j: (0, 0)),
                  pl.BlockSpec((tm, ya.shape[1]), lambda i, j: (i, 0)),
                  pl.BlockSpec((tm, yb.shape[1]), lambda i, j: (i, 0)),
                  pl.BlockSpec((tm, yc.shape[1]), lambda i, j: (i, 0)),
                  pl.BlockSpec((d, tn), lambda i, j: (0, j)),
                  pl.BlockSpec((d, tn), lambda i, j: (0, j + nj)),
                  pl.BlockSpec((d, tn), lambda i, j: (0, j + 2 * nj)),
                  pl.BlockSpec((wa.shape[0], tn), lambda i, j: (0, j)),
                  pl.BlockSpec((wb.shape[0], tn), lambda i, j: (0, j)),
                  pl.BlockSpec((wc.shape[0], tn), lambda i, j: (0, j))],
        out_specs=pl.BlockSpec((tm, tn), lambda i, j: (i, j)),
        out_shape=jax.ShapeDtypeStruct((n, d), BF16),
        scratch_shapes=[pltpu.VMEM((tm, d), BF16)],
        compiler_params=_params(("parallel", "arbitrary")),
        name="merge",
    )(x, g.reshape(1, d), ya, yb, yc, w_gate, w_gate, w_gate, wa, wb, wc)


def _rel_bucket(dist):
    d = jnp.maximum(dist, 0)
    max_exact = REL_BUCKETS // 2
    df = jnp.maximum(d, max_exact).astype(F32)
    large = max_exact + (jnp.log(df / max_exact) / math.log(REL_MAX_DIST / max_exact)
                         * (REL_BUCKETS - max_exact)).astype(I32)
    large = jnp.minimum(large, REL_BUCKETS - 1)
    return jnp.where(d < max_exact, d, large)


def _bucket_bias(bt, dist):
    onehot = jax.nn.one_hot(_rel_bucket(dist), REL_BUCKETS, dtype=F32)
    return jnp.einsum("...k,kh->...h", onehot, bt.astype(F32), precision=lax.Precision.HIGHEST)


def _bias_windows(bt, seq):
    dist = (jnp.arange(seq // TILE)[:, None] * TILE - (TILE - 1) + jnp.arange(2 * TILE)[None, :])
    w = _bucket_bias(bt, dist)
    w = jnp.where((dist >= 0)[..., None], w, 0.0)
    return jnp.transpose(w, (2, 0, 1))


def _gen_bias_tiles(win_ref, tiles_ref):
    n_heads, n_rel = win_ref.shape[0], win_ref.shape[1]

    def body(t, carry):
        h = t // n_rel
        r = t % n_rel
        row = win_ref[h, pl.ds(r, 1), :]
        x = jnp.broadcast_to(row, (TILE, 2 * TILE))
        tiles_ref[h, r] = pltpu.roll(x, TILE + 1, 1, stride=1, stride_axis=0)[:, :TILE]
        return carry

    lax.fori_loop(0, n_heads * n_rel, body, 0)


def _swa_kernel(q_ref, kp_ref, kc_ref, vp_ref, vc_ref, bias_ref, sink_ref, o_ref):
    i = pl.program_id(1)
    which = jnp.minimum(i, 1)
    group = A_HEADS // A_KV_HEADS
    outs = []
    for kvh in range(A_KV_HEADS):
        lo, hi = kvh * HEAD_DIM, (kvh + 1) * HEAD_DIM
        kk = jnp.concatenate([kp_ref[0, :, lo:hi], kc_ref[0, :, lo:hi]], axis=0)
        vv = jnp.concatenate([vp_ref[0, :, lo:hi], vc_ref[0, :, lo:hi]], axis=0)
        for g in range(group):
            h = kvh * group + g
            q = q_ref[0, :, h * HEAD_DIM:(h + 1) * HEAD_DIM]
            s = lax.dot_general(q, kk, NT, preferred_element_type=F32) * (HEAD_DIM ** -0.5)
            s = s + bias_ref[which, h]
            sink = sink_ref[h]
            m = jnp.maximum(jnp.max(s, axis=-1, keepdims=True), sink)
            p = jnp.exp(s - m)
            denom = jnp.sum(p, axis=-1, keepdims=True) + jnp.exp(sink - m)
            o = jnp.dot(p.astype(BF16), vv, preferred_element_type=F32) / denom
            outs.append(o.astype(BF16))
    o_ref[0] = jnp.concatenate(outs, axis=-1)


def _swa_bias(bt_a):
    qpos = jnp.arange(A_BLOCK)[:, None] + A_BLOCK
    kpos = jnp.arange(2 * A_BLOCK)[None, :]
    dist = qpos - kpos
    band = (dist >= 0) & (dist < A_WINDOW)
    bias = jnp.transpose(_bucket_bias(bt_a, dist), (2, 0, 1))
    general = jnp.where(band[None], bias, NEG)
    first = jnp.where((band & (kpos >= A_BLOCK))[None], bias, NEG)
    return jnp.stack([first, general])


def swa(p_a, bias, sinks):
    bsz, seq, _ = p_a.shape
    qw, kw = A_HEADS * HEAD_DIM, A_KV_HEADS * HEAD_DIM
    kblk, vblk = qw // kw, qw // kw + 1
    return pl.pallas_call(
        _swa_kernel,
        grid=(bsz, seq // A_BLOCK),
        in_specs=[pl.BlockSpec((1, A_BLOCK, qw), lambda b, i: (b, i, 0)),
                  pl.BlockSpec((1, A_BLOCK, kw), lambda b, i: (b, jnp.maximum(i - 1, 0), kblk)),
                  pl.BlockSpec((1, A_BLOCK, kw), lambda b, i: (b, i, kblk)),
                  pl.BlockSpec((1, A_BLOCK, kw), lambda b, i: (b, jnp.maximum(i - 1, 0), vblk)),
                  pl.BlockSpec((1, A_BLOCK, kw), lambda b, i: (b, i, vblk)),
                  pl.BlockSpec(bias.shape, lambda b, i: (0, 0, 0, 0)),
                  pl.BlockSpec(memory_space=pltpu.SMEM)],
        out_specs=pl.BlockSpec((1, A_BLOCK, qw), lambda b, i: (b, i, 0)),
        out_shape=jax.ShapeDtypeStruct((bsz, seq, qw), BF16),
        compiler_params=_params(("parallel", "arbitrary")),
        name="swa",
    )(p_a, p_a, p_a, p_a, p_a, bias, sinks)


def _moba_kernel(q_ref, k_ref, vt_ref, win_ref, o_ref, tiles_ref, kmean_ref, sel_ref):
    b, i = pl.program_id(0), pl.program_id(1)
    n_heads, n_blk = kmean_ref.shape[0], kmean_ref.shape[1]

    @pl.when((b == 0) & (i == 0))
    def _():
        _gen_bias_tiles(win_ref, tiles_ref)

    @pl.when(i == 0)
    def _():
        for h in range(n_heads):
            kmean_ref[h] = jnp.mean(k_ref[0, h].astype(F32), axis=1)

    row = lax.broadcasted_iota(I32, (n_blk, TILE), 0)
    for h in range(n_heads):
        q = q_ref[0, h]
        kmean = kmean_ref[h]
        kmean_hi = kmean.astype(BF16)
        kmean_lo = (kmean - kmean_hi.astype(F32)).astype(BF16)
        gate = (lax.dot_general(kmean_hi, q, NT, preferred_element_type=F32)
                + lax.dot_general(kmean_lo, q, NT, preferred_element_type=F32))
        gate = jnp.where(row < i, gate, NEG)
        sel = jnp.zeros(gate.shape, F32)
        for _ in range(B_TOPK):
            best = jnp.max(gate, axis=0, keepdims=True)
            first = jnp.min(jnp.where(gate == best, row, n_blk), axis=0, keepdims=True)
            pick = row == first
            sel = jnp.where(pick, 1.0, sel)
            gate = jnp.where(pick, -jnp.inf, gate)
        sel_ref[h] = jnp.where(row < i, sel, 0.0)

    key_pos = lax.broadcasted_iota(I32, (TILE, TILE), 0)
    qry_pos = lax.broadcasted_iota(I32, (TILE, TILE), 1)
    causal = jnp.where(qry_pos >= key_pos, 1.0, 0.0)

    def body(j, carry):
        out = []
        for g in range(0, n_heads, LOCKSTEP):
            pair = tuple(range(g, g + LOCKSTEP))
            s = [lax.dot_general(k_ref[0, h, j], q_ref[0, h], NT, preferred_element_type=F32) * (HEAD_DIM ** -0.5)
                 for h in pair]
            s = [s[k] + tiles_ref[pair[k], i - j] for k in range(LOCKSTEP)]
            ok = [jnp.where(j == i, causal, jnp.broadcast_to(sel_ref[h, pl.ds(j, 1), :], (TILE, TILE))) > 0.0
                  for h in pair]
            s = [jnp.where(ok[k], s[k], NEG) for k in range(LOCKSTEP)]
            m_new = [jnp.maximum(carry[pair[k]][0], jnp.max(s[k], axis=0, keepdims=True)) for k in range(LOCKSTEP)]
            alpha = [jnp.exp(carry[pair[k]][0] - m_new[k]) for k in range(LOCKSTEP)]
            p = [jnp.where(ok[k], jnp.exp(s[k] - m_new[k]), 0.0) for k in range(LOCKSTEP)]
            l_new = [alpha[k] * carry[pair[k]][1] + jnp.sum(p[k], axis=0, keepdims=True) for k in range(LOCKSTEP)]
            pv = [jnp.dot(vt_ref[0, pair[k], j], p[k].astype(BF16), preferred_element_type=F32) for k in range(LOCKSTEP)]
            for k in range(LOCKSTEP):
                out.append((m_new[k], l_new[k], alpha[k] * carry[pair[k]][2] + pv[k]))
        return tuple(out)

    init = tuple((jnp.full((1, TILE), NEG, F32), jnp.zeros((1, TILE), F32), jnp.zeros((HEAD_DIM, TILE), F32))
                 for _ in range(n_heads))
    final = lax.fori_loop(0, i + 1, body, init)
    outs = [jnp.transpose(final[h][2] / final[h][1]).astype(o_ref.dtype) for h in range(n_heads)]
    o_ref[0] = jnp.concatenate(outs, axis=-1)


def moba(q, k, vt, windows):
    bsz, n_heads, seq, d = q.shape
    n_blk = seq // TILE
    return pl.pallas_call(
        _moba_kernel,
        grid=(bsz, n_blk),
        in_specs=[pl.BlockSpec((1, n_heads, TILE, d), lambda b, i: (b, 0, i, 0)),
                  pl.BlockSpec((1, n_heads, n_blk, TILE, d), lambda b, i: (b, 0, 0, 0, 0)),
                  pl.BlockSpec((1, n_heads, n_blk, d, TILE), lambda b, i: (b, 0, 0, 0, 0)),
                  pl.BlockSpec(windows.shape, lambda b, i: (0, 0, 0))],
        out_specs=pl.BlockSpec((1, TILE, n_heads * d), lambda b, i: (b, i, 0)),
        out_shape=jax.ShapeDtypeStruct((bsz, seq, n_heads * d), BF16),
        scratch_shapes=[pltpu.VMEM((n_heads, n_blk, TILE, TILE), F32),
                        pltpu.VMEM((n_heads, n_blk, d), F32),
                        pltpu.VMEM((n_heads, n_blk, TILE), F32)],
        compiler_params=_params(("arbitrary", "arbitrary")),
        name="moba",
    )(q, k, vt, windows)


def _dsa_kernel(qc_ref, iq_ref, mq_ref, clat_ref, mk_ref, gain_ref, wuk_ref, wuv_ref, win_ref, o_ref,
                tiles_ref, cn_ref, cnt_ref, ik_ref, keys_ref, mask_ref, qabs_ref, *, top_k):
    b, c = pl.program_id(0), pl.program_id(1)
    n_blk = cn_ref.shape[0]
    n_kb = c + 1

    @pl.when((b == 0) & (c == 0))
    def _():
        _gen_bias_tiles(win_ref, tiles_ref)

    @pl.when(c == 0)
    def _():
        lat = _rms(clat_ref[0], gain_ref[...])
        for kb in range(n_blk):
            blk = lat[kb * TILE:(kb + 1) * TILE]
            cn_ref[kb] = blk.astype(BF16)
            cnt_ref[kb] = jnp.transpose(blk).astype(BF16)
            ik_ref[kb] = mk_ref[0, kb * TILE:(kb + 1) * TILE, :IDX_DIM].astype(BF16)

    key_in = lax.broadcasted_iota(I32, (TILE, TILE), 0)
    qry_in = lax.broadcasted_iota(I32, (TILE, TILE), 1)
    diag_causal = key_in <= qry_in

    iq = iq_ref[0].astype(BF16)
    iw_t = jnp.transpose(mq_ref[0])[IDX_DIM:IDX_DIM + IDX_HEADS, :]

    def score_block(kb, carry):
        score = jnp.zeros((TILE, TILE), F32)
        ik = ik_ref[kb]
        logits = [lax.dot_general(ik, iq[:, hh * IDX_DIM:(hh + 1) * IDX_DIM], NT, preferred_element_type=F32)
                  for hh in range(IDX_HEADS)]
        for hh in range(IDX_HEADS):
            score = score + jnp.maximum(logits[hh], 0.0) * iw_t[hh:hh + 1, :]
        score = score * ((IDX_DIM ** -0.5) * (IDX_HEADS ** -0.5))
        score = jnp.where(score == 0.0, 0.0, score)
        key = _sortable(score)
        keys_ref[kb] = jnp.where(kb < c, key, jnp.where(diag_causal, key, INT_MIN))
        return carry

    lax.fori_loop(0, n_kb, score_block, 0)

    def count(pred):
        def blk(kb, part):
            return part + jnp.sum(pred(keys_ref[kb], kb).reshape(TILE // SUBLANES, SUBLANES, TILE), axis=0)
        part = lax.fori_loop(0, n_kb, blk, jnp.zeros((SUBLANES, TILE), F32))
        return jnp.sum(part, axis=0, keepdims=True)

    def value_bit(t, tau):
        cand = tau + lax.shift_left(jnp.int32(1), 31 - t)
        cnt = count(lambda key, kb: jnp.where(key >= cand, 1.0, 0.0))
        return jnp.where(cnt >= top_k, cand, tau)

    tau = lax.fori_loop(0, 32, value_bit, jnp.full((1, TILE), INT_MIN, I32))
    need = top_k - count(lambda key, kb: jnp.where(key > tau, 1.0, 0.0))

    def index_bit(t, bound):
        cand = bound + lax.shift_left(jnp.int32(1), 11 - t)
        cnt = count(lambda key, kb: jnp.where(key == tau, jnp.where(kb * TILE + key_in < cand, 1.0, 0.0), 0.0))
        return jnp.where(cnt <= need, cand, bound)

    at_or_above = count(lambda key, kb: jnp.where(key >= tau, 1.0, 0.0))
    surplus = jnp.max(jnp.where(tau > INT_MIN, at_or_above - top_k, 0.0))
    bound = lax.cond(surplus > 0.0,
                     lambda: lax.fori_loop(0, 12, index_bit, jnp.zeros((1, TILE), I32)),
                     lambda: jnp.full((1, TILE), 2 ** 30, I32))

    def mask_block(kb, carry):
        key = keys_ref[kb]
        tied = jnp.where(key == tau, jnp.where(kb * TILE + key_in < bound, 1.0, 0.0), 0.0)
        chosen = jnp.where(key > tau, 1.0, tied)
        mask_ref[kb] = jnp.where(kb < c, chosen, jnp.where(diag_causal, chosen, 0.0))
        return carry

    lax.fori_loop(0, n_kb, mask_block, 0)

    qc = qc_ref[0].astype(BF16)
    for h in range(C_HEADS):
        qabs_ref[h] = lax.dot_general(wuk_ref[h], qc[:, h * HEAD_DIM:(h + 1) * HEAD_DIM], NT,
                                      preferred_element_type=F32).astype(BF16)

    def attend_block(kb, carry):
        ok = mask_ref[kb] > 0.0
        keys_lat, lat_keys = cn_ref[kb], cnt_ref[kb]
        out = []
        for g in range(0, C_HEADS, LOCKSTEP):
            pair = tuple(range(g, g + LOCKSTEP))
            s = [jnp.dot(keys_lat, qabs_ref[h], preferred_element_type=F32) * (HEAD_DIM ** -0.5) for h in pair]
            s = [jnp.where(ok, s[k] + tiles_ref[pair[k], c - kb], NEG) for k in range(LOCKSTEP)]
            m_new = [jnp.maximum(carry[pair[k]][0], jnp.max(s[k], axis=0, keepdims=True)) for k in range(LOCKSTEP)]
            alpha = [jnp.exp(carry[pair[k]][0] - m_new[k]) for k in range(LOCKSTEP)]
            p = [jnp.where(ok, jnp.exp(s[k] - m_new[k]), 0.0) for k in range(LOCKSTEP)]
            l_new = [alpha[k] * carry[pair[k]][1] + jnp.sum(p[k], axis=0, keepdims=True) for k in range(LOCKSTEP)]
            pv = [jnp.dot(lat_keys, p[k].astype(BF16), preferred_element_type=F32) for k in range(LOCKSTEP)]
            for k in range(LOCKSTEP):
                out.append((m_new[k], l_new[k], alpha[k] * carry[pair[k]][2] + pv[k]))
        return tuple(out)

    init = tuple((jnp.full((1, TILE), NEG, F32), jnp.zeros((1, TILE), F32), jnp.zeros((C_LATENT, TILE), F32))
                 for _ in range(C_HEADS))
    final = lax.fori_loop(0, n_kb, attend_block, init)
    outs = []
    for h in range(C_HEADS):
        _, l, acc = final[h]
        oc = jnp.transpose(acc / l).astype(BF16)
        outs.append(jnp.dot(oc, wuv_ref[h], preferred_element_type=F32).astype(o_ref.dtype))
    o_ref[0] = jnp.concatenate(outs, axis=-1)


def dsa(p_c, gain, w_uk, w_uv, windows):
    bsz, seq, _ = p_c.shape
    top_k = min(C_TOPK_MAX, seq // 4)
    qw = C_HEADS * HEAD_DIM
    n_blk = seq // TILE
    return pl.pallas_call(
        functools.partial(_dsa_kernel, top_k=top_k),
        grid=(bsz, n_blk),
        in_specs=[pl.BlockSpec((1, TILE, qw), lambda b, c: (b, c, 0)),
                  pl.BlockSpec((1, TILE, 256), lambda b, c: (b, c, 2)),
                  pl.BlockSpec((1, TILE, 128), lambda b, c: (b, c, 7)),
                  pl.BlockSpec((1, seq, 128), lambda b, c: (b, 0, 6)),
                  pl.BlockSpec((1, seq, 128), lambda b, c: (b, 0, 7)),
                  pl.BlockSpec((1, C_LATENT), lambda b, c: (0, 0)),
                  pl.BlockSpec(w_uk.shape, lambda b, c: (0, 0, 0)),
                  pl.BlockSpec(w_uv.shape, lambda b, c: (0, 0, 0)),
                  pl.BlockSpec(windows.shape, lambda b, c: (0, 0, 0))],
        out_specs=pl.BlockSpec((1, TILE, qw), lambda b, c: (b, c, 0)),
        out_shape=jax.ShapeDtypeStruct((bsz, seq, qw), BF16),
        scratch_shapes=[pltpu.VMEM((C_HEADS, n_blk, TILE, TILE), F32),
                        pltpu.VMEM((n_blk, TILE, C_LATENT), BF16),
                        pltpu.VMEM((n_blk, C_LATENT, TILE), BF16),
                        pltpu.VMEM((n_blk, TILE, IDX_DIM), BF16),
                        pltpu.VMEM((n_blk, TILE, TILE), I32),
                        pltpu.VMEM((n_blk, TILE, TILE), F32),
                        pltpu.VMEM((C_HEADS, C_LATENT, TILE), BF16)],
        compiler_params=_params(("arbitrary", "arbitrary")),
        name="dsa",
    )(p_c, p_c, p_c, p_c, p_c, gain.reshape(1, C_LATENT), w_uk, w_uv, windows)


PEER_TE = 1024
ROUTE_TM = 256
ROUTE_HEADS = 8
ROUTE_ROWS = 72


def _top_values(arrays, count):
    n = len(arrays)
    vals = [[] for _ in range(n)]
    rank = [jnp.full(a.shape, float(count), F32) for a in arrays]
    rest = list(arrays)
    for r in range(count):
        best = [jnp.max(rest[k], axis=0, keepdims=True) for k in range(n)]
        hit = [rest[k] == best[k] for k in range(n)]
        rank = [jnp.where(hit[k], float(r), rank[k]) for k in range(n)]
        rest = [jnp.where(hit[k], -jnp.inf, rest[k]) for k in range(n)]
        for k in range(n):
            vals[k].append(best[k])
    taken = [jnp.sum(jnp.where(rank[k] < count, 1.0, 0.0), axis=0, keepdims=True) for k in range(n)]
    return [(jnp.concatenate(vals[k], axis=0), rank[k], taken[k]) for k in range(n)]


def _multiplicities(rank, count):
    return jnp.concatenate([jnp.sum(jnp.where(rank == float(r), 1.0, 0.0), axis=0, keepdims=True)
                            for r in range(count)], axis=0)


def _route_kernel(q_ref, k_ref, c1_ref, r1_ref, e2_ref, reach_ref, keys_ref, wts_ref, s2_ref, t1_ref):
    heads = range(k_ref.shape[0])
    tm = q_ref.shape[0]
    for hh in heads:
        _route_prepare(q_ref[:, hh * PEER_QDIM:(hh + 1) * PEER_QDIM], k_ref.at[hh], c1_ref.at[hh], r1_ref.at[hh],
                       e2_ref.at[hh], keys_ref.at[hh], wts_ref.at[hh], s2_ref.at[hh], t1_ref.at[hh])

    def value_bit(t, taus):
        out = []
        for hh in heads:
            c = taus[hh] + lax.shift_left(jnp.int32(1), 31 - t)
            cnt = jnp.sum(jnp.where(keys_ref[hh] >= c, wts_ref[hh], 0.0), axis=0, keepdims=True)
            out.append(jnp.where(cnt >= PEER_TOPK, c, taus[hh]))
        return tuple(out)

    taus = lax.fori_loop(0, 32, value_bit, tuple(jnp.full((1, tm), INT_MIN, I32) for _ in heads))
    for hh in heads:
        _route_finish(taus[hh], c1_ref.at[hh], reach_ref.at[hh], keys_ref.at[hh], wts_ref.at[hh], s2_ref.at[hh],
                      t1_ref.at[hh])


def _route_prepare(q, k_ref, c1_ref, r1_ref, e2_ref, keys_ref, wts_ref, s2_ref, t1_ref):
    tm = q.shape[0]
    half = PEER_QDIM // 2
    s1 = lax.dot_general(k_ref[0], q[:, :half], NT, preferred_element_type=F32)
    s2 = lax.dot_general(k_ref[1], q[:, half:], NT, preferred_element_type=F32)
    (t1, rank, taken1), (t2, rank2, taken2) = _top_values([s1, s2], PEER_TOPK)
    repeats = jnp.max(taken1 + taken2) > 2.0 * PEER_TOPK
    ones = jnp.ones(t1.shape, F32)
    w1, w2 = lax.cond(repeats,
                      lambda: (_multiplicities(rank, PEER_TOPK), _multiplicities(rank2, PEER_TOPK)),
                      lambda: (ones, ones))

    r16 = lax.broadcasted_iota(I32, (16, tm), 0)
    r8 = lax.broadcasted_iota(I32, (8, tm), 0)
    pieces = [
        (t1[0:1] + t2, w1[0:1] * w2),
        (t1[1:2] + t2[:8], w1[1:2] * w2[:8]),
        (t1[2:3] + t2[:8], jnp.where(r8 < 5, w1[2:3] * w2[:8], 0.0)),
        (t1[3:4] + t2[:8], jnp.where(r8 < 4, w1[3:4] * w2[:8], 0.0)),
        (t2[0:1] + t1, jnp.where(r16 >= 4, w2[0:1] * w1, 0.0)),
        (t2[1:2] + t1[:8], jnp.where(r8 >= 4, w2[1:2] * w1[:8], 0.0)),
        (t2[2:3] + t1[:8], jnp.where(r8 == 4, w2[2:3] * w1[:8], 0.0)),
    ]
    keys_ref[...] = _sortable(jnp.concatenate([p[0] for p in pieces], axis=0))
    wts_ref[...] = jnp.concatenate([p[1] for p in pieces], axis=0)
    c1_ref[...] = s1.reshape(c1_ref.shape)
    r1_ref[...] = rank.reshape(r1_ref.shape)
    e2_ref[...] = jnp.exp(s2 - t2[0:1]).astype(BF16)
    s2_ref[...] = s2
    t1_ref[...] = t1


def _route_finish(tau_key, c1_ref, reach_ref, keys_ref, wts_ref, s2_ref, t1_ref):
    tau = _unsortable(tau_key)
    cand = _unsortable(keys_ref[...])
    top = cand[0:1]
    z = jnp.sum(jnp.where(keys_ref[...] >= tau_key, wts_ref[...] * jnp.exp(cand - top), 0.0), axis=0, keepdims=True)
    s2 = s2_ref[...]
    reach = jnp.zeros(s2.shape, F32)
    for r in range(PEER_TOPK):
        reach = reach + jnp.where(t1_ref[r:r + 1, :] + s2 >= tau, 1.0, 0.0)
    reach_ref[...] = reach.astype(BF16)
    c1_ref[...] = jnp.exp(c1_ref[...] - (t1_ref[0:1, :] + jnp.log(z)))


def peer_route(qp, sub_keys):
    n = qp.shape[0]
    tm = ROUTE_TM
    half = PEER_QDIM // 2
    hs = ROUTE_HEADS
    spec = pl.BlockSpec((hs, PEER_KEYS, tm), lambda i, h: (h, 0, i))
    sub = PEER_TE // PEER_KEYS
    chunked = pl.BlockSpec((hs, PEER_KEYS // sub, sub, tm), lambda i, h: (h, 0, 0, i))
    return pl.pallas_call(
        _route_kernel,
        grid=(n // tm, PEER_HEADS // hs),
        in_specs=[pl.BlockSpec((tm, hs * PEER_QDIM), lambda i, h: (i, h)),
                  pl.BlockSpec((hs, 2, PEER_KEYS, half), lambda i, h: (h, 0, 0, 0))],
        out_specs=[chunked, chunked, spec, spec],
        out_shape=[jax.ShapeDtypeStruct((PEER_HEADS, PEER_KEYS // sub, sub, n), F32),
                   jax.ShapeDtypeStruct((PEER_HEADS, PEER_KEYS // sub, sub, n), F32),
                   jax.ShapeDtypeStruct((PEER_HEADS, PEER_KEYS, n), BF16),
                   jax.ShapeDtypeStruct((PEER_HEADS, PEER_KEYS, n), BF16)],
        scratch_shapes=[pltpu.VMEM((hs, ROUTE_ROWS, tm), I32),
                        pltpu.VMEM((hs, ROUTE_ROWS, tm), F32),
                        pltpu.VMEM((hs, PEER_KEYS, tm), F32),
                        pltpu.VMEM((hs, PEER_TOPK, tm), F32)],
        compiler_params=_params(("parallel", "arbitrary")),
        name="peer_route",
    )(qp, sub_keys)


PACK = 16


def _peer_kernel(x_ref, xn_ref, c1_ref, r1_ref, e2_ref, reach_ref, u_ref, vt_ref, gout_ref, o_ref,
                 acc_ref, xnt_ref, hid_ref, *, norm_out):
    j = pl.program_id(1)
    n_sub = u_ref.shape[0] // PEER_KEYS
    tm = xn_ref.shape[0]

    @pl.when(j == 0)
    def _():
        acc_ref[...] = jnp.zeros_like(acc_ref)
        xnt_ref[...] = jnp.transpose(xn_ref[...].astype(F32)).astype(BF16)

    act = jnp.dot(u_ref[...], xnt_ref[...], preferred_element_type=F32)
    for ab in range(n_sub):
        rows = slice(ab * PEER_KEYS, (ab + 1) * PEER_KEYS)
        rank = [jnp.broadcast_to(r1_ref[h, 0, ab:ab + 1, :], (PACK, tm)).astype(BF16) for h in range(PEER_HEADS)]
        fac = [jnp.broadcast_to(c1_ref[h, 0, ab:ab + 1, :], (PACK, tm)).astype(BF16) for h in range(PEER_HEADS)]
        for tc in range(tm // LANES):
            cols = slice(tc * LANES, (tc + 1) * LANES)
            gate = jnp.zeros((PEER_KEYS, LANES), BF16)
            for h in range(PEER_HEADS):
                chosen = jnp.tile(rank[h][:, cols], (PEER_KEYS // PACK, 1)) < reach_ref[h, :, cols]
                e2 = e2_ref[h, :, cols]
                gate = gate + jnp.where(chosen, e2, jnp.zeros_like(e2)) * jnp.tile(fac[h][:, cols], (PEER_KEYS // PACK, 1))
            hid_ref[rows, cols] = gate * jax.nn.gelu(act[rows, cols].astype(BF16))
    acc_ref[...] += jnp.dot(vt_ref[...], hid_ref[...], preferred_element_type=F32)

    @pl.when(j == pl.num_programs(1) - 1)
    def _():
        y = x_ref[...] + jnp.transpose(acc_ref[...])
        o_ref[...] = _rms(y, gout_ref[...]) if norm_out else y


def peer(x, xn, c1, r1, e2, reach, u, vt, gain_out, norm_out, tm=512):
    n, d = x.shape
    te = PEER_TE
    n_chunks = PEER_N // te
    n_sub = te // PEER_KEYS
    chunk_spec = pl.BlockSpec((PEER_HEADS, 1, n_sub, tm), lambda i, j: (0, j, 0, i))
    tile_spec = pl.BlockSpec((PEER_HEADS, PEER_KEYS, tm), lambda i, j: (0, 0, i))
    return pl.pallas_call(
        functools.partial(_peer_kernel, norm_out=norm_out),
        grid=(n // tm, n_chunks),
        in_specs=[pl.BlockSpec((tm, d), lambda i, j: (i, 0)),
                  pl.BlockSpec((tm, d), lambda i, j: (i, 0)),
                  chunk_spec, chunk_spec, tile_spec, tile_spec,
                  pl.BlockSpec((te, d), lambda i, j: (j, 0)),
                  pl.BlockSpec((d, te), lambda i, j: (0, j)),
                  pl.BlockSpec((1, d), lambda i, j: (0, 0))],
        out_specs=pl.BlockSpec((tm, d), lambda i, j: (i, 0)),
        out_shape=jax.ShapeDtypeStruct((n, d), F32),
        scratch_shapes=[pltpu.VMEM((d, tm), F32),
                        pltpu.VMEM((d, tm), BF16),
                        pltpu.VMEM((te, tm), BF16)],
        compiler_params=_params(("parallel", "arbitrary"), vmem=PEER_VMEM_LIMIT),
        name="peer",
    )(x, xn, c1, r1, e2, reach, u, vt, gain_out.reshape(1, d))


def kernel(x, rel_bias, g_mix, w_in, a_sinks, c_lat_gain, c_w_uk, c_w_uv, w_br_a, w_br_b, w_br_c, w_out,
           g_ffn, peer_w_q, peer_sub_keys, peer_u, peer_v, g_final):
    bsz, seq, d = x.shape
    n = bsz * seq
    depth = w_in.shape[0]
    n_blk = seq // TILE
    off = [0] + [int(o) for o in np.cumsum(IN_SPLITS)]
    bt_a = rel_bias[:, :A_HEADS]
    bt_b = rel_bias[:, A_HEADS:A_HEADS + B_HEADS]
    bt_c = rel_bias[:, A_HEADS + B_HEADS:]
    bias_a = _swa_bias(bt_a)
    win_b = _bias_windows(bt_b, seq)
    win_c = _bias_windows(bt_c, seq)

    xf = x.reshape(n, d)
    for l in range(depth):
        w = w_in[l]
        pad = jnp.zeros((d, C_COLS - (off[11] - off[6])), F32)
        w_abc = jnp.concatenate([w[:, off[0]:off[6]],
                                 w[:, off[6]:off[7]], w[:, off[8]:off[9]], w[:, off[7]:off[8]],
                                 w[:, off[9]:off[11]], pad], axis=1).astype(BF16)
        w_g = w[:, off[11]:off[14]].astype(BF16)

        p_ab, p_c = in_proj(xf, g_mix[l], w_abc)
        p_a = p_ab.reshape(bsz, seq, -1)
        p_b = p_ab[:, A_COLS:].reshape(bsz, seq, 3, B_HEADS, HEAD_DIM)
        p_c = p_c.reshape(bsz, seq, -1)

        ya = swa(p_a, bias_a, a_sinks[l]).reshape(n, -1)

        qb = jnp.transpose(p_b[:, :, 0], (0, 2, 1, 3))
        kb = jnp.transpose(p_b[:, :, 1], (0, 2, 1, 3)).reshape(bsz, B_HEADS, n_blk, TILE, HEAD_DIM)
        vbt = jnp.transpose(p_b[:, :, 2].reshape(bsz, n_blk, TILE, B_HEADS, HEAD_DIM), (0, 3, 1, 4, 2))
        yb = moba(qb, kb, vbt, win_b).reshape(n, -1)

        w_uk = jnp.transpose(c_w_uk[l], (1, 0, 2)).astype(BF16)
        w_uv = jnp.transpose(c_w_uv[l], (1, 0, 2)).astype(BF16)
        yc = dsa(p_c, c_lat_gain[l], w_uk, w_uv, win_c).reshape(n, -1)

        merged = merge(xf, g_mix[l], ya, yb, yc, w_g, w_br_a[l].astype(BF16), w_br_b[l].astype(BF16),
                       w_br_c[l].astype(BF16))
        xf = matmul_res(merged, w_out[l].astype(BF16), xf)

        qp, xn = norm_matmul(xf, g_ffn[l], peer_w_q[l].astype(BF16), BF16)
        c1, r1, e2, reach = peer_route(qp, peer_sub_keys[l].astype(BF16))
        xf = peer(xf, xn, c1, r1, e2, reach, peer_u[l].astype(BF16), jnp.transpose(peer_v[l]).astype(BF16),
                  g_final, norm_out=(l == depth - 1))

    return xf.reshape(bsz, seq, d)
```
